```python
import jax
import jax.numpy as jnp
from jax import lax
import numpy as np

D_MODEL = 2048
BATCH = 4
SEQ = 2048
DEPTH = 2
DEC_BATCH = 128
DEC_SEQ = 4
PAST_LEN = 16384
PAGE_SIZE = 128

D_CONV = D_MODEL // 2
D_GLA = D_MODEL - D_CONV
GLA_HEADS = 4
GLA_DK = (D_GLA // 2) // GLA_HEADS
GLA_DV = D_GLA // GLA_HEADS
QK_WIDTH = GLA_HEADS * GLA_DK
GATE_RANK = 16
GATE_TAU = 16.0
GLA_CHUNK = 32
CONV_WIDTH = 31
N_GROUPS = 4
EXPERTS_PER_GROUP = 8
TOP_K = 2
D_EXPERT = D_MODEL // 8
EPS = 1e-6
SPLITS = (D_CONV, 2 * D_CONV, 2 * D_CONV + QK_WIDTH, 2 * D_CONV + 2 * QK_WIDTH,
          2 * D_CONV + 2 * QK_WIDTH + D_GLA, 2 * D_CONV + 2 * QK_WIDTH + 2 * D_GLA)
D_IN = 2 * D_CONV + 2 * QK_WIDTH + 2 * D_GLA + GATE_RANK

kernel_name = 'hymba_conformer_gla_hmoe_step'


def _rmsnorm(x, g):
    xf = x.astype(jnp.float32)
    y = xf * lax.rsqrt(jnp.mean(xf * xf, -1, keepdims=True) + EPS)
    return (y * g.astype(jnp.float32)).astype(x.dtype)


def _layernorm(x, g, b):
    xf = x.astype(jnp.float32)
    mu = jnp.mean(xf, -1, keepdims=True)
    var = jnp.mean(jnp.square(xf - mu), -1, keepdims=True)
    y = (xf - mu) * lax.rsqrt(var + EPS)
    return (y * g.astype(jnp.float32) + b.astype(jnp.float32)).astype(x.dtype)


def _conformer_conv(u, buf, conv_w, conv_b, ln_g, ln_b):
    full = jnp.concatenate([buf.astype(u.dtype), u], axis=1)
    y = lax.conv_general_dilated(full, conv_w[:, None, :].astype(u.dtype), window_strides=(1,),
                                 padding='VALID', dimension_numbers=('NWC', 'WIO', 'NWC'),
                                 feature_group_count=D_CONV)
    y = jax.nn.silu(_layernorm(y + conv_b, ln_g, ln_b))
    return y, full[:, -(CONV_WIDTH - 1):]


def _gla(q, k, v, log_a, s0):
    bsz, L = q.shape[0], q.shape[1]
    C = min(GLA_CHUNK, L)
    n = -(-L // C)
    pad = n * C - L

    def blocks(t):
        t = t.astype(jnp.float32)
        if pad:
            t = jnp.pad(t, ((0, 0), (0, pad), (0, 0), (0, 0)))
        return t.reshape(bsz, n, C, t.shape[2], t.shape[3])

    q, k, v, log_a = blocks(q), blocks(k), blocks(v), blocks(log_a)
    b = jnp.cumsum(log_a, axis=2)
    q_dec = q * jnp.exp(b)
    k_inv = k * jnp.exp(-b)
    causal = jnp.tril(jnp.ones((C, C), dtype=bool))
    att = jnp.where(causal, jnp.einsum('bnthk,bnshk->bnhts', q_dec, k_inv), 0.0)
    o_intra = jnp.einsum('bnhts,bnshv->bnthv', att, v)
    b_last = b[:, :, -1]
    k_end = k * jnp.exp(b_last[:, :, None] - b)
    upd = jnp.einsum('bnshk,bnshv->bnhkv', k_end, v)
    decay = jnp.exp(b_last)

    def step(s, inp):
        d, du = inp
        return s * d[..., None] + du, s

    s_fin, s_prev = lax.scan(step, s0.astype(jnp.float32),
                             (jnp.moveaxis(decay, 1, 0), jnp.moveaxis(upd, 1, 0)))
    s_prev = jnp.moveaxis(s_prev, 0, 1)
    o = o_intra + jnp.einsum('bnthk,bnhkv->bnthv', q_dec, s_prev)
    return o.reshape(bsz, n * C, GLA_HEADS, GLA_DV)[:, :L], s_fin


def _hier_moe(h, router_grp_w, router_grp_b, router_exp_w, router_exp_b, exp_w_gate, exp_w_up, exp_w_down):
    grp_logits = (h @ router_grp_w + router_grp_b).astype(jnp.float32)
    grp_prob = jax.nn.softmax(grp_logits, -1)
    g_idx = jnp.argmax(grp_logits, -1)
    g_w = jnp.take_along_axis(grp_prob, g_idx[:, None], 1)[:, 0]
    exp_logits = (jnp.einsum('td,dge->tge', h, router_exp_w) + router_exp_b).astype(jnp.float32)
    sel_logits = jnp.take_along_axis(exp_logits, g_idx[:, None, None], 1)[:, 0]
    top_w, top_i = lax.top_k(jax.nn.softmax(sel_logits, -1), TOP_K)
    top_w = top_w / jnp.sum(top_w, -1, keepdims=True)
    exp_gate = jnp.einsum('tk,tke->te', top_w, jax.nn.one_hot(top_i, EXPERTS_PER_GROUP, dtype=jnp.float32))
    combine = (jax.nn.one_hot(g_idx, N_GROUPS, dtype=jnp.float32)[:, :, None]
               * (g_w[:, None] * exp_gate)[:, None, :]).astype(h.dtype)
    y = jnp.zeros_like(h)
    for g in range(N_GROUPS):
        a = jnp.einsum('td,edf->tef', h, exp_w_gate[g])
        u = jnp.einsum('td,edf->tef', h, exp_w_up[g])
        hid = jax.nn.silu(a) * u * combine[:, g, :, None]
        y = y + jnp.einsum('tef,efd->td', hid, exp_w_down[g])
    return y


def _layer(x, c, conv_buf, gla_s, w_ada, b_ada, norm1_g, norm2_g, w_in, conv_w, conv_b, conv_ln_g,
           conv_ln_b, gate_w2, gate_b, gla_norm_g, w_out, router_grp_w, router_grp_b, router_exp_w,
           router_exp_b, exp_w_gate, exp_w_up, exp_w_down):
    bsz, L, _ = x.shape
    mod = (jax.nn.silu(c) @ w_ada + b_ada).reshape(bsz, 6, 1, D_MODEL)
    shift1, scale1, gate1, shift2, scale2, gate2 = [mod[:, i] for i in range(6)]
    h = _rmsnorm(x, norm1_g) * (1 + scale1) + shift1
    z = h @ w_in
    glu_a, glu_b, q, k, v, g_out, gate_lr = jnp.split(z, list(SPLITS), axis=-1)
    conv_out, new_buf = _conformer_conv(glu_a * jax.nn.sigmoid(glu_b), conv_buf, conv_w, conv_b,
                                        conv_ln_g, conv_ln_b)
    q = q.reshape(bsz, L, GLA_HEADS, GLA_DK) * (GLA_DK ** -0.5)
    k = k.reshape(bsz, L, GLA_HEADS, GLA_DK)
    v = v.reshape(bsz, L, GLA_HEADS, GLA_DV)
    log_a = jax.nn.log_sigmoid((gate_lr @ gate_w2 + gate_b).astype(jnp.float32)) / GATE_TAU
    log_a = log_a.reshape(bsz, L, GLA_HEADS, GLA_DK)
    o, new_s = _gla(q, k, v, log_a, gla_s)
    o = _rmsnorm(o.astype(x.dtype), gla_norm_g).reshape(bsz, L, D_GLA) * jax.nn.silu(g_out)
    x = x + gate1 * (jnp.concatenate([conv_out, o], axis=-1) @ w_out)
    h2 = _rmsnorm(x, norm2_g) * (1 + scale2) + shift2
    ff = _hier_moe(h2.reshape(bsz * L, D_MODEL), router_grp_w, router_grp_b, router_exp_w, router_exp_b,
                   exp_w_gate, exp_w_up, exp_w_down).reshape(bsz, L, D_MODEL)
    x = x + gate2 * ff
    return x, new_buf, new_s.astype(x.dtype)


def setup_inputs(seed: int = 0) -> dict:
    key = jax.random.key(seed)
    ks = iter(jax.random.split(key, 32))

    def nrm(shape, s):
        return jax.random.normal(next(ks), shape, jnp.float32) * s

    G, E, F, H = N_GROUPS, EXPERTS_PER_GROUP, D_EXPERT, GLA_HEADS
    return {
        'x_prompt': nrm((BATCH, SEQ, D_MODEL), 1.0),
        'x_sample': nrm((DEC_BATCH, DEC_SEQ, D_MODEL), 1.0),
        'c_prompt': nrm((BATCH, D_MODEL), 1.0),
        'c_sample': nrm((DEC_BATCH, D_MODEL), 1.0),
        'state_conv': nrm((DEPTH, DEC_BATCH, CONV_WIDTH - 1, D_CONV), 1.0),
        'state_gla': nrm((DEPTH, DEC_BATCH, H, GLA_DK, GLA_DV), 1.0),
        'w_ada': nrm((DEPTH, D_MODEL, 6 * D_MODEL), 0.5 * D_MODEL ** -0.5),
        'b_ada': nrm((DEPTH, 6 * D_MODEL), 0.02),
        'norm1_g': 1.0 + nrm((DEPTH, D_MODEL), 0.02),
        'norm2_g': 1.0 + nrm((DEPTH, D_MODEL), 0.02),
        'w_in': nrm((DEPTH, D_MODEL, D_IN), D_MODEL ** -0.5),
        'conv_w': nrm((DEPTH, CONV_WIDTH, D_CONV), CONV_WIDTH ** -0.5),
        'conv_b': nrm((DEPTH, D_CONV), 0.02),
        'conv_ln_g': 1.0 + nrm((DEPTH, D_CONV), 0.02),
        'conv_ln_b': nrm((DEPTH, D_CONV), 0.02),
        'gate_w2': nrm((DEPTH, GATE_RANK, QK_WIDTH), GATE_RANK ** -0.5),
        'gate_b': nrm((DEPTH, QK_WIDTH), 0.1),
        'gla_norm_g': 1.0 + nrm((DEPTH, H, GLA_DV), 0.02),
        'w_out': nrm((DEPTH, D_MODEL, D_MODEL), D_MODEL ** -0.5),
        'router_grp_w': nrm((DEPTH, D_MODEL, G), D_MODEL ** -0.5),
        'router_grp_b': nrm((DEPTH, G), 0.01),
        'router_exp_w': nrm((DEPTH, D_MODEL, G, E), D_MODEL ** -0.5),
        'router_exp_b': nrm((DEPTH, G, E), 0.01),
        'exp_w_gate': nrm((DEPTH, G, E, D_MODEL, F), D_MODEL ** -0.5),
        'exp_w_up': nrm((DEPTH, G, E, D_MODEL, F), D_MODEL ** -0.5),
        'exp_w_down': nrm((DEPTH, G, E, F, D_MODEL), F ** -0.5),
        'final_norm_g': 1.0 + nrm((D_MODEL,), 0.02),
    }


def reference(x_prompt, x_sample, c_prompt, c_sample, state_conv, state_gla, w_ada, b_ada, norm1_g,
              norm2_g, w_in, conv_w, conv_b, conv_ln_g, conv_ln_b, gate_w2, gate_b, gla_norm_g, w_out,
              router_grp_w, router_grp_b, router_exp_w, router_exp_b, exp_w_gate, exp_w_up, exp_w_down,
              final_norm_g):
    bsz = x_prompt.shape[0]
    dt = x_prompt.dtype
    xp, xs = x_prompt, x_sample
    conv_p, gla_p, conv_s, gla_s = [], [], [], []
    for l in range(DEPTH):
        params = (w_ada[l], b_ada[l], norm1_g[l], norm2_g[l], w_in[l], conv_w[l], conv_b[l], conv_ln_g[l],
                  conv_ln_b[l], gate_w2[l], gate_b[l], gla_norm_g[l], w_out[l], router_grp_w[l],
                  router_grp_b[l], router_exp_w[l], router_exp_b[l], exp_w_gate[l], exp_w_up[l],
                  exp_w_down[l])
        conv0 = jnp.zeros((bsz, CONV_WIDTH - 1, D_CONV), dt)
        gla0 = jnp.zeros((bsz, GLA_HEADS, GLA_DK, GLA_DV), jnp.float32)
        xp, cb_p, sg_p = _layer(xp, c_prompt, conv0, gla0, *params)
        xs, cb_s, sg_s = _layer(xs, c_sample, state_conv[l], state_gla[l], *params)
        conv_p.append(cb_p)
        gla_p.append(sg_p)
        conv_s.append(cb_s)
        gla_s.append(sg_s)
    y_prompt = _rmsnorm(xp, final_norm_g)
    y_sample = _rmsnorm(xs, final_norm_g)
    return (y_prompt, y_sample, jnp.stack(conv_p), jnp.stack(gla_p), jnp.stack(conv_s), jnp.stack(gla_s))
```

```python
import functools

import jax
import jax.numpy as jnp
from jax import lax
from jax.experimental import pallas as pl
from jax.experimental.pallas import tpu as pltpu

F32 = jnp.float32
BF16 = jnp.bfloat16

EPS = 1e-6
GATE_TAU = 16.0
GLA_CHUNK = 32
TOP_K = 2

ROW_TILE = 512
EXPERT_TILE = 256
CONV_ROWS = 32
CONV_HALO = 32
LANES = 128
VMEM_LIMIT = 56 * 1024 * 1024


def _params(*sem):
    return pltpu.CompilerParams(dimension_semantics=sem, vmem_limit_bytes=VMEM_LIMIT)


def _bdot(a, b):
    return jnp.dot(a.astype(BF16), b.astype(BF16), preferred_element_type=F32)


def _round_bf16(x):
    return x.astype(BF16).astype(F32)


def _split3(x):
    hi = x.astype(BF16)
    r = x - hi.astype(F32)
    mid = r.astype(BF16)
    lo = (r - mid.astype(F32)).astype(BF16)
    return hi, mid, lo


def _silu(x):
    return x * jax.nn.sigmoid(x)


def _store_by_group(i, n_prompt_tiles, dec_seq, out_ref, fn, vals, p_refs, s_refs):
    @pl.when(i < n_prompt_tiles)
    def _():
        out_ref[...] = fn(vals, [r[...] for r in p_refs]).astype(out_ref.dtype)

    @pl.when(i >= n_prompt_tiles)
    def _():
        mods = [r[...] for r in s_refs]
        nb = mods[0].shape[0]
        for t in range(dec_seq):
            rows = slice(t * nb, (t + 1) * nb)
            out_ref[rows, :] = fn([v[rows] for v in vals], mods).astype(out_ref.dtype)


def _ada_kernel(c_ref, w_ref, b_ref, o_ref):
    c = c_ref[...]
    o_ref[...] = _bdot(_silu(c), w_ref[...]) + b_ref[...]


def _ada(c_all, w_ada, b_ada):
    depth, d, n = w_ada.shape
    rows = c_all.shape[0]
    tn = 1024
    return pl.pallas_call(
        _ada_kernel,
        out_shape=jax.ShapeDtypeStruct((depth, rows, n), F32),
        grid=(depth, n // tn),
        in_specs=[
            pl.BlockSpec((rows, d), lambda l, j: (0, 0)),
            pl.BlockSpec((None, d, tn), lambda l, j: (l, 0, j)),
            pl.BlockSpec((None, 1, tn), lambda l, j: (l, 0, j)),
        ],
        out_specs=pl.BlockSpec((None, rows, tn), lambda l, j: (l, 0, j)),
        compiler_params=_params("arbitrary", "arbitrary"),
        name="ada",
    )(c_all, w_ada, b_ada.reshape(depth, 1, n))


def _mod_specs(cols, width, n_prompt_tiles, tiles_per_seq, n_seq, dec_batch, grid_rank=1, row_axis=0, col_fn=None):
    p_specs, s_specs = [], []
    for c in cols:
        def p_map(*idx, c=c):
            b = jnp.minimum(idx[row_axis] // tiles_per_seq, n_seq - 1)
            return (b, 0, c if col_fn is None else col_fn(c, idx))

        def s_map(*idx, c=c):
            return (0, c if col_fn is None else col_fn(c, idx))

        p_specs.append(pl.BlockSpec((None, 1, width), p_map))
        s_specs.append(pl.BlockSpec((dec_batch, width), s_map))
    return p_specs, s_specs


def _rms(x, g):
    return x * lax.rsqrt(jnp.mean(x * x, axis=-1, keepdims=True) + EPS) * g


def _norm1_kernel(x_ref, g_ref, shp, scp, shs, scs, wgl_ref, gw2_ref, gb_ref, h_ref, la_ref, *, npt, dec_seq):
    i = pl.program_id(0)
    y = _rms(x_ref[...], g_ref[...])
    _store_by_group(i, npt, dec_seq, h_ref, lambda v, m: v[0] * (1.0 + m[1]) + m[0], [y], [shp, scp], [shs, scs])
    gate_lr = jnp.dot(h_ref[...], wgl_ref[...], preferred_element_type=F32)
    pre = _bdot(gate_lr, gw2_ref[...]) + gb_ref[...]
    la_ref[...] = (jnp.minimum(pre, 0.0) - jnp.log1p(jnp.exp(-jnp.abs(pre)))) * (1.0 / GATE_TAU)


def _norm1(x, g, mod_p, mod_s, w_gl, gw2, gb, dims):
    t, d = x.shape
    tm = ROW_TILE
    p_specs, s_specs = _mod_specs((0, 1), d, dims["npt"], dims["tps"], dims["n_seq"], dims["dec_batch"])
    qk = gw2.shape[1]
    return pl.pallas_call(
        functools.partial(_norm1_kernel, npt=dims["npt"], dec_seq=dims["dec_seq"]),
        out_shape=(jax.ShapeDtypeStruct((t, d), BF16), jax.ShapeDtypeStruct((t, qk), F32)),
        grid=(t // tm,),
        in_specs=[
            pl.BlockSpec((tm, d), lambda i: (i, 0)),
            pl.BlockSpec((1, d), lambda i: (0, 0)),
            *p_specs,
            *s_specs,
            pl.BlockSpec(w_gl.shape, lambda i: (0, 0)),
            pl.BlockSpec(gw2.shape, lambda i: (0, 0)),
            pl.BlockSpec((1, qk), lambda i: (0, 0)),
        ],
        out_specs=(pl.BlockSpec((tm, d), lambda i: (i, 0)), pl.BlockSpec((tm, qk), lambda i: (i, 0))),
        compiler_params=_params("arbitrary"),
        name="norm1",
    )(x, g.reshape(1, d), mod_p, mod_p, mod_s, mod_s, w_gl, gw2, gb.reshape(1, qk))


def _inproj_kernel(h_ref, w_ref, o_ref, wb_ref):
    @pl.when(pl.program_id(1) == 0)
    def _():
        wb_ref[...] = w_ref[...].astype(BF16)

    o_ref[...] = jnp.dot(h_ref[...], wb_ref[...], preferred_element_type=F32)


def _inproj(h, w_in, layer, n_cols):
    t, d = h.shape
    tm, tn = ROW_TILE, 1024
    return pl.pallas_call(
        _inproj_kernel,
        out_shape=jax.ShapeDtypeStruct((t, n_cols), F32),
        grid=(n_cols // tn, t // tm),
        in_specs=[
            pl.BlockSpec((tm, d), lambda j, i: (i, 0)),
            pl.BlockSpec((None, d, tn), lambda j, i: (layer, 0, j)),
        ],
        out_specs=pl.BlockSpec((tm, tn), lambda j, i: (i, j)),
        scratch_shapes=[pltpu.VMEM((d, tn), BF16)],
        compiler_params=_params("arbitrary", "arbitrary"),
        name="inproj",
    )(h, w_in)


def _ln_silu(y, g, b):
    mu = jnp.mean(y, axis=-1, keepdims=True)
    yc = y - mu
    var = jnp.mean(yc * yc, axis=-1, keepdims=True)
    return _silu(yc * lax.rsqrt(var + EPS) * g + b)


def _conv_prompt_kernel(a_ref, b_ref, cw_ref, cb_ref, lng_ref, lnb_ref, o_ref, st_ref, full_ref, *, kw):
    j = pl.program_id(1)
    tm, c = a_ref.shape
    halo = CONV_HALO

    @pl.when(j == 0)
    def _():
        full_ref[0:halo, :] = jnp.zeros((halo, c), F32)

    @pl.when(j > 0)
    def _():
        full_ref[0:halo, :] = full_ref[tm:tm + halo, :]

    full_ref[halo:halo + tm, :] = a_ref[...] * jax.nn.sigmoid(b_ref[...])
    off = halo - (kw - 1)
    rb = CONV_ROWS

    def body(r, carry):
        r0 = pl.multiple_of(r * rb, rb)
        win = full_ref.at[pl.ds(r0, rb + halo), :]
        acc = jnp.zeros((rb, c), F32)
        for w in range(kw):
            acc = acc + win[off + w:off + w + rb, :] * cw_ref[w:w + 1, :]
        y = _ln_silu(acc + cb_ref[...], lng_ref[...], lnb_ref[...])
        o_ref[pl.ds(r0, rb), :] = y.astype(o_ref.dtype)
        return carry

    lax.fori_loop(0, tm // rb, body, 0)

    @pl.when(j == pl.num_programs(1) - 1)
    def _():
        st_ref[...] = full_ref[halo + tm - (kw - 1):halo + tm, :]


def _conv_prompt(z, cw, cb, lng, lnb, n_seq, seq_len):
    kw, c = cw.shape
    tm = ROW_TILE
    tps = seq_len // tm
    return pl.pallas_call(
        functools.partial(_conv_prompt_kernel, kw=kw),
        out_shape=(jax.ShapeDtypeStruct((n_seq * seq_len, c), BF16), jax.ShapeDtypeStruct((n_seq, kw - 1, c), F32)),
        grid=(n_seq, tps),
        in_specs=[
            pl.BlockSpec((tm, c), lambda b, j: (b * tps + j, 0)),
            pl.BlockSpec((tm, c), lambda b, j: (b * tps + j, 1)),
            pl.BlockSpec((kw, c), lambda b, j: (0, 0)),
            pl.BlockSpec((1, c), lambda b, j: (0, 0)),
            pl.BlockSpec((1, c), lambda b, j: (0, 0)),
            pl.BlockSpec((1, c), lambda b, j: (0, 0)),
        ],
        out_specs=(
            pl.BlockSpec((tm, c), lambda b, j: (b * tps + j, 0)),
            pl.BlockSpec((None, kw - 1, c), lambda b, j: (b, 0, 0)),
        ),
        scratch_shapes=[pltpu.VMEM((tm + CONV_HALO, c), F32)],
        compiler_params=_params("arbitrary", "arbitrary"),
        name="conv_prompt",
    )(z, z, cw, cb.reshape(1, c), lng.reshape(1, c), lnb.reshape(1, c))


def _conv_sample_kernel(*refs, kw, dec_seq):
    a_refs = refs[0:dec_seq]
    b_refs = refs[dec_seq:2 * dec_seq]
    st_ref, cw_ref, cb_ref, lng_ref, lnb_ref, o_ref, ns_ref = refs[2 * dec_seq:]
    hist = kw - 1
    u = [a_refs[t][...] * jax.nn.sigmoid(b_refs[t][...]) for t in range(dec_seq)]
    def row(j):
        return st_ref[j] if j < hist else u[j - hist]

    for t in range(dec_seq):
        acc = row(t) * cw_ref[0:1, :]
        for w in range(1, kw):
            acc = acc + row(t + w) * cw_ref[w:w + 1, :]
        y = _ln_silu(acc + cb_ref[...], lng_ref[...], lnb_ref[...])
        o_ref[t] = y.astype(o_ref.dtype)
    for j in range(hist):
        ns_ref[j] = row(j + dec_seq)


def _conv_sample(z, st_t, cw, cb, lng, lnb, row0, dec_seq, dec_batch):
    kw, c = cw.shape
    bs = 16
    a_specs = [pl.BlockSpec((bs, c), lambda s, t=t: ((row0 + t * dec_batch) // bs + s, 0)) for t in range(dec_seq)]
    b_specs = [pl.BlockSpec((bs, c), lambda s, t=t: ((row0 + t * dec_batch) // bs + s, 1)) for t in range(dec_seq)]
    vec = pl.BlockSpec((1, c), lambda s: (0, 0))
    return pl.pallas_call(
        functools.partial(_conv_sample_kernel, kw=kw, dec_seq=dec_seq),
        out_shape=(jax.ShapeDtypeStruct((dec_seq, dec_batch, c), BF16), jax.ShapeDtypeStruct((kw - 1, dec_batch, c), F32)),
        grid=(dec_batch // bs,),
        in_specs=[*a_specs, *b_specs, pl.BlockSpec((kw - 1, bs, c), lambda s: (0, s, 0)),
                  pl.BlockSpec((kw, c), lambda s: (0, 0)), vec, vec, vec],
        out_specs=(pl.BlockSpec((dec_seq, bs, c), lambda s: (0, s, 0)), pl.BlockSpec((kw - 1, bs, c), lambda s: (0, s, 0))),
        compiler_params=_params("arbitrary"),
        name="conv_sample",
    )(*([z] * (2 * dec_seq)), st_t, cw, cb.reshape(1, c), lng.reshape(1, c), lnb.reshape(1, c))


def _gla_prompt_kernel(qk_ref, v_ref, g_ref, la_ref, gn_ref, o_ref, sfin_ref, st_ref, sn_ref, *, heads, dk, dv):
    j = pl.program_id(1)
    tm = qk_ref.shape[0]
    ck = GLA_CHUNK
    nch = tm // ck
    qkw = heads * dk

    @pl.when(j == 0)
    def _():
        st_ref[...] = jnp.zeros(st_ref.shape, F32)

    la = la_ref[...]
    row = lax.broadcasted_iota(jnp.int32, (tm, tm), 0)
    col = lax.broadcasted_iota(jnp.int32, (tm, tm), 1)
    same = (row // ck) == (col // ck)
    causal = same & (col <= row)
    tri_incl = causal.astype(BF16)
    tri_after = (same & (col > row)).astype(BF16)
    sel = (lax.broadcasted_iota(jnp.int32, (nch, tm), 1) // ck == lax.broadcasted_iota(jnp.int32, (nch, tm), 0)).astype(BF16)
    parts = _split3(la)
    b = sum(jnp.dot(tri_incl, p, preferred_element_type=F32) for p in parts)
    rest = sum(jnp.dot(tri_after, p, preferred_element_type=F32) for p in parts)
    tot = sum(jnp.dot(sel, p, preferred_element_type=F32) for p in parts)
    qk = qk_ref[...]
    q = qk[:, :qkw] * (dk ** -0.5)
    k = qk[:, qkw:]
    q_dec = (q * jnp.exp(b)).astype(BF16)
    k_inv = (k * jnp.exp(-b)).astype(BF16)
    k_end = _round_bf16(k * jnp.exp(rest))
    decay = jnp.exp(tot)
    v_all = v_ref[...]
    g_all = g_ref[...]
    for h in range(heads):
        ks = slice(h * dk, (h + 1) * dk)
        vs = slice(h * dv, (h + 1) * dv)
        qd, ki, ke = q_dec[:, ks], k_inv[:, ks], k_end[:, ks]
        vh = v_all[:, vs]
        vb = vh.astype(BF16)
        vr = _round_bf16(vh)
        att = lax.dot_general(qd, ki, (((1,), (1,)), ((), ())), preferred_element_type=F32)
        att = jnp.where(causal, att, 0.0).astype(BF16)
        o = jnp.dot(att, vb, preferred_element_type=F32)
        s = st_ref[h]
        for n in range(nch):
            rs = slice(n * ck, (n + 1) * ck)
            sn_ref[n] = s.astype(BF16)
            upd = lax.dot_general(vr[rs], ke[rs], (((0,), (0,)), ((), ())), preferred_element_type=F32)
            s = s * decay[n:n + 1, ks] + upd
        st_ref[h] = s
        inter = [lax.dot_general(qd[n * ck:(n + 1) * ck], sn_ref[n], (((1,), (1,)), ((), ())), preferred_element_type=F32)
                 for n in range(nch)]
        o = o + jnp.concatenate(inter, axis=0)
        o = _rms(o, gn_ref[:, vs]) * _silu(g_all[:, vs])
        o_ref[:, vs] = o.astype(o_ref.dtype)

    @pl.when(j == pl.num_programs(1) - 1)
    def _():
        for h in range(heads):
            sfin_ref[h] = st_ref[h].T


def _gla_prompt(z, la, gn, n_seq, seq_len, heads, dk, dv):
    tm = ROW_TILE
    tps = seq_len // tm
    w = heads * dv
    qkw = heads * dk
    assert 2 * qkw == w
    return pl.pallas_call(
        functools.partial(_gla_prompt_kernel, heads=heads, dk=dk, dv=dv),
        out_shape=(jax.ShapeDtypeStruct((n_seq * seq_len, w), BF16), jax.ShapeDtypeStruct((n_seq, heads, dk, dv), F32)),
        grid=(n_seq, tps),
        in_specs=[
            pl.BlockSpec((tm, w), lambda b, j: (b * tps + j, 2)),
            pl.BlockSpec((tm, w), lambda b, j: (b * tps + j, 3)),
            pl.BlockSpec((tm, w), lambda b, j: (b * tps + j, 4)),
            pl.BlockSpec((tm, qkw), lambda b, j: (b * tps + j, 0)),
            pl.BlockSpec((1, w), lambda b, j: (0, 0)),
        ],
        out_specs=(
            pl.BlockSpec((tm, w), lambda b, j: (b * tps + j, 0)),
            pl.BlockSpec((None, heads, dk, dv), lambda b, j: (b, 0, 0, 0)),
        ),
        scratch_shapes=[pltpu.VMEM((heads, dv, dk), F32), pltpu.VMEM((tm // GLA_CHUNK, dv, dk), BF16)],
        compiler_params=_params("arbitrary", "arbitrary"),
        name="gla_prompt",
    )(z, z, z, la, gn.reshape(1, w))


def _gla_sample_kernel(q_ref, k_ref, v_ref, g_ref, la_ref, s_ref, gn_ref, o_ref, ns_ref, *, heads, dk, dv):
    bs, ln, _ = q_ref.shape
    tril = lax.broadcasted_iota(jnp.int32, (ln, ln), 1) <= lax.broadcasted_iota(jnp.int32, (ln, ln), 0)
    for s in range(bs):
        q_s, k_s, v_s, g_s, la_s = q_ref[s], k_ref[s], v_ref[s], g_ref[s], la_ref[s]
        for h in range(heads):
            ks = slice(h * dk, (h + 1) * dk)
            vs = slice(h * dv, (h + 1) * dv)
            la = la_s[:, ks]
            rows = [la[0:1]]
            for t in range(1, ln):
                rows.append(rows[-1] + la[t:t + 1])
            b = jnp.concatenate(rows, axis=0)
            b_last = rows[-1]
            q_dec = _round_bf16(q_s[:, ks] * (dk ** -0.5) * jnp.exp(b))
            k_inv = _round_bf16(k_s[:, ks] * jnp.exp(-b))
            k_end = _round_bf16(k_s[:, ks] * jnp.exp(b_last - b))
            vr = _round_bf16(v_s[:, vs])
            s0 = s_ref[s, h]
            att = lax.dot_general(q_dec, k_inv, (((1,), (1,)), ((), ())), preferred_element_type=F32)
            att = _round_bf16(jnp.where(tril, att, 0.0))
            o = jnp.dot(att, vr, preferred_element_type=F32) + jnp.dot(q_dec, _round_bf16(s0), preferred_element_type=F32)
            upd = lax.dot_general(k_end, vr, (((0,), (0,)), ((), ())), preferred_element_type=F32)
            d_col = jnp.broadcast_to(jnp.exp(b_last), (dk, dk)).T
            ns_ref[s, h] = s0 * jnp.concatenate([d_col] * (dv // dk), axis=1) + upd
            o = _rms(o, gn_ref[:, vs]) * _silu(g_s[:, vs])
            o_ref[s, :, vs] = o.astype(o_ref.dtype)


def _gla_sample(q3, k3, v3, g3, la3, state, layer, gn, heads, dk, dv):
    nb, ln, w = v3.shape
    qkw = heads * dk
    bs = 8
    return pl.pallas_call(
        functools.partial(_gla_sample_kernel, heads=heads, dk=dk, dv=dv),
        out_shape=(jax.ShapeDtypeStruct((nb, ln, w), F32), jax.ShapeDtypeStruct((nb, heads, dk, dv), F32)),
        grid=(nb // bs,),
        in_specs=[
            pl.BlockSpec((bs, ln, qkw), lambda s: (s, 0, 0)),
            pl.BlockSpec((bs, ln, qkw), lambda s: (s, 0, 0)),
            pl.BlockSpec((bs, ln, w), lambda s: (s, 0, 0)),
            pl.BlockSpec((bs, ln, w), lambda s: (s, 0, 0)),
            pl.BlockSpec((bs, ln, qkw), lambda s: (s, 0, 0)),
            pl.BlockSpec((None, bs, heads, dk, dv), lambda s: (layer, s, 0, 0, 0)),
            pl.BlockSpec((1, w), lambda s: (0, 0)),
        ],
        out_specs=(
            pl.BlockSpec((bs, ln, w), lambda s: (s, 0, 0)),
            pl.BlockSpec((bs, heads, dk, dv), lambda s: (s, 0, 0, 0)),
        ),
        compiler_params=_params("arbitrary"),
        name="gla_sample",
    )(q3, k3, v3, g3, la3, state, gn.reshape(1, w))


def _outproj_kernel(c_ref, o_ref, x_ref, w_ref, gp, gs, y_ref, wb_ref, *, npt, dec_seq):
    i = pl.program_id(1)

    @pl.when(i == 0)
    def _():
        wb_ref[...] = w_ref[...].astype(BF16)

    half = c_ref.shape[1]
    mix = (jnp.dot(c_ref[...], wb_ref[0:half, :], preferred_element_type=F32)
           + jnp.dot(o_ref[...], wb_ref[half:, :], preferred_element_type=F32))
    _store_by_group(i, npt, dec_seq, y_ref, lambda v, m: v[0] + m[0] * v[1], [x_ref[...], mix], [gp], [gs])


def _outproj(conv_out, gla_out, x, w_out, layer, mod_p, mod_s, dims):
    t, d = x.shape
    half = conv_out.shape[1]
    tm, tn = ROW_TILE, 1024
    nj = d // tn
    p_specs, s_specs = _mod_specs((2,), tn, dims["npt"], dims["tps"], dims["n_seq"], dims["dec_batch"],
                                  row_axis=1, col_fn=lambda c, idx: c * nj + idx[0])
    return pl.pallas_call(
        functools.partial(_outproj_kernel, npt=dims["npt"], dec_seq=dims["dec_seq"]),
        out_shape=jax.ShapeDtypeStruct((t, d), F32),
        grid=(nj, t // tm),
        in_specs=[
            pl.BlockSpec((tm, half), lambda j, i: (i, 0)),
            pl.BlockSpec((tm, half), lambda j, i: (i, 0)),
            pl.BlockSpec((tm, tn), lambda j, i: (i, j)),
            pl.BlockSpec((None, d, tn), lambda j, i: (layer, 0, j)),
            *p_specs,
            *s_specs,
        ],
        out_specs=pl.BlockSpec((tm, tn), lambda j, i: (i, j)),
        scratch_shapes=[pltpu.VMEM((d, tn), BF16)],
        compiler_params=_params("arbitrary", "arbitrary"),
        name="outproj",
    )(conv_out, gla_out, x, w_out, mod_p, mod_s)


def _norm2_kernel(x_ref, g_ref, shp, scp, shs, scs, wr_hi, wr_lo, br_ref, h_ref, ids_ref, wts_ref, *, npt, dec_seq, n_grp, n_exp):
    i = pl.program_id(0)
    y = _rms(x_ref[...], g_ref[...])
    _store_by_group(i, npt, dec_seq, h_ref, lambda v, m: v[0] * (1.0 + m[1]) + m[0], [y], [shp, scp], [shs, scs])
    h = h_ref[...]
    h_hi = h.astype(BF16)
    h_lo = (h - h_hi.astype(F32)).astype(BF16)
    logits = (jnp.dot(h_hi, wr_hi[...], preferred_element_type=F32) + jnp.dot(h_lo, wr_hi[...], preferred_element_type=F32)
              + jnp.dot(h_hi, wr_lo[...], preferred_element_type=F32)) + br_ref[...]
    lane = lax.broadcasted_iota(jnp.int32, logits.shape, 1).astype(F32)
    big = jnp.float32(LANES)
    neg = jnp.float32(-jnp.inf)
    gl = jnp.where(lane < n_grp, logits, neg)
    gmax = jnp.max(gl, axis=-1, keepdims=True)
    gidx = jnp.min(jnp.where(gl == gmax, lane, big), axis=-1, keepdims=True)
    g_w = 1.0 / jnp.sum(jnp.exp(gl - gmax), axis=-1, keepdims=True)
    lo = n_grp + gidx * n_exp
    in_grp = (lane >= lo) & (lane < lo + n_exp)
    sl = jnp.where(in_grp, logits, neg)
    p = jnp.exp(sl - jnp.max(sl, axis=-1, keepdims=True))
    p = p / jnp.sum(p, axis=-1, keepdims=True)
    p = jnp.where(in_grp, p, -1.0)
    p1 = jnp.max(p, axis=-1, keepdims=True)
    i1 = jnp.min(jnp.where(p == p1, lane, big), axis=-1, keepdims=True)
    p_rest = jnp.where(lane == i1, -1.0, p)
    p2 = jnp.max(p_rest, axis=-1, keepdims=True)
    i2 = jnp.min(jnp.where(p_rest == p2, lane, big), axis=-1, keepdims=True)
    denom = p1 + p2
    ids_ref[...] = jnp.where(lane == 0, i1 - n_grp, jnp.where(lane == 1, i2 - n_grp, 0.0)).astype(jnp.int32)
    wts_ref[...] = jnp.where(lane == 0, g_w * (p1 / denom), jnp.where(lane == 1, g_w * (p2 / denom), 0.0))


def _norm2(x, g, mod_p, mod_s, wr_hi, wr_lo, br, dims, n_grp, n_exp):
    t, d = x.shape
    tm = ROW_TILE
    p_specs, s_specs = _mod_specs((3, 4), d, dims["npt"], dims["tps"], dims["n_seq"], dims["dec_batch"])
    return pl.pallas_call(
        functools.partial(_norm2_kernel, npt=dims["npt"], dec_seq=dims["dec_seq"], n_grp=n_grp, n_exp=n_exp),
        out_shape=(jax.ShapeDtypeStruct((t, d), F32), jax.ShapeDtypeStruct((t, LANES), jnp.int32),
                   jax.ShapeDtypeStruct((t, LANES), F32)),
        grid=(t // tm,),
        in_specs=[
            pl.BlockSpec((tm, d), lambda i: (i, 0)),
            pl.BlockSpec((1, d), lambda i: (0, 0)),
            *p_specs,
            *s_specs,
            pl.BlockSpec((d, LANES), lambda i: (0, 0)),
            pl.BlockSpec((d, LANES), lambda i: (0, 0)),
            pl.BlockSpec((1, LANES), lambda i: (0, 0)),
        ],
        out_specs=(pl.BlockSpec((tm, d), lambda i: (i, 0)), pl.BlockSpec((tm, LANES), lambda i: (i, 0)),
                   pl.BlockSpec((tm, LANES), lambda i: (i, 0))),
        compiler_params=_params("arbitrary"),
        name="norm2_router",
    )(x, g.reshape(1, d), mod_p, mod_p, mod_s, mod_s, wr_hi, wr_lo, br)


def _dispatch(ids, wts, n_experts, te, n_tiles_max):
    t = ids.shape[0]
    e_flat = ids.reshape(-1)
    onehot = (e_flat[:, None] == jnp.arange(n_experts, dtype=jnp.int32)[None, :]).astype(jnp.int32)
    csum = jnp.cumsum(onehot, axis=0)
    rank = jnp.take_along_axis(csum, e_flat[:, None], axis=1)[:, 0] - 1
    counts = csum[-1]
    padded = ((counts + te - 1) // te) * te
    ends = jnp.cumsum(padded)
    offs = ends - padded
    slot = offs[e_flat] + rank
    n_slots = n_tiles_max * te
    tok_of_slot = jnp.zeros((n_slots,), jnp.int32).at[slot].set(jnp.arange(TOP_K * t, dtype=jnp.int32) // TOP_K)
    w_of_slot = jnp.zeros((n_slots,), F32).at[slot].set(wts.reshape(-1))
    n_used = (ends[-1] // te).astype(jnp.int32)
    tile_start = jnp.arange(n_tiles_max, dtype=jnp.int32) * te
    tile_expert = jnp.minimum(jnp.searchsorted(ends, tile_start, side="right"), n_experts - 1).astype(jnp.int32)
    last = tile_expert[jnp.maximum(n_used - 1, 0)]
    tile_expert = jnp.where(jnp.arange(n_tiles_max) < n_used, tile_expert, last)
    return tile_expert, tok_of_slot, w_of_slot, n_used.reshape(1), slot.reshape(t, TOP_K)


def _gather_rows(src_hbm, idx_ref, base, n_rows, dst, sem):
    def body(r, carry):
        pltpu.make_async_copy(src_hbm.at[pl.ds(idx_ref[base + r], 1), :], dst.at[pl.ds(r, 1), :], sem).start()
        return carry

    lax.fori_loop(0, n_rows, body, 0)


def _wait_rows(src_hbm, n_rows, dst, sem):
    pltpu.make_async_copy(src_hbm.at[pl.ds(0, n_rows), :], dst, sem).wait()


def _expert_kernel(te_ref, tok_ref, nu_ref, h_hbm, wg_ref, wu_ref, wd_ref, ws_ref, o_ref, xbuf, sem, wgb, wub, wdb):
    i = pl.program_id(0)
    n_used = nu_ref[0]
    rows = xbuf.shape[1]
    slot = lax.rem(i, 2)

    @pl.when(i == 0)
    def _():
        _gather_rows(h_hbm, tok_ref, 0, rows, xbuf.at[0], sem.at[0])

    @pl.when(i + 1 < n_used)
    def _():
        _gather_rows(h_hbm, tok_ref, (i + 1) * rows, rows, xbuf.at[1 - slot], sem.at[1 - slot])

    @pl.when(i < n_used)
    def _():
        changed = jnp.logical_or(i == 0, te_ref[i] != te_ref[jnp.maximum(i - 1, 0)])

        @pl.when(changed)
        def _():
            wgb[...] = wg_ref[...].astype(BF16)
            wub[...] = wu_ref[...].astype(BF16)
            wdb[...] = wd_ref[...].astype(BF16)

        _wait_rows(h_hbm, rows, xbuf.at[slot], sem.at[slot])
        x = xbuf[slot].astype(BF16)
        a = jnp.dot(x, wgb[...], preferred_element_type=F32)
        u = jnp.dot(x, wub[...], preferred_element_type=F32)
        hid = _silu(a) * u * ws_ref[...]
        o_ref[...] = jnp.dot(hid.astype(BF16), wdb[...], preferred_element_type=F32)

    @pl.when(i >= n_used)
    def _():
        o_ref[...] = jnp.zeros(o_ref.shape, F32)


def _experts(h2, tile_expert, tok_of_slot, w_of_slot, n_used, wg, wu, wd, layer_base):
    t, d = h2.shape
    f = wg.shape[-1]
    te = EXPERT_TILE
    n_tiles = tile_expert.shape[0]

    def w_map(i, te_ref, tok_ref, nu_ref):
        return (layer_base + te_ref[i], 0, 0)

    def row_map(i, te_ref, tok_ref, nu_ref):
        return (jnp.minimum(i, nu_ref[0] - 1), 0)

    return pl.pallas_call(
        _expert_kernel,
        out_shape=jax.ShapeDtypeStruct((n_tiles * te, d), F32),
        grid_spec=pltpu.PrefetchScalarGridSpec(
            num_scalar_prefetch=3,
            grid=(n_tiles,),
            in_specs=[
                pl.BlockSpec(memory_space=pl.ANY),
                pl.BlockSpec((None, d, f), w_map),
                pl.BlockSpec((None, d, f), w_map),
                pl.BlockSpec((None, f, d), w_map),
                pl.BlockSpec((te, 1), row_map),
            ],
            out_specs=pl.BlockSpec((te, d), lambda i, te_ref, tok_ref, nu_ref: (i, 0)),
            scratch_shapes=[
                pltpu.VMEM((2, te, d), F32),
                pltpu.SemaphoreType.DMA((2,)),
                pltpu.VMEM((d, f), BF16),
                pltpu.VMEM((d, f), BF16),
                pltpu.VMEM((f, d), BF16),
            ],
        ),
        compiler_params=_params("arbitrary"),
        name="experts",
    )(tile_expert, tok_of_slot, n_used, h2, wg, wu, wd, w_of_slot.reshape(-1, 1))


def _combine_kernel(p0_ref, p1_ref, y_hbm, x_ref, gp, gs, o_ref, buf, sem, *, npt, dec_seq):
    i = pl.program_id(0)
    n = pl.num_programs(0)
    tm = x_ref.shape[0]
    slot = lax.rem(i, 2)

    def start(tile, sl):
        _gather_rows(y_hbm, p0_ref, tile * tm, tm, buf.at[sl, 0], sem.at[sl])
        _gather_rows(y_hbm, p1_ref, tile * tm, tm, buf.at[sl, 1], sem.at[sl])

    @pl.when(i == 0)
    def _():
        start(0, 0)

    @pl.when(i + 1 < n)
    def _():
        start(i + 1, 1 - slot)

    _wait_rows(y_hbm, tm, buf.at[slot, 0], sem.at[slot])
    _wait_rows(y_hbm, tm, buf.at[slot, 1], sem.at[slot])
    ff = buf[slot, 0] + buf[slot, 1]
    _store_by_group(i, npt, dec_seq, o_ref, lambda v, m: v[0] + m[0] * v[1], [x_ref[...], ff], [gp], [gs])


def _combine(y_sorted, pos, x, mod_p, mod_s, dims):
    t, d = x.shape
    tm = ROW_TILE
    p_specs, s_specs = _mod_specs((5,), d, dims["npt"], dims["tps"], dims["n_seq"], dims["dec_batch"])
    def strip(spec):
        return pl.BlockSpec(spec.block_shape, lambda i, a, b, m=spec.index_map: m(i))

    return pl.pallas_call(
        functools.partial(_combine_kernel, npt=dims["npt"], dec_seq=dims["dec_seq"]),
        out_shape=jax.ShapeDtypeStruct((t, d), F32),
        grid_spec=pltpu.PrefetchScalarGridSpec(
            num_scalar_prefetch=2,
            grid=(t // tm,),
            in_specs=[
                pl.BlockSpec(memory_space=pl.ANY),
                pl.BlockSpec((tm, d), lambda i, a, b: (i, 0)),
                *[strip(s) for s in p_specs],
                *[strip(s) for s in s_specs],
            ],
            out_specs=pl.BlockSpec((tm, d), lambda i, a, b: (i, 0)),
            scratch_shapes=[pltpu.VMEM((2, TOP_K, tm, d), F32), pltpu.SemaphoreType.DMA((2,))],
        ),
        compiler_params=_params("arbitrary"),
        name="combine",
    )(pos[:, 0], pos[:, 1], y_sorted, x, mod_p, mod_s)


def _final_norm_kernel(x_ref, g_ref, o_ref):
    o_ref[...] = _rms(x_ref[...], g_ref[...])


def _final_norm(x, g):
    t, d = x.shape
    tm = ROW_TILE
    return pl.pallas_call(
        _final_norm_kernel,
        out_shape=jax.ShapeDtypeStruct((t, d), F32),
        grid=(t // tm,),
        in_specs=[pl.BlockSpec((tm, d), lambda i: (i, 0)), pl.BlockSpec((1, d), lambda i: (0, 0))],
        out_specs=pl.BlockSpec((tm, d), lambda i: (i, 0)),
        compiler_params=_params("arbitrary"),
        name="final_norm",
    )(x, g.reshape(1, d))


def kernel(x_prompt, x_sample, c_prompt, c_sample, state_conv, state_gla, w_ada, b_ada, norm1_g, norm2_g, w_in, conv_w, conv_b, conv_ln_g, conv_ln_b, gate_w2, gate_b, gla_norm_g, w_out, router_grp_w, router_grp_b, router_exp_w, router_exp_b, exp_w_gate, exp_w_up, exp_w_down, final_norm_g):
    n_seq, seq_len, d = x_prompt.shape
    dec_batch, dec_seq, _ = x_sample.shape
    depth = w_ada.shape[0]
    kw, d_conv = conv_w.shape[1:]
    heads, dv = gla_norm_g.shape[1:]
    rank, qkw = gate_w2.shape[1:]
    dk = qkw // heads
    n_grp, n_exp = router_exp_w.shape[2:]
    n_experts = n_grp * n_exp
    tp, ts = n_seq * seq_len, dec_batch * dec_seq
    t = tp + ts
    tm = ROW_TILE
    assert ts == tm and seq_len % tm == 0 and d_conv == heads * dv and kw - 1 <= CONV_HALO
    n_main = 2 * d_conv + 2 * qkw + 2 * heads * dv
    dims = dict(npt=tp // tm, tps=seq_len // tm, n_seq=n_seq, dec_batch=dec_batch, dec_seq=dec_seq)

    x = jnp.concatenate([x_prompt.reshape(tp, d), x_sample.transpose(1, 0, 2).reshape(ts, d)], axis=0)
    pad = (-n_seq) % 8
    c_all = jnp.concatenate([c_prompt, jnp.zeros((pad, d), F32), c_sample], axis=0)
    mod = _ada(c_all, w_ada, b_ada)
    n_tiles_max = (TOP_K * t) // EXPERT_TILE + n_experts

    conv_p, gla_p, conv_s, gla_s = [], [], [], []
    for l in range(depth):
        mod_p = mod[l, :n_seq].reshape(n_seq, 1, 6 * d)
        mod_s = mod[l, n_seq + pad:]
        w_gl = jnp.pad(w_in[l, :, n_main:], ((0, 0), (0, LANES - rank))).astype(BF16)
        gw2 = jnp.pad(gate_w2[l], ((0, LANES - rank), (0, 0))).astype(BF16)
        h, la = _norm1(x, norm1_g[l], mod_p, mod_s, w_gl, gw2, gate_b[l], dims)
        z = _inproj(h, w_in, l, n_main)

        cv_p, cb_p = _conv_prompt(z, conv_w[l], conv_b[l], conv_ln_g[l], conv_ln_b[l], n_seq, seq_len)
        cv_s, cb_s = _conv_sample(z, state_conv[l].transpose(1, 0, 2), conv_w[l], conv_b[l], conv_ln_g[l],
                                  conv_ln_b[l], tp, dec_seq, dec_batch)
        conv_out = jnp.concatenate([cv_p, cv_s.reshape(ts, d_conv)], axis=0)
        conv_p.append(cb_p)
        conv_s.append(cb_s.transpose(1, 0, 2))

        go_p, gs_p = _gla_prompt(z, la, gla_norm_g[l], n_seq, seq_len, heads, dk, dv)

        def seq_major(a):
            return a.reshape(dec_seq, dec_batch, a.shape[-1]).transpose(1, 0, 2)

        zs = z[tp:]
        q3 = seq_major(zs[:, 2 * d_conv:2 * d_conv + qkw])
        k3 = seq_major(zs[:, 2 * d_conv + qkw:2 * d_conv + 2 * qkw])
        v3 = seq_major(zs[:, 2 * d_conv + 2 * qkw:2 * d_conv + 2 * qkw + heads * dv])
        g3 = seq_major(zs[:, 2 * d_conv + 2 * qkw + heads * dv:n_main])
        go_s, gs_s = _gla_sample(q3, k3, v3, g3, seq_major(la[tp:]), state_gla, l, gla_norm_g[l], heads, dk, dv)
        gla_out = jnp.concatenate([go_p, go_s.transpose(1, 0, 2).reshape(ts, heads * dv).astype(BF16)], axis=0)
        gla_p.append(gs_p)
        gla_s.append(gs_s)

        x = _outproj(conv_out, gla_out, x, w_out, l, mod_p, mod_s, dims)

        wr = jnp.concatenate([router_grp_w[l], router_exp_w[l].reshape(d, n_experts)], axis=1)
        wr = jnp.pad(wr, ((0, 0), (0, LANES - wr.shape[1])))
        wr_hi = wr.astype(BF16)
        wr_lo = (wr - wr_hi.astype(F32)).astype(BF16)
        br = jnp.concatenate([router_grp_b[l], router_exp_b[l].reshape(-1)])
        br = jnp.pad(br, (0, LANES - br.shape[0])).reshape(1, LANES)
        h2, ids, wts = _norm2(x, norm2_g[l], mod_p, mod_s, wr_hi, wr_lo, br, dims, n_grp, n_exp)
        tile_expert, tok_of_slot, w_of_slot, n_used, pos = _dispatch(ids[:, :TOP_K], wts[:, :TOP_K], n_experts,
                                                                     EXPERT_TILE, n_tiles_max)
        f = exp_w_gate.shape[-1]
        y_sorted = _experts(h2, tile_expert, tok_of_slot, w_of_slot, n_used,
                            exp_w_gate.reshape(depth * n_experts, d, f), exp_w_up.reshape(depth * n_experts, d, f),
                            exp_w_down.reshape(depth * n_experts, f, d), l * n_experts)
        x = _combine(y_sorted, pos, x, mod_p, mod_s, dims)

    y = _final_norm(x, final_norm_g)
    y_prompt = y[:tp].reshape(n_seq, seq_len, d)
    y_sample = y[tp:].reshape(dec_seq, dec_batch, d).transpose(1, 0, 2)
    return (y_prompt, y_sample, jnp.stack(conv_p), jnp.stack(gla_p), jnp.stack(conv_s), jnp.stack(gla_s))
```

```python
import functools

import jax
import jax.numpy as jnp
from jax import lax
from jax.experimental import pallas as pl
from jax.experimental.pallas import tpu as pltpu

F32 = jnp.float32
BF16 = jnp.bfloat16

EPS = 1e-6
GATE_TAU = 16.0
GLA_CHUNK = 32
TOP_K = 2

ROW_TILE = 512
EXPERT_TILE = 256
CONV_ROWS = 64
CONV_HALO = 32
LANES = 128
SUBLANES = 8
VMEM_LIMIT = 56 * 1024 * 1024


def _params(*sem):
    return pltpu.CompilerParams(dimension_semantics=sem, vmem_limit_bytes=VMEM_LIMIT)


def _bdot(a, b):
    return jnp.dot(a.astype(BF16), b.astype(BF16), preferred_element_type=F32)


def _round_bf16(x):
    return x.astype(BF16).astype(F32)


def _split3(x):
    hi = x.astype(BF16)
    r = x - hi.astype(F32)
    mid = r.astype(BF16)
    lo = (r - mid.astype(F32)).astype(BF16)
    return hi, mid, lo


def _silu(x):
    return x * jax.nn.sigmoid(x)


def _store_by_group(i, n_prompt_tiles, dec_seq, out_ref, fn, vals, p_refs, s_refs):
    @pl.when(i < n_prompt_tiles)
    def _():
        out_ref[...] = fn(vals, [r[...] for r in p_refs]).astype(out_ref.dtype)

    @pl.when(i >= n_prompt_tiles)
    def _():
        mods = [r[...] for r in s_refs]
        nb = mods[0].shape[0]
        for t in range(dec_seq):
            rows = slice(t * nb, (t + 1) * nb)
            out_ref[rows, :] = fn([v[rows] for v in vals], mods).astype(out_ref.dtype)


def _ada_kernel(c_ref, w_ref, b_ref, o_ref):
    c = c_ref[...]
    o_ref[...] = _bdot(_silu(c), w_ref[...]) + b_ref[...]


def _ada(c_all, w_ada, b_ada):
    depth, d, n = w_ada.shape
    rows = c_all.shape[0]
    tn = 1024
    return pl.pallas_call(
        _ada_kernel,
        out_shape=jax.ShapeDtypeStruct((depth, rows, n), F32),
        grid=(depth, n // tn),
        in_specs=[
            pl.BlockSpec((rows, d), lambda l, j: (0, 0)),
            pl.BlockSpec((None, d, tn), lambda l, j: (l, 0, j)),
            pl.BlockSpec((None, 1, tn), lambda l, j: (l, 0, j)),
        ],
        out_specs=pl.BlockSpec((None, rows, tn), lambda l, j: (l, 0, j)),
        compiler_params=_params("arbitrary", "arbitrary"),
        name="ada",
    )(c_all, w_ada, b_ada.reshape(depth, 1, n))


def _mod_specs(cols, width, n_prompt_tiles, tiles_per_seq, n_seq, dec_batch, grid_rank=1, row_axis=0, col_fn=None):
    p_specs, s_specs = [], []
    for c in cols:
        def p_map(*idx, c=c):
            b = jnp.minimum(idx[row_axis] // tiles_per_seq, n_seq - 1)
            return (b, 0, c if col_fn is None else col_fn(c, idx))

        def s_map(*idx, c=c):
            return (0, c if col_fn is None else col_fn(c, idx))

        p_specs.append(pl.BlockSpec((None, 1, width), p_map))
        s_specs.append(pl.BlockSpec((dec_batch, width), s_map))
    return p_specs, s_specs


def _rms(x, g):
    return x * lax.rsqrt(jnp.mean(x * x, axis=-1, keepdims=True) + EPS) * g


def _norm1_kernel(x_ref, g_ref, shp, scp, shs, scs, wgl_ref, gw2_ref, gb_ref, h_ref, la_ref, *, npt, dec_seq, rank):
    i = pl.program_id(0)
    y = _rms(x_ref[...], g_ref[...])
    _store_by_group(i, npt, dec_seq, h_ref, lambda v, m: v[0] * (1.0 + m[1]) + m[0], [y], [shp, scp], [shs, scs])
    lane = lax.broadcasted_iota(jnp.int32, wgl_ref.shape, 1)
    w_gl = jnp.where(lane < rank, wgl_ref[...], 0.0).astype(BF16)
    gate_lr = jnp.dot(h_ref[...], w_gl, preferred_element_type=F32)
    pre = _bdot(gate_lr, gw2_ref[...]) + gb_ref[...]
    la_ref[...] = (jnp.minimum(pre, 0.0) - jnp.log1p(jnp.exp(-jnp.abs(pre)))) * (1.0 / GATE_TAU)


def _norm1(x, g, mod_p, mod_s, w_in, layer, n_main, rank, gw2, gb, dims):
    t, d = x.shape
    tm = ROW_TILE
    p_specs, s_specs = _mod_specs((0, 1), d, dims["npt"], dims["tps"], dims["n_seq"], dims["dec_batch"])
    qk = gw2.shape[1]
    assert n_main % LANES == 0 and rank <= LANES
    return pl.pallas_call(
        functools.partial(_norm1_kernel, npt=dims["npt"], dec_seq=dims["dec_seq"], rank=rank),
        out_shape=(jax.ShapeDtypeStruct((t, d), BF16), jax.ShapeDtypeStruct((t, qk), F32)),
        grid=(t // tm,),
        in_specs=[
            pl.BlockSpec((tm, d), lambda i: (i, 0)),
            pl.BlockSpec((1, d), lambda i: (0, 0)),
            *p_specs,
            *s_specs,
            pl.BlockSpec((None, d, LANES), lambda i: (layer, 0, n_main // LANES)),
            pl.BlockSpec(gw2.shape, lambda i: (0, 0)),
            pl.BlockSpec((1, qk), lambda i: (0, 0)),
        ],
        out_specs=(pl.BlockSpec((tm, d), lambda i: (i, 0)), pl.BlockSpec((tm, qk), lambda i: (i, 0))),
        compiler_params=_params("arbitrary"),
        name="norm1",
    )(x, g.reshape(1, d), mod_p, mod_p, mod_s, mod_s, w_in, gw2, gb.reshape(1, qk))


def _inproj_kernel(h_ref, w_ref, o_ref, wb_ref):
    @pl.when(pl.program_id(1) == 0)
    def _():
        wb_ref[...] = w_ref[...].astype(BF16)

    o_ref[...] = jnp.dot(h_ref[...], wb_ref[...], preferred_element_type=F32)


def _inproj(h, w_in, layer, n_cols):
    t, d = h.shape
    tm, tn = ROW_TILE, 1024
    return pl.pallas_call(
        _inproj_kernel,
        out_shape=jax.ShapeDtypeStruct((t, n_cols), F32),
        grid=(n_cols // tn, t // tm),
        in_specs=[
            pl.BlockSpec((tm, d), lambda j, i: (i, 0)),
            pl.BlockSpec((None, d, tn), lambda j, i: (layer, 0, j)),
        ],
        out_specs=pl.BlockSpec((tm, tn), lambda j, i: (i, j)),
        scratch_shapes=[pltpu.VMEM((d, tn), BF16)],
        compiler_params=_params("arbitrary", "arbitrary"),
        name="inproj",
    )(h, w_in)


def _ln_silu(y, g, b):
    mu = jnp.mean(y, axis=-1, keepdims=True)
    yc = y - mu
    var = jnp.mean(yc * yc, axis=-1, keepdims=True)
    return _silu(yc * lax.rsqrt(var + EPS) * g + b)


def _conv_prompt_kernel(a_ref, b_ref, cw_ref, cb_ref, lng_ref, lnb_ref, o_ref, st_ref, full_ref, cwb_ref, y_ref, *, kw):
    j = pl.program_id(1)
    tm, c = a_ref.shape
    halo = CONV_HALO
    phases = full_ref.shape[0]
    assert phases == SUBLANES

    @pl.when(jnp.logical_and(pl.program_id(0) == 0, j == 0))
    def _():
        full_ref[...] = jnp.zeros(full_ref.shape, F32)
        for w in range(kw):
            cwb_ref[w] = jnp.broadcast_to(cw_ref[w:w + 1, :], (SUBLANES, c))

    prev = full_ref[0, tm + halo - SUBLANES:tm + halo, :]
    tail = jnp.where(j == 0, 0.0, prev)

    @pl.when(j == 0)
    def _():
        for p in range(phases):
            full_ref[p, 0:halo, :] = jnp.zeros((halo, c), F32)

    @pl.when(j > 0)
    def _():
        for p in range(phases):
            full_ref[p, 0:halo, :] = full_ref[p, tm:tm + halo, :]

    u = a_ref[...] * jax.nn.sigmoid(b_ref[...])
    full_ref[0, halo:halo + tm, :] = u
    ext = jnp.concatenate([tail, u], axis=0)
    for p in range(1, phases):
        full_ref[p, halo - SUBLANES:halo - SUBLANES + tm, :] = pltpu.roll(ext, tm + SUBLANES - p, 0)[0:tm]
    off = halo - (kw - 1)
    rb = CONV_ROWS

    def body(r, carry):
        r0 = pl.multiple_of(r * rb, rb)
        for lt in range(c // LANES):
            cols = slice(lt * LANES, (lt + 1) * LANES)
            acc = None
            for p in range(phases):
                x = full_ref[p, pl.ds(r0, rb + halo), cols]
                for a in range(halo // phases + 1):
                    w = a * phases + p - off
                    if 0 <= w < kw and a * phases + rb <= rb + halo:
                        term = x[a * phases:a * phases + rb] * jnp.concatenate([cwb_ref[w, :, cols]] * (rb // SUBLANES), axis=0)
                        acc = term if acc is None else acc + term
            y_ref[pl.ds(r0, rb), cols] = acc
        return carry

    lax.fori_loop(0, tm // rb, body, 0)
    y = _ln_silu(y_ref[...] + cb_ref[...], lng_ref[...], lnb_ref[...])
    o_ref[...] = y.astype(o_ref.dtype)

    @pl.when(j == pl.num_programs(1) - 1)
    def _():
        st_ref[...] = full_ref[0, halo + tm - (kw - 1):halo + tm, :]


def _conv_prompt(z, cw, cb, lng, lnb, n_seq, seq_len):
    kw, c = cw.shape
    tm = ROW_TILE
    tps = seq_len // tm
    return pl.pallas_call(
        functools.partial(_conv_prompt_kernel, kw=kw),
        out_shape=(jax.ShapeDtypeStruct((n_seq * seq_len, c), BF16), jax.ShapeDtypeStruct((n_seq, kw - 1, c), F32)),
        grid=(n_seq, tps),
        in_specs=[
            pl.BlockSpec((tm, c), lambda b, j: (b * tps + j, 0)),
            pl.BlockSpec((tm, c), lambda b, j: (b * tps + j, 1)),
            pl.BlockSpec((kw, c), lambda b, j: (0, 0)),
            pl.BlockSpec((1, c), lambda b, j: (0, 0)),
            pl.BlockSpec((1, c), lambda b, j: (0, 0)),
            pl.BlockSpec((1, c), lambda b, j: (0, 0)),
        ],
        out_specs=(
            pl.BlockSpec((tm, c), lambda b, j: (b * tps + j, 0)),
            pl.BlockSpec((None, kw - 1, c), lambda b, j: (b, 0, 0)),
        ),
        scratch_shapes=[pltpu.VMEM((SUBLANES, tm + CONV_HALO, c), F32), pltpu.VMEM((kw, SUBLANES, c), F32),
                        pltpu.VMEM((tm, c), F32)],
        compiler_params=_params("arbitrary", "arbitrary"),
        name="conv_prompt",
    )(z, z, cw, cb.reshape(1, c), lng.reshape(1, c), lnb.reshape(1, c))


def _conv_sample_kernel(*refs, kw, dec_seq):
    a_refs = refs[0:dec_seq]
    b_refs = refs[dec_seq:2 * dec_seq]
    st_ref, cw_ref, cb_ref, lng_ref, lnb_ref, o_ref, ns_ref = refs[2 * dec_seq:]
    hist = kw - 1
    u = [a_refs[t][...] * jax.nn.sigmoid(b_refs[t][...]) for t in range(dec_seq)]
    def row(j):
        return st_ref[j] if j < hist else u[j - hist]

    for t in range(dec_seq):
        acc = row(t) * cw_ref[0:1, :]
        for w in range(1, kw):
            acc = acc + row(t + w) * cw_ref[w:w + 1, :]
        y = _ln_silu(acc + cb_ref[...], lng_ref[...], lnb_ref[...])
        o_ref[t] = y.astype(o_ref.dtype)
    for j in range(hist):
        ns_ref[j] = row(j + dec_seq)


def _conv_sample(z, st_t, cw, cb, lng, lnb, row0, dec_seq, dec_batch):
    kw, c = cw.shape
    bs = 16
    a_specs = [pl.BlockSpec((bs, c), lambda s, t=t: ((row0 + t * dec_batch) // bs + s, 0)) for t in range(dec_seq)]
    b_specs = [pl.BlockSpec((bs, c), lambda s, t=t: ((row0 + t * dec_batch) // bs + s, 1)) for t in range(dec_seq)]
    vec = pl.BlockSpec((1, c), lambda s: (0, 0))
    return pl.pallas_call(
        functools.partial(_conv_sample_kernel, kw=kw, dec_seq=dec_seq),
        out_shape=(jax.ShapeDtypeStruct((dec_seq, dec_batch, c), BF16), jax.ShapeDtypeStruct((kw - 1, dec_batch, c), F32)),
        grid=(dec_batch // bs,),
        in_specs=[*a_specs, *b_specs, pl.BlockSpec((kw - 1, bs, c), lambda s: (0, s, 0)),
                  pl.BlockSpec((kw, c), lambda s: (0, 0)), vec, vec, vec],
        out_specs=(pl.BlockSpec((dec_seq, bs, c), lambda s: (0, s, 0)), pl.BlockSpec((kw - 1, bs, c), lambda s: (0, s, 0))),
        compiler_params=_params("arbitrary"),
        name="conv_sample",
    )(*([z] * (2 * dec_seq)), st_t, cw, cb.reshape(1, c), lng.reshape(1, c), lnb.reshape(1, c))


def _gla_prompt_kernel(qk_ref, v_ref, g_ref, la_ref, gn_ref, o_ref, sfin_ref, st_ref, sn_ref, *, heads, dk, dv):
    j = pl.program_id(1)
    tm = qk_ref.shape[0]
    ck = GLA_CHUNK
    nch = tm // ck
    qkw = heads * dk

    @pl.when(j == 0)
    def _():
        st_ref[...] = jnp.zeros(st_ref.shape, F32)

    la = la_ref[...]
    row = lax.broadcasted_iota(jnp.int32, (tm, tm), 0)
    col = lax.broadcasted_iota(jnp.int32, (tm, tm), 1)
    same = (row // ck) == (col // ck)
    causal = same & (col <= row)
    tri_incl = causal.astype(BF16)
    tri_after = (same & (col > row)).astype(BF16)
    sel = (lax.broadcasted_iota(jnp.int32, (nch, tm), 1) // ck == lax.broadcasted_iota(jnp.int32, (nch, tm), 0)).astype(BF16)
    parts = _split3(la)
    b = sum(jnp.dot(tri_incl, p, preferred_element_type=F32) for p in parts)
    rest = sum(jnp.dot(tri_after, p, preferred_element_type=F32) for p in parts)
    tot = sum(jnp.dot(sel, p, preferred_element_type=F32) for p in parts)
    qk = qk_ref[...]
    q = qk[:, :qkw] * (dk ** -0.5)
    k = qk[:, qkw:]
    q_dec = (q * jnp.exp(b)).astype(BF16)
    k_inv = (k * jnp.exp(-b)).astype(BF16)
    k_end = _round_bf16(k * jnp.exp(rest))
    decay = jnp.exp(tot)
    v_all = v_ref[...]
    g_all = g_ref[...]
    for h in range(heads):
        ks = slice(h * dk, (h + 1) * dk)
        vs = slice(h * dv, (h + 1) * dv)
        qd, ki, ke = q_dec[:, ks], k_inv[:, ks], k_end[:, ks]
        vh = v_all[:, vs]
        vb = vh.astype(BF16)
        vr = _round_bf16(vh)
        att = lax.dot_general(qd, ki, (((1,), (1,)), ((), ())), preferred_element_type=F32)
        att = jnp.where(causal, att, 0.0).astype(BF16)
        o = jnp.dot(att, vb, preferred_element_type=F32)
        s = st_ref[h]
        for n in range(nch):
            rs = slice(n * ck, (n + 1) * ck)
            sn_ref[n] = s.astype(BF16)
            upd = lax.dot_general(vr[rs], ke[rs], (((0,), (0,)), ((), ())), preferred_element_type=F32)
            s = s * decay[n:n + 1, ks] + upd
        st_ref[h] = s
        inter = [lax.dot_general(qd[n * ck:(n + 1) * ck], sn_ref[n], (((1,), (1,)), ((), ())), preferred_element_type=F32)
                 for n in range(nch)]
        o = o + jnp.concatenate(inter, axis=0)
        o = _rms(o, gn_ref[:, vs]) * _silu(g_all[:, vs])
        o_ref[:, vs] = o.astype(o_ref.dtype)

    @pl.when(j == pl.num_programs(1) - 1)
    def _():
        for h in range(heads):
            sfin_ref[h] = st_ref[h].T


def _gla_prompt(z, la, gn, n_seq, seq_len, heads, dk, dv):
    tm = ROW_TILE
    tps = seq_len // tm
    w = heads * dv
    qkw = heads * dk
    assert 2 * qkw == w
    return pl.pallas_call(
        functools.partial(_gla_prompt_kernel, heads=heads, dk=dk, dv=dv),
        out_shape=(jax.ShapeDtypeStruct((n_seq * seq_len, w), BF16), jax.ShapeDtypeStruct((n_seq, heads, dk, dv), F32)),
        grid=(n_seq, tps),
        in_specs=[
            pl.BlockSpec((tm, w), lambda b, j: (b * tps + j, 2)),
            pl.BlockSpec((tm, w), lambda b, j: (b * tps + j, 3)),
            pl.BlockSpec((tm, w), lambda b, j: (b * tps + j, 4)),
            pl.BlockSpec((tm, qkw), lambda b, j: (b * tps + j, 0)),
            pl.BlockSpec((1, w), lambda b, j: (0, 0)),
        ],
        out_specs=(
            pl.BlockSpec((tm, w), lambda b, j: (b * tps + j, 0)),
            pl.BlockSpec((None, heads, dk, dv), lambda b, j: (b, 0, 0, 0)),
        ),
        scratch_shapes=[pltpu.VMEM((heads, dv, dk), F32), pltpu.VMEM((tm // GLA_CHUNK, dv, dk), BF16)],
        compiler_params=_params("arbitrary", "arbitrary"),
        name="gla_prompt",
    )(z, z, z, la, gn.reshape(1, w))


def _gla_sample_kernel(q_ref, k_ref, v_ref, g_ref, la_ref, s_ref, gn_ref, o_ref, ns_ref, *, heads, dk, dv):
    bs, ln, _ = q_ref.shape
    tril = lax.broadcasted_iota(jnp.int32, (ln, ln), 1) <= lax.broadcasted_iota(jnp.int32, (ln, ln), 0)
    for s in range(bs):
        q_s, k_s, v_s, g_s, la_s = q_ref[s], k_ref[s], v_ref[s], g_ref[s], la_ref[s]
        for h in range(heads):
            ks = slice(h * dk, (h + 1) * dk)
            vs = slice(h * dv, (h + 1) * dv)
            la = la_s[:, ks]
            rows = [la[0:1]]
            for t in range(1, ln):
                rows.append(rows[-1] + la[t:t + 1])
            b = jnp.concatenate(rows, axis=0)
            b_last = rows[-1]
            q_dec = _round_bf16(q_s[:, ks] * (dk ** -0.5) * jnp.exp(b))
            k_inv = _round_bf16(k_s[:, ks] * jnp.exp(-b))
            k_end = _round_bf16(k_s[:, ks] * jnp.exp(b_last - b))
            vr = _round_bf16(v_s[:, vs])
            s0 = s_ref[s, h]
            att = lax.dot_general(q_dec, k_inv, (((1,), (1,)), ((), ())), preferred_element_type=F32)
            att = _round_bf16(jnp.where(tril, att, 0.0))
            o = jnp.dot(att, vr, preferred_element_type=F32) + jnp.dot(q_dec, _round_bf16(s0), preferred_element_type=F32)
            upd = lax.dot_general(k_end, vr, (((0,), (0,)), ((), ())), preferred_element_type=F32)
            d_col = jnp.broadcast_to(jnp.exp(b_last), (dk, dk)).T
            ns_ref[s, h] = s0 * jnp.concatenate([d_col] * (dv // dk), axis=1) + upd
            o = _rms(o, gn_ref[:, vs]) * _silu(g_s[:, vs])
            o_ref[s, :, vs] = o.astype(o_ref.dtype)


def _gla_sample(q3, k3, v3, g3, la3, state, layer, gn, heads, dk, dv):
    nb, ln, w = v3.shape
    qkw = heads * dk
    bs = 8
    return pl.pallas_call(
        functools.partial(_gla_sample_kernel, heads=heads, dk=dk, dv=dv),
        out_shape=(jax.ShapeDtypeStruct((nb, ln, w), F32), jax.ShapeDtypeStruct((nb, heads, dk, dv), F32)),
        grid=(nb // bs,),
        in_specs=[
            pl.BlockSpec((bs, ln, qkw), lambda s: (s, 0, 0)),
            pl.BlockSpec((bs, ln, qkw), lambda s: (s, 0, 0)),
            pl.BlockSpec((bs, ln, w), lambda s: (s, 0, 0)),
            pl.BlockSpec((bs, ln, w), lambda s: (s, 0, 0)),
            pl.BlockSpec((bs, ln, qkw), lambda s: (s, 0, 0)),
            pl.BlockSpec((None, bs, heads, dk, dv), lambda s: (layer, s, 0, 0, 0)),
            pl.BlockSpec((1, w), lambda s: (0, 0)),
        ],
        out_specs=(
            pl.BlockSpec((bs, ln, w), lambda s: (s, 0, 0)),
            pl.BlockSpec((bs, heads, dk, dv), lambda s: (s, 0, 0, 0)),
        ),
        compiler_params=_params("arbitrary"),
        name="gla_sample",
    )(q3, k3, v3, g3, la3, state, gn.reshape(1, w))


def _outproj_kernel(cp_ref, op_ref, cs_ref, os_ref, x_ref, w_ref, gp, gs, y_ref, wb_ref, *, npt, dec_seq):
    i = pl.program_id(1)

    @pl.when(i == 0)
    def _():
        wb_ref[...] = w_ref[...].astype(BF16)

    half = cp_ref.shape[1]

    def mixed(c_ref, o_ref):
        return (jnp.dot(c_ref[...], wb_ref[0:half, :], preferred_element_type=F32)
                + jnp.dot(o_ref[...], wb_ref[half:, :], preferred_element_type=F32))

    @pl.when(i < npt)
    def _():
        y_ref[...] = x_ref[...] + gp[...] * mixed(cp_ref, op_ref)

    @pl.when(i >= npt)
    def _():
        mix = mixed(cs_ref, os_ref)
        nb = gs.shape[0]
        for t in range(dec_seq):
            rows = slice(t * nb, (t + 1) * nb)
            y_ref[rows, :] = x_ref[rows, :] + gs[...] * mix[rows]


def _outproj(conv_p, gla_p, conv_s, gla_s, x, w_out, layer, mod_p, mod_s, dims):
    t, d = x.shape
    half = conv_p.shape[1]
    tm, tn = ROW_TILE, 1024
    nj = d // tn
    npt = dims["npt"]
    p_specs, s_specs = _mod_specs((2,), tn, dims["npt"], dims["tps"], dims["n_seq"], dims["dec_batch"],
                                  row_axis=1, col_fn=lambda c, idx: c * nj + idx[0])
    return pl.pallas_call(
        functools.partial(_outproj_kernel, npt=dims["npt"], dec_seq=dims["dec_seq"]),
        out_shape=jax.ShapeDtypeStruct((t, d), F32),
        grid=(nj, t // tm),
        in_specs=[
            pl.BlockSpec((tm, half), lambda j, i: (jnp.minimum(i, npt - 1), 0)),
            pl.BlockSpec((tm, half), lambda j, i: (jnp.minimum(i, npt - 1), 0)),
            pl.BlockSpec((tm, half), lambda j, i: (0, 0)),
            pl.BlockSpec((tm, half), lambda j, i: (0, 0)),
            pl.BlockSpec((tm, tn), lambda j, i: (i, j)),
            pl.BlockSpec((None, d, tn), lambda j, i: (layer, 0, j)),
            *p_specs,
            *s_specs,
        ],
        out_specs=pl.BlockSpec((tm, tn), lambda j, i: (i, j)),
        scratch_shapes=[pltpu.VMEM((d, tn), BF16)],
        compiler_params=_params("arbitrary", "arbitrary"),
        name="outproj",
    )(conv_p, gla_p, conv_s, gla_s, x, w_out, mod_p, mod_s)


def _norm2_kernel(x_ref, g_ref, shp, scp, shs, scs, wr_hi, wr_lo, br_ref, h3_ref, ids_ref, wts_ref, h_ref, *, npt, dec_seq, n_grp, n_exp):
    i = pl.program_id(0)
    y = _rms(x_ref[...], g_ref[...])
    _store_by_group(i, npt, dec_seq, h_ref, lambda v, m: v[0] * (1.0 + m[1]) + m[0], [y], [shp, scp], [shs, scs])
    h = h_ref[...]
    _to_row_tiles(h3_ref, h)
    h_hi = h.astype(BF16)
    h_lo = (h - h_hi.astype(F32)).astype(BF16)
    logits = (jnp.dot(h_hi, wr_hi[...], preferred_element_type=F32) + jnp.dot(h_lo, wr_hi[...], preferred_element_type=F32)
              + jnp.dot(h_hi, wr_lo[...], preferred_element_type=F32)) + br_ref[...]
    lane = lax.broadcasted_iota(jnp.int32, logits.shape, 1).astype(F32)
    big = jnp.float32(LANES)
    neg = jnp.float32(-jnp.inf)
    gl = jnp.where(lane < n_grp, logits, neg)
    gmax = jnp.max(gl, axis=-1, keepdims=True)
    gidx = jnp.min(jnp.where(gl == gmax, lane, big), axis=-1, keepdims=True)
    g_w = 1.0 / jnp.sum(jnp.exp(gl - gmax), axis=-1, keepdims=True)
    lo = n_grp + gidx * n_exp
    in_grp = (lane >= lo) & (lane < lo + n_exp)
    sl = jnp.where(in_grp, logits, neg)
    p = jnp.exp(sl - jnp.max(sl, axis=-1, keepdims=True))
    p = p / jnp.sum(p, axis=-1, keepdims=True)
    p = jnp.where(in_grp, p, -1.0)
    p1 = jnp.max(p, axis=-1, keepdims=True)
    i1 = jnp.min(jnp.where(p == p1, lane, big), axis=-1, keepdims=True)
    p_rest = jnp.where(lane == i1, -1.0, p)
    p2 = jnp.max(p_rest, axis=-1, keepdims=True)
    i2 = jnp.min(jnp.where(p_rest == p2, lane, big), axis=-1, keepdims=True)
    denom = p1 + p2
    ids_ref[...] = jnp.where(lane == 0, i1 - n_grp, jnp.where(lane == 1, i2 - n_grp, 0.0)).astype(jnp.int32)
    wts_ref[...] = jnp.where(lane == 0, g_w * (p1 / denom), jnp.where(lane == 1, g_w * (p2 / denom), 0.0))


def _norm2(x, g, mod_p, mod_s, wr_hi, wr_lo, br, dims, n_grp, n_exp):
    t, d = x.shape
    tm = ROW_TILE
    p_specs, s_specs = _mod_specs((3, 4), d, dims["npt"], dims["tps"], dims["n_seq"], dims["dec_batch"])
    return pl.pallas_call(
        functools.partial(_norm2_kernel, npt=dims["npt"], dec_seq=dims["dec_seq"], n_grp=n_grp, n_exp=n_exp),
        out_shape=(jax.ShapeDtypeStruct((t, d // LANES, LANES), F32), jax.ShapeDtypeStruct((t, LANES), jnp.int32),
                   jax.ShapeDtypeStruct((t, LANES), F32)),
        grid=(t // tm,),
        in_specs=[
            pl.BlockSpec((tm, d), lambda i: (i, 0)),
            pl.BlockSpec((1, d), lambda i: (0, 0)),
            *p_specs,
            *s_specs,
            pl.BlockSpec((d, LANES), lambda i: (0, 0)),
            pl.BlockSpec((d, LANES), lambda i: (0, 0)),
            pl.BlockSpec((1, LANES), lambda i: (0, 0)),
        ],
        out_specs=(pl.BlockSpec((tm, d // LANES, LANES), lambda i: (i, 0, 0)), pl.BlockSpec((tm, LANES), lambda i: (i, 0)),
                   pl.BlockSpec((tm, LANES), lambda i: (i, 0))),
        scratch_shapes=[pltpu.VMEM((tm, d), F32)],
        compiler_params=_params("arbitrary"),
        name="norm2_router",
    )(x, g.reshape(1, d), mod_p, mod_p, mod_s, mod_s, wr_hi, wr_lo, br)


def _dispatch(ids, wts, n_experts, te, n_tiles_max):
    t = ids.shape[0]
    e_flat = ids.reshape(-1)
    onehot = (e_flat[:, None] == jnp.arange(n_experts, dtype=jnp.int32)[None, :]).astype(jnp.int32)
    csum = jnp.cumsum(onehot, axis=0)
    rank = jnp.take_along_axis(csum, e_flat[:, None], axis=1)[:, 0] - 1
    counts = csum[-1]
    padded = ((counts + te - 1) // te) * te
    ends = jnp.cumsum(padded)
    offs = ends - padded
    slot = offs[e_flat] + rank
    n_slots = n_tiles_max * te
    tok_of_slot = jnp.zeros((n_slots,), jnp.int32).at[slot].set(jnp.arange(TOP_K * t, dtype=jnp.int32) // TOP_K)
    w_of_slot = jnp.zeros((n_slots,), F32).at[slot].set(wts.reshape(-1))
    n_used = (ends[-1] // te).astype(jnp.int32)
    tile_start = jnp.arange(n_tiles_max, dtype=jnp.int32) * te
    tile_expert = jnp.minimum(jnp.sum(ends[None, :] <= tile_start[:, None], axis=1), n_experts - 1).astype(jnp.int32)
    last = tile_expert[jnp.maximum(n_used - 1, 0)]
    tile_expert = jnp.where(jnp.arange(n_tiles_max) < n_used, tile_expert, last)
    return tile_expert, tok_of_slot, w_of_slot, n_used.reshape(1), slot.reshape(t, TOP_K)


def _gather_rows(src_hbm, idx_ref, base, n_rows, dst, sem):
    def body(r, carry):
        pltpu.make_async_copy(src_hbm.at[idx_ref[base + r]], dst.at[r], sem).start()
        return carry

    lax.fori_loop(0, n_rows, body, 0)


def _wait_rows(src_hbm, n_rows, dst, sem):
    pltpu.make_async_copy(src_hbm.at[pl.ds(0, n_rows)], dst, sem).wait()


def _to_row_tiles(ref, val):
    for s in range(ref.shape[1]):
        ref[:, s, :] = val[:, s * LANES:(s + 1) * LANES]


def _from_row_tiles(ref):
    return jnp.concatenate([ref[:, s, :] for s in range(ref.shape[1])], axis=1)


def _expert_kernel(te_ref, tok_ref, nu_ref, h_hbm, wg_ref, wu_ref, wd_ref, ws_ref, o_ref, xbuf, sem, wgb, wub, wdb):
    i = pl.program_id(0)
    n_used = nu_ref[0]
    rows = xbuf.shape[1]
    slot = lax.rem(i, 2)

    @pl.when(i == 0)
    def _():
        _gather_rows(h_hbm, tok_ref, 0, rows, xbuf.at[0], sem.at[0])

    @pl.when(i + 1 < n_used)
    def _():
        _gather_rows(h_hbm, tok_ref, (i + 1) * rows, rows, xbuf.at[1 - slot], sem.at[1 - slot])

    @pl.when(i < n_used)
    def _():
        changed = jnp.logical_or(i == 0, te_ref[i] != te_ref[jnp.maximum(i - 1, 0)])

        @pl.when(changed)
        def _():
            wgb[...] = wg_ref[...].astype(BF16)
            wub[...] = wu_ref[...].astype(BF16)
            wdb[...] = wd_ref[...].astype(BF16)

        _wait_rows(h_hbm, rows, xbuf.at[slot], sem.at[slot])
        x = _from_row_tiles(xbuf.at[slot]).astype(BF16)
        a = jnp.dot(x, wgb[...], preferred_element_type=F32)
        u = jnp.dot(x, wub[...], preferred_element_type=F32)
        hid = _silu(a) * u * ws_ref[...]
        _to_row_tiles(o_ref, jnp.dot(hid.astype(BF16), wdb[...], preferred_element_type=F32))

    @pl.when(i >= n_used)
    def _():
        o_ref[...] = jnp.zeros(o_ref.shape, F32)


def _experts(h2, tile_expert, tok_of_slot, w_of_slot, n_used, wg, wu, wd, layer_base):
    t, nd, _ = h2.shape
    d = nd * LANES
    f = wg.shape[-1]
    te = EXPERT_TILE
    n_tiles = tile_expert.shape[0]

    def w_map(i, te_ref, tok_ref, nu_ref):
        return (layer_base + te_ref[i], 0, 0)

    def row_map(i, te_ref, tok_ref, nu_ref):
        return (jnp.minimum(i, nu_ref[0] - 1), 0)

    return pl.pallas_call(
        _expert_kernel,
        out_shape=jax.ShapeDtypeStruct((n_tiles * te, nd, LANES), F32),
        grid_spec=pltpu.PrefetchScalarGridSpec(
            num_scalar_prefetch=3,
            grid=(n_tiles,),
            in_specs=[
                pl.BlockSpec(memory_space=pl.ANY),
                pl.BlockSpec((None, d, f), w_map),
                pl.BlockSpec((None, d, f), w_map),
                pl.BlockSpec((None, f, d), w_map),
                pl.BlockSpec((te, 1), row_map),
            ],
            out_specs=pl.BlockSpec((te, nd, LANES), lambda i, te_ref, tok_ref, nu_ref: (i, 0, 0)),
            scratch_shapes=[
                pltpu.VMEM((2, te, nd, LANES), F32),
                pltpu.SemaphoreType.DMA((2,)),
                pltpu.VMEM((d, f), BF16),
                pltpu.VMEM((d, f), BF16),
                pltpu.VMEM((f, d), BF16),
            ],
        ),
        compiler_params=_params("arbitrary"),
        name="experts",
    )(tile_expert, tok_of_slot, n_used, h2, wg, wu, wd, w_of_slot.reshape(-1, 1))


def _combine_kernel(p0_ref, p1_ref, y_hbm, x_ref, gp, gs, o_ref, buf, sem, *, npt, dec_seq):
    i = pl.program_id(0)
    n = pl.num_programs(0)
    tm = x_ref.shape[0]
    slot = lax.rem(i, 2)

    def start(tile, sl):
        _gather_rows(y_hbm, p0_ref, tile * tm, tm, buf.at[sl, 0], sem.at[sl])
        _gather_rows(y_hbm, p1_ref, tile * tm, tm, buf.at[sl, 1], sem.at[sl])

    @pl.when(i == 0)
    def _():
        start(0, 0)

    @pl.when(i + 1 < n)
    def _():
        start(i + 1, 1 - slot)

    _wait_rows(y_hbm, tm, buf.at[slot, 0], sem.at[slot])
    _wait_rows(y_hbm, tm, buf.at[slot, 1], sem.at[slot])
    ff = _from_row_tiles(buf.at[slot, 0]) + _from_row_tiles(buf.at[slot, 1])
    _store_by_group(i, npt, dec_seq, o_ref, lambda v, m: v[0] + m[0] * v[1], [x_ref[...], ff], [gp], [gs])


def _combine(y_sorted, pos, x, mod_p, mod_s, dims):
    t, d = x.shape
    tm = ROW_TILE
    p_specs, s_specs = _mod_specs((5,), d, dims["npt"], dims["tps"], dims["n_seq"], dims["dec_batch"])
    def strip(spec):
        return pl.BlockSpec(spec.block_shape, lambda i, a, b, m=spec.index_map: m(i))

    return pl.pallas_call(
        functools.partial(_combine_kernel, npt=dims["npt"], dec_seq=dims["dec_seq"]),
        out_shape=jax.ShapeDtypeStruct((t, d), F32),
        grid_spec=pltpu.PrefetchScalarGridSpec(
            num_scalar_prefetch=2,
            grid=(t // tm,),
            in_specs=[
                pl.BlockSpec(memory_space=pl.ANY),
                pl.BlockSpec((tm, d), lambda i, a, b: (i, 0)),
                *[strip(s) for s in p_specs],
                *[strip(s) for s in s_specs],
            ],
            out_specs=pl.BlockSpec((tm, d), lambda i, a, b: (i, 0)),
            scratch_shapes=[pltpu.VMEM((2, TOP_K, tm, d // LANES, LANES), F32), pltpu.SemaphoreType.DMA((2,))],
        ),
        compiler_params=_params("arbitrary"),
        name="combine",
    )(pos[:, 0], pos[:, 1], y_sorted, x, mod_p, mod_s)


def _final_norm_kernel(x_ref, g_ref, op_ref, os_ref, *, npt):
    i = pl.program_id(0)
    y = _rms(x_ref[...], g_ref[...])

    @pl.when(i < npt)
    def _():
        op_ref[...] = y

    @pl.when(i >= npt)
    def _():
        os_ref[...] = y


def _final_norm(x, g, npt):
    t, d = x.shape
    tm = ROW_TILE
    return pl.pallas_call(
        functools.partial(_final_norm_kernel, npt=npt),
        out_shape=(jax.ShapeDtypeStruct((npt * tm, d), F32), jax.ShapeDtypeStruct((t - npt * tm, d), F32)),
        grid=(t // tm,),
        in_specs=[pl.BlockSpec((tm, d), lambda i: (i, 0)), pl.BlockSpec((1, d), lambda i: (0, 0))],
        out_specs=(pl.BlockSpec((tm, d), lambda i: (jnp.minimum(i, npt - 1), 0)),
                   pl.BlockSpec((tm, d), lambda i: (jnp.maximum(i - npt, 0), 0))),
        compiler_params=_params("arbitrary"),
        name="final_norm",
    )(x, g.reshape(1, d))


def kernel(x_prompt, x_sample, c_prompt, c_sample, state_conv, state_gla, w_ada, b_ada, norm1_g, norm2_g, w_in, conv_w, conv_b, conv_ln_g, conv_ln_b, gate_w2, gate_b, gla_norm_g, w_out, router_grp_w, router_grp_b, router_exp_w, router_exp_b, exp_w_gate, exp_w_up, exp_w_down, final_norm_g):
    n_seq, seq_len, d = x_prompt.shape
    dec_batch, dec_seq, _ = x_sample.shape
    depth = w_ada.shape[0]
    kw, d_conv = conv_w.shape[1:]
    heads, dv = gla_norm_g.shape[1:]
    rank, qkw = gate_w2.shape[1:]
    dk = qkw // heads
    n_grp, n_exp = router_exp_w.shape[2:]
    n_experts = n_grp * n_exp
    tp, ts = n_seq * seq_len, dec_batch * dec_seq
    t = tp + ts
    tm = ROW_TILE
    assert ts == tm and seq_len % tm == 0 and d_conv == heads * dv and kw - 1 <= CONV_HALO
    n_main = 2 * d_conv + 2 * qkw + 2 * heads * dv
    dims = dict(npt=tp // tm, tps=seq_len // tm, n_seq=n_seq, dec_batch=dec_batch, dec_seq=dec_seq)

    x = jnp.concatenate([x_prompt.reshape(tp, d), x_sample.transpose(1, 0, 2).reshape(ts, d)], axis=0)
    pad = (-n_seq) % 8
    c_all = jnp.concatenate([c_prompt, jnp.zeros((pad, d), F32), c_sample], axis=0)
    mod = _ada(c_all, w_ada, b_ada)
    n_tiles_max = (TOP_K * t) // EXPERT_TILE + n_experts

    conv_p, gla_p, conv_s, gla_s = [], [], [], []
    for l in range(depth):
        mod_p = mod[l, :n_seq].reshape(n_seq, 1, 6 * d)
        mod_s = mod[l, n_seq + pad:]
        gw2 = jnp.pad(gate_w2[l], ((0, LANES - rank), (0, 0))).astype(BF16)
        h, la = _norm1(x, norm1_g[l], mod_p, mod_s, w_in, l, n_main, rank, gw2, gate_b[l], dims)
        z = _inproj(h, w_in, l, n_main)

        cv_p, cb_p = _conv_prompt(z, conv_w[l], conv_b[l], conv_ln_g[l], conv_ln_b[l], n_seq, seq_len)
        cv_s, cb_s = _conv_sample(z, state_conv[l].transpose(1, 0, 2), conv_w[l], conv_b[l], conv_ln_g[l],
                                  conv_ln_b[l], tp, dec_seq, dec_batch)
        conv_p.append(cb_p)
        conv_s.append(cb_s.transpose(1, 0, 2))

        go_p, gs_p = _gla_prompt(z, la, gla_norm_g[l], n_seq, seq_len, heads, dk, dv)

        def seq_major(a):
            return a.reshape(dec_seq, dec_batch, a.shape[-1]).transpose(1, 0, 2)

        zs = z[tp:]
        q3 = seq_major(zs[:, 2 * d_conv:2 * d_conv + qkw])
        k3 = seq_major(zs[:, 2 * d_conv + qkw:2 * d_conv + 2 * qkw])
        v3 = seq_major(zs[:, 2 * d_conv + 2 * qkw:2 * d_conv + 2 * qkw + heads * dv])
        g3 = seq_major(zs[:, 2 * d_conv + 2 * qkw + heads * dv:n_main])
        go_s, gs_s = _gla_sample(q3, k3, v3, g3, seq_major(la[tp:]), state_gla, l, gla_norm_g[l], heads, dk, dv)
        go_s = go_s.transpose(1, 0, 2).reshape(ts, heads * dv).astype(BF16)
        gla_p.append(gs_p)
        gla_s.append(gs_s)

        x = _outproj(cv_p, go_p, cv_s.reshape(ts, d_conv), go_s, x, w_out, l, mod_p, mod_s, dims)

        wr = jnp.concatenate([router_grp_w[l], router_exp_w[l].reshape(d, n_experts)], axis=1)
        wr = jnp.pad(wr, ((0, 0), (0, LANES - wr.shape[1])))
        wr_hi = wr.astype(BF16)
        wr_lo = (wr - wr_hi.astype(F32)).astype(BF16)
        br = jnp.concatenate([router_grp_b[l], router_exp_b[l].reshape(-1)])
        br = jnp.pad(br, (0, LANES - br.shape[0])).reshape(1, LANES)
        h2, ids, wts = _norm2(x, norm2_g[l], mod_p, mod_s, wr_hi, wr_lo, br, dims, n_grp, n_exp)
        tile_expert, tok_of_slot, w_of_slot, n_used, pos = _dispatch(ids[:, :TOP_K], wts[:, :TOP_K], n_experts,
                                                                     EXPERT_TILE, n_tiles_max)
        f = exp_w_gate.shape[-1]
        y_sorted = _experts(h2, tile_expert, tok_of_slot, w_of_slot, n_used,
                            exp_w_gate.reshape(depth * n_experts, d, f), exp_w_up.reshape(depth * n_experts, d, f),
                            exp_w_down.reshape(depth * n_experts, f, d), l * n_experts)
        x = _combine(y_sorted, pos, x, mod_p, mod_s, dims)

    y_p, y_s = _final_norm(x, final_norm_g, dims["npt"])
    y_prompt = y_p.reshape(n_seq, seq_len, d)
    y_sample = y_s.reshape(dec_seq, dec_batch, d).transpose(1, 0, 2)
    return (y_prompt, y_sample, jnp.stack(conv_p), jnp.stack(gla_p), jnp.stack(conv_s), jnp.stack(gla_s))
```

```python
import functools

import jax
import jax.numpy as jnp
from jax import lax
from jax.experimental import pallas as pl
from jax.experimental.pallas import tpu as pltpu

F32 = jnp.float32
BF16 = jnp.bfloat16

EPS = 1e-6
GATE_TAU = 16.0
GLA_CHUNK = 32
TOP_K = 2

ROW_TILE = 512
EXPERT_TILE = 256
SORT_TILE = 256
SEG_ALIGN = 16
CONV_ROWS = 64
CONV_HALO = 32
LANES = 128
SUBLANES = 8
VMEM_LIMIT = 56 * 1024 * 1024


def _params(*sem):
    return pltpu.CompilerParams(dimension_semantics=sem, vmem_limit_bytes=VMEM_LIMIT)


def _bdot(a, b):
    return jnp.dot(a.astype(BF16), b.astype(BF16), preferred_element_type=F32)


def _round_bf16(x):
    return x.astype(BF16).astype(F32)


def _split3(x):
    hi = x.astype(BF16)
    r = x - hi.astype(F32)
    mid = r.astype(BF16)
    lo = (r - mid.astype(F32)).astype(BF16)
    return hi, mid, lo


def _silu(x):
    return x * jax.nn.sigmoid(x)


def _store_by_group(i, n_prompt_tiles, dec_seq, out_ref, fn, vals, p_refs, s_refs):
    @pl.when(i < n_prompt_tiles)
    def _():
        out_ref[...] = fn(vals, [r[...] for r in p_refs]).astype(out_ref.dtype)

    @pl.when(i >= n_prompt_tiles)
    def _():
        mods = [r[...] for r in s_refs]
        nb = mods[0].shape[0]
        for t in range(out_ref.shape[0] // nb):
            rows = slice(t * nb, (t + 1) * nb)
            out_ref[rows, :] = fn([v[rows] for v in vals], mods).astype(out_ref.dtype)


def _ada_kernel(c_ref, w_ref, b_ref, o_ref):
    c = c_ref[...]
    o_ref[...] = _bdot(_silu(c), w_ref[...]) + b_ref[...]


def _ada(c_all, w_ada, b_ada):
    depth, d, n = w_ada.shape
    rows = c_all.shape[0]
    tn = 1024
    return pl.pallas_call(
        _ada_kernel,
        out_shape=jax.ShapeDtypeStruct((depth, rows, n), F32),
        grid=(depth, n // tn),
        in_specs=[
            pl.BlockSpec((rows, d), lambda l, j: (0, 0)),
            pl.BlockSpec((None, d, tn), lambda l, j: (l, 0, j)),
            pl.BlockSpec((None, 1, tn), lambda l, j: (l, 0, j)),
        ],
        out_specs=pl.BlockSpec((None, rows, tn), lambda l, j: (l, 0, j)),
        compiler_params=_params("arbitrary", "arbitrary"),
        name="ada",
    )(c_all, w_ada, b_ada.reshape(depth, 1, n))


def _mod_specs(cols, width, n_prompt_tiles, tiles_per_seq, n_seq, dec_batch, grid_rank=1, row_axis=0, col_fn=None):
    p_specs, s_specs = [], []
    for c in cols:
        def p_map(*idx, c=c):
            b = jnp.minimum(idx[row_axis] // tiles_per_seq, n_seq - 1)
            return (b, 0, c if col_fn is None else col_fn(c, idx))

        def s_map(*idx, c=c):
            return (0, c if col_fn is None else col_fn(c, idx))

        p_specs.append(pl.BlockSpec((None, 1, width), p_map))
        s_specs.append(pl.BlockSpec((dec_batch, width), s_map))
    return p_specs, s_specs


def _rms(x, g):
    return x * lax.rsqrt(jnp.mean(x * x, axis=-1, keepdims=True) + EPS) * g


def _norm1_kernel(x_ref, g_ref, shp, scp, shs, scs, wgl_ref, gw2_ref, gb_ref, h_ref, la_ref, *, npt, dec_seq, rank):
    i = pl.program_id(0)
    y = _rms(x_ref[...], g_ref[...])
    _store_by_group(i, npt, dec_seq, h_ref, lambda v, m: v[0] * (1.0 + m[1]) + m[0], [y], [shp, scp], [shs, scs])
    lane = lax.broadcasted_iota(jnp.int32, wgl_ref.shape, 1)
    w_gl = jnp.where(lane < rank, wgl_ref[...], 0.0).astype(BF16)
    gate_lr = jnp.dot(h_ref[...], w_gl, preferred_element_type=F32)
    pre = _bdot(gate_lr, gw2_ref[...]) + gb_ref[...]
    la_ref[...] = (jnp.minimum(pre, 0.0) - jnp.log1p(jnp.exp(-jnp.abs(pre)))) * (1.0 / GATE_TAU)


def _norm1(x, g, mod_p, mod_s, w_in, layer, n_main, rank, gw2, gb, dims):
    t, d = x.shape
    tm = ROW_TILE
    p_specs, s_specs = _mod_specs((0, 1), d, dims["npt"], dims["tps"], dims["n_seq"], dims["dec_batch"])
    qk = gw2.shape[1]
    assert n_main % LANES == 0 and rank <= LANES
    return pl.pallas_call(
        functools.partial(_norm1_kernel, npt=dims["npt"], dec_seq=dims["dec_seq"], rank=rank),
        out_shape=(jax.ShapeDtypeStruct((t, d), BF16), jax.ShapeDtypeStruct((t, qk), F32)),
        grid=(t // tm,),
        in_specs=[
            pl.BlockSpec((tm, d), lambda i: (i, 0)),
            pl.BlockSpec((1, d), lambda i: (0, 0)),
            *p_specs,
            *s_specs,
            pl.BlockSpec((None, d, LANES), lambda i: (layer, 0, n_main // LANES)),
            pl.BlockSpec(gw2.shape, lambda i: (0, 0)),
            pl.BlockSpec((1, qk), lambda i: (0, 0)),
        ],
        out_specs=(pl.BlockSpec((tm, d), lambda i: (i, 0)), pl.BlockSpec((tm, qk), lambda i: (i, 0))),
        compiler_params=_params("arbitrary"),
        name="norm1",
    )(x, g.reshape(1, d), mod_p, mod_p, mod_s, mod_s, w_in, gw2, gb.reshape(1, qk))


def _inproj_kernel(h_ref, w_ref, o_ref, wb_ref):
    @pl.when(pl.program_id(1) == 0)
    def _():
        wb_ref[...] = w_ref[...].astype(BF16)

    o_ref[...] = jnp.dot(h_ref[...], wb_ref[...], preferred_element_type=F32)


def _inproj(h, w_in, layer, n_cols):
    t, d = h.shape
    tm, tn = ROW_TILE, 1024
    return pl.pallas_call(
        _inproj_kernel,
        out_shape=jax.ShapeDtypeStruct((t, n_cols), F32),
        grid=(n_cols // tn, t // tm),
        in_specs=[
            pl.BlockSpec((tm, d), lambda j, i: (i, 0)),
            pl.BlockSpec((None, d, tn), lambda j, i: (layer, 0, j)),
        ],
        out_specs=pl.BlockSpec((tm, tn), lambda j, i: (i, j)),
        scratch_shapes=[pltpu.VMEM((d, tn), BF16)],
        compiler_params=_params("arbitrary", "arbitrary"),
        name="inproj",
    )(h, w_in)


def _ln_silu(y, g, b):
    mu = jnp.mean(y, axis=-1, keepdims=True)
    yc = y - mu
    var = jnp.mean(yc * yc, axis=-1, keepdims=True)
    return _silu(yc * lax.rsqrt(var + EPS) * g + b)


def _conv_prompt_kernel(a_ref, b_ref, cw_ref, cb_ref, lng_ref, lnb_ref, o_ref, st_ref, full_ref, cwb_ref, y_ref, *, kw):
    j = pl.program_id(1)
    tm, c = a_ref.shape
    halo = CONV_HALO
    phases = full_ref.shape[0]
    assert phases == SUBLANES

    @pl.when(jnp.logical_and(pl.program_id(0) == 0, j == 0))
    def _():
        full_ref[...] = jnp.zeros(full_ref.shape, F32)
        for w in range(kw):
            cwb_ref[w] = jnp.broadcast_to(cw_ref[w:w + 1, :], (SUBLANES, c))

    prev = full_ref[0, tm + halo - SUBLANES:tm + halo, :]
    tail = jnp.where(j == 0, 0.0, prev)

    @pl.when(j == 0)
    def _():
        for p in range(phases):
            full_ref[p, 0:halo, :] = jnp.zeros((halo, c), F32)

    @pl.when(j > 0)
    def _():
        for p in range(phases):
            full_ref[p, 0:halo, :] = full_ref[p, tm:tm + halo, :]

    u = a_ref[...] * jax.nn.sigmoid(b_ref[...])
    full_ref[0, halo:halo + tm, :] = u
    ext = jnp.concatenate([tail, u], axis=0)
    for p in range(1, phases):
        full_ref[p, halo - SUBLANES:halo - SUBLANES + tm, :] = pltpu.roll(ext, tm + SUBLANES - p, 0)[0:tm]
    off = halo - (kw - 1)
    rb = CONV_ROWS

    def body(r, carry):
        r0 = pl.multiple_of(r * rb, rb)
        for lt in range(c // LANES):
            cols = slice(lt * LANES, (lt + 1) * LANES)
            acc = None
            for p in range(phases):
                x = full_ref[p, pl.ds(r0, rb + halo), cols]
                for a in range(halo // phases + 1):
                    w = a * phases + p - off
                    if 0 <= w < kw and a * phases + rb <= rb + halo:
                        term = x[a * phases:a * phases + rb] * jnp.concatenate([cwb_ref[w, :, cols]] * (rb // SUBLANES), axis=0)
                        acc = term if acc is None else acc + term
            y_ref[pl.ds(r0, rb), cols] = acc
        return carry

    lax.fori_loop(0, tm // rb, body, 0)
    y = _ln_silu(y_ref[...] + cb_ref[...], lng_ref[...], lnb_ref[...])
    o_ref[...] = y.astype(o_ref.dtype)

    @pl.when(j == pl.num_programs(1) - 1)
    def _():
        st_ref[...] = full_ref[0, halo + tm - (kw - 1):halo + tm, :]


def _conv_prompt(z, cw, cb, lng, lnb, n_seq, seq_len):
    kw, c = cw.shape
    tm = ROW_TILE
    tps = seq_len // tm
    return pl.pallas_call(
        functools.partial(_conv_prompt_kernel, kw=kw),
        out_shape=(jax.ShapeDtypeStruct((n_seq * seq_len, c), BF16), jax.ShapeDtypeStruct((n_seq, kw - 1, c), F32)),
        grid=(n_seq, tps),
        in_specs=[
            pl.BlockSpec((tm, c), lambda b, j: (b * tps + j, 0)),
            pl.BlockSpec((tm, c), lambda b, j: (b * tps + j, 1)),
            pl.BlockSpec((kw, c), lambda b, j: (0, 0)),
            pl.BlockSpec((1, c), lambda b, j: (0, 0)),
            pl.BlockSpec((1, c), lambda b, j: (0, 0)),
            pl.BlockSpec((1, c), lambda b, j: (0, 0)),
        ],
        out_specs=(
            pl.BlockSpec((tm, c), lambda b, j: (b * tps + j, 0)),
            pl.BlockSpec((None, kw - 1, c), lambda b, j: (b, 0, 0)),
        ),
        scratch_shapes=[pltpu.VMEM((SUBLANES, tm + CONV_HALO, c), F32), pltpu.VMEM((kw, SUBLANES, c), F32),
                        pltpu.VMEM((tm, c), F32)],
        compiler_params=_params("arbitrary", "arbitrary"),
        name="conv_prompt",
    )(z, z, cw, cb.reshape(1, c), lng.reshape(1, c), lnb.reshape(1, c))


def _conv_sample_kernel(*refs, kw, dec_seq):
    a_refs = refs[0:dec_seq]
    b_refs = refs[dec_seq:2 * dec_seq]
    st_ref, cw_ref, cb_ref, lng_ref, lnb_ref, o_ref, ns_ref = refs[2 * dec_seq:]
    hist = kw - 1
    u = [a_refs[t][...] * jax.nn.sigmoid(b_refs[t][...]) for t in range(dec_seq)]
    def row(j):
        return st_ref[j] if j < hist else u[j - hist]

    for t in range(dec_seq):
        acc = row(t) * cw_ref[0:1, :]
        for w in range(1, kw):
            acc = acc + row(t + w) * cw_ref[w:w + 1, :]
        y = _ln_silu(acc + cb_ref[...], lng_ref[...], lnb_ref[...])
        o_ref[t] = y.astype(o_ref.dtype)
    for j in range(hist):
        ns_ref[j] = row(j + dec_seq)


def _conv_sample(z, st_t, cw, cb, lng, lnb, row0, dec_seq, dec_batch):
    kw, c = cw.shape
    bs = 16
    a_specs = [pl.BlockSpec((bs, c), lambda s, t=t: ((row0 + t * dec_batch) // bs + s, 0)) for t in range(dec_seq)]
    b_specs = [pl.BlockSpec((bs, c), lambda s, t=t: ((row0 + t * dec_batch) // bs + s, 1)) for t in range(dec_seq)]
    vec = pl.BlockSpec((1, c), lambda s: (0, 0))
    return pl.pallas_call(
        functools.partial(_conv_sample_kernel, kw=kw, dec_seq=dec_seq),
        out_shape=(jax.ShapeDtypeStruct((dec_seq, dec_batch, c), BF16), jax.ShapeDtypeStruct((kw - 1, dec_batch, c), F32)),
        grid=(dec_batch // bs,),
        in_specs=[*a_specs, *b_specs, pl.BlockSpec((kw - 1, bs, c), lambda s: (0, s, 0)),
                  pl.BlockSpec((kw, c), lambda s: (0, 0)), vec, vec, vec],
        out_specs=(pl.BlockSpec((dec_seq, bs, c), lambda s: (0, s, 0)), pl.BlockSpec((kw - 1, bs, c), lambda s: (0, s, 0))),
        compiler_params=_params("arbitrary"),
        name="conv_sample",
    )(*([z] * (2 * dec_seq)), st_t, cw, cb.reshape(1, c), lng.reshape(1, c), lnb.reshape(1, c))


def _gla_prompt_kernel(qk_ref, v_ref, g_ref, la_ref, gn_ref, o_ref, sfin_ref, st_ref, sn_ref, *, heads, dk, dv):
    j = pl.program_id(1)
    tm = qk_ref.shape[0]
    ck = GLA_CHUNK
    nch = tm // ck
    qkw = heads * dk

    @pl.when(j == 0)
    def _():
        st_ref[...] = jnp.zeros(st_ref.shape, F32)

    la = la_ref[...]
    row = lax.broadcasted_iota(jnp.int32, (tm, tm), 0)
    col = lax.broadcasted_iota(jnp.int32, (tm, tm), 1)
    same = (row // ck) == (col // ck)
    causal = same & (col <= row)
    tri_incl = causal.astype(BF16)
    tri_after = (same & (col > row)).astype(BF16)
    sel = (lax.broadcasted_iota(jnp.int32, (nch, tm), 1) // ck == lax.broadcasted_iota(jnp.int32, (nch, tm), 0)).astype(BF16)
    parts = _split3(la)
    b = sum(jnp.dot(tri_incl, p, preferred_element_type=F32) for p in parts)
    rest = sum(jnp.dot(tri_after, p, preferred_element_type=F32) for p in parts)
    tot = sum(jnp.dot(sel, p, preferred_element_type=F32) for p in parts)
    qk = qk_ref[...]
    q = qk[:, :qkw] * (dk ** -0.5)
    k = qk[:, qkw:]
    q_dec = (q * jnp.exp(b)).astype(BF16)
    k_inv = (k * jnp.exp(-b)).astype(BF16)
    k_end = _round_bf16(k * jnp.exp(rest))
    decay = jnp.exp(tot)
    v_all = v_ref[...]
    g_all = g_ref[...]
    for h in range(heads):
        ks = slice(h * dk, (h + 1) * dk)
        vs = slice(h * dv, (h + 1) * dv)
        qd, ki, ke = q_dec[:, ks], k_inv[:, ks], k_end[:, ks]
        vh = v_all[:, vs]
        vb = vh.astype(BF16)
        vr = _round_bf16(vh)
        att = lax.dot_general(qd, ki, (((1,), (1,)), ((), ())), preferred_element_type=F32)
        att = jnp.where(causal, att, 0.0).astype(BF16)
        o = jnp.dot(att, vb, preferred_element_type=F32)
        s = st_ref[h]
        for n in range(nch):
            rs = slice(n * ck, (n + 1) * ck)
            sn_ref[n] = s.astype(BF16)
            upd = lax.dot_general(vr[rs], ke[rs], (((0,), (0,)), ((), ())), preferred_element_type=F32)
            s = s * decay[n:n + 1, ks] + upd
        st_ref[h] = s
        inter = [lax.dot_general(qd[n * ck:(n + 1) * ck], sn_ref[n], (((1,), (1,)), ((), ())), preferred_element_type=F32)
                 for n in range(nch)]
        o = o + jnp.concatenate(inter, axis=0)
        o = _rms(o, gn_ref[:, vs]) * _silu(g_all[:, vs])
        o_ref[:, vs] = o.astype(o_ref.dtype)

    @pl.when(j == pl.num_programs(1) - 1)
    def _():
        for h in range(heads):
            sfin_ref[h] = st_ref[h].T


def _gla_prompt(z, la, gn, n_seq, seq_len, heads, dk, dv):
    tm = ROW_TILE
    tps = seq_len // tm
    w = heads * dv
    qkw = heads * dk
    assert 2 * qkw == w
    return pl.pallas_call(
        functools.partial(_gla_prompt_kernel, heads=heads, dk=dk, dv=dv),
        out_shape=(jax.ShapeDtypeStruct((n_seq * seq_len, w), BF16), jax.ShapeDtypeStruct((n_seq, heads, dk, dv), F32)),
        grid=(n_seq, tps),
        in_specs=[
            pl.BlockSpec((tm, w), lambda b, j: (b * tps + j, 2)),
            pl.BlockSpec((tm, w), lambda b, j: (b * tps + j, 3)),
            pl.BlockSpec((tm, w), lambda b, j: (b * tps + j, 4)),
            pl.BlockSpec((tm, qkw), lambda b, j: (b * tps + j, 0)),
            pl.BlockSpec((1, w), lambda b, j: (0, 0)),
        ],
        out_specs=(
            pl.BlockSpec((tm, w), lambda b, j: (b * tps + j, 0)),
            pl.BlockSpec((None, heads, dk, dv), lambda b, j: (b, 0, 0, 0)),
        ),
        scratch_shapes=[pltpu.VMEM((heads, dv, dk), F32), pltpu.VMEM((tm // GLA_CHUNK, dv, dk), BF16)],
        compiler_params=_params("arbitrary", "arbitrary"),
        name="gla_prompt",
    )(z, z, z, la, gn.reshape(1, w))


def _gla_sample_kernel(q_ref, k_ref, v_ref, g_ref, la_ref, s_ref, gn_ref, o_ref, ns_ref, *, heads, dk, dv):
    bs, ln, _ = q_ref.shape
    tril = lax.broadcasted_iota(jnp.int32, (ln, ln), 1) <= lax.broadcasted_iota(jnp.int32, (ln, ln), 0)
    for s in range(bs):
        q_s, k_s, v_s, g_s, la_s = q_ref[s], k_ref[s], v_ref[s], g_ref[s], la_ref[s]
        for h in range(heads):
            ks = slice(h * dk, (h + 1) * dk)
            vs = slice(h * dv, (h + 1) * dv)
            la = la_s[:, ks]
            rows = [la[0:1]]
            for t in range(1, ln):
                rows.append(rows[-1] + la[t:t + 1])
            b = jnp.concatenate(rows, axis=0)
            b_last = rows[-1]
            q_dec = _round_bf16(q_s[:, ks] * (dk ** -0.5) * jnp.exp(b))
            k_inv = _round_bf16(k_s[:, ks] * jnp.exp(-b))
            k_end = _round_bf16(k_s[:, ks] * jnp.exp(b_last - b))
            vr = _round_bf16(v_s[:, vs])
            s0 = s_ref[s, h]
            att = lax.dot_general(q_dec, k_inv, (((1,), (1,)), ((), ())), preferred_element_type=F32)
            att = _round_bf16(jnp.where(tril, att, 0.0))
            o = jnp.dot(att, vr, preferred_element_type=F32) + jnp.dot(q_dec, _round_bf16(s0), preferred_element_type=F32)
            upd = lax.dot_general(k_end, vr, (((0,), (0,)), ((), ())), preferred_element_type=F32)
            d_col = jnp.broadcast_to(jnp.exp(b_last), (dk, dk)).T
            ns_ref[s, h] = s0 * jnp.concatenate([d_col] * (dv // dk), axis=1) + upd
            o = _rms(o, gn_ref[:, vs]) * _silu(g_s[:, vs])
            o_ref[s, :, vs] = o.astype(o_ref.dtype)


def _gla_sample(q3, k3, v3, g3, la3, state, layer, gn, heads, dk, dv):
    nb, ln, w = v3.shape
    qkw = heads * dk
    bs = 8
    return pl.pallas_call(
        functools.partial(_gla_sample_kernel, heads=heads, dk=dk, dv=dv),
        out_shape=(jax.ShapeDtypeStruct((nb, ln, w), F32), jax.ShapeDtypeStruct((nb, heads, dk, dv), F32)),
        grid=(nb // bs,),
        in_specs=[
            pl.BlockSpec((bs, ln, qkw), lambda s: (s, 0, 0)),
            pl.BlockSpec((bs, ln, qkw), lambda s: (s, 0, 0)),
            pl.BlockSpec((bs, ln, w), lambda s: (s, 0, 0)),
            pl.BlockSpec((bs, ln, w), lambda s: (s, 0, 0)),
            pl.BlockSpec((bs, ln, qkw), lambda s: (s, 0, 0)),
            pl.BlockSpec((None, bs, heads, dk, dv), lambda s: (layer, s, 0, 0, 0)),
            pl.BlockSpec((1, w), lambda s: (0, 0)),
        ],
        out_specs=(
            pl.BlockSpec((bs, ln, w), lambda s: (s, 0, 0)),
            pl.BlockSpec((bs, heads, dk, dv), lambda s: (s, 0, 0, 0)),
        ),
        compiler_params=_params("arbitrary"),
        name="gla_sample",
    )(q3, k3, v3, g3, la3, state, gn.reshape(1, w))


def _outproj_kernel(cp_ref, op_ref, cs_ref, os_ref, x_ref, w_ref, gp, gs, y_ref, wb_ref, *, npt, dec_seq):
    i = pl.program_id(1)

    @pl.when(i == 0)
    def _():
        wb_ref[...] = w_ref[...].astype(BF16)

    half = cp_ref.shape[1]

    def mixed(c_ref, o_ref):
        return (jnp.dot(c_ref[...], wb_ref[0:half, :], preferred_element_type=F32)
                + jnp.dot(o_ref[...], wb_ref[half:, :], preferred_element_type=F32))

    @pl.when(i < npt)
    def _():
        y_ref[...] = x_ref[...] + gp[...] * mixed(cp_ref, op_ref)

    @pl.when(i >= npt)
    def _():
        mix = mixed(cs_ref, os_ref)
        nb = gs.shape[0]
        for t in range(dec_seq):
            rows = slice(t * nb, (t + 1) * nb)
            y_ref[rows, :] = x_ref[rows, :] + gs[...] * mix[rows]


def _outproj(conv_p, gla_p, conv_s, gla_s, x, w_out, layer, mod_p, mod_s, dims):
    t, d = x.shape
    half = conv_p.shape[1]
    tm, tn = ROW_TILE, 1024
    nj = d // tn
    npt = dims["npt"]
    p_specs, s_specs = _mod_specs((2,), tn, dims["npt"], dims["tps"], dims["n_seq"], dims["dec_batch"],
                                  row_axis=1, col_fn=lambda c, idx: c * nj + idx[0])
    return pl.pallas_call(
        functools.partial(_outproj_kernel, npt=dims["npt"], dec_seq=dims["dec_seq"]),
        out_shape=jax.ShapeDtypeStruct((t, d), F32),
        grid=(nj, t // tm),
        in_specs=[
            pl.BlockSpec((tm, half), lambda j, i: (jnp.minimum(i, npt - 1), 0)),
            pl.BlockSpec((tm, half), lambda j, i: (jnp.minimum(i, npt - 1), 0)),
            pl.BlockSpec((tm, half), lambda j, i: (0, 0)),
            pl.BlockSpec((tm, half), lambda j, i: (0, 0)),
            pl.BlockSpec((tm, tn), lambda j, i: (i, j)),
            pl.BlockSpec((None, d, tn), lambda j, i: (layer, 0, j)),
            *p_specs,
            *s_specs,
        ],
        out_specs=pl.BlockSpec((tm, tn), lambda j, i: (i, j)),
        scratch_shapes=[pltpu.VMEM((d, tn), BF16)],
        compiler_params=_params("arbitrary", "arbitrary"),
        name="outproj",
    )(conv_p, gla_p, conv_s, gla_s, x, w_out, mod_p, mod_s)


def _norm2_kernel(x_ref, g_ref, shp, scp, shs, scs, wr_hi, wr_lo, br_ref, hb_ref, ids_ref, wts_ref, h_ref, *, npt, dec_seq, n_grp, n_exp):
    i = pl.program_id(0)
    y = _rms(x_ref[...], g_ref[...])
    _store_by_group(i, npt, dec_seq, h_ref, lambda v, m: v[0] * (1.0 + m[1]) + m[0], [y], [shp, scp], [shs, scs])
    h = h_ref[...]
    h_hi = h.astype(BF16)
    hb_ref[...] = h_hi
    h_lo = (h - h_hi.astype(F32)).astype(BF16)
    logits = (jnp.dot(h_hi, wr_hi[...], preferred_element_type=F32) + jnp.dot(h_lo, wr_hi[...], preferred_element_type=F32)
              + jnp.dot(h_hi, wr_lo[...], preferred_element_type=F32)) + br_ref[...]
    lane = lax.broadcasted_iota(jnp.int32, logits.shape, 1).astype(F32)
    big = jnp.float32(LANES)
    neg = jnp.float32(-jnp.inf)
    gl = jnp.where(lane < n_grp, logits, neg)
    gmax = jnp.max(gl, axis=-1, keepdims=True)
    gidx = jnp.min(jnp.where(gl == gmax, lane, big), axis=-1, keepdims=True)
    g_w = 1.0 / jnp.sum(jnp.exp(gl - gmax), axis=-1, keepdims=True)
    lo = n_grp + gidx * n_exp
    in_grp = (lane >= lo) & (lane < lo + n_exp)
    sl = jnp.where(in_grp, logits, neg)
    p = jnp.exp(sl - jnp.max(sl, axis=-1, keepdims=True))
    p = p / jnp.sum(p, axis=-1, keepdims=True)
    p = jnp.where(in_grp, p, -1.0)
    p1 = jnp.max(p, axis=-1, keepdims=True)
    i1 = jnp.min(jnp.where(p == p1, lane, big), axis=-1, keepdims=True)
    p_rest = jnp.where(lane == i1, -1.0, p)
    p2 = jnp.max(p_rest, axis=-1, keepdims=True)
    i2 = jnp.min(jnp.where(p_rest == p2, lane, big), axis=-1, keepdims=True)
    denom = p1 + p2
    ids_ref[...] = jnp.where(lane == 0, i1 - n_grp, jnp.where(lane == 1, i2 - n_grp, 0.0)).astype(jnp.int32)
    wts_ref[...] = jnp.where(lane == 0, g_w * (p1 / denom), jnp.where(lane == 1, g_w * (p2 / denom), 0.0))


def _norm2(x, g, mod_p, mod_s, wr_hi, wr_lo, br, dims, n_grp, n_exp):
    t, d = x.shape
    tm = ROW_TILE
    p_specs, s_specs = _mod_specs((3, 4), d, dims["npt"], dims["tps"], dims["n_seq"], dims["dec_batch"])
    return pl.pallas_call(
        functools.partial(_norm2_kernel, npt=dims["npt"], dec_seq=dims["dec_seq"], n_grp=n_grp, n_exp=n_exp),
        out_shape=(jax.ShapeDtypeStruct((t, d), BF16), jax.ShapeDtypeStruct((t, LANES), jnp.int32),
                   jax.ShapeDtypeStruct((t, LANES), F32)),
        grid=(t // tm,),
        in_specs=[
            pl.BlockSpec((tm, d), lambda i: (i, 0)),
            pl.BlockSpec((1, d), lambda i: (0, 0)),
            *p_specs,
            *s_specs,
            pl.BlockSpec((d, LANES), lambda i: (0, 0)),
            pl.BlockSpec((d, LANES), lambda i: (0, 0)),
            pl.BlockSpec((1, LANES), lambda i: (0, 0)),
        ],
        out_specs=(pl.BlockSpec((tm, d), lambda i: (i, 0)), pl.BlockSpec((tm, LANES), lambda i: (i, 0)),
                   pl.BlockSpec((tm, LANES), lambda i: (i, 0))),
        scratch_shapes=[pltpu.VMEM((tm, d), F32)],
        compiler_params=_params("arbitrary"),
        name="norm2_router",
    )(x, g.reshape(1, d), mod_p, mod_p, mod_s, mod_s, wr_hi, wr_lo, br)


def _segments(ids, n_experts, n_tiles_max):
    t = ids.shape[0]
    ts, te, al = SORT_TILE, EXPERT_TILE, SEG_ALIGN
    nt = t // ts
    e_iota = jnp.arange(n_experts, dtype=jnp.int32)
    cnt = jnp.sum((ids.reshape(nt, ts * TOP_K, 1) == e_iota).astype(jnp.int32), axis=1)
    size = (cnt + al - 1) // al * al
    src = jnp.cumsum(size, axis=1) - size
    tot = jnp.sum(size, axis=0)
    tot_pad = (tot + te - 1) // te * te
    ends = jnp.cumsum(tot_pad)
    exp_off = ends - tot_pad
    dst = exp_off[None, :] + jnp.cumsum(size, axis=0) - size
    n_used = (ends[-1] // te).astype(jnp.int32)
    tile_start = jnp.arange(n_tiles_max, dtype=jnp.int32) * te
    tile_expert = jnp.minimum(jnp.sum(ends[None, :] <= tile_start[:, None], axis=1), n_experts - 1).astype(jnp.int32)
    last = tile_expert[jnp.maximum(n_used - 1, 0)]
    tile_expert = jnp.where(jnp.arange(n_tiles_max) < n_used, tile_expert, last)
    return dict(src=src.reshape(-1).astype(jnp.int32), dst=dst.reshape(-1).astype(jnp.int32),
                size=size.reshape(-1).astype(jnp.int32), tile_tot=jnp.sum(size, axis=1).astype(jnp.int32),
                fill_start=(exp_off + tot).astype(jnp.int32), fill_size=(tot_pad - tot).astype(jnp.int32),
                tile_expert=tile_expert, n_used=n_used.reshape(1))


def _lane_pick(x, lane, k):
    return jnp.sum(jnp.where(lane == k, x, 0.0), axis=-1, keepdims=True)


def _sorted_positions(ids, lane):
    ts = ids.shape[0]
    onehot = [(lane == _lane_pick(ids, lane, k)).astype(F32) for k in range(TOP_K)]
    row = lax.broadcasted_iota(jnp.int32, (ts, ts), 0)
    col = lax.broadcasted_iota(jnp.int32, (ts, ts), 1)
    before = (col < row).astype(BF16)
    earlier = [jnp.dot(before, o.astype(BF16), preferred_element_type=F32) for o in onehot]
    cnt = [jnp.sum(o, axis=0, keepdims=True) for o in onehot]
    size = jnp.ceil((cnt[0] + cnt[1]) * (1.0 / SEG_ALIGN)) * SEG_ALIGN
    er = lax.broadcasted_iota(jnp.int32, (LANES, LANES), 0)
    ec = lax.broadcasted_iota(jnp.int32, (LANES, LANES), 1)
    start = jnp.dot(jnp.broadcast_to(size, (SUBLANES, LANES)).astype(BF16), (er < ec).astype(BF16),
                    preferred_element_type=F32)[0:1]
    base = [start + earlier[0], start + cnt[0] + earlier[1]]
    return [jnp.sum(onehot[k] * base[k], axis=-1, keepdims=True) for k in range(TOP_K)]


def _as_row(col_vals, lane):
    hi = jnp.floor(col_vals * (1.0 / 32.0))
    lo = col_vals - hi * 32.0
    ones = jnp.ones((SUBLANES, LANES), BF16)
    nt = (((1,), (1,)), ((), ()))
    hi_row = lax.dot_general(ones, jnp.where(lane == 0, hi, 0.0).astype(BF16), nt, preferred_element_type=F32)
    lo_row = lax.dot_general(ones, jnp.where(lane == 0, lo, 0.0).astype(BF16), nt, preferred_element_type=F32)
    return (hi_row * 32.0 + lo_row)[0:1]


def _dispatch_kernel(src_ref, dst_ref, size_ref, tot_ref, fst_ref, fsz_ref, nu_ref, h_ref, ids_ref, wts_ref, xe_hbm,
                     we_hbm, pos_ref, sbuf, wbuf, zx, zw, semx, semw, semz, semt, *, n_experts):
    i = pl.program_id(0)
    n = pl.num_programs(0)
    ts = h_ref.shape[0]
    rows = sbuf.shape[1]
    te = zx.shape[0]
    slot = lax.rem(i, 2)

    def tail_copies(tl):
        r0 = pl.multiple_of(tl * te, te)
        return (pltpu.make_async_copy(zx, xe_hbm.at[pl.ds(r0, te), :], semt.at[0]),
                pltpu.make_async_copy(zw, we_hbm.at[pl.ds(r0, te), :], semt.at[1]))

    def for_tail(action):
        def body(tl, carry):
            for c in tail_copies(tl):
                action(c)
            return carry

        lax.fori_loop(nu_ref[0], xe_hbm.shape[0] // te, body, 0)

    def seg_copies(tile, sl, e):
        k = tile * n_experts + e
        sz = pl.multiple_of(size_ref[k], SEG_ALIGN)
        s0 = pl.multiple_of(src_ref[k], SEG_ALIGN)
        d0 = pl.multiple_of(dst_ref[k], SEG_ALIGN)
        return sz, (pltpu.make_async_copy(sbuf.at[sl, pl.ds(s0, sz), :], xe_hbm.at[pl.ds(d0, sz), :], semx.at[sl]),
                    pltpu.make_async_copy(wbuf.at[sl, pl.ds(s0, sz), :], we_hbm.at[pl.ds(d0, sz), :], semw.at[sl]))

    def wait_tile(tile, sl):
        tot = pl.multiple_of(tot_ref[tile], SEG_ALIGN)
        pltpu.make_async_copy(sbuf.at[sl, pl.ds(0, tot), :], xe_hbm.at[pl.ds(0, tot), :], semx.at[sl]).wait()
        pltpu.make_async_copy(wbuf.at[sl, pl.ds(0, tot), :], we_hbm.at[pl.ds(0, tot), :], semw.at[sl]).wait()

    @pl.when(i == 0)
    def _():
        zx[...] = jnp.zeros(zx.shape, zx.dtype)
        zw[...] = jnp.zeros(zw.shape, zw.dtype)

        def fill(e, carry):
            sz = pl.multiple_of(fsz_ref[e], SEG_ALIGN)
            d0 = pl.multiple_of(fst_ref[e], SEG_ALIGN)

            @pl.when(sz > 0)
            def _():
                cx = pltpu.make_async_copy(zx.at[pl.ds(0, sz), :], xe_hbm.at[pl.ds(d0, sz), :], semz.at[0])
                cw = pltpu.make_async_copy(zw.at[pl.ds(0, sz), :], we_hbm.at[pl.ds(d0, sz), :], semz.at[1])
                cx.start()
                cw.start()
                cx.wait()
                cw.wait()

            return carry

        lax.fori_loop(0, n_experts, fill, 0)
        for_tail(lambda c: c.start())

    @pl.when(i >= 2)
    def _():
        wait_tile(i - 2, slot)

    lane = lax.broadcasted_iota(jnp.int32, (ts, LANES), 1).astype(F32)
    pos = _sorted_positions(ids_ref[...].astype(F32), lane)
    pos_ref[...] = jnp.where(lane == 0, pos[0], jnp.where(lane == 1, pos[1], 0.0))
    r_iota = lax.broadcasted_iota(jnp.int32, (rows, ts), 0).astype(F32)
    assert rows <= 1024
    sel = [(r_iota == _as_row(p, lane)).astype(BF16) for p in pos]
    sbuf[slot] = jnp.dot(sel[0] + sel[1], h_ref[...], preferred_element_type=F32).astype(BF16)
    wts = wts_ref[...]
    wsorted = jnp.zeros((rows, LANES), F32)
    for k in range(TOP_K):
        pieces = _split3(_lane_pick(wts, lane, k))
        wk = jnp.where(lane == 0, pieces[0].astype(F32), jnp.where(lane == 1, pieces[1].astype(F32),
                       jnp.where(lane == 2, pieces[2].astype(F32), 0.0)))
        wsorted = wsorted + jnp.dot(sel[k], wk.astype(BF16), preferred_element_type=F32)
    wbuf[slot] = wsorted

    def issue(e, carry):
        sz, copies = seg_copies(i, slot, e)

        @pl.when(sz > 0)
        def _():
            for c in copies:
                c.start()

        return carry

    lax.fori_loop(0, n_experts, issue, 0)

    @pl.when(i == n - 1)
    def _():
        @pl.when(i >= 1)
        def _():
            wait_tile(i - 1, 1 - slot)

        wait_tile(i, slot)
        for_tail(lambda c: c.wait())


def _dispatch_rows(hb, ids, wts, seg, n_experts, n_tiles_max):
    t, d = hb.shape
    ts, te = SORT_TILE, EXPERT_TILE
    rows = TOP_K * ts + n_experts * SEG_ALIGN
    n_slots = n_tiles_max * te

    def tile_map(i, *_):
        return (i, 0)

    return pl.pallas_call(
        functools.partial(_dispatch_kernel, n_experts=n_experts),
        out_shape=(jax.ShapeDtypeStruct((n_slots, d), BF16), jax.ShapeDtypeStruct((n_slots, LANES), F32),
                   jax.ShapeDtypeStruct((t, LANES), F32)),
        grid_spec=pltpu.PrefetchScalarGridSpec(
            num_scalar_prefetch=7,
            grid=(t // ts,),
            in_specs=[pl.BlockSpec((ts, d), tile_map), pl.BlockSpec((ts, LANES), tile_map),
                      pl.BlockSpec((ts, LANES), tile_map)],
            out_specs=(pl.BlockSpec(memory_space=pl.ANY), pl.BlockSpec(memory_space=pl.ANY),
                       pl.BlockSpec((ts, LANES), tile_map)),
            scratch_shapes=[
                pltpu.VMEM((2, rows, d), BF16),
                pltpu.VMEM((2, rows, LANES), F32),
                pltpu.VMEM((te, d), BF16),
                pltpu.VMEM((te, LANES), F32),
                pltpu.SemaphoreType.DMA((2,)),
                pltpu.SemaphoreType.DMA((2,)),
                pltpu.SemaphoreType.DMA((2,)),
                pltpu.SemaphoreType.DMA((2,)),
            ],
        ),
        compiler_params=_params("arbitrary"),
        name="dispatch",
    )(seg["src"], seg["dst"], seg["size"], seg["tile_tot"], seg["fill_start"], seg["fill_size"], seg["n_used"],
      hb, ids, wts)


def _expert_rows_kernel(te_ref, nu_ref, x_ref, w_ref, wg_ref, wu_ref, wd_ref, o_ref, wgb, wub, wdb):
    i = pl.program_id(0)
    n_used = nu_ref[0]

    @pl.when(i < n_used)
    def _():
        changed = jnp.logical_or(i == 0, te_ref[i] != te_ref[jnp.maximum(i - 1, 0)])

        @pl.when(changed)
        def _():
            wgb[...] = wg_ref[...].astype(BF16)
            wub[...] = wu_ref[...].astype(BF16)
            wdb[...] = wd_ref[...].astype(BF16)

        x = x_ref[...]
        a = jnp.dot(x, wgb[...], preferred_element_type=F32)
        u = jnp.dot(x, wub[...], preferred_element_type=F32)
        w = jnp.sum(w_ref[...], axis=-1, keepdims=True)
        hid = _silu(a) * u * w
        o_ref[...] = jnp.dot(hid.astype(BF16), wdb[...], preferred_element_type=F32).astype(o_ref.dtype)

    @pl.when(i >= n_used)
    def _():
        o_ref[...] = jnp.zeros(o_ref.shape, o_ref.dtype)


def _expert_rows(xe, we, seg, wg, wu, wd, layer_base):
    n_slots, d = xe.shape
    f = wg.shape[-1]
    te = EXPERT_TILE

    def w_map(i, te_ref, nu_ref):
        return (layer_base + te_ref[i], 0, 0)

    def row_map(i, te_ref, nu_ref):
        return (jnp.minimum(i, nu_ref[0] - 1), 0)

    return pl.pallas_call(
        _expert_rows_kernel,
        out_shape=jax.ShapeDtypeStruct((n_slots, d), BF16),
        grid_spec=pltpu.PrefetchScalarGridSpec(
            num_scalar_prefetch=2,
            grid=(n_slots // te,),
            in_specs=[
                pl.BlockSpec((te, d), row_map),
                pl.BlockSpec((te, LANES), row_map),
                pl.BlockSpec((None, d, f), w_map),
                pl.BlockSpec((None, d, f), w_map),
                pl.BlockSpec((None, f, d), w_map),
            ],
            out_specs=pl.BlockSpec((te, d), lambda i, te_ref, nu_ref: (i, 0)),
            scratch_shapes=[pltpu.VMEM((d, f), BF16), pltpu.VMEM((d, f), BF16), pltpu.VMEM((f, d), BF16)],
        ),
        compiler_params=_params("arbitrary"),
        name="experts",
    )(seg["tile_expert"], seg["n_used"], xe, we, wg, wu, wd)


def _collect_kernel(src_ref, dst_ref, size_ref, tot_ref, y_hbm, x_ref, pos_ref, gp, gs, o_ref, buf, sem, *, npt, n_experts):
    i = pl.program_id(0)
    n = pl.num_programs(0)
    ts = x_ref.shape[0]
    rows = buf.shape[1]
    slot = lax.rem(i, 2)

    def start(tile, sl):
        def body(e, carry):
            k = tile * n_experts + e
            sz = pl.multiple_of(size_ref[k], SEG_ALIGN)
            s0 = pl.multiple_of(src_ref[k], SEG_ALIGN)
            d0 = pl.multiple_of(dst_ref[k], SEG_ALIGN)

            @pl.when(sz > 0)
            def _():
                pltpu.make_async_copy(y_hbm.at[pl.ds(d0, sz), :], buf.at[sl, pl.ds(s0, sz), :], sem.at[sl]).start()

            return carry

        lax.fori_loop(0, n_experts, body, 0)

    @pl.when(i == 0)
    def _():
        buf[...] = jnp.zeros(buf.shape, buf.dtype)
        start(0, 0)

    @pl.when(i + 1 < n)
    def _():
        start(i + 1, 1 - slot)

    tot = pl.multiple_of(tot_ref[i], SEG_ALIGN)
    pltpu.make_async_copy(y_hbm.at[pl.ds(0, tot), :], buf.at[slot, pl.ds(0, tot), :], sem.at[slot]).wait()
    lane = lax.broadcasted_iota(jnp.int32, (ts, LANES), 1).astype(F32)
    pos = pos_ref[...]
    r_iota = lax.broadcasted_iota(jnp.int32, (ts, rows), 1).astype(F32)
    pick = ((r_iota == _lane_pick(pos, lane, 0)) | (r_iota == _lane_pick(pos, lane, 1))).astype(BF16)
    ff = jnp.dot(pick, buf[slot], preferred_element_type=F32)
    _store_by_group(i, npt, None, o_ref, lambda v, m: v[0] + m[0] * v[1], [x_ref[...], ff], [gp], [gs])


def _collect(ye, pos, x, seg, mod_p, mod_s, dims, n_experts):
    t, d = x.shape
    ts = SORT_TILE
    rows = TOP_K * ts + n_experts * SEG_ALIGN
    p_specs, s_specs = _mod_specs((5,), d, dims["npt"], dims["tps"], dims["n_seq"], dims["dec_batch"])

    def strip(spec):
        return pl.BlockSpec(spec.block_shape, lambda i, *_, m=spec.index_map: m(i))

    def tile_map(i, *_):
        return (i, 0)

    return pl.pallas_call(
        functools.partial(_collect_kernel, npt=dims["npt"], n_experts=n_experts),
        out_shape=jax.ShapeDtypeStruct((t, d), F32),
        grid_spec=pltpu.PrefetchScalarGridSpec(
            num_scalar_prefetch=4,
            grid=(t // ts,),
            in_specs=[
                pl.BlockSpec(memory_space=pl.ANY),
                pl.BlockSpec((ts, d), tile_map),
                pl.BlockSpec((ts, LANES), tile_map),
                *[strip(s) for s in p_specs],
                *[strip(s) for s in s_specs],
            ],
            out_specs=pl.BlockSpec((ts, d), tile_map),
            scratch_shapes=[pltpu.VMEM((2, rows, d), BF16), pltpu.SemaphoreType.DMA((2,))],
        ),
        compiler_params=_params("arbitrary"),
        name="collect",
    )(seg["src"], seg["dst"], seg["size"], seg["tile_tot"], ye, x, pos, mod_p, mod_s)


def _final_norm_kernel(x_ref, g_ref, op_ref, os_ref, *, npt):
    i = pl.program_id(0)
    y = _rms(x_ref[...], g_ref[...])

    @pl.when(i < npt)
    def _():
        op_ref[...] = y

    @pl.when(i >= npt)
    def _():
        os_ref[...] = y


def _final_norm(x, g, npt):
    t, d = x.shape
    tm = ROW_TILE
    return pl.pallas_call(
        functools.partial(_final_norm_kernel, npt=npt),
        out_shape=(jax.ShapeDtypeStruct((npt * tm, d), F32), jax.ShapeDtypeStruct((t - npt * tm, d), F32)),
        grid=(t // tm,),
        in_specs=[pl.BlockSpec((tm, d), lambda i: (i, 0)), pl.BlockSpec((1, d), lambda i: (0, 0))],
        out_specs=(pl.BlockSpec((tm, d), lambda i: (jnp.minimum(i, npt - 1), 0)),
                   pl.BlockSpec((tm, d), lambda i: (jnp.maximum(i - npt, 0), 0))),
        compiler_params=_params("arbitrary"),
        name="final_norm",
    )(x, g.reshape(1, d))


def kernel(x_prompt, x_sample, c_prompt, c_sample, state_conv, state_gla, w_ada, b_ada, norm1_g, norm2_g, w_in, conv_w, conv_b, conv_ln_g, conv_ln_b, gate_w2, gate_b, gla_norm_g, w_out, router_grp_w, router_grp_b, router_exp_w, router_exp_b, exp_w_gate, exp_w_up, exp_w_down, final_norm_g):
    n_seq, seq_len, d = x_prompt.shape
    dec_batch, dec_seq, _ = x_sample.shape
    depth = w_ada.shape[0]
    kw, d_conv = conv_w.shape[1:]
    heads, dv = gla_norm_g.shape[1:]
    rank, qkw = gate_w2.shape[1:]
    dk = qkw // heads
    n_grp, n_exp = router_exp_w.shape[2:]
    n_experts = n_grp * n_exp
    tp, ts = n_seq * seq_len, dec_batch * dec_seq
    t = tp + ts
    tm = ROW_TILE
    assert ts == tm and seq_len % tm == 0 and d_conv == heads * dv and kw - 1 <= CONV_HALO
    n_main = 2 * d_conv + 2 * qkw + 2 * heads * dv
    dims = dict(npt=tp // tm, tps=seq_len // tm, n_seq=n_seq, dec_batch=dec_batch, dec_seq=dec_seq)

    x = jnp.concatenate([x_prompt.reshape(tp, d), x_sample.transpose(1, 0, 2).reshape(ts, d)], axis=0)
    pad = (-n_seq) % 8
    c_all = jnp.concatenate([c_prompt, jnp.zeros((pad, d), F32), c_sample], axis=0)
    mod = _ada(c_all, w_ada, b_ada)
    st = SORT_TILE
    assert tm % st == 0 and st % dec_batch == 0
    dims_sort = dict(npt=tp // st, tps=seq_len // st, n_seq=n_seq, dec_batch=dec_batch, dec_seq=dec_seq)
    n_tiles_max = -(-(TOP_K * t + (t // st) * n_experts * (SEG_ALIGN - 1)) // EXPERT_TILE) + n_experts

    conv_p, gla_p, conv_s, gla_s = [], [], [], []
    for l in range(depth):
        mod_p = mod[l, :n_seq].reshape(n_seq, 1, 6 * d)
        mod_s = mod[l, n_seq + pad:]
        gw2 = jnp.pad(gate_w2[l], ((0, LANES - rank), (0, 0))).astype(BF16)
        h, la = _norm1(x, norm1_g[l], mod_p, mod_s, w_in, l, n_main, rank, gw2, gate_b[l], dims)
        z = _inproj(h, w_in, l, n_main)

        cv_p, cb_p = _conv_prompt(z, conv_w[l], conv_b[l], conv_ln_g[l], conv_ln_b[l], n_seq, seq_len)
        cv_s, cb_s = _conv_sample(z, state_conv[l].transpose(1, 0, 2), conv_w[l], conv_b[l], conv_ln_g[l],
                                  conv_ln_b[l], tp, dec_seq, dec_batch)
        conv_p.append(cb_p)
        conv_s.append(cb_s.transpose(1, 0, 2))

        go_p, gs_p = _gla_prompt(z, la, gla_norm_g[l], n_seq, seq_len, heads, dk, dv)

        def seq_major(a):
            return a.reshape(dec_seq, dec_batch, a.shape[-1]).transpose(1, 0, 2)

        zs = z[tp:]
        q3 = seq_major(zs[:, 2 * d_conv:2 * d_conv + qkw])
        k3 = seq_major(zs[:, 2 * d_conv + qkw:2 * d_conv + 2 * qkw])
        v3 = seq_major(zs[:, 2 * d_conv + 2 * qkw:2 * d_conv + 2 * qkw + heads * dv])
        g3 = seq_major(zs[:, 2 * d_conv + 2 * qkw + heads * dv:n_main])
        go_s, gs_s = _gla_sample(q3, k3, v3, g3, seq_major(la[tp:]), state_gla, l, gla_norm_g[l], heads, dk, dv)
        go_s = go_s.transpose(1, 0, 2).reshape(ts, heads * dv).astype(BF16)
        gla_p.append(gs_p)
        gla_s.append(gs_s)

        x = _outproj(cv_p, go_p, cv_s.reshape(ts, d_conv), go_s, x, w_out, l, mod_p, mod_s, dims)

        wr = jnp.concatenate([router_grp_w[l], router_exp_w[l].reshape(d, n_experts)], axis=1)
        wr = jnp.pad(wr, ((0, 0), (0, LANES - wr.shape[1])))
        wr_hi = wr.astype(BF16)
        wr_lo = (wr - wr_hi.astype(F32)).astype(BF16)
        br = jnp.concatenate([router_grp_b[l], router_exp_b[l].reshape(-1)])
        br = jnp.pad(br, (0, LANES - br.shape[0])).reshape(1, LANES)
        hb, ids, wts = _norm2(x, norm2_g[l], mod_p, mod_s, wr_hi, wr_lo, br, dims, n_grp, n_exp)
        seg = _segments(ids[:, :TOP_K], n_experts, n_tiles_max)
        xe, we, pos = _dispatch_rows(hb, ids, wts, seg, n_experts, n_tiles_max)
        f = exp_w_gate.shape[-1]
        ye = _expert_rows(xe, we, seg, exp_w_gate.reshape(depth * n_experts, d, f),
                          exp_w_up.reshape(depth * n_experts, d, f), exp_w_down.reshape(depth * n_experts, f, d),
                          l * n_experts)
        x = _collect(ye, pos, x, seg, mod_p, mod_s, dims_sort, n_experts)

    y_p, y_s = _final_norm(x, final_norm_g, dims["npt"])
    y_prompt = y_p.reshape(n_seq, seq_len, d)
    y_sample = y_s.reshape(dec_seq, dec_batch, d).transpose(1, 0, 2)
    return (y_prompt, y_sample, jnp.stack(conv_p), jnp.stack(gla_p), jnp.stack(conv_s), jnp.stack(gla_s))
```

```python
import functools

import jax
import jax.numpy as jnp
from jax import lax
from jax.experimental import pallas as pl
from jax.experimental.pallas import tpu as pltpu

F32 = jnp.float32
BF16 = jnp.bfloat16

EPS = 1e-6
GATE_TAU = 16.0
GLA_CHUNK = 32
TOP_K = 2

ROW_TILE = 512
EXPERT_TILE = 512
SORT_TILE = 256
SEG_ALIGN = 16
CONV_ROWS = 64
CONV_HALO = 32
LANES = 128
SUBLANES = 8
VMEM_LIMIT = 56 * 1024 * 1024


def _params(*sem):
    return pltpu.CompilerParams(dimension_semantics=sem, vmem_limit_bytes=VMEM_LIMIT)


def _bdot(a, b):
    return jnp.dot(a.astype(BF16), b.astype(BF16), preferred_element_type=F32)


def _round_bf16(x):
    return x.astype(BF16).astype(F32)


def _split3(x):
    hi = x.astype(BF16)
    r = x - hi.astype(F32)
    mid = r.astype(BF16)
    lo = (r - mid.astype(F32)).astype(BF16)
    return hi, mid, lo


def _silu(x):
    return x * jax.nn.sigmoid(x)


def _store_by_group(i, n_prompt_tiles, dec_seq, out_ref, fn, vals, p_refs, s_refs):
    @pl.when(i < n_prompt_tiles)
    def _():
        out_ref[...] = fn(vals, [r[...] for r in p_refs]).astype(out_ref.dtype)

    @pl.when(i >= n_prompt_tiles)
    def _():
        mods = [r[...] for r in s_refs]
        nb = mods[0].shape[0]
        for t in range(out_ref.shape[0] // nb):
            rows = slice(t * nb, (t + 1) * nb)
            out_ref[rows, :] = fn([v[rows] for v in vals], mods).astype(out_ref.dtype)


def _ada_kernel(c_ref, w_ref, b_ref, o_ref):
    c = c_ref[...]
    o_ref[...] = _bdot(_silu(c), w_ref[...]) + b_ref[...]


def _ada(c_all, w_ada, b_ada):
    depth, d, n = w_ada.shape
    rows = c_all.shape[0]
    tn = 1024
    return pl.pallas_call(
        _ada_kernel,
        out_shape=jax.ShapeDtypeStruct((depth, rows, n), F32),
        grid=(depth, n // tn),
        in_specs=[
            pl.BlockSpec((rows, d), lambda l, j: (0, 0)),
            pl.BlockSpec((None, d, tn), lambda l, j: (l, 0, j)),
            pl.BlockSpec((None, 1, tn), lambda l, j: (l, 0, j)),
        ],
        out_specs=pl.BlockSpec((None, rows, tn), lambda l, j: (l, 0, j)),
        compiler_params=_params("arbitrary", "arbitrary"),
        name="ada",
    )(c_all, w_ada, b_ada.reshape(depth, 1, n))


def _mod_specs(cols, width, n_prompt_tiles, tiles_per_seq, n_seq, dec_batch, grid_rank=1, row_axis=0, col_fn=None):
    p_specs, s_specs = [], []
    for c in cols:
        def p_map(*idx, c=c):
            b = jnp.minimum(idx[row_axis] // tiles_per_seq, n_seq - 1)
            return (b, 0, c if col_fn is None else col_fn(c, idx))

        def s_map(*idx, c=c):
            return (0, c if col_fn is None else col_fn(c, idx))

        p_specs.append(pl.BlockSpec((None, 1, width), p_map))
        s_specs.append(pl.BlockSpec((dec_batch, width), s_map))
    return p_specs, s_specs


def _rms(x, g):
    return x * lax.rsqrt(jnp.mean(x * x, axis=-1, keepdims=True) + EPS) * g


def _norm1_kernel(xp_ref, xs_ref, g_ref, shp, scp, shs, scs, wgl_ref, gw2_ref, gb_ref, h_ref, la_ref, *, npt, dec_seq, rank):
    i = pl.program_id(0)
    y = _rms(jnp.where(i < npt, xp_ref[...], xs_ref[...]), g_ref[...])
    _store_by_group(i, npt, dec_seq, h_ref, lambda v, m: v[0] * (1.0 + m[1]) + m[0], [y], [shp, scp], [shs, scs])
    row = lax.broadcasted_iota(jnp.int32, wgl_ref.shape, 0)
    w_gl = jnp.where(row < rank, wgl_ref[...], 0.0).astype(BF16)
    gate_lr = lax.dot_general(h_ref[...], w_gl, (((1,), (1,)), ((), ())), preferred_element_type=F32)
    pre = _bdot(gate_lr, gw2_ref[...]) + gb_ref[...]
    la_ref[...] = (jnp.minimum(pre, 0.0) - jnp.log1p(jnp.exp(-jnp.abs(pre)))) * (1.0 / GATE_TAU)


def _norm1(xp, xs, xs_blk, g, mod_p, mod_s, w_in_t, layer, n_main, rank, gw2, gb, dims):
    d = xp.shape[1]
    tm = ROW_TILE
    npt = dims["npt"]
    t = (npt + 1) * tm
    p_specs, s_specs = _mod_specs((0, 1), d, npt, dims["tps"], dims["n_seq"], dims["dec_batch"])
    qk = gw2.shape[1]
    assert n_main % LANES == 0 and rank <= LANES
    return pl.pallas_call(
        functools.partial(_norm1_kernel, npt=npt, dec_seq=dims["dec_seq"], rank=rank),
        out_shape=(jax.ShapeDtypeStruct((t, d), BF16), jax.ShapeDtypeStruct((t, qk), F32)),
        grid=(t // tm,),
        in_specs=[
            pl.BlockSpec((tm, d), lambda i: (jnp.minimum(i, npt - 1), 0)),
            pl.BlockSpec((tm, d), lambda i: (xs_blk, 0)),
            pl.BlockSpec((1, d), lambda i: (0, 0)),
            *p_specs,
            *s_specs,
            pl.BlockSpec((None, LANES, d), lambda i: (layer, n_main // LANES, 0)),
            pl.BlockSpec(gw2.shape, lambda i: (0, 0)),
            pl.BlockSpec((1, qk), lambda i: (0, 0)),
        ],
        out_specs=(pl.BlockSpec((tm, d), lambda i: (i, 0)), pl.BlockSpec((tm, qk), lambda i: (i, 0))),
        compiler_params=_params("arbitrary"),
        name="norm1",
    )(xp, xs, g.reshape(1, d), mod_p, mod_p, mod_s, mod_s, w_in_t, gw2, gb.reshape(1, qk))


def _inproj_kernel(h_ref, w_ref, o_ref, wb_ref):
    @pl.when(pl.program_id(1) == 0)
    def _():
        wb_ref[...] = w_ref[...].astype(BF16)

    o_ref[...] = lax.dot_general(h_ref[...], wb_ref[...], (((1,), (1,)), ((), ())), preferred_element_type=F32)


def _inproj(h, w_in_t, layer, n_cols):
    t, d = h.shape
    tm, tn = ROW_TILE, 1024
    return pl.pallas_call(
        _inproj_kernel,
        out_shape=jax.ShapeDtypeStruct((t, n_cols), F32),
        grid=(n_cols // tn, t // tm),
        in_specs=[
            pl.BlockSpec((tm, d), lambda j, i: (i, 0)),
            pl.BlockSpec((None, tn, d), lambda j, i: (layer, j, 0)),
        ],
        out_specs=pl.BlockSpec((tm, tn), lambda j, i: (i, j)),
        scratch_shapes=[pltpu.VMEM((tn, d), BF16)],
        compiler_params=_params("arbitrary", "arbitrary"),
        name="inproj",
    )(h, w_in_t)


def _ln_silu(y, g, b):
    mu = jnp.mean(y, axis=-1, keepdims=True)
    yc = y - mu
    var = jnp.mean(yc * yc, axis=-1, keepdims=True)
    return _silu(yc * lax.rsqrt(var + EPS) * g + b)


def _conv_prompt_kernel(a_ref, b_ref, cw_ref, cb_ref, lng_ref, lnb_ref, o_ref, st_ref, full_ref, cwb_ref, y_ref, *, kw):
    j = pl.program_id(1)
    tm, c = a_ref.shape
    halo = CONV_HALO
    phases = full_ref.shape[0]
    assert phases == SUBLANES

    @pl.when(jnp.logical_and(pl.program_id(0) == 0, j == 0))
    def _():
        full_ref[...] = jnp.zeros(full_ref.shape, F32)
        for w in range(kw):
            cwb_ref[w] = jnp.broadcast_to(cw_ref[w:w + 1, :], (SUBLANES, c))

    prev = full_ref[0, tm + halo - SUBLANES:tm + halo, :]
    tail = jnp.where(j == 0, 0.0, prev)

    @pl.when(j == 0)
    def _():
        for p in range(phases):
            full_ref[p, 0:halo, :] = jnp.zeros((halo, c), F32)

    @pl.when(j > 0)
    def _():
        for p in range(phases):
            full_ref[p, 0:halo, :] = full_ref[p, tm:tm + halo, :]

    u = a_ref[...] * jax.nn.sigmoid(b_ref[...])
    full_ref[0, halo:halo + tm, :] = u
    ext = jnp.concatenate([tail, u], axis=0)
    for p in range(1, phases):
        full_ref[p, halo - SUBLANES:halo - SUBLANES + tm, :] = pltpu.roll(ext, tm + SUBLANES - p, 0)[0:tm]
    off = halo - (kw - 1)
    rb = CONV_ROWS

    def body(r, carry):
        r0 = pl.multiple_of(r * rb, rb)
        for lt in range(c // LANES):
            cols = slice(lt * LANES, (lt + 1) * LANES)
            acc = None
            for p in range(phases):
                x = full_ref[p, pl.ds(r0, rb + halo), cols]
                for a in range(halo // phases + 1):
                    w = a * phases + p - off
                    if 0 <= w < kw and a * phases + rb <= rb + halo:
                        term = x[a * phases:a * phases + rb] * jnp.concatenate([cwb_ref[w, :, cols]] * (rb // SUBLANES), axis=0)
                        acc = term if acc is None else acc + term
            y_ref[pl.ds(r0, rb), cols] = acc
        return carry

    lax.fori_loop(0, tm // rb, body, 0)
    y = _ln_silu(y_ref[...] + cb_ref[...], lng_ref[...], lnb_ref[...])
    o_ref[...] = y.astype(o_ref.dtype)

    @pl.when(j == pl.num_programs(1) - 1)
    def _():
        st_ref[...] = full_ref[0, halo + tm - (kw - 1):halo + tm, :]


def _conv_prompt(z, cw, cb, lng, lnb, n_seq, seq_len):
    kw, c = cw.shape
    tm = ROW_TILE
    tps = seq_len // tm
    return pl.pallas_call(
        functools.partial(_conv_prompt_kernel, kw=kw),
        out_shape=(jax.ShapeDtypeStruct((n_seq * seq_len, c), BF16), jax.ShapeDtypeStruct((n_seq, kw - 1, c), F32)),
        grid=(n_seq, tps),
        in_specs=[
            pl.BlockSpec((tm, c), lambda b, j: (b * tps + j, 0)),
            pl.BlockSpec((tm, c), lambda b, j: (b * tps + j, 1)),
            pl.BlockSpec((kw, c), lambda b, j: (0, 0)),
            pl.BlockSpec((1, c), lambda b, j: (0, 0)),
            pl.BlockSpec((1, c), lambda b, j: (0, 0)),
            pl.BlockSpec((1, c), lambda b, j: (0, 0)),
        ],
        out_specs=(
            pl.BlockSpec((tm, c), lambda b, j: (b * tps + j, 0)),
            pl.BlockSpec((None, kw - 1, c), lambda b, j: (b, 0, 0)),
        ),
        scratch_shapes=[pltpu.VMEM((SUBLANES, tm + CONV_HALO, c), F32), pltpu.VMEM((kw, SUBLANES, c), F32),
                        pltpu.VMEM((tm, c), F32)],
        compiler_params=_params("arbitrary", "arbitrary"),
        name="conv_prompt",
    )(z, z, cw, cb.reshape(1, c), lng.reshape(1, c), lnb.reshape(1, c))


def _conv_sample_kernel(*refs, kw, dec_seq):
    a_refs = refs[0:dec_seq]
    b_refs = refs[dec_seq:2 * dec_seq]
    st_ref, cw_ref, cb_ref, lng_ref, lnb_ref, o_ref, u_ref = refs[2 * dec_seq:]
    hist = kw - 1
    u = [a_refs[t][...] * jax.nn.sigmoid(b_refs[t][...]) for t in range(dec_seq)]
    def row(j):
        return st_ref[j] if j < hist else u[j - hist]

    for t in range(dec_seq):
        acc = row(t) * cw_ref[0:1, :]
        for w in range(1, kw):
            acc = acc + row(t + w) * cw_ref[w:w + 1, :]
        y = _ln_silu(acc + cb_ref[...], lng_ref[...], lnb_ref[...])
        o_ref[t] = y.astype(o_ref.dtype)
        u_ref[t] = u[t]


def _conv_sample(z, st_t, layer, cw, cb, lng, lnb, row0, dec_seq, dec_batch):
    kw, c = cw.shape
    bs = 16
    a_specs = [pl.BlockSpec((bs, c), lambda s, t=t: ((row0 + t * dec_batch) // bs + s, 0)) for t in range(dec_seq)]
    b_specs = [pl.BlockSpec((bs, c), lambda s, t=t: ((row0 + t * dec_batch) // bs + s, 1)) for t in range(dec_seq)]
    vec = pl.BlockSpec((1, c), lambda s: (0, 0))
    return pl.pallas_call(
        functools.partial(_conv_sample_kernel, kw=kw, dec_seq=dec_seq),
        out_shape=(jax.ShapeDtypeStruct((dec_seq, dec_batch, c), BF16), jax.ShapeDtypeStruct((dec_seq, dec_batch, c), F32)),
        grid=(dec_batch // bs,),
        in_specs=[*a_specs, *b_specs, pl.BlockSpec((None, kw - 1, bs, c), lambda s: (layer, 0, s, 0)),
                  pl.BlockSpec((kw, c), lambda s: (0, 0)), vec, vec, vec],
        out_specs=(pl.BlockSpec((dec_seq, bs, c), lambda s: (0, s, 0)), pl.BlockSpec((dec_seq, bs, c), lambda s: (0, s, 0))),
        compiler_params=_params("arbitrary"),
        name="conv_sample",
    )(*([z] * (2 * dec_seq)), st_t, cw, cb.reshape(1, c), lng.reshape(1, c), lnb.reshape(1, c))


def _conv_state_kernel(st_ref, u_ref, o_ref):
    hist, n_new = st_ref.shape[0], u_ref.shape[0]
    for j in range(hist - n_new):
        o_ref[j] = st_ref[j + n_new]
    for t in range(n_new):
        o_ref[hist - n_new + t] = u_ref[t]


def _conv_state_sample(st_t, u_all):
    depth, hist, nb, c = st_t.shape
    n_new = u_all.shape[1]
    assert n_new <= hist
    bs = 16
    return pl.pallas_call(
        _conv_state_kernel,
        out_shape=jax.ShapeDtypeStruct(st_t.shape, st_t.dtype),
        grid=(depth, nb // bs),
        in_specs=[pl.BlockSpec((None, hist, bs, c), lambda l, s: (l, 0, s, 0)),
                  pl.BlockSpec((None, n_new, bs, c), lambda l, s: (l, 0, s, 0))],
        out_specs=pl.BlockSpec((None, hist, bs, c), lambda l, s: (l, 0, s, 0)),
        compiler_params=_params("arbitrary", "arbitrary"),
        name="conv_state",
    )(st_t, u_all)


def _gla_prompt_kernel(qk_ref, v_ref, g_ref, la_ref, gn_ref, o_ref, sfin_ref, st_ref, sn_ref, *, heads, dk, dv):
    j = pl.program_id(1)
    tm = qk_ref.shape[0]
    ck = GLA_CHUNK
    nch = tm // ck
    qkw = heads * dk

    @pl.when(j == 0)
    def _():
        st_ref[...] = jnp.zeros(st_ref.shape, F32)

    la = la_ref[...]
    row = lax.broadcasted_iota(jnp.int32, (tm, tm), 0)
    col = lax.broadcasted_iota(jnp.int32, (tm, tm), 1)
    same = (row // ck) == (col // ck)
    causal = same & (col <= row)
    tri_incl = causal.astype(BF16)
    tri_after = (same & (col > row)).astype(BF16)
    sel = (lax.broadcasted_iota(jnp.int32, (nch, tm), 1) // ck == lax.broadcasted_iota(jnp.int32, (nch, tm), 0)).astype(BF16)
    parts = _split3(la)
    b = sum(jnp.dot(tri_incl, p, preferred_element_type=F32) for p in parts)
    rest = sum(jnp.dot(tri_after, p, preferred_element_type=F32) for p in parts)
    tot = sum(jnp.dot(sel, p, preferred_element_type=F32) for p in parts)
    qk = qk_ref[...]
    q = qk[:, :qkw] * (dk ** -0.5)
    k = qk[:, qkw:]
    q_dec = (q * jnp.exp(b)).astype(BF16)
    k_inv = (k * jnp.exp(-b)).astype(BF16)
    k_end = _round_bf16(k * jnp.exp(rest))
    decay = jnp.exp(tot)
    v_all = v_ref[...]
    g_all = g_ref[...]
    for h in range(heads):
        ks = slice(h * dk, (h + 1) * dk)
        vs = slice(h * dv, (h + 1) * dv)
        qd, ki, ke = q_dec[:, ks], k_inv[:, ks], k_end[:, ks]
        vh = v_all[:, vs]
        vb = vh.astype(BF16)
        vr = _round_bf16(vh)
        att = lax.dot_general(qd, ki, (((1,), (1,)), ((), ())), preferred_element_type=F32)
        att = jnp.where(causal, att, 0.0).astype(BF16)
        o = jnp.dot(att, vb, preferred_element_type=F32)
        s = st_ref[h]
        for n in range(nch):
            rs = slice(n * ck, (n + 1) * ck)
            sn_ref[n] = s.astype(BF16)
            upd = lax.dot_general(vr[rs], ke[rs], (((0,), (0,)), ((), ())), preferred_element_type=F32)
            s = s * decay[n:n + 1, ks] + upd
        st_ref[h] = s
        inter = [lax.dot_general(qd[n * ck:(n + 1) * ck], sn_ref[n], (((1,), (1,)), ((), ())), preferred_element_type=F32)
                 for n in range(nch)]
        o = o + jnp.concatenate(inter, axis=0)
        o = _rms(o, gn_ref[:, vs]) * _silu(g_all[:, vs])
        o_ref[:, vs] = o.astype(o_ref.dtype)

    @pl.when(j == pl.num_programs(1) - 1)
    def _():
        for h in range(heads):
            sfin_ref[h] = st_ref[h].T


def _gla_prompt(z, la, gn, n_seq, seq_len, heads, dk, dv):
    tm = ROW_TILE
    tps = seq_len // tm
    w = heads * dv
    qkw = heads * dk
    assert 2 * qkw == w
    return pl.pallas_call(
        functools.partial(_gla_prompt_kernel, heads=heads, dk=dk, dv=dv),
        out_shape=(jax.ShapeDtypeStruct((n_seq * seq_len, w), BF16), jax.ShapeDtypeStruct((n_seq, heads, dk, dv), F32)),
        grid=(n_seq, tps),
        in_specs=[
            pl.BlockSpec((tm, w), lambda b, j: (b * tps + j, 2)),
            pl.BlockSpec((tm, w), lambda b, j: (b * tps + j, 3)),
            pl.BlockSpec((tm, w), lambda b, j: (b * tps + j, 4)),
            pl.BlockSpec((tm, qkw), lambda b, j: (b * tps + j, 0)),
            pl.BlockSpec((1, w), lambda b, j: (0, 0)),
        ],
        out_specs=(
            pl.BlockSpec((tm, w), lambda b, j: (b * tps + j, 0)),
            pl.BlockSpec((None, heads, dk, dv), lambda b, j: (b, 0, 0, 0)),
        ),
        scratch_shapes=[pltpu.VMEM((heads, dv, dk), F32), pltpu.VMEM((tm // GLA_CHUNK, dv, dk), BF16)],
        compiler_params=_params("arbitrary", "arbitrary"),
        name="gla_prompt",
    )(z, z, z, la, gn.reshape(1, w))


def _gla_sample_kernel(q_ref, k_ref, v_ref, g_ref, la_ref, s_ref, gn_ref, o_ref, ns_ref, *, heads, dk, dv):
    bs, ln, _ = q_ref.shape
    tril = lax.broadcasted_iota(jnp.int32, (ln, ln), 1) <= lax.broadcasted_iota(jnp.int32, (ln, ln), 0)
    for s in range(bs):
        q_s, k_s, v_s, g_s, la_s = q_ref[s], k_ref[s], v_ref[s], g_ref[s], la_ref[s]
        for h in range(heads):
            ks = slice(h * dk, (h + 1) * dk)
            vs = slice(h * dv, (h + 1) * dv)
            la = la_s[:, ks]
            rows = [la[0:1]]
            for t in range(1, ln):
                rows.append(rows[-1] + la[t:t + 1])
            b = jnp.concatenate(rows, axis=0)
            b_last = rows[-1]
            q_dec = _round_bf16(q_s[:, ks] * (dk ** -0.5) * jnp.exp(b))
            k_inv = _round_bf16(k_s[:, ks] * jnp.exp(-b))
            k_end = _round_bf16(k_s[:, ks] * jnp.exp(b_last - b))
            vr = _round_bf16(v_s[:, vs])
            s0 = s_ref[s, h]
            att = lax.dot_general(q_dec, k_inv, (((1,), (1,)), ((), ())), preferred_element_type=F32)
            att = _round_bf16(jnp.where(tril, att, 0.0))
            o = jnp.dot(att, vr, preferred_element_type=F32) + jnp.dot(q_dec, _round_bf16(s0), preferred_element_type=F32)
            upd = lax.dot_general(k_end, vr, (((0,), (0,)), ((), ())), preferred_element_type=F32)
            d_col = jnp.broadcast_to(jnp.exp(b_last), (dk, dk)).T
            ns_ref[s, h] = s0 * jnp.concatenate([d_col] * (dv // dk), axis=1) + upd
            o = _rms(o, gn_ref[:, vs]) * _silu(g_s[:, vs])
            o_ref[s, :, vs] = o.astype(o_ref.dtype)


def _gla_sample(q3, k3, v3, g3, la3, state, layer, gn, heads, dk, dv):
    nb, ln, w = v3.shape
    qkw = heads * dk
    bs = 8
    return pl.pallas_call(
        functools.partial(_gla_sample_kernel, heads=heads, dk=dk, dv=dv),
        out_shape=(jax.ShapeDtypeStruct((nb, ln, w), F32), jax.ShapeDtypeStruct((nb, heads, dk, dv), F32)),
        grid=(nb // bs,),
        in_specs=[
            pl.BlockSpec((bs, ln, qkw), lambda s: (s, 0, 0)),
            pl.BlockSpec((bs, ln, qkw), lambda s: (s, 0, 0)),
            pl.BlockSpec((bs, ln, w), lambda s: (s, 0, 0)),
            pl.BlockSpec((bs, ln, w), lambda s: (s, 0, 0)),
            pl.BlockSpec((bs, ln, qkw), lambda s: (s, 0, 0)),
            pl.BlockSpec((None, bs, heads, dk, dv), lambda s: (layer, s, 0, 0, 0)),
            pl.BlockSpec((1, w), lambda s: (0, 0)),
        ],
        out_specs=(
            pl.BlockSpec((bs, ln, w), lambda s: (s, 0, 0)),
            pl.BlockSpec((bs, heads, dk, dv), lambda s: (s, 0, 0, 0)),
        ),
        compiler_params=_params("arbitrary"),
        name="gla_sample",
    )(q3, k3, v3, g3, la3, state, gn.reshape(1, w))


def _outproj_kernel(cp_ref, op_ref, cs_ref, os_ref, xp_ref, xs_ref, w_ref, gp, gs, y_ref, wb_ref, *, npt, dec_seq):
    i = pl.program_id(1)

    @pl.when(i == 0)
    def _():
        wb_ref[...] = w_ref[...].astype(BF16)

    half = cp_ref.shape[1]

    def mixed(c_ref, o_ref):
        return (jnp.dot(c_ref[...], wb_ref[0:half, :], preferred_element_type=F32)
                + jnp.dot(o_ref[...], wb_ref[half:, :], preferred_element_type=F32))

    @pl.when(i < npt)
    def _():
        y_ref[...] = xp_ref[...] + gp[...] * mixed(cp_ref, op_ref)

    @pl.when(i >= npt)
    def _():
        mix = mixed(cs_ref, os_ref)
        nb = gs.shape[0]
        for t in range(dec_seq):
            rows = slice(t * nb, (t + 1) * nb)
            y_ref[rows, :] = xs_ref[rows, :] + gs[...] * mix[rows]


def _outproj(conv_p, gla_p, conv_s, gla_s, xp, xs, xs_blk, w_out, layer, mod_p, mod_s, dims):
    d = xp.shape[1]
    half = conv_p.shape[1]
    tm, tn = ROW_TILE, 1024
    nj = d // tn
    npt = dims["npt"]
    t = (npt + 1) * tm
    p_specs, s_specs = _mod_specs((2,), tn, dims["npt"], dims["tps"], dims["n_seq"], dims["dec_batch"],
                                  row_axis=1, col_fn=lambda c, idx: c * nj + idx[0])
    return pl.pallas_call(
        functools.partial(_outproj_kernel, npt=dims["npt"], dec_seq=dims["dec_seq"]),
        out_shape=jax.ShapeDtypeStruct((t, d), F32),
        grid=(nj, t // tm),
        in_specs=[
            pl.BlockSpec((tm, half), lambda j, i: (jnp.minimum(i, npt - 1), 0)),
            pl.BlockSpec((tm, half), lambda j, i: (jnp.minimum(i, npt - 1), 0)),
            pl.BlockSpec((tm, half), lambda j, i: (0, 0)),
            pl.BlockSpec((tm, half), lambda j, i: (0, 0)),
            pl.BlockSpec((tm, tn), lambda j, i: (jnp.minimum(i, npt - 1), j)),
            pl.BlockSpec((tm, tn), lambda j, i: (xs_blk, j)),
            pl.BlockSpec((None, d, tn), lambda j, i: (layer, 0, j)),
            *p_specs,
            *s_specs,
        ],
        out_specs=pl.BlockSpec((tm, tn), lambda j, i: (i, j)),
        scratch_shapes=[pltpu.VMEM((d, tn), BF16)],
        compiler_params=_params("arbitrary", "arbitrary"),
        name="outproj",
    )(conv_p, gla_p, conv_s, gla_s, xp, xs, w_out, mod_p, mod_s)


def _norm2_kernel(x_ref, g_ref, shp, scp, shs, scs, wr_hi, wr_lo, br_ref, hb_ref, ids_ref, wts_ref, h_ref, *, npt, dec_seq, n_grp, n_exp):
    i = pl.program_id(0)
    y = _rms(x_ref[...], g_ref[...])
    _store_by_group(i, npt, dec_seq, h_ref, lambda v, m: v[0] * (1.0 + m[1]) + m[0], [y], [shp, scp], [shs, scs])
    h = h_ref[...]
    h_hi = h.astype(BF16)
    hb_ref[...] = h_hi
    h_lo = (h - h_hi.astype(F32)).astype(BF16)
    logits = (jnp.dot(h_hi, wr_hi[...], preferred_element_type=F32) + jnp.dot(h_lo, wr_hi[...], preferred_element_type=F32)
              + jnp.dot(h_hi, wr_lo[...], preferred_element_type=F32)) + br_ref[...]
    lane = lax.broadcasted_iota(jnp.int32, logits.shape, 1).astype(F32)
    big = jnp.float32(LANES)
    neg = jnp.float32(-jnp.inf)
    gl = jnp.where(lane < n_grp, logits, neg)
    gmax = jnp.max(gl, axis=-1, keepdims=True)
    gidx = jnp.min(jnp.where(gl == gmax, lane, big), axis=-1, keepdims=True)
    g_w = 1.0 / jnp.sum(jnp.exp(gl - gmax), axis=-1, keepdims=True)
    lo = n_grp + gidx * n_exp
    in_grp = (lane >= lo) & (lane < lo + n_exp)
    sl = jnp.where(in_grp, logits, neg)
    p = jnp.exp(sl - jnp.max(sl, axis=-1, keepdims=True))
    p = p / jnp.sum(p, axis=-1, keepdims=True)
    p = jnp.where(in_grp, p, -1.0)
    p1 = jnp.max(p, axis=-1, keepdims=True)
    i1 = jnp.min(jnp.where(p == p1, lane, big), axis=-1, keepdims=True)
    p_rest = jnp.where(lane == i1, -1.0, p)
    p2 = jnp.max(p_rest, axis=-1, keepdims=True)
    i2 = jnp.min(jnp.where(p_rest == p2, lane, big), axis=-1, keepdims=True)
    denom = p1 + p2
    ids_ref[...] = jnp.where(lane == 0, i1 - n_grp, jnp.where(lane == 1, i2 - n_grp, 0.0)).astype(jnp.int32)
    wts_ref[...] = jnp.where(lane == 0, g_w * (p1 / denom), jnp.where(lane == 1, g_w * (p2 / denom), 0.0))


def _norm2(x, g, mod_p, mod_s, wr_hi, wr_lo, br, dims, n_grp, n_exp):
    t, d = x.shape
    tm = ROW_TILE
    p_specs, s_specs = _mod_specs((3, 4), d, dims["npt"], dims["tps"], dims["n_seq"], dims["dec_batch"])
    return pl.pallas_call(
        functools.partial(_norm2_kernel, npt=dims["npt"], dec_seq=dims["dec_seq"], n_grp=n_grp, n_exp=n_exp),
        out_shape=(jax.ShapeDtypeStruct((t, d), BF16), jax.ShapeDtypeStruct((t, LANES), jnp.int32),
                   jax.ShapeDtypeStruct((t, LANES), F32)),
        grid=(t // tm,),
        in_specs=[
            pl.BlockSpec((tm, d), lambda i: (i, 0)),
            pl.BlockSpec((1, d), lambda i: (0, 0)),
            *p_specs,
            *s_specs,
            pl.BlockSpec((d, LANES), lambda i: (0, 0)),
            pl.BlockSpec((d, LANES), lambda i: (0, 0)),
            pl.BlockSpec((1, LANES), lambda i: (0, 0)),
        ],
        out_specs=(pl.BlockSpec((tm, d), lambda i: (i, 0)), pl.BlockSpec((tm, LANES), lambda i: (i, 0)),
                   pl.BlockSpec((tm, LANES), lambda i: (i, 0))),
        scratch_shapes=[pltpu.VMEM((tm, d), F32)],
        compiler_params=_params("arbitrary"),
        name="norm2_router",
    )(x, g.reshape(1, d), mod_p, mod_p, mod_s, mod_s, wr_hi, wr_lo, br)


def _segments(ids, n_experts, n_tiles_max):
    t = ids.shape[0]
    ts, te, al = SORT_TILE, EXPERT_TILE, SEG_ALIGN
    nt = t // ts
    e_iota = jnp.arange(n_experts, dtype=jnp.int32)
    cnt = jnp.sum((ids.reshape(nt, ts * TOP_K, 1) == e_iota).astype(jnp.int32), axis=1)
    size = (cnt + al - 1) // al * al
    src = jnp.cumsum(size, axis=1) - size
    tot = jnp.sum(size, axis=0)
    tot_pad = (tot + te - 1) // te * te
    ends = jnp.cumsum(tot_pad)
    exp_off = ends - tot_pad
    dst = exp_off[None, :] + jnp.cumsum(size, axis=0) - size
    n_used = (ends[-1] // te).astype(jnp.int32)
    tile_start = jnp.arange(n_tiles_max, dtype=jnp.int32) * te
    tile_expert = jnp.minimum(jnp.sum(ends[None, :] <= tile_start[:, None], axis=1), n_experts - 1).astype(jnp.int32)
    last = tile_expert[jnp.maximum(n_used - 1, 0)]
    tile_expert = jnp.where(jnp.arange(n_tiles_max) < n_used, tile_expert, last)
    return dict(src=src.reshape(-1).astype(jnp.int32), dst=dst.reshape(-1).astype(jnp.int32),
                size=size.reshape(-1).astype(jnp.int32), tile_tot=jnp.sum(size, axis=1).astype(jnp.int32),
                fill_start=(exp_off + tot).astype(jnp.int32), fill_size=(tot_pad - tot).astype(jnp.int32),
                tile_expert=tile_expert, n_used=n_used.reshape(1))


def _lane_pick(x, lane, k):
    return jnp.sum(jnp.where(lane == k, x, 0.0), axis=-1, keepdims=True)


def _sorted_positions(ids, lane):
    ts = ids.shape[0]
    onehot = [(lane == _lane_pick(ids, lane, k)).astype(F32) for k in range(TOP_K)]
    row = lax.broadcasted_iota(jnp.int32, (ts, ts), 0)
    col = lax.broadcasted_iota(jnp.int32, (ts, ts), 1)
    before = (col < row).astype(BF16)
    earlier = [jnp.dot(before, o.astype(BF16), preferred_element_type=F32) for o in onehot]
    cnt = [jnp.sum(o, axis=0, keepdims=True) for o in onehot]
    size = jnp.ceil((cnt[0] + cnt[1]) * (1.0 / SEG_ALIGN)) * SEG_ALIGN
    er = lax.broadcasted_iota(jnp.int32, (LANES, LANES), 0)
    ec = lax.broadcasted_iota(jnp.int32, (LANES, LANES), 1)
    start = jnp.dot(jnp.broadcast_to(size, (SUBLANES, LANES)).astype(BF16), (er < ec).astype(BF16),
                    preferred_element_type=F32)[0:1]
    base = [start + earlier[0], start + cnt[0] + earlier[1]]
    return [jnp.sum(onehot[k] * base[k], axis=-1, keepdims=True) for k in range(TOP_K)]


def _as_row(col_vals, lane):
    hi = jnp.floor(col_vals * (1.0 / 32.0))
    lo = col_vals - hi * 32.0
    ones = jnp.ones((SUBLANES, LANES), BF16)
    nt = (((1,), (1,)), ((), ()))
    hi_row = lax.dot_general(ones, jnp.where(lane == 0, hi, 0.0).astype(BF16), nt, preferred_element_type=F32)
    lo_row = lax.dot_general(ones, jnp.where(lane == 0, lo, 0.0).astype(BF16), nt, preferred_element_type=F32)
    return (hi_row * 32.0 + lo_row)[0:1]


def _dispatch_kernel(src_ref, dst_ref, size_ref, tot_ref, fst_ref, fsz_ref, nu_ref, h_ref, ids_ref, wts_ref, xe_hbm,
                     we_hbm, pos_ref, sbuf, wbuf, zx, zw, semx, semw, semz, semt, *, n_experts):
    i = pl.program_id(0)
    n = pl.num_programs(0)
    ts = h_ref.shape[0]
    rows = sbuf.shape[1]
    te = zx.shape[0]
    slot = lax.rem(i, 2)

    def tail_copies(tl):
        r0 = pl.multiple_of(tl * te, te)
        return (pltpu.make_async_copy(zx, xe_hbm.at[pl.ds(r0, te), :], semt.at[0]),
                pltpu.make_async_copy(zw, we_hbm.at[pl.ds(r0, te), :], semt.at[1]))

    def for_tail(action):
        def body(tl, carry):
            for c in tail_copies(tl):
                action(c)
            return carry

        lax.fori_loop(nu_ref[0], xe_hbm.shape[0] // te, body, 0)

    def seg_copies(tile, sl, e):
        k = tile * n_experts + e
        sz = pl.multiple_of(size_ref[k], SEG_ALIGN)
        s0 = pl.multiple_of(src_ref[k], SEG_ALIGN)
        d0 = pl.multiple_of(dst_ref[k], SEG_ALIGN)
        return sz, (pltpu.make_async_copy(sbuf.at[sl, pl.ds(s0, sz), :], xe_hbm.at[pl.ds(d0, sz), :], semx.at[sl]),
                    pltpu.make_async_copy(wbuf.at[sl, pl.ds(s0, sz), :], we_hbm.at[pl.ds(d0, sz), :], semw.at[sl]))

    def wait_tile(tile, sl):
        tot = pl.multiple_of(tot_ref[tile], SEG_ALIGN)
        pltpu.make_async_copy(sbuf.at[sl, pl.ds(0, tot), :], xe_hbm.at[pl.ds(0, tot), :], semx.at[sl]).wait()
        pltpu.make_async_copy(wbuf.at[sl, pl.ds(0, tot), :], we_hbm.at[pl.ds(0, tot), :], semw.at[sl]).wait()

    @pl.when(i == 0)
    def _():
        zx[...] = jnp.zeros(zx.shape, zx.dtype)
        zw[...] = jnp.zeros(zw.shape, zw.dtype)

        def fill(e, carry):
            sz = pl.multiple_of(fsz_ref[e], SEG_ALIGN)
            d0 = pl.multiple_of(fst_ref[e], SEG_ALIGN)

            @pl.when(sz > 0)
            def _():
                cx = pltpu.make_async_copy(zx.at[pl.ds(0, sz), :], xe_hbm.at[pl.ds(d0, sz), :], semz.at[0])
                cw = pltpu.make_async_copy(zw.at[pl.ds(0, sz), :], we_hbm.at[pl.ds(d0, sz), :], semz.at[1])
                cx.start()
                cw.start()
                cx.wait()
                cw.wait()

            return carry

        lax.fori_loop(0, n_experts, fill, 0)
        for_tail(lambda c: c.start())

    @pl.when(i >= 2)
    def _():
        wait_tile(i - 2, slot)

    lane = lax.broadcasted_iota(jnp.int32, (ts, LANES), 1).astype(F32)
    pos = _sorted_positions(ids_ref[...].astype(F32), lane)
    pos_ref[...] = jnp.where(lane == 0, pos[0], jnp.where(lane == 1, pos[1], 0.0))
    r_iota = lax.broadcasted_iota(jnp.int32, (rows, ts), 0).astype(F32)
    assert rows <= 1024
    sel = [(r_iota == _as_row(p, lane)).astype(BF16) for p in pos]
    sbuf[slot] = jnp.dot(sel[0] + sel[1], h_ref[...], preferred_element_type=F32).astype(BF16)
    wts = wts_ref[...]
    wsorted = jnp.zeros((rows, LANES), F32)
    for k in range(TOP_K):
        pieces = _split3(_lane_pick(wts, lane, k))
        wk = jnp.where(lane == 0, pieces[0].astype(F32), jnp.where(lane == 1, pieces[1].astype(F32),
                       jnp.where(lane == 2, pieces[2].astype(F32), 0.0)))
        wsorted = wsorted + jnp.dot(sel[k], wk.astype(BF16), preferred_element_type=F32)
    wbuf[slot] = wsorted

    def issue(e, carry):
        sz, copies = seg_copies(i, slot, e)

        @pl.when(sz > 0)
        def _():
            for c in copies:
                c.start()

        return carry

    lax.fori_loop(0, n_experts, issue, 0)

    @pl.when(i == n - 1)
    def _():
        @pl.when(i >= 1)
        def _():
            wait_tile(i - 1, 1 - slot)

        wait_tile(i, slot)
        for_tail(lambda c: c.wait())


def _dispatch_rows(hb, ids, wts, seg, n_experts, n_tiles_max):
    t, d = hb.shape
    ts, te = SORT_TILE, EXPERT_TILE
    rows = TOP_K * ts + n_experts * SEG_ALIGN
    n_slots = n_tiles_max * te

    def tile_map(i, *_):
        return (i, 0)

    return pl.pallas_call(
        functools.partial(_dispatch_kernel, n_experts=n_experts),
        out_shape=(jax.ShapeDtypeStruct((n_slots, d), BF16), jax.ShapeDtypeStruct((n_slots, LANES), F32),
                   jax.ShapeDtypeStruct((t, LANES), F32)),
        grid_spec=pltpu.PrefetchScalarGridSpec(
            num_scalar_prefetch=7,
            grid=(t // ts,),
            in_specs=[pl.BlockSpec((ts, d), tile_map), pl.BlockSpec((ts, LANES), tile_map),
                      pl.BlockSpec((ts, LANES), tile_map)],
            out_specs=(pl.BlockSpec(memory_space=pl.ANY), pl.BlockSpec(memory_space=pl.ANY),
                       pl.BlockSpec((ts, LANES), tile_map)),
            scratch_shapes=[
                pltpu.VMEM((2, rows, d), BF16),
                pltpu.VMEM((2, rows, LANES), F32),
                pltpu.VMEM((te, d), BF16),
                pltpu.VMEM((te, LANES), F32),
                pltpu.SemaphoreType.DMA((2,)),
                pltpu.SemaphoreType.DMA((2,)),
                pltpu.SemaphoreType.DMA((2,)),
                pltpu.SemaphoreType.DMA((2,)),
            ],
        ),
        compiler_params=_params("arbitrary"),
        name="dispatch",
    )(seg["src"], seg["dst"], seg["size"], seg["tile_tot"], seg["fill_start"], seg["fill_size"], seg["n_used"],
      hb, ids, wts)


def _expert_rows_kernel(te_ref, nu_ref, x_ref, w_ref, wg_ref, wu_ref, wd_ref, o_ref, wgb, wub, wdb):
    i = pl.program_id(0)
    n_used = nu_ref[0]

    @pl.when(i < n_used)
    def _():
        changed = jnp.logical_or(i == 0, te_ref[i] != te_ref[jnp.maximum(i - 1, 0)])

        @pl.when(changed)
        def _():
            wgb[...] = wg_ref[...].astype(BF16)
            wub[...] = wu_ref[...].astype(BF16)
            wdb[...] = wd_ref[...].astype(BF16)

        x = x_ref[...]
        a = jnp.dot(x, wgb[...], preferred_element_type=F32)
        u = jnp.dot(x, wub[...], preferred_element_type=F32)
        w = jnp.sum(w_ref[...], axis=-1, keepdims=True)
        hid = _silu(a) * u * w
        o_ref[...] = jnp.dot(hid.astype(BF16), wdb[...], preferred_element_type=F32).astype(o_ref.dtype)

    @pl.when(i >= n_used)
    def _():
        o_ref[...] = jnp.zeros(o_ref.shape, o_ref.dtype)


def _expert_rows(xe, we, seg, wg, wu, wd, layer_base):
    n_slots, d = xe.shape
    f = wg.shape[-1]
    te = EXPERT_TILE

    def w_map(i, te_ref, nu_ref):
        return (layer_base + te_ref[i], 0, 0)

    def row_map(i, te_ref, nu_ref):
        return (jnp.minimum(i, nu_ref[0] - 1), 0)

    return pl.pallas_call(
        _expert_rows_kernel,
        out_shape=jax.ShapeDtypeStruct((n_slots, d), BF16),
        grid_spec=pltpu.PrefetchScalarGridSpec(
            num_scalar_prefetch=2,
            grid=(n_slots // te,),
            in_specs=[
                pl.BlockSpec((te, d), row_map),
                pl.BlockSpec((te, LANES), row_map),
                pl.BlockSpec((None, d, f), w_map),
                pl.BlockSpec((None, d, f), w_map),
                pl.BlockSpec((None, f, d), w_map),
            ],
            out_specs=pl.BlockSpec((te, d), lambda i, te_ref, nu_ref: (i, 0)),
            scratch_shapes=[pltpu.VMEM((d, f), BF16), pltpu.VMEM((d, f), BF16), pltpu.VMEM((f, d), BF16)],
        ),
        compiler_params=_params("arbitrary"),
        name="experts",
    )(seg["tile_expert"], seg["n_used"], xe, we, wg, wu, wd)


def _collect_kernel(src_ref, dst_ref, size_ref, tot_ref, y_hbm, x_ref, pos_ref, gp, gs, *rest, npt, n_experts, final):
    if final:
        fg_ref, op_ref, os_ref, buf, sem, o_ref = rest
    else:
        o_ref, buf, sem = rest
    i = pl.program_id(0)
    n = pl.num_programs(0)
    ts = x_ref.shape[0]
    rows = buf.shape[1]
    slot = lax.rem(i, 2)

    def start(tile, sl):
        def body(e, carry):
            k = tile * n_experts + e
            sz = pl.multiple_of(size_ref[k], SEG_ALIGN)
            s0 = pl.multiple_of(src_ref[k], SEG_ALIGN)
            d0 = pl.multiple_of(dst_ref[k], SEG_ALIGN)

            @pl.when(sz > 0)
            def _():
                pltpu.make_async_copy(y_hbm.at[pl.ds(d0, sz), :], buf.at[sl, pl.ds(s0, sz), :], sem.at[sl]).start()

            return carry

        lax.fori_loop(0, n_experts, body, 0)

    @pl.when(i == 0)
    def _():
        buf[...] = jnp.zeros(buf.shape, buf.dtype)
        start(0, 0)

    @pl.when(i + 1 < n)
    def _():
        start(i + 1, 1 - slot)

    tot = pl.multiple_of(tot_ref[i], SEG_ALIGN)
    pltpu.make_async_copy(y_hbm.at[pl.ds(0, tot), :], buf.at[slot, pl.ds(0, tot), :], sem.at[slot]).wait()
    lane = lax.broadcasted_iota(jnp.int32, (ts, LANES), 1).astype(F32)
    pos = pos_ref[...]
    r_iota = lax.broadcasted_iota(jnp.int32, (ts, rows), 1).astype(F32)
    pick = ((r_iota == _lane_pick(pos, lane, 0)) | (r_iota == _lane_pick(pos, lane, 1))).astype(BF16)
    ff = jnp.dot(pick, buf[slot], preferred_element_type=F32)
    _store_by_group(i, npt, None, o_ref, lambda v, m: v[0] + m[0] * v[1], [x_ref[...], ff], [gp], [gs])
    if final:
        y = _rms(o_ref[...], fg_ref[...])

        @pl.when(i < npt)
        def _():
            op_ref[...] = y

        @pl.when(i >= npt)
        def _():
            os_ref[...] = y


def _collect(ye, pos, x, seg, mod_p, mod_s, dims, n_experts, final_g=None):
    t, d = x.shape
    ts = SORT_TILE
    npt = dims["npt"]
    rows = TOP_K * ts + n_experts * SEG_ALIGN
    p_specs, s_specs = _mod_specs((5,), d, npt, dims["tps"], dims["n_seq"], dims["dec_batch"])
    final = final_g is not None

    def strip(spec):
        return pl.BlockSpec(spec.block_shape, lambda i, *_, m=spec.index_map: m(i))

    def tile_map(i, *_):
        return (i, 0)

    in_specs = [
        pl.BlockSpec(memory_space=pl.ANY),
        pl.BlockSpec((ts, d), tile_map),
        pl.BlockSpec((ts, LANES), tile_map),
        *[strip(s) for s in p_specs],
        *[strip(s) for s in s_specs],
    ]
    scratch = [pltpu.VMEM((2, rows, d), BF16), pltpu.SemaphoreType.DMA((2,))]
    args = [seg["src"], seg["dst"], seg["size"], seg["tile_tot"], ye, x, pos, mod_p, mod_s]
    if final:
        in_specs.append(pl.BlockSpec((1, d), lambda i, *_: (0, 0)))
        args.append(final_g.reshape(1, d))
        out_shape = (jax.ShapeDtypeStruct((npt * ts, d), F32), jax.ShapeDtypeStruct((t - npt * ts, d), F32))
        out_specs = (pl.BlockSpec((ts, d), lambda i, *_: (jnp.minimum(i, npt - 1), 0)),
                     pl.BlockSpec((ts, d), lambda i, *_: (jnp.maximum(i - npt, 0), 0)))
        scratch.append(pltpu.VMEM((ts, d), F32))
    else:
        out_shape = jax.ShapeDtypeStruct((t, d), F32)
        out_specs = pl.BlockSpec((ts, d), tile_map)
    return pl.pallas_call(
        functools.partial(_collect_kernel, npt=npt, n_experts=n_experts, final=final),
        out_shape=out_shape,
        grid_spec=pltpu.PrefetchScalarGridSpec(
            num_scalar_prefetch=4, grid=(t // ts,), in_specs=in_specs, out_specs=out_specs, scratch_shapes=scratch),
        compiler_params=_params("arbitrary"),
        name="collect_final" if final else "collect",
    )(*args)


def _final_norm_kernel(x_ref, g_ref, op_ref, os_ref, *, npt):
    i = pl.program_id(0)
    y = _rms(x_ref[...], g_ref[...])

    @pl.when(i < npt)
    def _():
        op_ref[...] = y

    @pl.when(i >= npt)
    def _():
        os_ref[...] = y


def _final_norm(x, g, npt):
    t, d = x.shape
    tm = ROW_TILE
    return pl.pallas_call(
        functools.partial(_final_norm_kernel, npt=npt),
        out_shape=(jax.ShapeDtypeStruct((npt * tm, d), F32), jax.ShapeDtypeStruct((t - npt * tm, d), F32)),
        grid=(t // tm,),
        in_specs=[pl.BlockSpec((tm, d), lambda i: (i, 0)), pl.BlockSpec((1, d), lambda i: (0, 0))],
        out_specs=(pl.BlockSpec((tm, d), lambda i: (jnp.minimum(i, npt - 1), 0)),
                   pl.BlockSpec((tm, d), lambda i: (jnp.maximum(i - npt, 0), 0))),
        compiler_params=_params("arbitrary"),
        name="final_norm",
    )(x, g.reshape(1, d))


def kernel(x_prompt, x_sample, c_prompt, c_sample, state_conv, state_gla, w_ada, b_ada, norm1_g, norm2_g, w_in, conv_w, conv_b, conv_ln_g, conv_ln_b, gate_w2, gate_b, gla_norm_g, w_out, router_grp_w, router_grp_b, router_exp_w, router_exp_b, exp_w_gate, exp_w_up, exp_w_down, final_norm_g):
    n_seq, seq_len, d = x_prompt.shape
    dec_batch, dec_seq, _ = x_sample.shape
    depth = w_ada.shape[0]
    kw, d_conv = conv_w.shape[1:]
    heads, dv = gla_norm_g.shape[1:]
    rank, qkw = gate_w2.shape[1:]
    dk = qkw // heads
    n_grp, n_exp = router_exp_w.shape[2:]
    n_experts = n_grp * n_exp
    tp, ts = n_seq * seq_len, dec_batch * dec_seq
    t = tp + ts
    tm = ROW_TILE
    assert ts == tm and seq_len % tm == 0 and d_conv == heads * dv and kw - 1 <= CONV_HALO
    n_main = 2 * d_conv + 2 * qkw + 2 * heads * dv
    dims = dict(npt=tp // tm, tps=seq_len // tm, n_seq=n_seq, dec_batch=dec_batch, dec_seq=dec_seq)

    xp, xs, xs_blk = x_prompt.reshape(tp, d), x_sample.transpose(1, 0, 2).reshape(ts, d), 0
    w_in_t = jnp.swapaxes(w_in, 1, 2)
    st_t = jnp.transpose(state_conv, (0, 2, 1, 3))
    pad = (-n_seq) % 8
    c_all = jnp.concatenate([c_prompt, jnp.zeros((pad, d), F32), c_sample], axis=0)
    mod = _ada(c_all, w_ada, b_ada)
    st = SORT_TILE
    assert tm % st == 0 and st % dec_batch == 0
    dims_sort = dict(npt=tp // st, tps=seq_len // st, n_seq=n_seq, dec_batch=dec_batch, dec_seq=dec_seq)
    n_tiles_max = -(-(TOP_K * t + (t // st) * n_experts * (SEG_ALIGN - 1)) // EXPERT_TILE) + n_experts

    conv_p, gla_p, conv_u, gla_s = [], [], [], []
    for l in range(depth):
        mod_p = mod[l, :n_seq].reshape(n_seq, 1, 6 * d)
        mod_s = mod[l, n_seq + pad:]
        gw2 = jnp.pad(gate_w2[l], ((0, LANES - rank), (0, 0))).astype(BF16)
        h, la = _norm1(xp, xs, xs_blk, norm1_g[l], mod_p, mod_s, w_in_t, l, n_main, rank, gw2, gate_b[l], dims)
        z = _inproj(h, w_in_t, l, n_main)

        cv_p, cb_p = _conv_prompt(z, conv_w[l], conv_b[l], conv_ln_g[l], conv_ln_b[l], n_seq, seq_len)
        cv_s, u_s = _conv_sample(z, st_t, l, conv_w[l], conv_b[l], conv_ln_g[l], conv_ln_b[l], tp, dec_seq, dec_batch)
        conv_p.append(cb_p)
        conv_u.append(u_s)

        go_p, gs_p = _gla_prompt(z, la, gla_norm_g[l], n_seq, seq_len, heads, dk, dv)

        def seq_major(a):
            return a.reshape(dec_seq, dec_batch, a.shape[-1]).transpose(1, 0, 2)

        zs = z[tp:]
        q3 = seq_major(zs[:, 2 * d_conv:2 * d_conv + qkw])
        k3 = seq_major(zs[:, 2 * d_conv + qkw:2 * d_conv + 2 * qkw])
        v3 = seq_major(zs[:, 2 * d_conv + 2 * qkw:2 * d_conv + 2 * qkw + heads * dv])
        g3 = seq_major(zs[:, 2 * d_conv + 2 * qkw + heads * dv:n_main])
        go_s, gs_s = _gla_sample(q3, k3, v3, g3, seq_major(la[tp:]), state_gla, l, gla_norm_g[l], heads, dk, dv)
        go_s = go_s.transpose(1, 0, 2).reshape(ts, heads * dv).astype(BF16)
        gla_p.append(gs_p)
        gla_s.append(gs_s)

        x = _outproj(cv_p, go_p, cv_s.reshape(ts, d_conv), go_s, xp, xs, xs_blk, w_out, l, mod_p, mod_s, dims)

        wr = jnp.concatenate([router_grp_w[l], router_exp_w[l].reshape(d, n_experts)], axis=1)
        wr = jnp.pad(wr, ((0, 0), (0, LANES - wr.shape[1])))
        wr_hi = wr.astype(BF16)
        wr_lo = (wr - wr_hi.astype(F32)).astype(BF16)
        br = jnp.concatenate([router_grp_b[l], router_exp_b[l].reshape(-1)])
        br = jnp.pad(br, (0, LANES - br.shape[0])).reshape(1, LANES)
        hb, ids, wts = _norm2(x, norm2_g[l], mod_p, mod_s, wr_hi, wr_lo, br, dims, n_grp, n_exp)
        seg = _segments(ids[:, :TOP_K], n_experts, n_tiles_max)
        xe, we, pos = _dispatch_rows(hb, ids, wts, seg, n_experts, n_tiles_max)
        f = exp_w_gate.shape[-1]
        ye = _expert_rows(xe, we, seg, exp_w_gate.reshape(depth * n_experts, d, f),
                          exp_w_up.reshape(depth * n_experts, d, f), exp_w_down.reshape(depth * n_experts, f, d),
                          l * n_experts)
        if l + 1 < depth:
            x = _collect(ye, pos, x, seg, mod_p, mod_s, dims_sort, n_experts)
            xp, xs, xs_blk = x, x, dims["npt"]
        else:
            y_p, y_s = _collect(ye, pos, x, seg, mod_p, mod_s, dims_sort, n_experts, final_g=final_norm_g)

    y_prompt = y_p.reshape(n_seq, seq_len, d)
    y_sample = y_s.reshape(dec_seq, dec_batch, d).transpose(1, 0, 2)
    conv_s = jnp.transpose(_conv_state_sample(st_t, jnp.stack(conv_u)), (0, 2, 1, 3))
    return (y_prompt, y_sample, jnp.stack(conv_p), jnp.stack(gla_p), conv_s, jnp.stack(gla_s))
```

```python
import functools

import jax
import jax.numpy as jnp
from jax import lax
from jax.experimental import pallas as pl
from jax.experimental.pallas import tpu as pltpu

F32 = jnp.float32
BF16 = jnp.bfloat16

EPS = 1e-6
GATE_TAU = 16.0
GLA_CHUNK = 32
GLA_BLOCK = 128
TOP_K = 2

ROW_TILE = 512
EXPERT_TILE = 512
SORT_TILE = 256
SEG_ALIGN = 16
CONV_ROWS = 64
CONV_HALO = 32
LANES = 128
SUBLANES = 8
VMEM_LIMIT = 56 * 1024 * 1024


def _params(*sem):
    return pltpu.CompilerParams(dimension_semantics=sem, vmem_limit_bytes=VMEM_LIMIT)


def _bdot(a, b):
    return jnp.dot(a.astype(BF16), b.astype(BF16), preferred_element_type=F32)


def _round_bf16(x):
    return x.astype(BF16).astype(F32)


def _split3(x):
    hi = x.astype(BF16)
    r = x - hi.astype(F32)
    mid = r.astype(BF16)
    lo = (r - mid.astype(F32)).astype(BF16)
    return hi, mid, lo


def _silu(x):
    return x * jax.nn.sigmoid(x)


def _store_by_group(i, n_prompt_tiles, dec_seq, out_ref, fn, vals, p_refs, s_refs):
    @pl.when(i < n_prompt_tiles)
    def _():
        out_ref[...] = fn(vals, [r[...] for r in p_refs]).astype(out_ref.dtype)

    @pl.when(i >= n_prompt_tiles)
    def _():
        mods = [r[...] for r in s_refs]
        nb = mods[0].shape[0]
        for t in range(out_ref.shape[0] // nb):
            rows = slice(t * nb, (t + 1) * nb)
            out_ref[rows, :] = fn([v[rows] for v in vals], mods).astype(out_ref.dtype)


def _ada_kernel(c_ref, w_ref, b_ref, o_ref):
    c = c_ref[...]
    o_ref[...] = _bdot(_silu(c), w_ref[...]) + b_ref[...]


def _ada(c_all, w_ada, b_ada):
    depth, d, n = w_ada.shape
    rows = c_all.shape[0]
    tn = 1024
    return pl.pallas_call(
        _ada_kernel,
        out_shape=jax.ShapeDtypeStruct((depth, rows, n), F32),
        grid=(depth, n // tn),
        in_specs=[
            pl.BlockSpec((rows, d), lambda l, j: (0, 0)),
            pl.BlockSpec((None, d, tn), lambda l, j: (l, 0, j)),
            pl.BlockSpec((None, 1, tn), lambda l, j: (l, 0, j)),
        ],
        out_specs=pl.BlockSpec((None, rows, tn), lambda l, j: (l, 0, j)),
        compiler_params=_params("arbitrary", "arbitrary"),
        name="ada",
    )(c_all, w_ada, b_ada.reshape(depth, 1, n))


def _mod_specs(cols, width, n_prompt_tiles, tiles_per_seq, n_seq, dec_batch, grid_rank=1, row_axis=0, col_fn=None):
    p_specs, s_specs = [], []
    for c in cols:
        def p_map(*idx, c=c):
            b = jnp.minimum(idx[row_axis] // tiles_per_seq, n_seq - 1)
            return (b, 0, c if col_fn is None else col_fn(c, idx))

        def s_map(*idx, c=c):
            return (0, c if col_fn is None else col_fn(c, idx))

        p_specs.append(pl.BlockSpec((None, 1, width), p_map))
        s_specs.append(pl.BlockSpec((dec_batch, width), s_map))
    return p_specs, s_specs


def _rms(x, g):
    return x * lax.rsqrt(jnp.mean(x * x, axis=-1, keepdims=True) + EPS) * g


def _norm1_kernel(xp_ref, xs_ref, g_ref, shp, scp, shs, scs, wgl_ref, gw2_ref, gb_ref, h_ref, la_ref, *, npt, dec_seq, rank):
    i = pl.program_id(0)
    y = _rms(jnp.where(i < npt, xp_ref[...], xs_ref[...]), g_ref[...])
    _store_by_group(i, npt, dec_seq, h_ref, lambda v, m: v[0] * (1.0 + m[1]) + m[0], [y], [shp, scp], [shs, scs])
    row = lax.broadcasted_iota(jnp.int32, wgl_ref.shape, 0)
    w_gl = jnp.where(row < rank, wgl_ref[...], 0.0).astype(BF16)
    gate_lr = lax.dot_general(h_ref[...], w_gl, (((1,), (1,)), ((), ())), preferred_element_type=F32)
    pre = _bdot(gate_lr, gw2_ref[...]) + gb_ref[...]
    la_ref[...] = (jnp.minimum(pre, 0.0) - jnp.log1p(jnp.exp(-jnp.abs(pre)))) * (1.0 / GATE_TAU)


def _norm1(xp, xs, xs_blk, g, mod_p, mod_s, w_in_t, layer, n_main, rank, gw2, gb, dims):
    d = xp.shape[1]
    tm = ROW_TILE
    npt = dims["npt"]
    t = (npt + 1) * tm
    p_specs, s_specs = _mod_specs((0, 1), d, npt, dims["tps"], dims["n_seq"], dims["dec_batch"])
    qk = gw2.shape[1]
    assert n_main % LANES == 0 and rank <= LANES
    return pl.pallas_call(
        functools.partial(_norm1_kernel, npt=npt, dec_seq=dims["dec_seq"], rank=rank),
        out_shape=(jax.ShapeDtypeStruct((t, d), BF16), jax.ShapeDtypeStruct((t, qk), F32)),
        grid=(t // tm,),
        in_specs=[
            pl.BlockSpec((tm, d), lambda i: (jnp.minimum(i, npt - 1), 0)),
            pl.BlockSpec((tm, d), lambda i: (xs_blk, 0)),
            pl.BlockSpec((1, d), lambda i: (0, 0)),
            *p_specs,
            *s_specs,
            pl.BlockSpec((None, LANES, d), lambda i: (layer, n_main // LANES, 0)),
            pl.BlockSpec(gw2.shape, lambda i: (0, 0)),
            pl.BlockSpec((1, qk), lambda i: (0, 0)),
        ],
        out_specs=(pl.BlockSpec((tm, d), lambda i: (i, 0)), pl.BlockSpec((tm, qk), lambda i: (i, 0))),
        compiler_params=_params("arbitrary"),
        name="norm1",
    )(xp, xs, g.reshape(1, d), mod_p, mod_p, mod_s, mod_s, w_in_t, gw2, gb.reshape(1, qk))


def _inproj_kernel(h_ref, w_ref, o_ref, wb_ref):
    @pl.when(pl.program_id(1) == 0)
    def _():
        wb_ref[...] = w_ref[...].astype(BF16)

    o_ref[...] = lax.dot_general(h_ref[...], wb_ref[...], (((1,), (1,)), ((), ())), preferred_element_type=F32)


def _inproj(h, w_in_t, layer, n_cols):
    t, d = h.shape
    tm, tn = ROW_TILE, 1024
    return pl.pallas_call(
        _inproj_kernel,
        out_shape=jax.ShapeDtypeStruct((t, n_cols), F32),
        grid=(n_cols // tn, t // tm),
        in_specs=[
            pl.BlockSpec((tm, d), lambda j, i: (i, 0)),
            pl.BlockSpec((None, tn, d), lambda j, i: (layer, j, 0)),
        ],
        out_specs=pl.BlockSpec((tm, tn), lambda j, i: (i, j)),
        scratch_shapes=[pltpu.VMEM((tn, d), BF16)],
        compiler_params=_params("arbitrary", "arbitrary"),
        name="inproj",
    )(h, w_in_t)


def _ln_silu(y, g, b):
    mu = jnp.mean(y, axis=-1, keepdims=True)
    yc = y - mu
    var = jnp.mean(yc * yc, axis=-1, keepdims=True)
    return _silu(yc * lax.rsqrt(var + EPS) * g + b)


def _conv_prompt_kernel(a_ref, b_ref, cw_ref, cb_ref, lng_ref, lnb_ref, o_ref, st_ref, full_ref, cwb_ref, y_ref, *, kw):
    j = pl.program_id(1)
    tm, c = a_ref.shape
    halo = CONV_HALO
    phases = full_ref.shape[0]
    assert phases == SUBLANES

    @pl.when(jnp.logical_and(pl.program_id(0) == 0, j == 0))
    def _():
        full_ref[...] = jnp.zeros(full_ref.shape, F32)
        for w in range(kw):
            cwb_ref[w] = jnp.broadcast_to(cw_ref[w:w + 1, :], (SUBLANES, c))

    prev = full_ref[0, tm + halo - SUBLANES:tm + halo, :]
    tail = jnp.where(j == 0, 0.0, prev)

    @pl.when(j == 0)
    def _():
        for p in range(phases):
            full_ref[p, 0:halo, :] = jnp.zeros((halo, c), F32)

    @pl.when(j > 0)
    def _():
        for p in range(phases):
            full_ref[p, 0:halo, :] = full_ref[p, tm:tm + halo, :]

    u = a_ref[...] * jax.nn.sigmoid(b_ref[...])
    full_ref[0, halo:halo + tm, :] = u
    ext = jnp.concatenate([tail, u], axis=0)
    for p in range(1, phases):
        full_ref[p, halo - SUBLANES:halo - SUBLANES + tm, :] = pltpu.roll(ext, tm + SUBLANES - p, 0)[0:tm]
    off = halo - (kw - 1)
    rb = CONV_ROWS

    def body(r, carry):
        r0 = pl.multiple_of(r * rb, rb)
        for lt in range(c // LANES):
            cols = slice(lt * LANES, (lt + 1) * LANES)
            acc = None
            for p in range(phases):
                x = full_ref[p, pl.ds(r0, rb + halo), cols]
                for a in range(halo // phases + 1):
                    w = a * phases + p - off
                    if 0 <= w < kw and a * phases + rb <= rb + halo:
                        term = x[a * phases:a * phases + rb] * jnp.concatenate([cwb_ref[w, :, cols]] * (rb // SUBLANES), axis=0)
                        acc = term if acc is None else acc + term
            y_ref[pl.ds(r0, rb), cols] = acc
        return carry

    lax.fori_loop(0, tm // rb, body, 0)
    y = _ln_silu(y_ref[...] + cb_ref[...], lng_ref[...], lnb_ref[...])
    o_ref[...] = y.astype(o_ref.dtype)

    @pl.when(j == pl.num_programs(1) - 1)
    def _():
        st_ref[...] = full_ref[0, halo + tm - (kw - 1):halo + tm, :]


def _conv_prompt(z, cw, cb, lng, lnb, n_seq, seq_len):
    kw, c = cw.shape
    tm = ROW_TILE
    tps = seq_len // tm
    return pl.pallas_call(
        functools.partial(_conv_prompt_kernel, kw=kw),
        out_shape=(jax.ShapeDtypeStruct((n_seq * seq_len, c), BF16), jax.ShapeDtypeStruct((n_seq, kw - 1, c), F32)),
        grid=(n_seq, tps),
        in_specs=[
            pl.BlockSpec((tm, c), lambda b, j: (b * tps + j, 0)),
            pl.BlockSpec((tm, c), lambda b, j: (b * tps + j, 1)),
            pl.BlockSpec((kw, c), lambda b, j: (0, 0)),
            pl.BlockSpec((1, c), lambda b, j: (0, 0)),
            pl.BlockSpec((1, c), lambda b, j: (0, 0)),
            pl.BlockSpec((1, c), lambda b, j: (0, 0)),
        ],
        out_specs=(
            pl.BlockSpec((tm, c), lambda b, j: (b * tps + j, 0)),
            pl.BlockSpec((None, kw - 1, c), lambda b, j: (b, 0, 0)),
        ),
        scratch_shapes=[pltpu.VMEM((SUBLANES, tm + CONV_HALO, c), F32), pltpu.VMEM((kw, SUBLANES, c), F32),
                        pltpu.VMEM((tm, c), F32)],
        compiler_params=_params("arbitrary", "arbitrary"),
        name="conv_prompt",
    )(z, z, cw, cb.reshape(1, c), lng.reshape(1, c), lnb.reshape(1, c))


def _conv_sample_kernel(*refs, kw, dec_seq):
    a_refs = refs[0:dec_seq]
    b_refs = refs[dec_seq:2 * dec_seq]
    st_ref, cw_ref, cb_ref, lng_ref, lnb_ref, o_ref, u_ref = refs[2 * dec_seq:]
    hist = kw - 1
    u = [a_refs[t][...] * jax.nn.sigmoid(b_refs[t][...]) for t in range(dec_seq)]
    def row(j):
        return st_ref[j] if j < hist else u[j - hist]

    for t in range(dec_seq):
        acc = row(t) * cw_ref[0:1, :]
        for w in range(1, kw):
            acc = acc + row(t + w) * cw_ref[w:w + 1, :]
        y = _ln_silu(acc + cb_ref[...], lng_ref[...], lnb_ref[...])
        o_ref[t] = y.astype(o_ref.dtype)
        u_ref[t] = u[t]


def _conv_sample(z, st_t, layer, cw, cb, lng, lnb, row0, dec_seq, dec_batch):
    kw, c = cw.shape
    bs = 16
    a_specs = [pl.BlockSpec((bs, c), lambda s, t=t: ((row0 + t * dec_batch) // bs + s, 0)) for t in range(dec_seq)]
    b_specs = [pl.BlockSpec((bs, c), lambda s, t=t: ((row0 + t * dec_batch) // bs + s, 1)) for t in range(dec_seq)]
    vec = pl.BlockSpec((1, c), lambda s: (0, 0))
    return pl.pallas_call(
        functools.partial(_conv_sample_kernel, kw=kw, dec_seq=dec_seq),
        out_shape=(jax.ShapeDtypeStruct((dec_seq, dec_batch, c), BF16), jax.ShapeDtypeStruct((dec_seq, dec_batch, c), F32)),
        grid=(dec_batch // bs,),
        in_specs=[*a_specs, *b_specs, pl.BlockSpec((None, kw - 1, bs, c), lambda s: (layer, 0, s, 0)),
                  pl.BlockSpec((kw, c), lambda s: (0, 0)), vec, vec, vec],
        out_specs=(pl.BlockSpec((dec_seq, bs, c), lambda s: (0, s, 0)), pl.BlockSpec((dec_seq, bs, c), lambda s: (0, s, 0))),
        compiler_params=_params("arbitrary"),
        name="conv_sample",
    )(*([z] * (2 * dec_seq)), st_t, cw, cb.reshape(1, c), lng.reshape(1, c), lnb.reshape(1, c))


def _conv_state_kernel(st_ref, u_ref, o_ref):
    hist, n_new = st_ref.shape[0], u_ref.shape[0]
    for j in range(hist - n_new):
        o_ref[j] = st_ref[j + n_new]
    for t in range(n_new):
        o_ref[hist - n_new + t] = u_ref[t]


def _conv_state_sample(st_t, u_all):
    depth, hist, nb, c = st_t.shape
    n_new = u_all.shape[1]
    assert n_new <= hist
    bs = 16
    return pl.pallas_call(
        _conv_state_kernel,
        out_shape=jax.ShapeDtypeStruct(st_t.shape, st_t.dtype),
        grid=(depth, nb // bs),
        in_specs=[pl.BlockSpec((None, hist, bs, c), lambda l, s: (l, 0, s, 0)),
                  pl.BlockSpec((None, n_new, bs, c), lambda l, s: (l, 0, s, 0))],
        out_specs=pl.BlockSpec((None, hist, bs, c), lambda l, s: (l, 0, s, 0)),
        compiler_params=_params("arbitrary", "arbitrary"),
        name="conv_state",
    )(st_t, u_all)


def _gla_prompt_kernel(qk_ref, v_ref, g_ref, la_ref, gn_ref, o_ref, sfin_ref, st_ref, sn_ref, *, heads, dk, dv):
    j = pl.program_id(1)
    tm = qk_ref.shape[0]
    ck = GLA_CHUNK
    nch = tm // ck
    qkw = heads * dk

    @pl.when(j == 0)
    def _():
        st_ref[...] = jnp.zeros(st_ref.shape, F32)

    la = la_ref[...]
    sb = GLA_BLOCK
    blocks = [slice(r, r + sb) for r in range(0, tm, sb)]
    row = lax.broadcasted_iota(jnp.int32, (sb, sb), 0)
    col = lax.broadcasted_iota(jnp.int32, (sb, sb), 1)
    same = (row // ck) == (col // ck)
    causal = same & (col <= row)
    tri_incl = causal.astype(BF16)
    tri_after = (same & (col > row)).astype(BF16)
    sel = (lax.broadcasted_iota(jnp.int32, (nch, tm), 1) // ck == lax.broadcasted_iota(jnp.int32, (nch, tm), 0)).astype(BF16)
    parts = _split3(la)

    def blockwise(tri):
        return jnp.concatenate([sum(jnp.dot(tri, p[rs], preferred_element_type=F32) for p in parts) for rs in blocks], axis=0)

    b = blockwise(tri_incl)
    rest = blockwise(tri_after)
    tot = sum(jnp.dot(sel, p, preferred_element_type=F32) for p in parts)
    qk = qk_ref[...]
    q = qk[:, :qkw] * (dk ** -0.5)
    k = qk[:, qkw:]
    q_dec = (q * jnp.exp(b)).astype(BF16)
    k_inv = (k * jnp.exp(-b)).astype(BF16)
    k_end = _round_bf16(k * jnp.exp(rest))
    decay = jnp.exp(tot)
    v_all = v_ref[...]
    g_all = g_ref[...]
    for h in range(heads):
        ks = slice(h * dk, (h + 1) * dk)
        vs = slice(h * dv, (h + 1) * dv)
        qd, ki, ke = q_dec[:, ks], k_inv[:, ks], k_end[:, ks]
        vh = v_all[:, vs]
        vb = vh.astype(BF16)
        vr = _round_bf16(vh)
        intra = []
        for rs in blocks:
            att = lax.dot_general(qd[rs], ki[rs], (((1,), (1,)), ((), ())), preferred_element_type=F32)
            att = jnp.where(causal, att, 0.0).astype(BF16)
            intra.append(jnp.dot(att, vb[rs], preferred_element_type=F32))
        o = jnp.concatenate(intra, axis=0)
        s = st_ref[h]
        for n in range(nch):
            rs = slice(n * ck, (n + 1) * ck)
            sn_ref[n] = s.astype(BF16)
            upd = lax.dot_general(vr[rs], ke[rs], (((0,), (0,)), ((), ())), preferred_element_type=F32)
            s = s * decay[n:n + 1, ks] + upd
        st_ref[h] = s
        inter = [lax.dot_general(qd[n * ck:(n + 1) * ck], sn_ref[n], (((1,), (1,)), ((), ())), preferred_element_type=F32)
                 for n in range(nch)]
        o = o + jnp.concatenate(inter, axis=0)
        o = _rms(o, gn_ref[:, vs]) * _silu(g_all[:, vs])
        o_ref[:, vs] = o.astype(o_ref.dtype)

    @pl.when(j == pl.num_programs(1) - 1)
    def _():
        for h in range(heads):
            sfin_ref[h] = st_ref[h].T


def _gla_prompt(z, la, gn, n_seq, seq_len, heads, dk, dv):
    tm = ROW_TILE
    tps = seq_len // tm
    w = heads * dv
    qkw = heads * dk
    assert 2 * qkw == w
    return pl.pallas_call(
        functools.partial(_gla_prompt_kernel, heads=heads, dk=dk, dv=dv),
        out_shape=(jax.ShapeDtypeStruct((n_seq * seq_len, w), BF16), jax.ShapeDtypeStruct((n_seq, heads, dk, dv), F32)),
        grid=(n_seq, tps),
        in_specs=[
            pl.BlockSpec((tm, w), lambda b, j: (b * tps + j, 2)),
            pl.BlockSpec((tm, w), lambda b, j: (b * tps + j, 3)),
            pl.BlockSpec((tm, w), lambda b, j: (b * tps + j, 4)),
            pl.BlockSpec((tm, qkw), lambda b, j: (b * tps + j, 0)),
            pl.BlockSpec((1, w), lambda b, j: (0, 0)),
        ],
        out_specs=(
            pl.BlockSpec((tm, w), lambda b, j: (b * tps + j, 0)),
            pl.BlockSpec((None, heads, dk, dv), lambda b, j: (b, 0, 0, 0)),
        ),
        scratch_shapes=[pltpu.VMEM((heads, dv, dk), F32), pltpu.VMEM((tm // GLA_CHUNK, dv, dk), BF16)],
        compiler_params=_params("arbitrary", "arbitrary"),
        name="gla_prompt",
    )(z, z, z, la, gn.reshape(1, w))


def _gla_sample_kernel(q_ref, k_ref, v_ref, g_ref, la_ref, s_ref, gn_ref, *rest, heads, dk, dv, n_prev):
    bs, ln, _ = q_ref.shape
    if n_prev:
        prev_refs = rest[:n_prev]
        o_ref, all_hbm, ns_buf, sem_out, sem_cp = rest[n_prev:]
        i = pl.program_id(0)
        n = pl.num_programs(0)
        slot = lax.rem(i, 2)
        ns_ref = ns_buf.at[slot]

        def out_copy(step, sl):
            return pltpu.make_async_copy(ns_buf.at[sl], all_hbm.at[n_prev, pl.ds(step * bs, bs)], sem_out.at[sl])

        def prev_copy(step, p):
            rows = pl.ds(step * bs, bs)
            return pltpu.make_async_copy(prev_refs[p].at[rows], all_hbm.at[p, rows], sem_cp.at[p])

        for p in range(n_prev):
            prev_copy(i, p).start()

        @pl.when(i >= 2)
        def _():
            out_copy(i - 2, slot).wait()
    else:
        o_ref, ns_ref = rest
    tril = lax.broadcasted_iota(jnp.int32, (ln, ln), 1) <= lax.broadcasted_iota(jnp.int32, (ln, ln), 0)
    for s in range(bs):
        q_s, k_s, v_s, g_s, la_s = q_ref[s], k_ref[s], v_ref[s], g_ref[s], la_ref[s]
        for h in range(heads):
            ks = slice(h * dk, (h + 1) * dk)
            vs = slice(h * dv, (h + 1) * dv)
            la = la_s[:, ks]
            rows = [la[0:1]]
            for t in range(1, ln):
                rows.append(rows[-1] + la[t:t + 1])
            b = jnp.concatenate(rows, axis=0)
            b_last = rows[-1]
            q_dec = _round_bf16(q_s[:, ks] * (dk ** -0.5) * jnp.exp(b))
            k_inv = _round_bf16(k_s[:, ks] * jnp.exp(-b))
            k_end = _round_bf16(k_s[:, ks] * jnp.exp(b_last - b))
            vr = _round_bf16(v_s[:, vs])
            s0 = s_ref[s, h]
            att = lax.dot_general(q_dec, k_inv, (((1,), (1,)), ((), ())), preferred_element_type=F32)
            att = _round_bf16(jnp.where(tril, att, 0.0))
            o = jnp.dot(att, vr, preferred_element_type=F32) + jnp.dot(q_dec, _round_bf16(s0), preferred_element_type=F32)
            upd = lax.dot_general(k_end, vr, (((0,), (0,)), ((), ())), preferred_element_type=F32)
            d_col = jnp.broadcast_to(jnp.exp(b_last), (dk, dk)).T
            ns_ref[s, h] = s0 * jnp.concatenate([d_col] * (dv // dk), axis=1) + upd
            o = _rms(o, gn_ref[:, vs]) * _silu(g_s[:, vs])
            o_ref[s, :, vs] = o.astype(o_ref.dtype)
    if n_prev:
        out_copy(i, slot).start()

        @pl.when(i == n - 1)
        def _():
            @pl.when(i >= 1)
            def _():
                out_copy(i - 1, 1 - slot).wait()

            out_copy(i, slot).wait()

            def drain(step, carry):
                for p in range(n_prev):
                    prev_copy(step, p).wait()
                return carry

            lax.fori_loop(0, n, drain, 0)


def _gla_sample(q3, k3, v3, g3, la3, state, layer, gn, heads, dk, dv, prev_states=()):
    nb, ln, w = v3.shape
    qkw = heads * dk
    bs = 8
    n_prev = len(prev_states)
    seq_spec = pl.BlockSpec((bs, ln, qkw), lambda s: (s, 0, 0))
    wide_spec = pl.BlockSpec((bs, ln, w), lambda s: (s, 0, 0))
    in_specs = [seq_spec, seq_spec, wide_spec, wide_spec, seq_spec,
                pl.BlockSpec((None, bs, heads, dk, dv), lambda s: (layer, s, 0, 0, 0)),
                pl.BlockSpec((1, w), lambda s: (0, 0)),
                *[pl.BlockSpec(memory_space=pl.ANY)] * n_prev]
    if n_prev:
        state_shape = jax.ShapeDtypeStruct((n_prev + 1, nb, heads, dk, dv), F32)
        state_spec = pl.BlockSpec(memory_space=pl.ANY)
        scratch = [pltpu.VMEM((2, bs, heads, dk, dv), F32), pltpu.SemaphoreType.DMA((2,)),
                   pltpu.SemaphoreType.DMA((n_prev,))]
    else:
        state_shape = jax.ShapeDtypeStruct((nb, heads, dk, dv), F32)
        state_spec = pl.BlockSpec((bs, heads, dk, dv), lambda s: (s, 0, 0, 0))
        scratch = []
    return pl.pallas_call(
        functools.partial(_gla_sample_kernel, heads=heads, dk=dk, dv=dv, n_prev=n_prev),
        out_shape=(jax.ShapeDtypeStruct((nb, ln, w), F32), state_shape),
        grid=(nb // bs,),
        in_specs=in_specs,
        out_specs=(wide_spec, state_spec),
        scratch_shapes=scratch,
        compiler_params=_params("arbitrary"),
        name="gla_sample_last" if n_prev else "gla_sample",
    )(q3, k3, v3, g3, la3, state, gn.reshape(1, w), *prev_states)


def _outproj_kernel(cp_ref, op_ref, cs_ref, os_ref, xp_ref, xs_ref, w_ref, gp, gs, y_ref, wb_ref, *, npt, dec_seq):
    i = pl.program_id(1)

    @pl.when(i == 0)
    def _():
        wb_ref[...] = w_ref[...].astype(BF16)

    half = cp_ref.shape[1]

    def mixed(c_ref, o_ref):
        return (jnp.dot(c_ref[...], wb_ref[0:half, :], preferred_element_type=F32)
                + jnp.dot(o_ref[...], wb_ref[half:, :], preferred_element_type=F32))

    @pl.when(i < npt)
    def _():
        y_ref[...] = xp_ref[...] + gp[...] * mixed(cp_ref, op_ref)

    @pl.when(i >= npt)
    def _():
        mix = mixed(cs_ref, os_ref)
        nb = gs.shape[0]
        for t in range(dec_seq):
            rows = slice(t * nb, (t + 1) * nb)
            y_ref[rows, :] = xs_ref[rows, :] + gs[...] * mix[rows]


def _outproj(conv_p, gla_p, conv_s, gla_s, xp, xs, xs_blk, w_out, layer, mod_p, mod_s, dims):
    d = xp.shape[1]
    half = conv_p.shape[1]
    tm, tn = ROW_TILE, 1024
    nj = d // tn
    npt = dims["npt"]
    t = (npt + 1) * tm
    p_specs, s_specs = _mod_specs((2,), tn, dims["npt"], dims["tps"], dims["n_seq"], dims["dec_batch"],
                                  row_axis=1, col_fn=lambda c, idx: c * nj + idx[0])
    return pl.pallas_call(
        functools.partial(_outproj_kernel, npt=dims["npt"], dec_seq=dims["dec_seq"]),
        out_shape=jax.ShapeDtypeStruct((t, d), F32),
        grid=(nj, t // tm),
        in_specs=[
            pl.BlockSpec((tm, half), lambda j, i: (jnp.minimum(i, npt - 1), 0)),
            pl.BlockSpec((tm, half), lambda j, i: (jnp.minimum(i, npt - 1), 0)),
            pl.BlockSpec((tm, half), lambda j, i: (0, 0)),
            pl.BlockSpec((tm, half), lambda j, i: (0, 0)),
            pl.BlockSpec((tm, tn), lambda j, i: (jnp.minimum(i, npt - 1), j)),
            pl.BlockSpec((tm, tn), lambda j, i: (xs_blk, j)),
            pl.BlockSpec((None, d, tn), lambda j, i: (layer, 0, j)),
            *p_specs,
            *s_specs,
        ],
        out_specs=pl.BlockSpec((tm, tn), lambda j, i: (i, j)),
        scratch_shapes=[pltpu.VMEM((d, tn), BF16)],
        compiler_params=_params("arbitrary", "arbitrary"),
        name="outproj",
    )(conv_p, gla_p, conv_s, gla_s, xp, xs, w_out, mod_p, mod_s)


def _norm2_kernel(x_ref, g_ref, shp, scp, shs, scs, wr_hi, wr_lo, br_ref, hb_ref, ids_ref, wts_ref, h_ref, *, npt, dec_seq, n_grp, n_exp):
    i = pl.program_id(0)
    y = _rms(x_ref[...], g_ref[...])
    _store_by_group(i, npt, dec_seq, h_ref, lambda v, m: v[0] * (1.0 + m[1]) + m[0], [y], [shp, scp], [shs, scs])
    h = h_ref[...]
    h_hi = h.astype(BF16)
    hb_ref[...] = h_hi
    h_lo = (h - h_hi.astype(F32)).astype(BF16)
    logits = (jnp.dot(h_hi, wr_hi[...], preferred_element_type=F32) + jnp.dot(h_lo, wr_hi[...], preferred_element_type=F32)
              + jnp.dot(h_hi, wr_lo[...], preferred_element_type=F32)) + br_ref[...]
    lane = lax.broadcasted_iota(jnp.int32, logits.shape, 1).astype(F32)
    big = jnp.float32(LANES)
    neg = jnp.float32(-jnp.inf)
    gl = jnp.where(lane < n_grp, logits, neg)
    gmax = jnp.max(gl, axis=-1, keepdims=True)
    gidx = jnp.min(jnp.where(gl == gmax, lane, big), axis=-1, keepdims=True)
    g_w = 1.0 / jnp.sum(jnp.exp(gl - gmax), axis=-1, keepdims=True)
    lo = n_grp + gidx * n_exp
    in_grp = (lane >= lo) & (lane < lo + n_exp)
    sl = jnp.where(in_grp, logits, neg)
    p = jnp.exp(sl - jnp.max(sl, axis=-1, keepdims=True))
    p = p / jnp.sum(p, axis=-1, keepdims=True)
    p = jnp.where(in_grp, p, -1.0)
    p1 = jnp.max(p, axis=-1, keepdims=True)
    i1 = jnp.min(jnp.where(p == p1, lane, big), axis=-1, keepdims=True)
    p_rest = jnp.where(lane == i1, -1.0, p)
    p2 = jnp.max(p_rest, axis=-1, keepdims=True)
    i2 = jnp.min(jnp.where(p_rest == p2, lane, big), axis=-1, keepdims=True)
    denom = p1 + p2
    ids_ref[...] = jnp.where(lane == 0, i1 - n_grp, jnp.where(lane == 1, i2 - n_grp, 0.0)).astype(jnp.int32)
    wts_ref[...] = jnp.where(lane == 0, g_w * (p1 / denom), jnp.where(lane == 1, g_w * (p2 / denom), 0.0))


def _norm2(x, g, mod_p, mod_s, wr_hi, wr_lo, br, dims, n_grp, n_exp):
    t, d = x.shape
    tm = ROW_TILE
    p_specs, s_specs = _mod_specs((3, 4), d, dims["npt"], dims["tps"], dims["n_seq"], dims["dec_batch"])
    return pl.pallas_call(
        functools.partial(_norm2_kernel, npt=dims["npt"], dec_seq=dims["dec_seq"], n_grp=n_grp, n_exp=n_exp),
        out_shape=(jax.ShapeDtypeStruct((t, d), BF16), jax.ShapeDtypeStruct((t, LANES), jnp.int32),
                   jax.ShapeDtypeStruct((t, LANES), F32)),
        grid=(t // tm,),
        in_specs=[
            pl.BlockSpec((tm, d), lambda i: (i, 0)),
            pl.BlockSpec((1, d), lambda i: (0, 0)),
            *p_specs,
            *s_specs,
            pl.BlockSpec((d, LANES), lambda i: (0, 0)),
            pl.BlockSpec((d, LANES), lambda i: (0, 0)),
            pl.BlockSpec((1, LANES), lambda i: (0, 0)),
        ],
        out_specs=(pl.BlockSpec((tm, d), lambda i: (i, 0)), pl.BlockSpec((tm, LANES), lambda i: (i, 0)),
                   pl.BlockSpec((tm, LANES), lambda i: (i, 0))),
        scratch_shapes=[pltpu.VMEM((tm, d), F32)],
        compiler_params=_params("arbitrary"),
        name="norm2_router",
    )(x, g.reshape(1, d), mod_p, mod_p, mod_s, mod_s, wr_hi, wr_lo, br)


def _segments(ids, n_experts, n_tiles_max):
    t = ids.shape[0]
    ts, te, al = SORT_TILE, EXPERT_TILE, SEG_ALIGN
    nt = t // ts
    e_iota = jnp.arange(n_experts, dtype=jnp.int32)
    cnt = jnp.sum((ids.reshape(nt, ts * TOP_K, 1) == e_iota).astype(jnp.int32), axis=1)
    size = (cnt + al - 1) // al * al
    src = jnp.cumsum(size, axis=1) - size
    tot = jnp.sum(size, axis=0)
    tot_pad = (tot + te - 1) // te * te
    ends = jnp.cumsum(tot_pad)
    exp_off = ends - tot_pad
    dst = exp_off[None, :] + jnp.cumsum(size, axis=0) - size
    n_used = (ends[-1] // te).astype(jnp.int32)
    tile_start = jnp.arange(n_tiles_max, dtype=jnp.int32) * te
    tile_expert = jnp.minimum(jnp.sum(ends[None, :] <= tile_start[:, None], axis=1), n_experts - 1).astype(jnp.int32)
    last = tile_expert[jnp.maximum(n_used - 1, 0)]
    tile_expert = jnp.where(jnp.arange(n_tiles_max) < n_used, tile_expert, last)
    return dict(src=src.reshape(-1).astype(jnp.int32), dst=dst.reshape(-1).astype(jnp.int32),
                size=size.reshape(-1).astype(jnp.int32), tile_tot=jnp.sum(size, axis=1).astype(jnp.int32),
                fill_start=(exp_off + tot).astype(jnp.int32), fill_size=(tot_pad - tot).astype(jnp.int32),
                tile_expert=tile_expert, n_used=n_used.reshape(1))


def _lane_pick(x, lane, k):
    return jnp.sum(jnp.where(lane == k, x, 0.0), axis=-1, keepdims=True)


def _sorted_positions(ids, lane):
    ts = ids.shape[0]
    onehot = [(lane == _lane_pick(ids, lane, k)).astype(F32) for k in range(TOP_K)]
    row = lax.broadcasted_iota(jnp.int32, (ts, ts), 0)
    col = lax.broadcasted_iota(jnp.int32, (ts, ts), 1)
    before = (col < row).astype(BF16)
    earlier = [jnp.dot(before, o.astype(BF16), preferred_element_type=F32) for o in onehot]
    cnt = [jnp.sum(o, axis=0, keepdims=True) for o in onehot]
    size = jnp.ceil((cnt[0] + cnt[1]) * (1.0 / SEG_ALIGN)) * SEG_ALIGN
    er = lax.broadcasted_iota(jnp.int32, (LANES, LANES), 0)
    ec = lax.broadcasted_iota(jnp.int32, (LANES, LANES), 1)
    start = jnp.dot(jnp.broadcast_to(size, (SUBLANES, LANES)).astype(BF16), (er < ec).astype(BF16),
                    preferred_element_type=F32)[0:1]
    base = [start + earlier[0], start + cnt[0] + earlier[1]]
    return [jnp.sum(onehot[k] * base[k], axis=-1, keepdims=True) for k in range(TOP_K)]


def _as_row(col_vals, lane):
    hi = jnp.floor(col_vals * (1.0 / 32.0))
    lo = col_vals - hi * 32.0
    ones = jnp.ones((SUBLANES, LANES), BF16)
    nt = (((1,), (1,)), ((), ()))
    hi_row = lax.dot_general(ones, jnp.where(lane == 0, hi, 0.0).astype(BF16), nt, preferred_element_type=F32)
    lo_row = lax.dot_general(ones, jnp.where(lane == 0, lo, 0.0).astype(BF16), nt, preferred_element_type=F32)
    return (hi_row * 32.0 + lo_row)[0:1]


def _dispatch_kernel(src_ref, dst_ref, size_ref, tot_ref, fst_ref, fsz_ref, nu_ref, h_ref, ids_ref, wts_ref, xe_hbm,
                     we_hbm, pos_ref, sbuf, wbuf, zx, zw, semx, semw, semz, semt, *, n_experts):
    i = pl.program_id(0)
    n = pl.num_programs(0)
    ts = h_ref.shape[0]
    rows = sbuf.shape[1]
    te = zx.shape[0]
    slot = lax.rem(i, 2)

    def tail_copies(tl):
        r0 = pl.multiple_of(tl * te, te)
        return (pltpu.make_async_copy(zx, xe_hbm.at[pl.ds(r0, te), :], semt.at[0]),
                pltpu.make_async_copy(zw, we_hbm.at[pl.ds(r0, te), :], semt.at[1]))

    def for_tail(action):
        def body(tl, carry):
            for c in tail_copies(tl):
                action(c)
            return carry

        lax.fori_loop(nu_ref[0], xe_hbm.shape[0] // te, body, 0)

    def seg_copies(tile, sl, e):
        k = tile * n_experts + e
        sz = pl.multiple_of(size_ref[k], SEG_ALIGN)
        s0 = pl.multiple_of(src_ref[k], SEG_ALIGN)
        d0 = pl.multiple_of(dst_ref[k], SEG_ALIGN)
        return sz, (pltpu.make_async_copy(sbuf.at[sl, pl.ds(s0, sz), :], xe_hbm.at[pl.ds(d0, sz), :], semx.at[sl]),
                    pltpu.make_async_copy(wbuf.at[sl, pl.ds(s0, sz), :], we_hbm.at[pl.ds(d0, sz), :], semw.at[sl]))

    def wait_tile(tile, sl):
        tot = pl.multiple_of(tot_ref[tile], SEG_ALIGN)
        pltpu.make_async_copy(sbuf.at[sl, pl.ds(0, tot), :], xe_hbm.at[pl.ds(0, tot), :], semx.at[sl]).wait()
        pltpu.make_async_copy(wbuf.at[sl, pl.ds(0, tot), :], we_hbm.at[pl.ds(0, tot), :], semw.at[sl]).wait()

    @pl.when(i == 0)
    def _():
        zx[...] = jnp.zeros(zx.shape, zx.dtype)
        zw[...] = jnp.zeros(zw.shape, zw.dtype)

        def fill(e, carry):
            sz = pl.multiple_of(fsz_ref[e], SEG_ALIGN)
            d0 = pl.multiple_of(fst_ref[e], SEG_ALIGN)

            @pl.when(sz > 0)
            def _():
                cx = pltpu.make_async_copy(zx.at[pl.ds(0, sz), :], xe_hbm.at[pl.ds(d0, sz), :], semz.at[0])
                cw = pltpu.make_async_copy(zw.at[pl.ds(0, sz), :], we_hbm.at[pl.ds(d0, sz), :], semz.at[1])
                cx.start()
                cw.start()
                cx.wait()
                cw.wait()

            return carry

        lax.fori_loop(0, n_experts, fill, 0)
        for_tail(lambda c: c.start())

    @pl.when(i >= 2)
    def _():
        wait_tile(i - 2, slot)

    lane = lax.broadcasted_iota(jnp.int32, (ts, LANES), 1).astype(F32)
    pos = _sorted_positions(ids_ref[...].astype(F32), lane)
    pos_ref[...] = jnp.where(lane == 0, pos[0], jnp.where(lane == 1, pos[1], 0.0))
    r_iota = lax.broadcasted_iota(jnp.int32, (rows, ts), 0).astype(F32)
    assert rows <= 1024
    sel = [(r_iota == _as_row(p, lane)).astype(BF16) for p in pos]
    sbuf[slot] = jnp.dot(sel[0] + sel[1], h_ref[...], preferred_element_type=F32).astype(BF16)
    wts = wts_ref[...]
    wsorted = jnp.zeros((rows, LANES), F32)
    for k in range(TOP_K):
        pieces = _split3(_lane_pick(wts, lane, k))
        wk = jnp.where(lane == 0, pieces[0].astype(F32), jnp.where(lane == 1, pieces[1].astype(F32),
                       jnp.where(lane == 2, pieces[2].astype(F32), 0.0)))
        wsorted = wsorted + jnp.dot(sel[k], wk.astype(BF16), preferred_element_type=F32)
    wbuf[slot] = wsorted

    def issue(e, carry):
        sz, copies = seg_copies(i, slot, e)

        @pl.when(sz > 0)
        def _():
            for c in copies:
                c.start()

        return carry

    lax.fori_loop(0, n_experts, issue, 0)

    @pl.when(i == n - 1)
    def _():
        @pl.when(i >= 1)
        def _():
            wait_tile(i - 1, 1 - slot)

        wait_tile(i, slot)
        for_tail(lambda c: c.wait())


def _dispatch_rows(hb, ids, wts, seg, n_experts, n_tiles_max):
    t, d = hb.shape
    ts, te = SORT_TILE, EXPERT_TILE
    rows = TOP_K * ts + n_experts * SEG_ALIGN
    n_slots = n_tiles_max * te

    def tile_map(i, *_):
        return (i, 0)

    return pl.pallas_call(
        functools.partial(_dispatch_kernel, n_experts=n_experts),
        out_shape=(jax.ShapeDtypeStruct((n_slots, d), BF16), jax.ShapeDtypeStruct((n_slots, LANES), F32),
                   jax.ShapeDtypeStruct((t, LANES), F32)),
        grid_spec=pltpu.PrefetchScalarGridSpec(
            num_scalar_prefetch=7,
            grid=(t // ts,),
            in_specs=[pl.BlockSpec((ts, d), tile_map), pl.BlockSpec((ts, LANES), tile_map),
                      pl.BlockSpec((ts, LANES), tile_map)],
            out_specs=(pl.BlockSpec(memory_space=pl.ANY), pl.BlockSpec(memory_space=pl.ANY),
                       pl.BlockSpec((ts, LANES), tile_map)),
            scratch_shapes=[
                pltpu.VMEM((2, rows, d), BF16),
                pltpu.VMEM((2, rows, LANES), F32),
                pltpu.VMEM((te, d), BF16),
                pltpu.VMEM((te, LANES), F32),
                pltpu.SemaphoreType.DMA((2,)),
                pltpu.SemaphoreType.DMA((2,)),
                pltpu.SemaphoreType.DMA((2,)),
                pltpu.SemaphoreType.DMA((2,)),
            ],
        ),
        compiler_params=_params("arbitrary"),
        name="dispatch",
    )(seg["src"], seg["dst"], seg["size"], seg["tile_tot"], seg["fill_start"], seg["fill_size"], seg["n_used"],
      hb, ids, wts)


def _expert_rows_kernel(te_ref, nu_ref, x_ref, w_ref, wg_ref, wu_ref, wd_ref, o_ref, wgb, wub, wdb):
    i = pl.program_id(0)
    n_used = nu_ref[0]

    @pl.when(i < n_used)
    def _():
        changed = jnp.logical_or(i == 0, te_ref[i] != te_ref[jnp.maximum(i - 1, 0)])

        @pl.when(changed)
        def _():
            wgb[...] = wg_ref[...].astype(BF16)
            wub[...] = wu_ref[...].astype(BF16)
            wdb[...] = wd_ref[...].astype(BF16)

        x = x_ref[...]
        a = jnp.dot(x, wgb[...], preferred_element_type=F32)
        u = jnp.dot(x, wub[...], preferred_element_type=F32)
        w = jnp.sum(w_ref[...], axis=-1, keepdims=True)
        hid = _silu(a) * u * w
        o_ref[...] = jnp.dot(hid.astype(BF16), wdb[...], preferred_element_type=F32).astype(o_ref.dtype)


def _expert_rows(xe, we, seg, wg, wu, wd, layer_base):
    n_slots, d = xe.shape
    f = wg.shape[-1]
    te = EXPERT_TILE

    def w_map(i, te_ref, nu_ref):
        return (layer_base + te_ref[i], 0, 0)

    def row_map(i, te_ref, nu_ref):
        return (jnp.minimum(i, nu_ref[0] - 1), 0)

    return pl.pallas_call(
        _expert_rows_kernel,
        out_shape=jax.ShapeDtypeStruct((n_slots, d), BF16),
        grid_spec=pltpu.PrefetchScalarGridSpec(
            num_scalar_prefetch=2,
            grid=(n_slots // te,),
            in_specs=[
                pl.BlockSpec((te, d), row_map),
                pl.BlockSpec((te, LANES), row_map),
                pl.BlockSpec((None, d, f), w_map),
                pl.BlockSpec((None, d, f), w_map),
                pl.BlockSpec((None, f, d), w_map),
            ],
            out_specs=pl.BlockSpec((te, d), row_map),
            scratch_shapes=[pltpu.VMEM((d, f), BF16), pltpu.VMEM((d, f), BF16), pltpu.VMEM((f, d), BF16)],
        ),
        input_output_aliases={2: 0},
        compiler_params=_params("arbitrary"),
        name="experts",
    )(seg["tile_expert"], seg["n_used"], xe, we, wg, wu, wd)


def _collect_kernel(src_ref, dst_ref, size_ref, tot_ref, y_hbm, x_ref, pos_ref, gp, gs, *rest, npt, n_experts, final):
    if final:
        fg_ref, op_ref, os_ref, buf, sem, o_ref = rest
    else:
        o_ref, buf, sem = rest
    i = pl.program_id(0)
    n = pl.num_programs(0)
    ts = x_ref.shape[0]
    rows = buf.shape[1]
    slot = lax.rem(i, 2)

    def start(tile, sl):
        def body(e, carry):
            k = tile * n_experts + e
            sz = pl.multiple_of(size_ref[k], SEG_ALIGN)
            s0 = pl.multiple_of(src_ref[k], SEG_ALIGN)
            d0 = pl.multiple_of(dst_ref[k], SEG_ALIGN)

            @pl.when(sz > 0)
            def _():
                pltpu.make_async_copy(y_hbm.at[pl.ds(d0, sz), :], buf.at[sl, pl.ds(s0, sz), :], sem.at[sl]).start()

            return carry

        lax.fori_loop(0, n_experts, body, 0)

    @pl.when(i == 0)
    def _():
        buf[...] = jnp.zeros(buf.shape, buf.dtype)
        start(0, 0)

    @pl.when(i + 1 < n)
    def _():
        start(i + 1, 1 - slot)

    tot = pl.multiple_of(tot_ref[i], SEG_ALIGN)
    pltpu.make_async_copy(y_hbm.at[pl.ds(0, tot), :], buf.at[slot, pl.ds(0, tot), :], sem.at[slot]).wait()
    lane = lax.broadcasted_iota(jnp.int32, (ts, LANES), 1).astype(F32)
    pos = pos_ref[...]
    r_iota = lax.broadcasted_iota(jnp.int32, (ts, rows), 1).astype(F32)
    pick = ((r_iota == _lane_pick(pos, lane, 0)) | (r_iota == _lane_pick(pos, lane, 1))).astype(BF16)
    ff = jnp.dot(pick, buf[slot], preferred_element_type=F32)
    _store_by_group(i, npt, None, o_ref, lambda v, m: v[0] + m[0] * v[1], [x_ref[...], ff], [gp], [gs])
    if final:
        y = _rms(o_ref[...], fg_ref[...])

        @pl.when(i < npt)
        def _():
            op_ref[...] = y

        @pl.when(i >= npt)
        def _():
            os_ref[...] = y


def _collect(ye, pos, x, seg, mod_p, mod_s, dims, n_experts, final_g=None):
    t, d = x.shape
    ts = SORT_TILE
    npt = dims["npt"]
    rows = TOP_K * ts + n_experts * SEG_ALIGN
    p_specs, s_specs = _mod_specs((5,), d, npt, dims["tps"], dims["n_seq"], dims["dec_batch"])
    final = final_g is not None

    def strip(spec):
        return pl.BlockSpec(spec.block_shape, lambda i, *_, m=spec.index_map: m(i))

    def tile_map(i, *_):
        return (i, 0)

    in_specs = [
        pl.BlockSpec(memory_space=pl.ANY),
        pl.BlockSpec((ts, d), tile_map),
        pl.BlockSpec((ts, LANES), tile_map),
        *[strip(s) for s in p_specs],
        *[strip(s) for s in s_specs],
    ]
    scratch = [pltpu.VMEM((2, rows, d), BF16), pltpu.SemaphoreType.DMA((2,))]
    args = [seg["src"], seg["dst"], seg["size"], seg["tile_tot"], ye, x, pos, mod_p, mod_s]
    if final:
        in_specs.append(pl.BlockSpec((1, d), lambda i, *_: (0, 0)))
        args.append(final_g.reshape(1, d))
        out_shape = (jax.ShapeDtypeStruct((npt * ts, d), F32), jax.ShapeDtypeStruct((t - npt * ts, d), F32))
        out_specs = (pl.BlockSpec((ts, d), lambda i, *_: (jnp.minimum(i, npt - 1), 0)),
                     pl.BlockSpec((ts, d), lambda i, *_: (jnp.maximum(i - npt, 0), 0)))
        scratch.append(pltpu.VMEM((ts, d), F32))
    else:
        out_shape = jax.ShapeDtypeStruct((t, d), F32)
        out_specs = pl.BlockSpec((ts, d), tile_map)
    return pl.pallas_call(
        functools.partial(_collect_kernel, npt=npt, n_experts=n_experts, final=final),
        out_shape=out_shape,
        grid_spec=pltpu.PrefetchScalarGridSpec(
            num_scalar_prefetch=4, grid=(t // ts,), in_specs=in_specs, out_specs=out_specs, scratch_shapes=scratch),
        compiler_params=_params("arbitrary"),
        name="collect_final" if final else "collect",
    )(*args)


def kernel(x_prompt, x_sample, c_prompt, c_sample, state_conv, state_gla, w_ada, b_ada, norm1_g, norm2_g, w_in, conv_w, conv_b, conv_ln_g, conv_ln_b, gate_w2, gate_b, gla_norm_g, w_out, router_grp_w, router_grp_b, router_exp_w, router_exp_b, exp_w_gate, exp_w_up, exp_w_down, final_norm_g):
    n_seq, seq_len, d = x_prompt.shape
    dec_batch, dec_seq, _ = x_sample.shape
    depth = w_ada.shape[0]
    kw, d_conv = conv_w.shape[1:]
    heads, dv = gla_norm_g.shape[1:]
    rank, qkw = gate_w2.shape[1:]
    dk = qkw // heads
    n_grp, n_exp = router_exp_w.shape[2:]
    n_experts = n_grp * n_exp
    tp, ts = n_seq * seq_len, dec_batch * dec_seq
    t = tp + ts
    tm = ROW_TILE
    assert ts == tm and seq_len % tm == 0 and d_conv == heads * dv and kw - 1 <= CONV_HALO
    n_main = 2 * d_conv + 2 * qkw + 2 * heads * dv
    dims = dict(npt=tp // tm, tps=seq_len // tm, n_seq=n_seq, dec_batch=dec_batch, dec_seq=dec_seq)

    xp, xs, xs_blk = x_prompt.reshape(tp, d), x_sample.transpose(1, 0, 2).reshape(ts, d), 0
    w_in_t = jnp.swapaxes(w_in, 1, 2)
    st_t = jnp.transpose(state_conv, (0, 2, 1, 3))
    pad = (-n_seq) % 8
    c_all = jnp.concatenate([c_prompt, jnp.zeros((pad, d), F32), c_sample], axis=0)
    mod = _ada(c_all, w_ada, b_ada)
    st = SORT_TILE
    assert tm % st == 0 and st % dec_batch == 0
    dims_sort = dict(npt=tp // st, tps=seq_len // st, n_seq=n_seq, dec_batch=dec_batch, dec_seq=dec_seq)
    n_tiles_max = -(-(TOP_K * t + (t // st) * n_experts * (SEG_ALIGN - 1)) // EXPERT_TILE) + n_experts

    conv_p, gla_p, conv_u, gla_s = [], [], [], []
    for l in range(depth):
        mod_p = mod[l, :n_seq].reshape(n_seq, 1, 6 * d)
        mod_s = mod[l, n_seq + pad:]
        gw2 = jnp.pad(gate_w2[l], ((0, LANES - rank), (0, 0))).astype(BF16)
        h, la = _norm1(xp, xs, xs_blk, norm1_g[l], mod_p, mod_s, w_in_t, l, n_main, rank, gw2, gate_b[l], dims)
        z = _inproj(h, w_in_t, l, n_main)

        cv_p, cb_p = _conv_prompt(z, conv_w[l], conv_b[l], conv_ln_g[l], conv_ln_b[l], n_seq, seq_len)
        cv_s, u_s = _conv_sample(z, st_t, l, conv_w[l], conv_b[l], conv_ln_g[l], conv_ln_b[l], tp, dec_seq, dec_batch)
        conv_p.append(cb_p)
        conv_u.append(u_s)

        go_p, gs_p = _gla_prompt(z, la, gla_norm_g[l], n_seq, seq_len, heads, dk, dv)

        def seq_major(a):
            return a.reshape(dec_seq, dec_batch, a.shape[-1]).transpose(1, 0, 2)

        zs = z[tp:]
        q3 = seq_major(zs[:, 2 * d_conv:2 * d_conv + qkw])
        k3 = seq_major(zs[:, 2 * d_conv + qkw:2 * d_conv + 2 * qkw])
        v3 = seq_major(zs[:, 2 * d_conv + 2 * qkw:2 * d_conv + 2 * qkw + heads * dv])
        g3 = seq_major(zs[:, 2 * d_conv + 2 * qkw + heads * dv:n_main])
        go_s, gs_s = _gla_sample(q3, k3, v3, g3, seq_major(la[tp:]), state_gla, l, gla_norm_g[l], heads, dk, dv,
                                 prev_states=tuple(gla_s) if l + 1 == depth else ())
        go_s = go_s.transpose(1, 0, 2).reshape(ts, heads * dv).astype(BF16)
        gla_p.append(gs_p)
        gla_s.append(gs_s)

        x = _outproj(cv_p, go_p, cv_s.reshape(ts, d_conv), go_s, xp, xs, xs_blk, w_out, l, mod_p, mod_s, dims)

        wr = jnp.concatenate([router_grp_w[l], router_exp_w[l].reshape(d, n_experts)], axis=1)
        wr = jnp.pad(wr, ((0, 0), (0, LANES - wr.shape[1])))
        wr_hi = wr.astype(BF16)
        wr_lo = (wr - wr_hi.astype(F32)).astype(BF16)
        br = jnp.concatenate([router_grp_b[l], router_exp_b[l].reshape(-1)])
        br = jnp.pad(br, (0, LANES - br.shape[0])).reshape(1, LANES)
        hb, ids, wts = _norm2(x, norm2_g[l], mod_p, mod_s, wr_hi, wr_lo, br, dims, n_grp, n_exp)
        seg = _segments(ids[:, :TOP_K], n_experts, n_tiles_max)
        xe, we, pos = _dispatch_rows(hb, ids, wts, seg, n_experts, n_tiles_max)
        f = exp_w_gate.shape[-1]
        ye = _expert_rows(xe, we, seg, exp_w_gate.reshape(depth * n_experts, d, f),
                          exp_w_up.reshape(depth * n_experts, d, f), exp_w_down.reshape(depth * n_experts, f, d),
                          l * n_experts)
        if l + 1 < depth:
            x = _collect(ye, pos, x, seg, mod_p, mod_s, dims_sort, n_experts)
            xp, xs, xs_blk = x, x, dims["npt"]
        else:
            y_p, y_s = _collect(ye, pos, x, seg, mod_p, mod_s, dims_sort, n_experts, final_g=final_norm_g)

    y_prompt = y_p.reshape(n_seq, seq_len, d)
    y_sample = y_s.reshape(dec_seq, dec_batch, d).transpose(1, 0, 2)
    conv_s = jnp.transpose(_conv_state_sample(st_t, jnp.stack(conv_u)), (0, 2, 1, 3))
    gla_s_all = gla_s[-1] if depth > 1 else gla_s[0][None]
    return (y_prompt, y_sample, jnp.stack(conv_p), jnp.stack(gla_p), conv_s, gla_s_all)
```

```python
import functools

import jax
import jax.numpy as jnp
from jax import lax
from jax.experimental import pallas as pl
from jax.experimental.pallas import tpu as pltpu

F32 = jnp.float32
BF16 = jnp.bfloat16

EPS = 1e-6
GATE_TAU = 16.0
GLA_CHUNK = 32
GLA_BLOCK = 128
TOP_K = 2

ROW_TILE = 512
EXPERT_TILE = 512
SORT_TILE = 256
SEG_ALIGN = 16
CONV_ROWS = 64
CONV_HALO = 32
LANES = 128
SUBLANES = 8
VMEM_LIMIT = 56 * 1024 * 1024


def _params(*sem):
    return pltpu.CompilerParams(dimension_semantics=sem, vmem_limit_bytes=VMEM_LIMIT)


def _bdot(a, b):
    return jnp.dot(a.astype(BF16), b.astype(BF16), preferred_element_type=F32)


def _round_bf16(x):
    return x.astype(BF16).astype(F32)


def _split3(x):
    hi = x.astype(BF16)
    r = x - hi.astype(F32)
    mid = r.astype(BF16)
    lo = (r - mid.astype(F32)).astype(BF16)
    return hi, mid, lo


def _silu(x):
    return x * jax.nn.sigmoid(x)


def _store_by_group(i, n_prompt_tiles, dec_seq, out_ref, fn, vals, p_refs, s_refs):
    @pl.when(i < n_prompt_tiles)
    def _():
        out_ref[...] = fn(vals, [r[...] for r in p_refs]).astype(out_ref.dtype)

    @pl.when(i >= n_prompt_tiles)
    def _():
        mods = [r[...] for r in s_refs]
        nb = mods[0].shape[0]
        for t in range(out_ref.shape[0] // nb):
            rows = slice(t * nb, (t + 1) * nb)
            out_ref[rows, :] = fn([v[rows] for v in vals], mods).astype(out_ref.dtype)


def _ada_kernel(c_ref, w_ref, b_ref, o_ref):
    c = c_ref[...]
    o_ref[...] = _bdot(_silu(c), w_ref[...]) + b_ref[...]


def _ada(c_all, w_ada, b_ada):
    depth, d, n = w_ada.shape
    rows = c_all.shape[0]
    tn = 1024
    return pl.pallas_call(
        _ada_kernel,
        out_shape=jax.ShapeDtypeStruct((depth, rows, n), F32),
        grid=(depth, n // tn),
        in_specs=[
            pl.BlockSpec((rows, d), lambda l, j: (0, 0)),
            pl.BlockSpec((None, d, tn), lambda l, j: (l, 0, j)),
            pl.BlockSpec((None, 1, tn), lambda l, j: (l, 0, j)),
        ],
        out_specs=pl.BlockSpec((None, rows, tn), lambda l, j: (l, 0, j)),
        compiler_params=_params("arbitrary", "arbitrary"),
        name="ada",
    )(c_all, w_ada, b_ada.reshape(depth, 1, n))


def _mod_specs(cols, width, n_prompt_tiles, tiles_per_seq, n_seq, dec_batch, grid_rank=1, row_axis=0, col_fn=None):
    p_specs, s_specs = [], []
    for c in cols:
        def p_map(*idx, c=c):
            b = jnp.minimum(idx[row_axis] // tiles_per_seq, n_seq - 1)
            return (b, 0, c if col_fn is None else col_fn(c, idx))

        def s_map(*idx, c=c):
            return (0, c if col_fn is None else col_fn(c, idx))

        p_specs.append(pl.BlockSpec((None, 1, width), p_map))
        s_specs.append(pl.BlockSpec((dec_batch, width), s_map))
    return p_specs, s_specs


def _rms(x, g):
    return x * lax.rsqrt(jnp.mean(x * x, axis=-1, keepdims=True) + EPS) * g


def _norm1_kernel(xp_ref, xs_ref, g_ref, shp, scp, shs, scs, wgl_ref, gw2_ref, gb_ref, h_ref, la_ref, *, npt, dec_seq, rank):
    i = pl.program_id(0)
    y = _rms(jnp.where(i < npt, xp_ref[...], xs_ref[...]), g_ref[...])
    _store_by_group(i, npt, dec_seq, h_ref, lambda v, m: v[0] * (1.0 + m[1]) + m[0], [y], [shp, scp], [shs, scs])
    row = lax.broadcasted_iota(jnp.int32, wgl_ref.shape, 0)
    w_gl = jnp.where(row < rank, wgl_ref[...], 0.0).astype(BF16)
    gate_lr = lax.dot_general(h_ref[...], w_gl, (((1,), (1,)), ((), ())), preferred_element_type=F32)
    pre = _bdot(gate_lr, gw2_ref[...]) + gb_ref[...]
    la_ref[...] = (jnp.minimum(pre, 0.0) - jnp.log1p(jnp.exp(-jnp.abs(pre)))) * (1.0 / GATE_TAU)


def _norm1(xp, xs, xs_blk, g, mod_p, mod_s, w_in_t, layer, n_main, rank, gw2, gb, dims):
    d = xp.shape[1]
    tm = ROW_TILE
    npt = dims["npt"]
    t = (npt + 1) * tm
    p_specs, s_specs = _mod_specs((0, 1), d, npt, dims["tps"], dims["n_seq"], dims["dec_batch"])
    qk = gw2.shape[1]
    assert n_main % LANES == 0 and rank <= LANES
    return pl.pallas_call(
        functools.partial(_norm1_kernel, npt=npt, dec_seq=dims["dec_seq"], rank=rank),
        out_shape=(jax.ShapeDtypeStruct((t, d), BF16), jax.ShapeDtypeStruct((t, qk), F32)),
        grid=(t // tm,),
        in_specs=[
            pl.BlockSpec((tm, d), lambda i: (jnp.minimum(i, npt - 1), 0)),
            pl.BlockSpec((tm, d), lambda i: (xs_blk, 0)),
            pl.BlockSpec((1, d), lambda i: (0, 0)),
            *p_specs,
            *s_specs,
            pl.BlockSpec((None, LANES, d), lambda i: (layer, n_main // LANES, 0)),
            pl.BlockSpec(gw2.shape, lambda i: (0, 0)),
            pl.BlockSpec((1, qk), lambda i: (0, 0)),
        ],
        out_specs=(pl.BlockSpec((tm, d), lambda i: (i, 0)), pl.BlockSpec((tm, qk), lambda i: (i, 0))),
        compiler_params=_params("arbitrary"),
        name="norm1",
    )(xp, xs, g.reshape(1, d), mod_p, mod_p, mod_s, mod_s, w_in_t, gw2, gb.reshape(1, qk))


def _inproj_kernel(h_ref, w_ref, o_ref, wb_ref):
    @pl.when(pl.program_id(1) == 0)
    def _():
        wb_ref[...] = w_ref[...].astype(BF16)

    o_ref[...] = lax.dot_general(h_ref[...], wb_ref[...], (((1,), (1,)), ((), ())), preferred_element_type=F32)


def _inproj(h, w_in_t, layer, n_cols):
    t, d = h.shape
    tm, tn = ROW_TILE, 1024
    return pl.pallas_call(
        _inproj_kernel,
        out_shape=jax.ShapeDtypeStruct((t, n_cols), F32),
        grid=(n_cols // tn, t // tm),
        in_specs=[
            pl.BlockSpec((tm, d), lambda j, i: (i, 0)),
            pl.BlockSpec((None, tn, d), lambda j, i: (layer, j, 0)),
        ],
        out_specs=pl.BlockSpec((tm, tn), lambda j, i: (i, j)),
        scratch_shapes=[pltpu.VMEM((tn, d), BF16)],
        compiler_params=_params("arbitrary", "arbitrary"),
        name="inproj",
    )(h, w_in_t)


def _ln_silu(y, g, b):
    mu = jnp.mean(y, axis=-1, keepdims=True)
    yc = y - mu
    var = jnp.mean(yc * yc, axis=-1, keepdims=True)
    return _silu(yc * lax.rsqrt(var + EPS) * g + b)


def _conv_prompt_kernel(a_ref, b_ref, cw_ref, cb_ref, lng_ref, lnb_ref, o_ref, st_ref, full_ref, cwb_ref, y_ref, *, kw):
    j = pl.program_id(1)
    tm, c = a_ref.shape
    halo = CONV_HALO
    phases = full_ref.shape[0]
    assert phases == SUBLANES

    @pl.when(jnp.logical_and(pl.program_id(0) == 0, j == 0))
    def _():
        full_ref[...] = jnp.zeros(full_ref.shape, F32)
        for w in range(kw):
            cwb_ref[w] = jnp.broadcast_to(cw_ref[w:w + 1, :], (SUBLANES, c))

    prev = full_ref[0, tm + halo - SUBLANES:tm + halo, :]
    tail = jnp.where(j == 0, 0.0, prev)

    @pl.when(j == 0)
    def _():
        for p in range(phases):
            full_ref[p, 0:halo, :] = jnp.zeros((halo, c), F32)

    @pl.when(j > 0)
    def _():
        for p in range(phases):
            full_ref[p, 0:halo, :] = full_ref[p, tm:tm + halo, :]

    u = a_ref[...] * jax.nn.sigmoid(b_ref[...])
    full_ref[0, halo:halo + tm, :] = u
    ext = jnp.concatenate([tail, u], axis=0)
    for p in range(1, phases):
        full_ref[p, halo - SUBLANES:halo - SUBLANES + tm, :] = pltpu.roll(ext, tm + SUBLANES - p, 0)[0:tm]
    off = halo - (kw - 1)
    rb = CONV_ROWS

    def body(r, carry):
        r0 = pl.multiple_of(r * rb, rb)
        for lt in range(c // LANES):
            cols = slice(lt * LANES, (lt + 1) * LANES)
            acc = None
            for p in range(phases):
                x = full_ref[p, pl.ds(r0, rb + halo), cols]
                for a in range(halo // phases + 1):
                    w = a * phases + p - off
                    if 0 <= w < kw and a * phases + rb <= rb + halo:
                        term = x[a * phases:a * phases + rb] * jnp.concatenate([cwb_ref[w, :, cols]] * (rb // SUBLANES), axis=0)
                        acc = term if acc is None else acc + term
            y_ref[pl.ds(r0, rb), cols] = acc
        return carry

    lax.fori_loop(0, tm // rb, body, 0)
    y = _ln_silu(y_ref[...] + cb_ref[...], lng_ref[...], lnb_ref[...])
    o_ref[...] = y.astype(o_ref.dtype)

    @pl.when(j == pl.num_programs(1) - 1)
    def _():
        st_ref[...] = full_ref[0, halo + tm - (kw - 1):halo + tm, :]


def _conv_prompt(z, cw, cb, lng, lnb, n_seq, seq_len):
    kw, c = cw.shape
    tm = ROW_TILE
    tps = seq_len // tm
    return pl.pallas_call(
        functools.partial(_conv_prompt_kernel, kw=kw),
        out_shape=(jax.ShapeDtypeStruct((n_seq * seq_len, c), BF16), jax.ShapeDtypeStruct((n_seq, kw - 1, c), F32)),
        grid=(n_seq, tps),
        in_specs=[
            pl.BlockSpec((tm, c), lambda b, j: (b * tps + j, 0)),
            pl.BlockSpec((tm, c), lambda b, j: (b * tps + j, 1)),
            pl.BlockSpec((kw, c), lambda b, j: (0, 0)),
            pl.BlockSpec((1, c), lambda b, j: (0, 0)),
            pl.BlockSpec((1, c), lambda b, j: (0, 0)),
            pl.BlockSpec((1, c), lambda b, j: (0, 0)),
        ],
        out_specs=(
            pl.BlockSpec((tm, c), lambda b, j: (b * tps + j, 0)),
            pl.BlockSpec((None, kw - 1, c), lambda b, j: (b, 0, 0)),
        ),
        scratch_shapes=[pltpu.VMEM((SUBLANES, tm + CONV_HALO, c), F32), pltpu.VMEM((kw, SUBLANES, c), F32),
                        pltpu.VMEM((tm, c), F32)],
        compiler_params=_params("arbitrary", "arbitrary"),
        name="conv_prompt",
    )(z, z, cw, cb.reshape(1, c), lng.reshape(1, c), lnb.reshape(1, c))


def _conv_sample_kernel(*refs, kw, dec_seq):
    a_refs = refs[0:dec_seq]
    b_refs = refs[dec_seq:2 * dec_seq]
    st_ref, cw_ref, cb_ref, lng_ref, lnb_ref, o_ref, u_ref = refs[2 * dec_seq:]
    hist = kw - 1
    u = [a_refs[t][...] * jax.nn.sigmoid(b_refs[t][...]) for t in range(dec_seq)]
    def row(j):
        return st_ref[j] if j < hist else u[j - hist]

    for t in range(dec_seq):
        acc = row(t) * cw_ref[0:1, :]
        for w in range(1, kw):
            acc = acc + row(t + w) * cw_ref[w:w + 1, :]
        y = _ln_silu(acc + cb_ref[...], lng_ref[...], lnb_ref[...])
        o_ref[t] = y.astype(o_ref.dtype)
        u_ref[t] = u[t]


def _conv_sample(z, st_t, layer, cw, cb, lng, lnb, row0, dec_seq, dec_batch):
    kw, c = cw.shape
    bs = 16
    a_specs = [pl.BlockSpec((bs, c), lambda s, t=t: ((row0 + t * dec_batch) // bs + s, 0)) for t in range(dec_seq)]
    b_specs = [pl.BlockSpec((bs, c), lambda s, t=t: ((row0 + t * dec_batch) // bs + s, 1)) for t in range(dec_seq)]
    vec = pl.BlockSpec((1, c), lambda s: (0, 0))
    return pl.pallas_call(
        functools.partial(_conv_sample_kernel, kw=kw, dec_seq=dec_seq),
        out_shape=(jax.ShapeDtypeStruct((dec_seq, dec_batch, c), BF16), jax.ShapeDtypeStruct((dec_seq, dec_batch, c), F32)),
        grid=(dec_batch // bs,),
        in_specs=[*a_specs, *b_specs, pl.BlockSpec((None, kw - 1, bs, c), lambda s: (layer, 0, s, 0)),
                  pl.BlockSpec((kw, c), lambda s: (0, 0)), vec, vec, vec],
        out_specs=(pl.BlockSpec((dec_seq, bs, c), lambda s: (0, s, 0)), pl.BlockSpec((dec_seq, bs, c), lambda s: (0, s, 0))),
        compiler_params=_params("arbitrary"),
        name="conv_sample",
    )(*([z] * (2 * dec_seq)), st_t, cw, cb.reshape(1, c), lng.reshape(1, c), lnb.reshape(1, c))


def _conv_state_kernel(st_ref, u_ref, o_ref):
    hist, n_new = st_ref.shape[0], u_ref.shape[0]
    for j in range(hist - n_new):
        o_ref[j] = st_ref[j + n_new]
    for t in range(n_new):
        o_ref[hist - n_new + t] = u_ref[t]


def _conv_state_sample(st_t, u_all):
    depth, hist, nb, c = st_t.shape
    n_new = u_all.shape[1]
    assert n_new <= hist
    bs = 16
    return pl.pallas_call(
        _conv_state_kernel,
        out_shape=jax.ShapeDtypeStruct(st_t.shape, st_t.dtype),
        grid=(depth, nb // bs),
        in_specs=[pl.BlockSpec((None, hist, bs, c), lambda l, s: (l, 0, s, 0)),
                  pl.BlockSpec((None, n_new, bs, c), lambda l, s: (l, 0, s, 0))],
        out_specs=pl.BlockSpec((None, hist, bs, c), lambda l, s: (l, 0, s, 0)),
        compiler_params=_params("arbitrary", "arbitrary"),
        name="conv_state",
    )(st_t, u_all)


def _gla_prompt_kernel(qk_ref, v_ref, g_ref, la_ref, gn_ref, o_ref, sfin_ref, st_ref, sn_ref, *, heads, dk, dv):
    j = pl.program_id(1)
    tm = qk_ref.shape[0]
    ck = GLA_CHUNK
    nch = tm // ck
    qkw = heads * dk

    @pl.when(j == 0)
    def _():
        st_ref[...] = jnp.zeros(st_ref.shape, F32)

    la = la_ref[...]
    sb = GLA_BLOCK
    blocks = [slice(r, r + sb) for r in range(0, tm, sb)]
    row = lax.broadcasted_iota(jnp.int32, (sb, sb), 0)
    col = lax.broadcasted_iota(jnp.int32, (sb, sb), 1)
    same = (row // ck) == (col // ck)
    causal = same & (col <= row)
    tri_incl = causal.astype(BF16)
    tri_after = (same & (col > row)).astype(BF16)
    sel = (lax.broadcasted_iota(jnp.int32, (nch, tm), 1) // ck == lax.broadcasted_iota(jnp.int32, (nch, tm), 0)).astype(BF16)
    parts = _split3(la)

    def blockwise(tri):
        return jnp.concatenate([sum(jnp.dot(tri, p[rs], preferred_element_type=F32) for p in parts) for rs in blocks], axis=0)

    b = blockwise(tri_incl)
    rest = blockwise(tri_after)
    tot = sum(jnp.dot(sel, p, preferred_element_type=F32) for p in parts)
    qk = qk_ref[...]
    q = qk[:, :qkw] * (dk ** -0.5)
    k = qk[:, qkw:]
    q_dec = (q * jnp.exp(b)).astype(BF16)
    k_inv = (k * jnp.exp(-b)).astype(BF16)
    k_end = _round_bf16(k * jnp.exp(rest))
    decay = jnp.exp(tot)
    v_all = v_ref[...]
    g_all = g_ref[...]
    for h in range(heads):
        ks = slice(h * dk, (h + 1) * dk)
        vs = slice(h * dv, (h + 1) * dv)
        qd, ki, ke = q_dec[:, ks], k_inv[:, ks], k_end[:, ks]
        vh = v_all[:, vs]
        vb = vh.astype(BF16)
        vr = _round_bf16(vh)
        intra = []
        for rs in blocks:
            att = lax.dot_general(qd[rs], ki[rs], (((1,), (1,)), ((), ())), preferred_element_type=F32)
            att = jnp.where(causal, att, 0.0).astype(BF16)
            intra.append(jnp.dot(att, vb[rs], preferred_element_type=F32))
        o = jnp.concatenate(intra, axis=0)
        s = st_ref[h]
        for n in range(nch):
            rs = slice(n * ck, (n + 1) * ck)
            sn_ref[n] = s.astype(BF16)
            upd = lax.dot_general(vr[rs], ke[rs], (((0,), (0,)), ((), ())), preferred_element_type=F32)
            s = s * decay[n:n + 1, ks] + upd
        st_ref[h] = s
        inter = [lax.dot_general(qd[n * ck:(n + 1) * ck], sn_ref[n], (((1,), (1,)), ((), ())), preferred_element_type=F32)
                 for n in range(nch)]
        o = o + jnp.concatenate(inter, axis=0)
        o = _rms(o, gn_ref[:, vs]) * _silu(g_all[:, vs])
        o_ref[:, vs] = o.astype(o_ref.dtype)

    @pl.when(j == pl.num_programs(1) - 1)
    def _():
        for h in range(heads):
            sfin_ref[h] = st_ref[h].T


def _gla_prompt(z, la, gn, n_seq, seq_len, heads, dk, dv):
    tm = ROW_TILE
    tps = seq_len // tm
    w = heads * dv
    qkw = heads * dk
    assert 2 * qkw == w
    return pl.pallas_call(
        functools.partial(_gla_prompt_kernel, heads=heads, dk=dk, dv=dv),
        out_shape=(jax.ShapeDtypeStruct((n_seq * seq_len, w), BF16), jax.ShapeDtypeStruct((n_seq, heads, dk, dv), F32)),
        grid=(n_seq, tps),
        in_specs=[
            pl.BlockSpec((tm, w), lambda b, j: (b * tps + j, 2)),
            pl.BlockSpec((tm, w), lambda b, j: (b * tps + j, 3)),
            pl.BlockSpec((tm, w), lambda b, j: (b * tps + j, 4)),
            pl.BlockSpec((tm, qkw), lambda b, j: (b * tps + j, 0)),
            pl.BlockSpec((1, w), lambda b, j: (0, 0)),
        ],
        out_specs=(
            pl.BlockSpec((tm, w), lambda b, j: (b * tps + j, 0)),
            pl.BlockSpec((None, heads, dk, dv), lambda b, j: (b, 0, 0, 0)),
        ),
        scratch_shapes=[pltpu.VMEM((heads, dv, dk), F32), pltpu.VMEM((tm // GLA_CHUNK, dv, dk), BF16)],
        compiler_params=_params("arbitrary", "arbitrary"),
        name="gla_prompt",
    )(z, z, z, la, gn.reshape(1, w))


def _gla_sample_kernel(q_ref, k_ref, v_ref, g_ref, la_ref, s_ref, gn_ref, *rest, heads, dk, dv, n_prev):
    bs, ln, _ = q_ref.shape
    if n_prev:
        o_ref, all_ref = rest[n_prev:]
        for p in range(n_prev):
            all_ref[p] = rest[p][...]
        ns_ref = all_ref.at[n_prev]
    else:
        o_ref, ns_ref = rest
    tril = lax.broadcasted_iota(jnp.int32, (ln, ln), 1) <= lax.broadcasted_iota(jnp.int32, (ln, ln), 0)
    for s in range(bs):
        q_s, k_s, v_s, g_s, la_s = q_ref[s], k_ref[s], v_ref[s], g_ref[s], la_ref[s]
        for h in range(heads):
            ks = slice(h * dk, (h + 1) * dk)
            vs = slice(h * dv, (h + 1) * dv)
            la = la_s[:, ks]
            rows = [la[0:1]]
            for t in range(1, ln):
                rows.append(rows[-1] + la[t:t + 1])
            b = jnp.concatenate(rows, axis=0)
            b_last = rows[-1]
            q_dec = _round_bf16(q_s[:, ks] * (dk ** -0.5) * jnp.exp(b))
            k_inv = _round_bf16(k_s[:, ks] * jnp.exp(-b))
            k_end = _round_bf16(k_s[:, ks] * jnp.exp(b_last - b))
            vr = _round_bf16(v_s[:, vs])
            s0 = s_ref[s, h]
            att = lax.dot_general(q_dec, k_inv, (((1,), (1,)), ((), ())), preferred_element_type=F32)
            att = _round_bf16(jnp.where(tril, att, 0.0))
            o = jnp.dot(att, vr, preferred_element_type=F32) + jnp.dot(q_dec, _round_bf16(s0), preferred_element_type=F32)
            upd = lax.dot_general(k_end, vr, (((0,), (0,)), ((), ())), preferred_element_type=F32)
            d_col = jnp.broadcast_to(jnp.exp(b_last), (dk, dk)).T
            ns_ref[s, h] = s0 * jnp.concatenate([d_col] * (dv // dk), axis=1) + upd
            o = _rms(o, gn_ref[:, vs]) * _silu(g_s[:, vs])
            o_ref[s, :, vs] = o.astype(o_ref.dtype)


def _gla_sample(q3, k3, v3, g3, la3, state, layer, gn, heads, dk, dv, prev_states=()):
    nb, ln, w = v3.shape
    qkw = heads * dk
    bs = 8
    n_prev = len(prev_states)
    seq_spec = pl.BlockSpec((bs, ln, qkw), lambda s: (s, 0, 0))
    wide_spec = pl.BlockSpec((bs, ln, w), lambda s: (s, 0, 0))
    in_specs = [seq_spec, seq_spec, wide_spec, wide_spec, seq_spec,
                pl.BlockSpec((None, bs, heads, dk, dv), lambda s: (layer, s, 0, 0, 0)),
                pl.BlockSpec((1, w), lambda s: (0, 0)),
                *[pl.BlockSpec((bs, heads, dk, dv), lambda s: (s, 0, 0, 0))] * n_prev]
    if n_prev:
        state_shape = jax.ShapeDtypeStruct((n_prev + 1, nb, heads, dk, dv), F32)
        state_spec = pl.BlockSpec((n_prev + 1, bs, heads, dk, dv), lambda s: (0, s, 0, 0, 0))
    else:
        state_shape = jax.ShapeDtypeStruct((nb, heads, dk, dv), F32)
        state_spec = pl.BlockSpec((bs, heads, dk, dv), lambda s: (s, 0, 0, 0))
    return pl.pallas_call(
        functools.partial(_gla_sample_kernel, heads=heads, dk=dk, dv=dv, n_prev=n_prev),
        out_shape=(jax.ShapeDtypeStruct((nb, ln, w), F32), state_shape),
        grid=(nb // bs,),
        in_specs=in_specs,
        out_specs=(wide_spec, state_spec),
        compiler_params=_params("arbitrary"),
        name="gla_sample_last" if n_prev else "gla_sample",
    )(q3, k3, v3, g3, la3, state, gn.reshape(1, w), *prev_states)


def _outproj_kernel(cp_ref, op_ref, cs_ref, os_ref, xp_ref, xs_ref, w_ref, gp, gs, y_ref, wb_ref, *, npt, dec_seq):
    i = pl.program_id(1)

    @pl.when(i == 0)
    def _():
        wb_ref[...] = w_ref[...].astype(BF16)

    half = cp_ref.shape[1]

    def mixed(c_ref, o_ref):
        return (jnp.dot(c_ref[...], wb_ref[0:half, :], preferred_element_type=F32)
                + jnp.dot(o_ref[...], wb_ref[half:, :], preferred_element_type=F32))

    @pl.when(i < npt)
    def _():
        y_ref[...] = xp_ref[...] + gp[...] * mixed(cp_ref, op_ref)

    @pl.when(i >= npt)
    def _():
        mix = mixed(cs_ref, os_ref)
        nb = gs.shape[0]
        for t in range(dec_seq):
            rows = slice(t * nb, (t + 1) * nb)
            y_ref[rows, :] = xs_ref[rows, :] + gs[...] * mix[rows]


def _outproj(conv_p, gla_p, conv_s, gla_s, xp, xs, xs_blk, w_out, layer, mod_p, mod_s, dims):
    d = xp.shape[1]
    half = conv_p.shape[1]
    tm, tn = ROW_TILE, 1024
    nj = d // tn
    npt = dims["npt"]
    t = (npt + 1) * tm
    p_specs, s_specs = _mod_specs((2,), tn, dims["npt"], dims["tps"], dims["n_seq"], dims["dec_batch"],
                                  row_axis=1, col_fn=lambda c, idx: c * nj + idx[0])
    return pl.pallas_call(
        functools.partial(_outproj_kernel, npt=dims["npt"], dec_seq=dims["dec_seq"]),
        out_shape=jax.ShapeDtypeStruct((t, d), F32),
        grid=(nj, t // tm),
        in_specs=[
            pl.BlockSpec((tm, half), lambda j, i: (jnp.minimum(i, npt - 1), 0)),
            pl.BlockSpec((tm, half), lambda j, i: (jnp.minimum(i, npt - 1), 0)),
            pl.BlockSpec((tm, half), lambda j, i: (0, 0)),
            pl.BlockSpec((tm, half), lambda j, i: (0, 0)),
            pl.BlockSpec((tm, tn), lambda j, i: (jnp.minimum(i, npt - 1), j)),
            pl.BlockSpec((tm, tn), lambda j, i: (xs_blk, j)),
            pl.BlockSpec((None, d, tn), lambda j, i: (layer, 0, j)),
            *p_specs,
            *s_specs,
        ],
        out_specs=pl.BlockSpec((tm, tn), lambda j, i: (i, j)),
        scratch_shapes=[pltpu.VMEM((d, tn), BF16)],
        compiler_params=_params("arbitrary", "arbitrary"),
        name="outproj",
    )(conv_p, gla_p, conv_s, gla_s, xp, xs, w_out, mod_p, mod_s)


def _norm2_kernel(x_ref, g_ref, shp, scp, shs, scs, wr_hi, wr_lo, br_ref, hb_ref, ids_ref, wts_ref, h_ref, *, npt, dec_seq, n_grp, n_exp):
    i = pl.program_id(0)
    y = _rms(x_ref[...], g_ref[...])
    _store_by_group(i, npt, dec_seq, h_ref, lambda v, m: v[0] * (1.0 + m[1]) + m[0], [y], [shp, scp], [shs, scs])
    h = h_ref[...]
    h_hi = h.astype(BF16)
    hb_ref[...] = h_hi
    h_lo = (h - h_hi.astype(F32)).astype(BF16)
    logits = (jnp.dot(h_hi, wr_hi[...], preferred_element_type=F32) + jnp.dot(h_lo, wr_hi[...], preferred_element_type=F32)
              + jnp.dot(h_hi, wr_lo[...], preferred_element_type=F32)) + br_ref[...]
    lane = lax.broadcasted_iota(jnp.int32, logits.shape, 1).astype(F32)
    big = jnp.float32(LANES)
    neg = jnp.float32(-jnp.inf)
    gl = jnp.where(lane < n_grp, logits, neg)
    gmax = jnp.max(gl, axis=-1, keepdims=True)
    gidx = jnp.min(jnp.where(gl == gmax, lane, big), axis=-1, keepdims=True)
    g_w = 1.0 / jnp.sum(jnp.exp(gl - gmax), axis=-1, keepdims=True)
    lo = n_grp + gidx * n_exp
    in_grp = (lane >= lo) & (lane < lo + n_exp)
    sl = jnp.where(in_grp, logits, neg)
    p = jnp.exp(sl - jnp.max(sl, axis=-1, keepdims=True))
    p = p / jnp.sum(p, axis=-1, keepdims=True)
    p = jnp.where(in_grp, p, -1.0)
    p1 = jnp.max(p, axis=-1, keepdims=True)
    i1 = jnp.min(jnp.where(p == p1, lane, big), axis=-1, keepdims=True)
    p_rest = jnp.where(lane == i1, -1.0, p)
    p2 = jnp.max(p_rest, axis=-1, keepdims=True)
    i2 = jnp.min(jnp.where(p_rest == p2, lane, big), axis=-1, keepdims=True)
    denom = p1 + p2
    ids_ref[...] = jnp.where(lane == 0, i1 - n_grp, jnp.where(lane == 1, i2 - n_grp, 0.0)).astype(jnp.int32)
    wts_ref[...] = jnp.where(lane == 0, g_w * (p1 / denom), jnp.where(lane == 1, g_w * (p2 / denom), 0.0))


def _norm2(x, g, mod_p, mod_s, wr_hi, wr_lo, br, dims, n_grp, n_exp):
    t, d = x.shape
    tm = ROW_TILE
    p_specs, s_specs = _mod_specs((3, 4), d, dims["npt"], dims["tps"], dims["n_seq"], dims["dec_batch"])
    return pl.pallas_call(
        functools.partial(_norm2_kernel, npt=dims["npt"], dec_seq=dims["dec_seq"], n_grp=n_grp, n_exp=n_exp),
        out_shape=(jax.ShapeDtypeStruct((t, d), BF16), jax.ShapeDtypeStruct((t, LANES), jnp.int32),
                   jax.ShapeDtypeStruct((t, LANES), F32)),
        grid=(t // tm,),
        in_specs=[
            pl.BlockSpec((tm, d), lambda i: (i, 0)),
            pl.BlockSpec((1, d), lambda i: (0, 0)),
            *p_specs,
            *s_specs,
            pl.BlockSpec((d, LANES), lambda i: (0, 0)),
            pl.BlockSpec((d, LANES), lambda i: (0, 0)),
            pl.BlockSpec((1, LANES), lambda i: (0, 0)),
        ],
        out_specs=(pl.BlockSpec((tm, d), lambda i: (i, 0)), pl.BlockSpec((tm, LANES), lambda i: (i, 0)),
                   pl.BlockSpec((tm, LANES), lambda i: (i, 0))),
        scratch_shapes=[pltpu.VMEM((tm, d), F32)],
        compiler_params=_params("arbitrary"),
        name="norm2_router",
    )(x, g.reshape(1, d), mod_p, mod_p, mod_s, mod_s, wr_hi, wr_lo, br)


def _segments(ids, n_experts, n_tiles_max):
    t = ids.shape[0]
    ts, te, al = SORT_TILE, EXPERT_TILE, SEG_ALIGN
    nt = t // ts
    e_iota = jnp.arange(n_experts, dtype=jnp.int32)
    cnt = jnp.sum((ids.reshape(nt, ts * TOP_K, 1) == e_iota).astype(jnp.int32), axis=1)
    size = (cnt + al - 1) // al * al
    src = jnp.cumsum(size, axis=1) - size
    tot = jnp.sum(size, axis=0)
    tot_pad = (tot + te - 1) // te * te
    ends = jnp.cumsum(tot_pad)
    exp_off = ends - tot_pad
    dst = exp_off[None, :] + jnp.cumsum(size, axis=0) - size
    n_used = (ends[-1] // te).astype(jnp.int32)
    tile_start = jnp.arange(n_tiles_max, dtype=jnp.int32) * te
    tile_expert = jnp.minimum(jnp.sum(ends[None, :] <= tile_start[:, None], axis=1), n_experts - 1).astype(jnp.int32)
    last = tile_expert[jnp.maximum(n_used - 1, 0)]
    tile_expert = jnp.where(jnp.arange(n_tiles_max) < n_used, tile_expert, last)
    return dict(src=src.reshape(-1).astype(jnp.int32), dst=dst.reshape(-1).astype(jnp.int32),
                size=size.reshape(-1).astype(jnp.int32), tile_tot=jnp.sum(size, axis=1).astype(jnp.int32),
                fill_start=(exp_off + tot).astype(jnp.int32), fill_size=(tot_pad - tot).astype(jnp.int32),
                tile_expert=tile_expert, n_used=n_used.reshape(1))


def _lane_pick(x, lane, k):
    return jnp.sum(jnp.where(lane == k, x, 0.0), axis=-1, keepdims=True)


def _sorted_positions(ids, lane):
    ts = ids.shape[0]
    onehot = [(lane == _lane_pick(ids, lane, k)).astype(F32) for k in range(TOP_K)]
    row = lax.broadcasted_iota(jnp.int32, (ts, ts), 0)
    col = lax.broadcasted_iota(jnp.int32, (ts, ts), 1)
    before = (col < row).astype(BF16)
    earlier = [jnp.dot(before, o.astype(BF16), preferred_element_type=F32) for o in onehot]
    cnt = [jnp.sum(o, axis=0, keepdims=True) for o in onehot]
    size = jnp.ceil((cnt[0] + cnt[1]) * (1.0 / SEG_ALIGN)) * SEG_ALIGN
    er = lax.broadcasted_iota(jnp.int32, (LANES, LANES), 0)
    ec = lax.broadcasted_iota(jnp.int32, (LANES, LANES), 1)
    start = jnp.dot(jnp.broadcast_to(size, (SUBLANES, LANES)).astype(BF16), (er < ec).astype(BF16),
                    preferred_element_type=F32)[0:1]
    base = [start + earlier[0], start + cnt[0] + earlier[1]]
    return [jnp.sum(onehot[k] * base[k], axis=-1, keepdims=True) for k in range(TOP_K)]


def _as_row(col_vals, lane):
    hi = jnp.floor(col_vals * (1.0 / 32.0))
    lo = col_vals - hi * 32.0
    ones = jnp.ones((SUBLANES, LANES), BF16)
    nt = (((1,), (1,)), ((), ()))
    hi_row = lax.dot_general(ones, jnp.where(lane == 0, hi, 0.0).astype(BF16), nt, preferred_element_type=F32)
    lo_row = lax.dot_general(ones, jnp.where(lane == 0, lo, 0.0).astype(BF16), nt, preferred_element_type=F32)
    return (hi_row * 32.0 + lo_row)[0:1]


def _dispatch_kernel(src_ref, dst_ref, size_ref, tot_ref, fst_ref, fsz_ref, nu_ref, h_ref, ids_ref, wts_ref, xe_hbm,
                     we_hbm, pos_ref, sbuf, wbuf, zx, zw, semx, semw, semz, semt, *, n_experts):
    i = pl.program_id(0)
    n = pl.num_programs(0)
    ts = h_ref.shape[0]
    rows = sbuf.shape[1]
    te = zx.shape[0]
    slot = lax.rem(i, 2)

    def tail_copies(tl):
        r0 = pl.multiple_of(tl * te, te)
        return (pltpu.make_async_copy(zx, xe_hbm.at[pl.ds(r0, te), :], semt.at[0]),
                pltpu.make_async_copy(zw, we_hbm.at[pl.ds(r0, te), :], semt.at[1]))

    def for_tail(action):
        def body(tl, carry):
            for c in tail_copies(tl):
                action(c)
            return carry

        lax.fori_loop(nu_ref[0], xe_hbm.shape[0] // te, body, 0)

    def seg_copies(tile, sl, e):
        k = tile * n_experts + e
        sz = pl.multiple_of(size_ref[k], SEG_ALIGN)
        s0 = pl.multiple_of(src_ref[k], SEG_ALIGN)
        d0 = pl.multiple_of(dst_ref[k], SEG_ALIGN)
        return sz, (pltpu.make_async_copy(sbuf.at[sl, pl.ds(s0, sz), :], xe_hbm.at[pl.ds(d0, sz), :], semx.at[sl]),
                    pltpu.make_async_copy(wbuf.at[sl, pl.ds(s0, sz), :], we_hbm.at[pl.ds(d0, sz), :], semw.at[sl]))

    def wait_tile(tile, sl):
        tot = pl.multiple_of(tot_ref[tile], SEG_ALIGN)
        pltpu.make_async_copy(sbuf.at[sl, pl.ds(0, tot), :], xe_hbm.at[pl.ds(0, tot), :], semx.at[sl]).wait()
        pltpu.make_async_copy(wbuf.at[sl, pl.ds(0, tot), :], we_hbm.at[pl.ds(0, tot), :], semw.at[sl]).wait()

    @pl.when(i == 0)
    def _():
        zx[...] = jnp.zeros(zx.shape, zx.dtype)
        zw[...] = jnp.zeros(zw.shape, zw.dtype)

        def fill(e, carry):
            sz = pl.multiple_of(fsz_ref[e], SEG_ALIGN)
            d0 = pl.multiple_of(fst_ref[e], SEG_ALIGN)

            @pl.when(sz > 0)
            def _():
                cx = pltpu.make_async_copy(zx.at[pl.ds(0, sz), :], xe_hbm.at[pl.ds(d0, sz), :], semz.at[0])
                cw = pltpu.make_async_copy(zw.at[pl.ds(0, sz), :], we_hbm.at[pl.ds(d0, sz), :], semz.at[1])
                cx.start()
                cw.start()
                cx.wait()
                cw.wait()

            return carry

        lax.fori_loop(0, n_experts, fill, 0)
        for_tail(lambda c: c.start())

    @pl.when(i >= 2)
    def _():
        wait_tile(i - 2, slot)

    lane = lax.broadcasted_iota(jnp.int32, (ts, LANES), 1).astype(F32)
    pos = _sorted_positions(ids_ref[...].astype(F32), lane)
    pos_ref[...] = jnp.where(lane == 0, pos[0], jnp.where(lane == 1, pos[1], 0.0))
    r_iota = lax.broadcasted_iota(jnp.int32, (rows, ts), 0).astype(F32)
    assert rows <= 1024
    sel = [(r_iota == _as_row(p, lane)).astype(BF16) for p in pos]
    sbuf[slot] = jnp.dot(sel[0] + sel[1], h_ref[...], preferred_element_type=F32).astype(BF16)
    wts = wts_ref[...]
    wsorted = jnp.zeros((rows, LANES), F32)
    for k in range(TOP_K):
        pieces = _split3(_lane_pick(wts, lane, k))
        wk = jnp.where(lane == 0, pieces[0].astype(F32), jnp.where(lane == 1, pieces[1].astype(F32),
                       jnp.where(lane == 2, pieces[2].astype(F32), 0.0)))
        wsorted = wsorted + jnp.dot(sel[k], wk.astype(BF16), preferred_element_type=F32)
    wbuf[slot] = wsorted

    def issue(e, carry):
        sz, copies = seg_copies(i, slot, e)

        @pl.when(sz > 0)
        def _():
            for c in copies:
                c.start()

        return carry

    lax.fori_loop(0, n_experts, issue, 0)

    @pl.when(i == n - 1)
    def _():
        @pl.when(i >= 1)
        def _():
            wait_tile(i - 1, 1 - slot)

        wait_tile(i, slot)
        for_tail(lambda c: c.wait())


def _dispatch_rows(hb, ids, wts, seg, n_experts, n_tiles_max):
    t, d = hb.shape
    ts, te = SORT_TILE, EXPERT_TILE
    rows = TOP_K * ts + n_experts * SEG_ALIGN
    n_slots = n_tiles_max * te

    def tile_map(i, *_):
        return (i, 0)

    return pl.pallas_call(
        functools.partial(_dispatch_kernel, n_experts=n_experts),
        out_shape=(jax.ShapeDtypeStruct((n_slots, d), BF16), jax.ShapeDtypeStruct((n_slots, LANES), F32),
                   jax.ShapeDtypeStruct((t, LANES), F32)),
        grid_spec=pltpu.PrefetchScalarGridSpec(
            num_scalar_prefetch=7,
            grid=(t // ts,),
            in_specs=[pl.BlockSpec((ts, d), tile_map), pl.BlockSpec((ts, LANES), tile_map),
                      pl.BlockSpec((ts, LANES), tile_map)],
            out_specs=(pl.BlockSpec(memory_space=pl.ANY), pl.BlockSpec(memory_space=pl.ANY),
                       pl.BlockSpec((ts, LANES), tile_map)),
            scratch_shapes=[
                pltpu.VMEM((2, rows, d), BF16),
                pltpu.VMEM((2, rows, LANES), F32),
                pltpu.VMEM((te, d), BF16),
                pltpu.VMEM((te, LANES), F32),
                pltpu.SemaphoreType.DMA((2,)),
                pltpu.SemaphoreType.DMA((2,)),
                pltpu.SemaphoreType.DMA((2,)),
                pltpu.SemaphoreType.DMA((2,)),
            ],
        ),
        compiler_params=_params("arbitrary"),
        name="dispatch",
    )(seg["src"], seg["dst"], seg["size"], seg["tile_tot"], seg["fill_start"], seg["fill_size"], seg["n_used"],
      hb, ids, wts)


def _expert_rows_kernel(te_ref, nu_ref, x_ref, w_ref, wg_ref, wu_ref, wd_ref, o_ref, wgb, wub, wdb):
    i = pl.program_id(0)
    n_used = nu_ref[0]

    @pl.when(i < n_used)
    def _():
        changed = jnp.logical_or(i == 0, te_ref[i] != te_ref[jnp.maximum(i - 1, 0)])

        @pl.when(changed)
        def _():
            wgb[...] = wg_ref[...].astype(BF16)
            wub[...] = wu_ref[...].astype(BF16)
            wdb[...] = wd_ref[...].astype(BF16)

        x = x_ref[...]
        a = jnp.dot(x, wgb[...], preferred_element_type=F32)
        u = jnp.dot(x, wub[...], preferred_element_type=F32)
        w = jnp.sum(w_ref[...], axis=-1, keepdims=True)
        hid = _silu(a) * u * w
        o_ref[...] = jnp.dot(hid.astype(BF16), wdb[...], preferred_element_type=F32).astype(o_ref.dtype)


def _expert_rows(xe, we, seg, wg, wu, wd, layer_base):
    n_slots, d = xe.shape
    f = wg.shape[-1]
    te = EXPERT_TILE

    def w_map(i, te_ref, nu_ref):
        return (layer_base + te_ref[i], 0, 0)

    def row_map(i, te_ref, nu_ref):
        return (jnp.minimum(i, nu_ref[0] - 1), 0)

    return pl.pallas_call(
        _expert_rows_kernel,
        out_shape=jax.ShapeDtypeStruct((n_slots, d), BF16),
        grid_spec=pltpu.PrefetchScalarGridSpec(
            num_scalar_prefetch=2,
            grid=(n_slots // te,),
            in_specs=[
                pl.BlockSpec((te, d), row_map),
                pl.BlockSpec((te, LANES), row_map),
                pl.BlockSpec((None, d, f), w_map),
                pl.BlockSpec((None, d, f), w_map),
                pl.BlockSpec((None, f, d), w_map),
            ],
            out_specs=pl.BlockSpec((te, d), row_map),
            scratch_shapes=[pltpu.VMEM((d, f), BF16), pltpu.VMEM((d, f), BF16), pltpu.VMEM((f, d), BF16)],
        ),
        input_output_aliases={2: 0},
        compiler_params=_params("arbitrary"),
        name="experts",
    )(seg["tile_expert"], seg["n_used"], xe, we, wg, wu, wd)


def _collect_kernel(src_ref, dst_ref, size_ref, tot_ref, y_hbm, x_ref, pos_ref, gp, gs, *rest, npt, n_experts, final):
    if final:
        fg_ref, op_ref, os_ref, buf, sem, o_ref = rest
    else:
        o_ref, buf, sem = rest
    i = pl.program_id(0)
    n = pl.num_programs(0)
    ts = x_ref.shape[0]
    rows = buf.shape[1]
    slot = lax.rem(i, 2)

    def start(tile, sl):
        def body(e, carry):
            k = tile * n_experts + e
            sz = pl.multiple_of(size_ref[k], SEG_ALIGN)
            s0 = pl.multiple_of(src_ref[k], SEG_ALIGN)
            d0 = pl.multiple_of(dst_ref[k], SEG_ALIGN)

            @pl.when(sz > 0)
            def _():
                pltpu.make_async_copy(y_hbm.at[pl.ds(d0, sz), :], buf.at[sl, pl.ds(s0, sz), :], sem.at[sl]).start()

            return carry

        lax.fori_loop(0, n_experts, body, 0)

    @pl.when(i == 0)
    def _():
        buf[...] = jnp.zeros(buf.shape, buf.dtype)
        start(0, 0)

    @pl.when(i + 1 < n)
    def _():
        start(i + 1, 1 - slot)

    tot = pl.multiple_of(tot_ref[i], SEG_ALIGN)
    pltpu.make_async_copy(y_hbm.at[pl.ds(0, tot), :], buf.at[slot, pl.ds(0, tot), :], sem.at[slot]).wait()
    lane = lax.broadcasted_iota(jnp.int32, (ts, LANES), 1).astype(F32)
    pos = pos_ref[...]
    r_iota = lax.broadcasted_iota(jnp.int32, (ts, rows), 1).astype(F32)
    pick = ((r_iota == _lane_pick(pos, lane, 0)) | (r_iota == _lane_pick(pos, lane, 1))).astype(BF16)
    ff = jnp.dot(pick, buf[slot], preferred_element_type=F32)
    _store_by_group(i, npt, None, o_ref, lambda v, m: v[0] + m[0] * v[1], [x_ref[...], ff], [gp], [gs])
    if final:
        y = _rms(o_ref[...], fg_ref[...])

        @pl.when(i < npt)
        def _():
            op_ref[...] = y

        @pl.when(i >= npt)
        def _():
            os_ref[...] = y


def _collect(ye, pos, x, seg, mod_p, mod_s, dims, n_experts, final_g=None):
    t, d = x.shape
    ts = SORT_TILE
    npt = dims["npt"]
    rows = TOP_K * ts + n_experts * SEG_ALIGN
    p_specs, s_specs = _mod_specs((5,), d, npt, dims["tps"], dims["n_seq"], dims["dec_batch"])
    final = final_g is not None

    def strip(spec):
        return pl.BlockSpec(spec.block_shape, lambda i, *_, m=spec.index_map: m(i))

    def tile_map(i, *_):
        return (i, 0)

    in_specs = [
        pl.BlockSpec(memory_space=pl.ANY),
        pl.BlockSpec((ts, d), tile_map),
        pl.BlockSpec((ts, LANES), tile_map),
        *[strip(s) for s in p_specs],
        *[strip(s) for s in s_specs],
    ]
    scratch = [pltpu.VMEM((2, rows, d), BF16), pltpu.SemaphoreType.DMA((2,))]
    args = [seg["src"], seg["dst"], seg["size"], seg["tile_tot"], ye, x, pos, mod_p, mod_s]
    if final:
        in_specs.append(pl.BlockSpec((1, d), lambda i, *_: (0, 0)))
        args.append(final_g.reshape(1, d))
        out_shape = (jax.ShapeDtypeStruct((npt * ts, d), F32), jax.ShapeDtypeStruct((t - npt * ts, d), F32))
        out_specs = (pl.BlockSpec((ts, d), lambda i, *_: (jnp.minimum(i, npt - 1), 0)),
                     pl.BlockSpec((ts, d), lambda i, *_: (jnp.maximum(i - npt, 0), 0)))
        scratch.append(pltpu.VMEM((ts, d), F32))
    else:
        out_shape = jax.ShapeDtypeStruct((t, d), F32)
        out_specs = pl.BlockSpec((ts, d), tile_map)
    return pl.pallas_call(
        functools.partial(_collect_kernel, npt=npt, n_experts=n_experts, final=final),
        out_shape=out_shape,
        grid_spec=pltpu.PrefetchScalarGridSpec(
            num_scalar_prefetch=4, grid=(t // ts,), in_specs=in_specs, out_specs=out_specs, scratch_shapes=scratch),
        compiler_params=_params("arbitrary"),
        name="collect_final" if final else "collect",
    )(*args)


def kernel(x_prompt, x_sample, c_prompt, c_sample, state_conv, state_gla, w_ada, b_ada, norm1_g, norm2_g, w_in, conv_w, conv_b, conv_ln_g, conv_ln_b, gate_w2, gate_b, gla_norm_g, w_out, router_grp_w, router_grp_b, router_exp_w, router_exp_b, exp_w_gate, exp_w_up, exp_w_down, final_norm_g):
    n_seq, seq_len, d = x_prompt.shape
    dec_batch, dec_seq, _ = x_sample.shape
    depth = w_ada.shape[0]
    kw, d_conv = conv_w.shape[1:]
    heads, dv = gla_norm_g.shape[1:]
    rank, qkw = gate_w2.shape[1:]
    dk = qkw // heads
    n_grp, n_exp = router_exp_w.shape[2:]
    n_experts = n_grp * n_exp
    tp, ts = n_seq * seq_len, dec_batch * dec_seq
    t = tp + ts
    tm = ROW_TILE
    assert ts == tm and seq_len % tm == 0 and d_conv == heads * dv and kw - 1 <= CONV_HALO
    n_main = 2 * d_conv + 2 * qkw + 2 * heads * dv
    dims = dict(npt=tp // tm, tps=seq_len // tm, n_seq=n_seq, dec_batch=dec_batch, dec_seq=dec_seq)

    xp, xs, xs_blk = x_prompt.reshape(tp, d), x_sample.transpose(1, 0, 2).reshape(ts, d), 0
    w_in_t = jnp.swapaxes(w_in, 1, 2)
    st_t = jnp.transpose(state_conv, (0, 2, 1, 3))
    pad = (-n_seq) % 8
    c_all = jnp.concatenate([c_prompt, jnp.zeros((pad, d), F32), c_sample], axis=0)
    mod = _ada(c_all, w_ada, b_ada)
    st = SORT_TILE
    assert tm % st == 0 and st % dec_batch == 0
    dims_sort = dict(npt=tp // st, tps=seq_len // st, n_seq=n_seq, dec_batch=dec_batch, dec_seq=dec_seq)
    n_tiles_max = -(-(TOP_K * t + (t // st) * n_experts * (SEG_ALIGN - 1)) // EXPERT_TILE) + n_experts

    conv_p, gla_p, conv_u, gla_s = [], [], [], []
    for l in range(depth):
        mod_p = mod[l, :n_seq].reshape(n_seq, 1, 6 * d)
        mod_s = mod[l, n_seq + pad:]
        gw2 = jnp.pad(gate_w2[l], ((0, LANES - rank), (0, 0))).astype(BF16)
        h, la = _norm1(xp, xs, xs_blk, norm1_g[l], mod_p, mod_s, w_in_t, l, n_main, rank, gw2, gate_b[l], dims)
        z = _inproj(h, w_in_t, l, n_main)

        cv_p, cb_p = _conv_prompt(z, conv_w[l], conv_b[l], conv_ln_g[l], conv_ln_b[l], n_seq, seq_len)
        cv_s, u_s = _conv_sample(z, st_t, l, conv_w[l], conv_b[l], conv_ln_g[l], conv_ln_b[l], tp, dec_seq, dec_batch)
        conv_p.append(cb_p)
        conv_u.append(u_s)

        go_p, gs_p = _gla_prompt(z, la, gla_norm_g[l], n_seq, seq_len, heads, dk, dv)

        def seq_major(a):
            return a.reshape(dec_seq, dec_batch, a.shape[-1]).transpose(1, 0, 2)

        zs = z[tp:]
        q3 = seq_major(zs[:, 2 * d_conv:2 * d_conv + qkw])
        k3 = seq_major(zs[:, 2 * d_conv + qkw:2 * d_conv + 2 * qkw])
        v3 = seq_major(zs[:, 2 * d_conv + 2 * qkw:2 * d_conv + 2 * qkw + heads * dv])
        g3 = seq_major(zs[:, 2 * d_conv + 2 * qkw + heads * dv:n_main])
        go_s, gs_s = _gla_sample(q3, k3, v3, g3, seq_major(la[tp:]), state_gla, l, gla_norm_g[l], heads, dk, dv,
                                 prev_states=tuple(gla_s) if l + 1 == depth else ())
        go_s = go_s.transpose(1, 0, 2).reshape(ts, heads * dv).astype(BF16)
        gla_p.append(gs_p)
        gla_s.append(gs_s)

        x = _outproj(cv_p, go_p, cv_s.reshape(ts, d_conv), go_s, xp, xs, xs_blk, w_out, l, mod_p, mod_s, dims)

        wr = jnp.concatenate([router_grp_w[l], router_exp_w[l].reshape(d, n_experts)], axis=1)
        wr = jnp.pad(wr, ((0, 0), (0, LANES - wr.shape[1])))
        wr_hi = wr.astype(BF16)
        wr_lo = (wr - wr_hi.astype(F32)).astype(BF16)
        br = jnp.concatenate([router_grp_b[l], router_exp_b[l].reshape(-1)])
        br = jnp.pad(br, (0, LANES - br.shape[0])).reshape(1, LANES)
        hb, ids, wts = _norm2(x, norm2_g[l], mod_p, mod_s, wr_hi, wr_lo, br, dims, n_grp, n_exp)
        seg = _segments(ids[:, :TOP_K], n_experts, n_tiles_max)
        xe, we, pos = _dispatch_rows(hb, ids, wts, seg, n_experts, n_tiles_max)
        f = exp_w_gate.shape[-1]
        ye = _expert_rows(xe, we, seg, exp_w_gate.reshape(depth * n_experts, d, f),
                          exp_w_up.reshape(depth * n_experts, d, f), exp_w_down.reshape(depth * n_experts, f, d),
                          l * n_experts)
        if l + 1 < depth:
            x = _collect(ye, pos, x, seg, mod_p, mod_s, dims_sort, n_experts)
            xp, xs, xs_blk = x, x, dims["npt"]
        else:
            y_p, y_s = _collect(ye, pos, x, seg, mod_p, mod_s, dims_sort, n_experts, final_g=final_norm_g)

    y_prompt = y_p.reshape(n_seq, seq_len, d)
    y_sample = y_s.reshape(dec_seq, dec_batch, d).transpose(1, 0, 2)
    conv_s = jnp.transpose(_conv_state_sample(st_t, jnp.stack(conv_u)), (0, 2, 1, 3))
    gla_s_all = gla_s[-1] if depth > 1 else gla_s[0][None]
    return (y_prompt, y_sample, jnp.stack(conv_p), jnp.stack(gla_p), conv_s, gla_s_all)
```

```python
import functools

import jax
import jax.numpy as jnp
from jax import lax
from jax.experimental import pallas as pl
from jax.experimental.pallas import tpu as pltpu

F32 = jnp.float32
BF16 = jnp.bfloat16

EPS = 1e-6
GATE_TAU = 16.0
GLA_CHUNK = 32
GLA_BLOCK = 128
TOP_K = 2

ROW_TILE = 512
EXPERT_TILE = 512
SORT_TILE = 512
SEG_ALIGN = 16
CONV_ROWS = 64
CONV_HALO = 32
LANES = 128
SUBLANES = 8
VMEM_LIMIT = 56 * 1024 * 1024


def _params(*sem):
    return pltpu.CompilerParams(dimension_semantics=sem, vmem_limit_bytes=VMEM_LIMIT)


def _bdot(a, b):
    return jnp.dot(a.astype(BF16), b.astype(BF16), preferred_element_type=F32)


def _round_bf16(x):
    return x.astype(BF16).astype(F32)


def _split3(x):
    hi = x.astype(BF16)
    r = x - hi.astype(F32)
    mid = r.astype(BF16)
    lo = (r - mid.astype(F32)).astype(BF16)
    return hi, mid, lo


def _silu(x):
    return x * jax.nn.sigmoid(x)


def _store_by_group(i, n_prompt_tiles, dec_seq, out_ref, fn, vals, p_refs, s_refs):
    @pl.when(i < n_prompt_tiles)
    def _():
        out_ref[...] = fn(vals, [r[...] for r in p_refs]).astype(out_ref.dtype)

    @pl.when(i >= n_prompt_tiles)
    def _():
        mods = [r[...] for r in s_refs]
        nb = mods[0].shape[0]
        for t in range(out_ref.shape[0] // nb):
            rows = slice(t * nb, (t + 1) * nb)
            out_ref[rows, :] = fn([v[rows] for v in vals], mods).astype(out_ref.dtype)


def _ada_kernel(c_ref, w_ref, b_ref, o_ref):
    c = c_ref[...]
    o_ref[...] = _bdot(_silu(c), w_ref[...]) + b_ref[...]


def _ada(c_all, w_ada, b_ada):
    depth, d, n = w_ada.shape
    rows = c_all.shape[0]
    tn = 1024
    return pl.pallas_call(
        _ada_kernel,
        out_shape=jax.ShapeDtypeStruct((depth, rows, n), F32),
        grid=(depth, n // tn),
        in_specs=[
            pl.BlockSpec((rows, d), lambda l, j: (0, 0)),
            pl.BlockSpec((None, d, tn), lambda l, j: (l, 0, j)),
            pl.BlockSpec((None, 1, tn), lambda l, j: (l, 0, j)),
        ],
        out_specs=pl.BlockSpec((None, rows, tn), lambda l, j: (l, 0, j)),
        compiler_params=_params("arbitrary", "arbitrary"),
        name="ada",
    )(c_all, w_ada, b_ada.reshape(depth, 1, n))


def _mod_specs(cols, width, n_prompt_tiles, tiles_per_seq, n_seq, dec_batch, grid_rank=1, row_axis=0, col_fn=None):
    p_specs, s_specs = [], []
    for c in cols:
        def p_map(*idx, c=c):
            b = jnp.minimum(idx[row_axis] // tiles_per_seq, n_seq - 1)
            return (b, 0, c if col_fn is None else col_fn(c, idx))

        def s_map(*idx, c=c):
            return (0, c if col_fn is None else col_fn(c, idx))

        p_specs.append(pl.BlockSpec((None, 1, width), p_map))
        s_specs.append(pl.BlockSpec((dec_batch, width), s_map))
    return p_specs, s_specs


def _rms(x, g):
    return x * lax.rsqrt(jnp.mean(x * x, axis=-1, keepdims=True) + EPS) * g


def _norm1_kernel(xp_ref, xs_ref, g_ref, shp, scp, shs, scs, wgl_ref, gw2_ref, gb_ref, h_ref, la_ref, *, npt, dec_seq, rank):
    i = pl.program_id(0)
    y = _rms(jnp.where(i < npt, xp_ref[...], xs_ref[...]), g_ref[...])
    _store_by_group(i, npt, dec_seq, h_ref, lambda v, m: v[0] * (1.0 + m[1]) + m[0], [y], [shp, scp], [shs, scs])
    row = lax.broadcasted_iota(jnp.int32, wgl_ref.shape, 0)
    w_gl = jnp.where(row < rank, wgl_ref[...], 0.0).astype(BF16)
    gate_lr = lax.dot_general(h_ref[...], w_gl, (((1,), (1,)), ((), ())), preferred_element_type=F32)
    pre = _bdot(gate_lr, gw2_ref[...]) + gb_ref[...]
    la_ref[...] = (jnp.minimum(pre, 0.0) - jnp.log1p(jnp.exp(-jnp.abs(pre)))) * (1.0 / GATE_TAU)


def _norm1(xp, xs, xs_blk, g, mod_p, mod_s, w_in_t, layer, n_main, rank, gw2, gb, dims):
    d = xp.shape[1]
    tm = ROW_TILE
    npt = dims["npt"]
    t = (npt + 1) * tm
    p_specs, s_specs = _mod_specs((0, 1), d, npt, dims["tps"], dims["n_seq"], dims["dec_batch"])
    qk = gw2.shape[1]
    assert n_main % LANES == 0 and rank <= LANES
    return pl.pallas_call(
        functools.partial(_norm1_kernel, npt=npt, dec_seq=dims["dec_seq"], rank=rank),
        out_shape=(jax.ShapeDtypeStruct((t, d), BF16), jax.ShapeDtypeStruct((t, qk), F32)),
        grid=(t // tm,),
        in_specs=[
            pl.BlockSpec((tm, d), lambda i: (jnp.minimum(i, npt - 1), 0)),
            pl.BlockSpec((tm, d), lambda i: (xs_blk, 0)),
            pl.BlockSpec((1, d), lambda i: (0, 0)),
            *p_specs,
            *s_specs,
            pl.BlockSpec((None, LANES, d), lambda i: (layer, n_main // LANES, 0)),
            pl.BlockSpec(gw2.shape, lambda i: (0, 0)),
            pl.BlockSpec((1, qk), lambda i: (0, 0)),
        ],
        out_specs=(pl.BlockSpec((tm, d), lambda i: (i, 0)), pl.BlockSpec((tm, qk), lambda i: (i, 0))),
        compiler_params=_params("arbitrary"),
        name="norm1",
    )(xp, xs, g.reshape(1, d), mod_p, mod_p, mod_s, mod_s, w_in_t, gw2, gb.reshape(1, qk))


def _inproj_kernel(h_ref, w_ref, o_ref, wb_ref):
    @pl.when(pl.program_id(1) == 0)
    def _():
        wb_ref[...] = w_ref[...].astype(BF16)

    o_ref[...] = lax.dot_general(h_ref[...], wb_ref[...], (((1,), (1,)), ((), ())), preferred_element_type=F32)


def _inproj(h, w_in_t, layer, n_cols):
    t, d = h.shape
    tm, tn = ROW_TILE, 1024
    return pl.pallas_call(
        _inproj_kernel,
        out_shape=jax.ShapeDtypeStruct((t, n_cols), F32),
        grid=(n_cols // tn, t // tm),
        in_specs=[
            pl.BlockSpec((tm, d), lambda j, i: (i, 0)),
            pl.BlockSpec((None, tn, d), lambda j, i: (layer, j, 0)),
        ],
        out_specs=pl.BlockSpec((tm, tn), lambda j, i: (i, j)),
        scratch_shapes=[pltpu.VMEM((tn, d), BF16)],
        compiler_params=_params("arbitrary", "arbitrary"),
        name="inproj",
    )(h, w_in_t)


def _ln_silu(y, g, b):
    mu = jnp.mean(y, axis=-1, keepdims=True)
    yc = y - mu
    var = jnp.mean(yc * yc, axis=-1, keepdims=True)
    return _silu(yc * lax.rsqrt(var + EPS) * g + b)


def _conv_prompt_kernel(a_ref, b_ref, cw_ref, cb_ref, lng_ref, lnb_ref, o_ref, st_ref, full_ref, cwb_ref, y_ref, *, kw):
    j = pl.program_id(1)
    tm, c = a_ref.shape
    halo = CONV_HALO
    phases = full_ref.shape[0]
    assert phases == SUBLANES

    @pl.when(jnp.logical_and(pl.program_id(0) == 0, j == 0))
    def _():
        full_ref[...] = jnp.zeros(full_ref.shape, F32)
        for w in range(kw):
            cwb_ref[w] = jnp.broadcast_to(cw_ref[w:w + 1, :], (SUBLANES, c))

    prev = full_ref[0, tm + halo - SUBLANES:tm + halo, :]
    tail = jnp.where(j == 0, 0.0, prev)

    @pl.when(j == 0)
    def _():
        for p in range(phases):
            full_ref[p, 0:halo, :] = jnp.zeros((halo, c), F32)

    @pl.when(j > 0)
    def _():
        for p in range(phases):
            full_ref[p, 0:halo, :] = full_ref[p, tm:tm + halo, :]

    u = a_ref[...] * jax.nn.sigmoid(b_ref[...])
    full_ref[0, halo:halo + tm, :] = u
    ext = jnp.concatenate([tail, u], axis=0)
    for p in range(1, phases):
        full_ref[p, halo - SUBLANES:halo - SUBLANES + tm, :] = pltpu.roll(ext, tm + SUBLANES - p, 0)[0:tm]
    off = halo - (kw - 1)
    rb = CONV_ROWS

    def body(r, carry):
        r0 = pl.multiple_of(r * rb, rb)
        for lt in range(c // LANES):
            cols = slice(lt * LANES, (lt + 1) * LANES)
            acc = None
            for p in range(phases):
                x = full_ref[p, pl.ds(r0, rb + halo), cols]
                for a in range(halo // phases + 1):
                    w = a * phases + p - off
                    if 0 <= w < kw and a * phases + rb <= rb + halo:
                        term = x[a * phases:a * phases + rb] * jnp.concatenate([cwb_ref[w, :, cols]] * (rb // SUBLANES), axis=0)
                        acc = term if acc is None else acc + term
            y_ref[pl.ds(r0, rb), cols] = acc
        return carry

    lax.fori_loop(0, tm // rb, body, 0)
    y = _ln_silu(y_ref[...] + cb_ref[...], lng_ref[...], lnb_ref[...])
    o_ref[...] = y.astype(o_ref.dtype)

    @pl.when(j == pl.num_programs(1) - 1)
    def _():
        st_ref[...] = full_ref[0, halo + tm - (kw - 1):halo + tm, :]


def _conv_prompt(z, cw, cb, lng, lnb, n_seq, seq_len):
    kw, c = cw.shape
    tm = ROW_TILE
    tps = seq_len // tm
    return pl.pallas_call(
        functools.partial(_conv_prompt_kernel, kw=kw),
        out_shape=(jax.ShapeDtypeStruct((n_seq * seq_len, c), BF16), jax.ShapeDtypeStruct((n_seq, kw - 1, c), F32)),
        grid=(n_seq, tps),
        in_specs=[
            pl.BlockSpec((tm, c), lambda b, j: (b * tps + j, 0)),
            pl.BlockSpec((tm, c), lambda b, j: (b * tps + j, 1)),
            pl.BlockSpec((kw, c), lambda b, j: (0, 0)),
            pl.BlockSpec((1, c), lambda b, j: (0, 0)),
            pl.BlockSpec((1, c), lambda b, j: (0, 0)),
            pl.BlockSpec((1, c), lambda b, j: (0, 0)),
        ],
        out_specs=(
            pl.BlockSpec((tm, c), lambda b, j: (b * tps + j, 0)),
            pl.BlockSpec((None, kw - 1, c), lambda b, j: (b, 0, 0)),
        ),
        scratch_shapes=[pltpu.VMEM((SUBLANES, tm + CONV_HALO, c), F32), pltpu.VMEM((kw, SUBLANES, c), F32),
                        pltpu.VMEM((tm, c), F32)],
        compiler_params=_params("arbitrary", "arbitrary"),
        name="conv_prompt",
    )(z, z, cw, cb.reshape(1, c), lng.reshape(1, c), lnb.reshape(1, c))


def _conv_sample_kernel(*refs, kw, dec_seq):
    a_refs = refs[0:dec_seq]
    b_refs = refs[dec_seq:2 * dec_seq]
    st_ref, cw_ref, cb_ref, lng_ref, lnb_ref, o_ref, u_ref = refs[2 * dec_seq:]
    hist = kw - 1
    u = [a_refs[t][...] * jax.nn.sigmoid(b_refs[t][...]) for t in range(dec_seq)]
    def row(j):
        return st_ref[j] if j < hist else u[j - hist]

    for t in range(dec_seq):
        acc = row(t) * cw_ref[0:1, :]
        for w in range(1, kw):
            acc = acc + row(t + w) * cw_ref[w:w + 1, :]
        y = _ln_silu(acc + cb_ref[...], lng_ref[...], lnb_ref[...])
        o_ref[t] = y.astype(o_ref.dtype)
        u_ref[t] = u[t]


def _conv_sample(z, st_t, layer, cw, cb, lng, lnb, row0, dec_seq, dec_batch):
    kw, c = cw.shape
    bs = 16
    a_specs = [pl.BlockSpec((bs, c), lambda s, t=t: ((row0 + t * dec_batch) // bs + s, 0)) for t in range(dec_seq)]
    b_specs = [pl.BlockSpec((bs, c), lambda s, t=t: ((row0 + t * dec_batch) // bs + s, 1)) for t in range(dec_seq)]
    vec = pl.BlockSpec((1, c), lambda s: (0, 0))
    return pl.pallas_call(
        functools.partial(_conv_sample_kernel, kw=kw, dec_seq=dec_seq),
        out_shape=(jax.ShapeDtypeStruct((dec_seq, dec_batch, c), BF16), jax.ShapeDtypeStruct((dec_seq, dec_batch, c), F32)),
        grid=(dec_batch // bs,),
        in_specs=[*a_specs, *b_specs, pl.BlockSpec((None, kw - 1, bs, c), lambda s: (layer, 0, s, 0)),
                  pl.BlockSpec((kw, c), lambda s: (0, 0)), vec, vec, vec],
        out_specs=(pl.BlockSpec((dec_seq, bs, c), lambda s: (0, s, 0)), pl.BlockSpec((dec_seq, bs, c), lambda s: (0, s, 0))),
        compiler_params=_params("arbitrary"),
        name="conv_sample",
    )(*([z] * (2 * dec_seq)), st_t, cw, cb.reshape(1, c), lng.reshape(1, c), lnb.reshape(1, c))


def _conv_state_kernel(st_ref, u_ref, o_ref):
    hist, n_new = st_ref.shape[0], u_ref.shape[0]
    for j in range(hist - n_new):
        o_ref[j] = st_ref[j + n_new]
    for t in range(n_new):
        o_ref[hist - n_new + t] = u_ref[t]


def _conv_state_sample(st_t, u_all):
    depth, hist, nb, c = st_t.shape
    n_new = u_all.shape[1]
    assert n_new <= hist
    bs = 16
    return pl.pallas_call(
        _conv_state_kernel,
        out_shape=jax.ShapeDtypeStruct(st_t.shape, st_t.dtype),
        grid=(depth, nb // bs),
        in_specs=[pl.BlockSpec((None, hist, bs, c), lambda l, s: (l, 0, s, 0)),
                  pl.BlockSpec((None, n_new, bs, c), lambda l, s: (l, 0, s, 0))],
        out_specs=pl.BlockSpec((None, hist, bs, c), lambda l, s: (l, 0, s, 0)),
        compiler_params=_params("arbitrary", "arbitrary"),
        name="conv_state",
    )(st_t, u_all)


def _gla_prompt_kernel(qk_ref, v_ref, g_ref, la_ref, gn_ref, o_ref, sfin_ref, st_ref, sn_ref, *, heads, dk, dv):
    j = pl.program_id(1)
    tm = qk_ref.shape[0]
    ck = GLA_CHUNK
    nch = tm // ck
    qkw = heads * dk

    @pl.when(j == 0)
    def _():
        st_ref[...] = jnp.zeros(st_ref.shape, F32)

    la = la_ref[...]
    sb = GLA_BLOCK
    blocks = [slice(r, r + sb) for r in range(0, tm, sb)]
    row = lax.broadcasted_iota(jnp.int32, (sb, sb), 0)
    col = lax.broadcasted_iota(jnp.int32, (sb, sb), 1)
    same = (row // ck) == (col // ck)
    causal = same & (col <= row)
    tri_incl = causal.astype(BF16)
    tri_after = (same & (col > row)).astype(BF16)
    sel = (lax.broadcasted_iota(jnp.int32, (nch, tm), 1) // ck == lax.broadcasted_iota(jnp.int32, (nch, tm), 0)).astype(BF16)
    parts = _split3(la)

    def blockwise(tri):
        return jnp.concatenate([sum(jnp.dot(tri, p[rs], preferred_element_type=F32) for p in parts) for rs in blocks], axis=0)

    b = blockwise(tri_incl)
    rest = blockwise(tri_after)
    tot = sum(jnp.dot(sel, p, preferred_element_type=F32) for p in parts)
    qk = qk_ref[...]
    q = qk[:, :qkw] * (dk ** -0.5)
    k = qk[:, qkw:]
    q_dec = (q * jnp.exp(b)).astype(BF16)
    k_inv = (k * jnp.exp(-b)).astype(BF16)
    k_end = _round_bf16(k * jnp.exp(rest))
    decay = jnp.exp(tot)
    v_all = v_ref[...]
    g_all = g_ref[...]
    for h in range(heads):
        ks = slice(h * dk, (h + 1) * dk)
        vs = slice(h * dv, (h + 1) * dv)
        qd, ki, ke = q_dec[:, ks], k_inv[:, ks], k_end[:, ks]
        vh = v_all[:, vs]
        vb = vh.astype(BF16)
        vr = _round_bf16(vh)
        intra = []
        for rs in blocks:
            att = lax.dot_general(qd[rs], ki[rs], (((1,), (1,)), ((), ())), preferred_element_type=F32)
            att = jnp.where(causal, att, 0.0).astype(BF16)
            intra.append(jnp.dot(att, vb[rs], preferred_element_type=F32))
        o = jnp.concatenate(intra, axis=0)
        s = st_ref[h]
        for n in range(nch):
            rs = slice(n * ck, (n + 1) * ck)
            sn_ref[n] = s.astype(BF16)
            upd = lax.dot_general(vr[rs], ke[rs], (((0,), (0,)), ((), ())), preferred_element_type=F32)
            s = s * decay[n:n + 1, ks] + upd
        st_ref[h] = s
        inter = [lax.dot_general(qd[n * ck:(n + 1) * ck], sn_ref[n], (((1,), (1,)), ((), ())), preferred_element_type=F32)
                 for n in range(nch)]
        o = o + jnp.concatenate(inter, axis=0)
        o = _rms(o, gn_ref[:, vs]) * _silu(g_all[:, vs])
        o_ref[:, vs] = o.astype(o_ref.dtype)

    @pl.when(j == pl.num_programs(1) - 1)
    def _():
        for h in range(heads):
            sfin_ref[h] = st_ref[h].T


def _gla_prompt(z, la, gn, n_seq, seq_len, heads, dk, dv):
    tm = ROW_TILE
    tps = seq_len // tm
    w = heads * dv
    qkw = heads * dk
    assert 2 * qkw == w
    return pl.pallas_call(
        functools.partial(_gla_prompt_kernel, heads=heads, dk=dk, dv=dv),
        out_shape=(jax.ShapeDtypeStruct((n_seq * seq_len, w), BF16), jax.ShapeDtypeStruct((n_seq, heads, dk, dv), F32)),
        grid=(n_seq, tps),
        in_specs=[
            pl.BlockSpec((tm, w), lambda b, j: (b * tps + j, 2)),
            pl.BlockSpec((tm, w), lambda b, j: (b * tps + j, 3)),
            pl.BlockSpec((tm, w), lambda b, j: (b * tps + j, 4)),
            pl.BlockSpec((tm, qkw), lambda b, j: (b * tps + j, 0)),
            pl.BlockSpec((1, w), lambda b, j: (0, 0)),
        ],
        out_specs=(
            pl.BlockSpec((tm, w), lambda b, j: (b * tps + j, 0)),
            pl.BlockSpec((None, heads, dk, dv), lambda b, j: (b, 0, 0, 0)),
        ),
        scratch_shapes=[pltpu.VMEM((heads, dv, dk), F32), pltpu.VMEM((tm // GLA_CHUNK, dv, dk), BF16)],
        compiler_params=_params("arbitrary", "arbitrary"),
        name="gla_prompt",
    )(z, z, z, la, gn.reshape(1, w))


def _gla_sample_kernel(q_ref, k_ref, v_ref, g_ref, la_ref, s_ref, gn_ref, *rest, heads, dk, dv, n_prev):
    bs, ln, _ = q_ref.shape
    if n_prev:
        o_ref, all_ref = rest[n_prev:]
        for p in range(n_prev):
            all_ref[p] = rest[p][...]
        ns_ref = all_ref.at[n_prev]
    else:
        o_ref, ns_ref = rest
    tril = lax.broadcasted_iota(jnp.int32, (ln, ln), 1) <= lax.broadcasted_iota(jnp.int32, (ln, ln), 0)
    for s in range(bs):
        q_s, k_s, v_s, g_s, la_s = q_ref[s], k_ref[s], v_ref[s], g_ref[s], la_ref[s]
        for h in range(heads):
            ks = slice(h * dk, (h + 1) * dk)
            vs = slice(h * dv, (h + 1) * dv)
            la = la_s[:, ks]
            rows = [la[0:1]]
            for t in range(1, ln):
                rows.append(rows[-1] + la[t:t + 1])
            b = jnp.concatenate(rows, axis=0)
            b_last = rows[-1]
            q_dec = _round_bf16(q_s[:, ks] * (dk ** -0.5) * jnp.exp(b))
            k_inv = _round_bf16(k_s[:, ks] * jnp.exp(-b))
            k_end = _round_bf16(k_s[:, ks] * jnp.exp(b_last - b))
            vr = _round_bf16(v_s[:, vs])
            s0 = s_ref[s, h]
            att = lax.dot_general(q_dec, k_inv, (((1,), (1,)), ((), ())), preferred_element_type=F32)
            att = _round_bf16(jnp.where(tril, att, 0.0))
            o = jnp.dot(att, vr, preferred_element_type=F32) + jnp.dot(q_dec, _round_bf16(s0), preferred_element_type=F32)
            upd = lax.dot_general(k_end, vr, (((0,), (0,)), ((), ())), preferred_element_type=F32)
            d_col = jnp.broadcast_to(jnp.exp(b_last), (dk, dk)).T
            ns_ref[s, h] = s0 * jnp.concatenate([d_col] * (dv // dk), axis=1) + upd
            o = _rms(o, gn_ref[:, vs]) * _silu(g_s[:, vs])
            o_ref[s, :, vs] = o.astype(o_ref.dtype)


def _gla_sample(q3, k3, v3, g3, la3, state, layer, gn, heads, dk, dv, prev_states=()):
    nb, ln, w = v3.shape
    qkw = heads * dk
    bs = 8
    n_prev = len(prev_states)
    seq_spec = pl.BlockSpec((bs, ln, qkw), lambda s: (s, 0, 0))
    wide_spec = pl.BlockSpec((bs, ln, w), lambda s: (s, 0, 0))
    in_specs = [seq_spec, seq_spec, wide_spec, wide_spec, seq_spec,
                pl.BlockSpec((None, bs, heads, dk, dv), lambda s: (layer, s, 0, 0, 0)),
                pl.BlockSpec((1, w), lambda s: (0, 0)),
                *[pl.BlockSpec((bs, heads, dk, dv), lambda s: (s, 0, 0, 0))] * n_prev]
    if n_prev:
        state_shape = jax.ShapeDtypeStruct((n_prev + 1, nb, heads, dk, dv), F32)
        state_spec = pl.BlockSpec((n_prev + 1, bs, heads, dk, dv), lambda s: (0, s, 0, 0, 0))
    else:
        state_shape = jax.ShapeDtypeStruct((nb, heads, dk, dv), F32)
        state_spec = pl.BlockSpec((bs, heads, dk, dv), lambda s: (s, 0, 0, 0))
    return pl.pallas_call(
        functools.partial(_gla_sample_kernel, heads=heads, dk=dk, dv=dv, n_prev=n_prev),
        out_shape=(jax.ShapeDtypeStruct((nb, ln, w), F32), state_shape),
        grid=(nb // bs,),
        in_specs=in_specs,
        out_specs=(wide_spec, state_spec),
        compiler_params=_params("arbitrary"),
        name="gla_sample_last" if n_prev else "gla_sample",
    )(q3, k3, v3, g3, la3, state, gn.reshape(1, w), *prev_states)


def _outproj_kernel(cp_ref, op_ref, cs_ref, os_ref, xp_ref, xs_ref, w_ref, gp, gs, y_ref, wb_ref, *, npt, dec_seq):
    i = pl.program_id(1)

    @pl.when(i == 0)
    def _():
        wb_ref[...] = w_ref[...].astype(BF16)

    half = cp_ref.shape[1]

    def mixed(c_ref, o_ref):
        return (jnp.dot(c_ref[...], wb_ref[0:half, :], preferred_element_type=F32)
                + jnp.dot(o_ref[...], wb_ref[half:, :], preferred_element_type=F32))

    @pl.when(i < npt)
    def _():
        y_ref[...] = xp_ref[...] + gp[...] * mixed(cp_ref, op_ref)

    @pl.when(i >= npt)
    def _():
        mix = mixed(cs_ref, os_ref)
        nb = gs.shape[0]
        for t in range(dec_seq):
            rows = slice(t * nb, (t + 1) * nb)
            y_ref[rows, :] = xs_ref[rows, :] + gs[...] * mix[rows]


def _outproj(conv_p, gla_p, conv_s, gla_s, xp, xs, xs_blk, w_out, layer, mod_p, mod_s, dims):
    d = xp.shape[1]
    half = conv_p.shape[1]
    tm, tn = ROW_TILE, 1024
    nj = d // tn
    npt = dims["npt"]
    t = (npt + 1) * tm
    p_specs, s_specs = _mod_specs((2,), tn, dims["npt"], dims["tps"], dims["n_seq"], dims["dec_batch"],
                                  row_axis=1, col_fn=lambda c, idx: c * nj + idx[0])
    return pl.pallas_call(
        functools.partial(_outproj_kernel, npt=dims["npt"], dec_seq=dims["dec_seq"]),
        out_shape=jax.ShapeDtypeStruct((t, d), F32),
        grid=(nj, t // tm),
        in_specs=[
            pl.BlockSpec((tm, half), lambda j, i: (jnp.minimum(i, npt - 1), 0)),
            pl.BlockSpec((tm, half), lambda j, i: (jnp.minimum(i, npt - 1), 0)),
            pl.BlockSpec((tm, half), lambda j, i: (0, 0)),
            pl.BlockSpec((tm, half), lambda j, i: (0, 0)),
            pl.BlockSpec((tm, tn), lambda j, i: (jnp.minimum(i, npt - 1), j)),
            pl.BlockSpec((tm, tn), lambda j, i: (xs_blk, j)),
            pl.BlockSpec((None, d, tn), lambda j, i: (layer, 0, j)),
            *p_specs,
            *s_specs,
        ],
        out_specs=pl.BlockSpec((tm, tn), lambda j, i: (i, j)),
        scratch_shapes=[pltpu.VMEM((d, tn), BF16)],
        compiler_params=_params("arbitrary", "arbitrary"),
        name="outproj",
    )(conv_p, gla_p, conv_s, gla_s, xp, xs, w_out, mod_p, mod_s)


def _norm2_kernel(x_ref, g_ref, shp, scp, shs, scs, wr_hi, wr_lo, br_ref, hb_ref, ids_ref, wts_ref, h_ref, *, npt, dec_seq, n_grp, n_exp):
    i = pl.program_id(0)
    y = _rms(x_ref[...], g_ref[...])
    _store_by_group(i, npt, dec_seq, h_ref, lambda v, m: v[0] * (1.0 + m[1]) + m[0], [y], [shp, scp], [shs, scs])
    h = h_ref[...]
    h_hi = h.astype(BF16)
    hb_ref[...] = h_hi
    h_lo = (h - h_hi.astype(F32)).astype(BF16)
    logits = (jnp.dot(h_hi, wr_hi[...], preferred_element_type=F32) + jnp.dot(h_lo, wr_hi[...], preferred_element_type=F32)
              + jnp.dot(h_hi, wr_lo[...], preferred_element_type=F32)) + br_ref[...]
    lane = lax.broadcasted_iota(jnp.int32, logits.shape, 1).astype(F32)
    big = jnp.float32(LANES)
    neg = jnp.float32(-jnp.inf)
    gl = jnp.where(lane < n_grp, logits, neg)
    gmax = jnp.max(gl, axis=-1, keepdims=True)
    gidx = jnp.min(jnp.where(gl == gmax, lane, big), axis=-1, keepdims=True)
    g_w = 1.0 / jnp.sum(jnp.exp(gl - gmax), axis=-1, keepdims=True)
    lo = n_grp + gidx * n_exp
    in_grp = (lane >= lo) & (lane < lo + n_exp)
    sl = jnp.where(in_grp, logits, neg)
    p = jnp.exp(sl - jnp.max(sl, axis=-1, keepdims=True))
    p = p / jnp.sum(p, axis=-1, keepdims=True)
    p = jnp.where(in_grp, p, -1.0)
    p1 = jnp.max(p, axis=-1, keepdims=True)
    i1 = jnp.min(jnp.where(p == p1, lane, big), axis=-1, keepdims=True)
    p_rest = jnp.where(lane == i1, -1.0, p)
    p2 = jnp.max(p_rest, axis=-1, keepdims=True)
    i2 = jnp.min(jnp.where(p_rest == p2, lane, big), axis=-1, keepdims=True)
    denom = p1 + p2
    ids_ref[...] = jnp.where(lane == 0, i1 - n_grp, jnp.where(lane == 1, i2 - n_grp, 0.0)).astype(jnp.int32)
    wts_ref[...] = jnp.where(lane == 0, g_w * (p1 / denom), jnp.where(lane == 1, g_w * (p2 / denom), 0.0))


def _norm2(x, g, mod_p, mod_s, wr_hi, wr_lo, br, dims, n_grp, n_exp):
    t, d = x.shape
    tm = ROW_TILE
    p_specs, s_specs = _mod_specs((3, 4), d, dims["npt"], dims["tps"], dims["n_seq"], dims["dec_batch"])
    return pl.pallas_call(
        functools.partial(_norm2_kernel, npt=dims["npt"], dec_seq=dims["dec_seq"], n_grp=n_grp, n_exp=n_exp),
        out_shape=(jax.ShapeDtypeStruct((t, d), BF16), jax.ShapeDtypeStruct((t, LANES), jnp.int32),
                   jax.ShapeDtypeStruct((t, LANES), F32)),
        grid=(t // tm,),
        in_specs=[
            pl.BlockSpec((tm, d), lambda i: (i, 0)),
            pl.BlockSpec((1, d), lambda i: (0, 0)),
            *p_specs,
            *s_specs,
            pl.BlockSpec((d, LANES), lambda i: (0, 0)),
            pl.BlockSpec((d, LANES), lambda i: (0, 0)),
            pl.BlockSpec((1, LANES), lambda i: (0, 0)),
        ],
        out_specs=(pl.BlockSpec((tm, d), lambda i: (i, 0)), pl.BlockSpec((tm, LANES), lambda i: (i, 0)),
                   pl.BlockSpec((tm, LANES), lambda i: (i, 0))),
        scratch_shapes=[pltpu.VMEM((tm, d), F32)],
        compiler_params=_params("arbitrary"),
        name="norm2_router",
    )(x, g.reshape(1, d), mod_p, mod_p, mod_s, mod_s, wr_hi, wr_lo, br)


def _segments(ids, n_experts, n_tiles_max):
    t = ids.shape[0]
    ts, te, al = SORT_TILE, EXPERT_TILE, SEG_ALIGN
    nt = t // ts
    e_iota = jnp.arange(n_experts, dtype=jnp.int32)
    cnt = jnp.sum((ids.reshape(nt, ts * TOP_K, 1) == e_iota).astype(jnp.int32), axis=1)
    size = (cnt + al - 1) // al * al
    src = jnp.cumsum(size, axis=1) - size
    tot = jnp.sum(size, axis=0)
    tot_pad = (tot + te - 1) // te * te
    ends = jnp.cumsum(tot_pad)
    exp_off = ends - tot_pad
    dst = exp_off[None, :] + jnp.cumsum(size, axis=0) - size
    n_used = (ends[-1] // te).astype(jnp.int32)
    tile_start = jnp.arange(n_tiles_max, dtype=jnp.int32) * te
    tile_expert = jnp.minimum(jnp.sum(ends[None, :] <= tile_start[:, None], axis=1), n_experts - 1).astype(jnp.int32)
    last = tile_expert[jnp.maximum(n_used - 1, 0)]
    tile_expert = jnp.where(jnp.arange(n_tiles_max) < n_used, tile_expert, last)
    return dict(src=src.reshape(-1).astype(jnp.int32), dst=dst.reshape(-1).astype(jnp.int32),
                size=size.reshape(-1).astype(jnp.int32), tile_tot=jnp.sum(size, axis=1).astype(jnp.int32),
                fill_start=(exp_off + tot).astype(jnp.int32), fill_size=(tot_pad - tot).astype(jnp.int32),
                tile_expert=tile_expert, n_used=n_used.reshape(1))


def _lane_pick(x, lane, k):
    return jnp.sum(jnp.where(lane == k, x, 0.0), axis=-1, keepdims=True)


def _sorted_positions(ids, lane):
    ts = ids.shape[0]
    onehot = [(lane == _lane_pick(ids, lane, k)).astype(F32) for k in range(TOP_K)]
    row = lax.broadcasted_iota(jnp.int32, (ts, ts), 0)
    col = lax.broadcasted_iota(jnp.int32, (ts, ts), 1)
    before = (col < row).astype(BF16)
    earlier = [jnp.dot(before, o.astype(BF16), preferred_element_type=F32) for o in onehot]
    cnt = [jnp.sum(o, axis=0, keepdims=True) for o in onehot]
    size = jnp.ceil((cnt[0] + cnt[1]) * (1.0 / SEG_ALIGN)) * SEG_ALIGN
    er = lax.broadcasted_iota(jnp.int32, (LANES, LANES), 0)
    ec = lax.broadcasted_iota(jnp.int32, (LANES, LANES), 1)
    start = jnp.dot(jnp.broadcast_to(size, (SUBLANES, LANES)).astype(BF16), (er < ec).astype(BF16),
                    preferred_element_type=F32)[0:1]
    base = [start + earlier[0], start + cnt[0] + earlier[1]]
    return [jnp.sum(onehot[k] * base[k], axis=-1, keepdims=True) for k in range(TOP_K)]


def _as_row(col_vals, lane):
    hi = jnp.floor(col_vals * (1.0 / 64.0))
    lo = col_vals - hi * 64.0
    ones = jnp.ones((SUBLANES, LANES), BF16)
    nt = (((1,), (1,)), ((), ()))
    hi_row = lax.dot_general(ones, jnp.where(lane == 0, hi, 0.0).astype(BF16), nt, preferred_element_type=F32)
    lo_row = lax.dot_general(ones, jnp.where(lane == 0, lo, 0.0).astype(BF16), nt, preferred_element_type=F32)
    return (hi_row * 64.0 + lo_row)[0:1]


def _dispatch_kernel(src_ref, dst_ref, size_ref, tot_ref, fst_ref, fsz_ref, nu_ref, h_ref, ids_ref, wts_ref, xe_hbm,
                     we_hbm, pos_ref, sbuf, wbuf, zx, zw, semx, semw, semz, semt, *, n_experts):
    i = pl.program_id(0)
    n = pl.num_programs(0)
    ts = h_ref.shape[0]
    rows = sbuf.shape[1]
    te = zx.shape[0]
    slot = lax.rem(i, 2)

    def tail_copies(tl):
        r0 = pl.multiple_of(tl * te, te)
        return (pltpu.make_async_copy(zx, xe_hbm.at[pl.ds(r0, te), :], semt.at[0]),
                pltpu.make_async_copy(zw, we_hbm.at[pl.ds(r0, te), :], semt.at[1]))

    n_tail_tiles = xe_hbm.shape[0] // te

    def for_tail(action, first):
        def body(tl, carry):
            for c in tail_copies(tl):
                action(c)
            return carry

        lax.fori_loop(first, n_tail_tiles, body, 0)

    def seg_copies(tile, sl, e):
        k = tile * n_experts + e
        sz = pl.multiple_of(size_ref[k], SEG_ALIGN)
        s0 = pl.multiple_of(src_ref[k], SEG_ALIGN)
        d0 = pl.multiple_of(dst_ref[k], SEG_ALIGN)
        return sz, (pltpu.make_async_copy(sbuf.at[sl, pl.ds(s0, sz), :], xe_hbm.at[pl.ds(d0, sz), :], semx.at[sl]),
                    pltpu.make_async_copy(wbuf.at[sl, pl.ds(s0, sz), :], we_hbm.at[pl.ds(d0, sz), :], semw.at[sl]))

    def wait_tile(tile, sl):
        tot = pl.multiple_of(tot_ref[tile], SEG_ALIGN)
        pltpu.make_async_copy(sbuf.at[sl, pl.ds(0, tot), :], xe_hbm.at[pl.ds(0, tot), :], semx.at[sl]).wait()
        pltpu.make_async_copy(wbuf.at[sl, pl.ds(0, tot), :], we_hbm.at[pl.ds(0, tot), :], semw.at[sl]).wait()

    @pl.when(i == 0)
    def _():
        zx[...] = jnp.zeros(zx.shape, zx.dtype)
        zw[...] = jnp.zeros(zw.shape, zw.dtype)

        def fill(e, carry):
            sz = pl.multiple_of(fsz_ref[e], SEG_ALIGN)
            d0 = pl.multiple_of(fst_ref[e], SEG_ALIGN)

            @pl.when(sz > 0)
            def _():
                cx = pltpu.make_async_copy(zx.at[pl.ds(0, sz), :], xe_hbm.at[pl.ds(d0, sz), :], semz.at[0])
                cw = pltpu.make_async_copy(zw.at[pl.ds(0, sz), :], we_hbm.at[pl.ds(d0, sz), :], semz.at[1])
                cx.start()
                cw.start()
                cx.wait()
                cw.wait()

            return carry

        lax.fori_loop(0, n_experts, fill, 0)

    @pl.when(i >= 2)
    def _():
        wait_tile(i - 2, slot)

    lane = lax.broadcasted_iota(jnp.int32, (ts, LANES), 1).astype(F32)
    pos = _sorted_positions(ids_ref[...].astype(F32), lane)
    pos_ref[...] = jnp.where(lane == 0, pos[0], jnp.where(lane == 1, pos[1], 0.0))
    r_iota = lax.broadcasted_iota(jnp.int32, (rows, ts), 0).astype(F32)
    assert rows <= 4096
    sel = [(r_iota == _as_row(p, lane)).astype(BF16) for p in pos]
    sbuf[slot] = jnp.dot(sel[0] + sel[1], h_ref[...], preferred_element_type=F32).astype(BF16)
    wts = wts_ref[...]
    wsorted = jnp.zeros((rows, LANES), F32)
    for k in range(TOP_K):
        pieces = _split3(_lane_pick(wts, lane, k))
        wk = jnp.where(lane == 0, pieces[0].astype(F32), jnp.where(lane == 1, pieces[1].astype(F32),
                       jnp.where(lane == 2, pieces[2].astype(F32), 0.0)))
        wsorted = wsorted + jnp.dot(sel[k], wk.astype(BF16), preferred_element_type=F32)
    wbuf[slot] = wsorted

    def issue(e, carry):
        sz, copies = seg_copies(i, slot, e)

        @pl.when(sz > 0)
        def _():
            for c in copies:
                c.start()

        return carry

    lax.fori_loop(0, n_experts, issue, 0)

    tail_tile = nu_ref[0] + i

    @pl.when(jnp.logical_and(i < n - 1, tail_tile < n_tail_tiles))
    def _():
        for c in tail_copies(tail_tile):
            c.start()

    @pl.when(i == n - 1)
    def _():
        for_tail(lambda c: c.start(), tail_tile)

        @pl.when(i >= 1)
        def _():
            wait_tile(i - 1, 1 - slot)

        wait_tile(i, slot)
        for_tail(lambda c: c.wait(), nu_ref[0])


def _dispatch_rows(hb, ids, wts, seg, n_experts, n_tiles_max):
    t, d = hb.shape
    ts, te = SORT_TILE, EXPERT_TILE
    rows = TOP_K * ts + n_experts * SEG_ALIGN
    n_slots = n_tiles_max * te

    def tile_map(i, *_):
        return (i, 0)

    return pl.pallas_call(
        functools.partial(_dispatch_kernel, n_experts=n_experts),
        out_shape=(jax.ShapeDtypeStruct((n_slots, d), BF16), jax.ShapeDtypeStruct((n_slots, LANES), F32),
                   jax.ShapeDtypeStruct((t, LANES), F32)),
        grid_spec=pltpu.PrefetchScalarGridSpec(
            num_scalar_prefetch=7,
            grid=(t // ts,),
            in_specs=[pl.BlockSpec((ts, d), tile_map), pl.BlockSpec((ts, LANES), tile_map),
                      pl.BlockSpec((ts, LANES), tile_map)],
            out_specs=(pl.BlockSpec(memory_space=pl.ANY), pl.BlockSpec(memory_space=pl.ANY),
                       pl.BlockSpec((ts, LANES), tile_map)),
            scratch_shapes=[
                pltpu.VMEM((2, rows, d), BF16),
                pltpu.VMEM((2, rows, LANES), F32),
                pltpu.VMEM((te, d), BF16),
                pltpu.VMEM((te, LANES), F32),
                pltpu.SemaphoreType.DMA((2,)),
                pltpu.SemaphoreType.DMA((2,)),
                pltpu.SemaphoreType.DMA((2,)),
                pltpu.SemaphoreType.DMA((2,)),
            ],
        ),
        compiler_params=_params("arbitrary"),
        name="dispatch",
    )(seg["src"], seg["dst"], seg["size"], seg["tile_tot"], seg["fill_start"], seg["fill_size"], seg["n_used"],
      hb, ids, wts)


def _expert_rows_kernel(te_ref, nu_ref, x_ref, w_ref, wg_ref, wu_ref, wd_ref, o_ref, wgb, wub, wdb):
    i = pl.program_id(0)
    n_used = nu_ref[0]

    @pl.when(i < n_used)
    def _():
        changed = jnp.logical_or(i == 0, te_ref[i] != te_ref[jnp.maximum(i - 1, 0)])

        @pl.when(changed)
        def _():
            wgb[...] = wg_ref[...].astype(BF16)
            wub[...] = wu_ref[...].astype(BF16)
            wdb[...] = wd_ref[...].astype(BF16)

        x = x_ref[...]
        a = jnp.dot(x, wgb[...], preferred_element_type=F32)
        u = jnp.dot(x, wub[...], preferred_element_type=F32)
        w = jnp.sum(w_ref[...], axis=-1, keepdims=True)
        hid = _silu(a) * u * w
        o_ref[...] = jnp.dot(hid.astype(BF16), wdb[...], preferred_element_type=F32).astype(o_ref.dtype)


def _expert_rows(xe, we, seg, wg, wu, wd, layer_base):
    n_slots, d = xe.shape
    f = wg.shape[-1]
    te = EXPERT_TILE

    def w_map(i, te_ref, nu_ref):
        return (layer_base + te_ref[i], 0, 0)

    def row_map(i, te_ref, nu_ref):
        return (jnp.minimum(i, nu_ref[0] - 1), 0)

    return pl.pallas_call(
        _expert_rows_kernel,
        out_shape=jax.ShapeDtypeStruct((n_slots, d), BF16),
        grid_spec=pltpu.PrefetchScalarGridSpec(
            num_scalar_prefetch=2,
            grid=(n_slots // te,),
            in_specs=[
                pl.BlockSpec((te, d), row_map),
                pl.BlockSpec((te, LANES), row_map),
                pl.BlockSpec((None, d, f), w_map),
                pl.BlockSpec((None, d, f), w_map),
                pl.BlockSpec((None, f, d), w_map),
            ],
            out_specs=pl.BlockSpec((te, d), row_map),
            scratch_shapes=[pltpu.VMEM((d, f), BF16), pltpu.VMEM((d, f), BF16), pltpu.VMEM((f, d), BF16)],
        ),
        input_output_aliases={2: 0},
        compiler_params=_params("arbitrary"),
        name="experts",
    )(seg["tile_expert"], seg["n_used"], xe, we, wg, wu, wd)


def _collect_kernel(src_ref, dst_ref, size_ref, tot_ref, y_hbm, x_ref, pos_ref, gp, gs, *rest, npt, n_experts, final):
    if final:
        fg_ref, op_ref, os_ref, buf, sem, o_ref = rest
    else:
        o_ref, buf, sem = rest
    i = pl.program_id(0)
    n = pl.num_programs(0)
    ts = x_ref.shape[0]
    rows = buf.shape[1]
    slot = lax.rem(i, 2)

    def start(tile, sl):
        def body(e, carry):
            k = tile * n_experts + e
            sz = pl.multiple_of(size_ref[k], SEG_ALIGN)
            s0 = pl.multiple_of(src_ref[k], SEG_ALIGN)
            d0 = pl.multiple_of(dst_ref[k], SEG_ALIGN)

            @pl.when(sz > 0)
            def _():
                pltpu.make_async_copy(y_hbm.at[pl.ds(d0, sz), :], buf.at[sl, pl.ds(s0, sz), :], sem.at[sl]).start()

            return carry

        lax.fori_loop(0, n_experts, body, 0)

    @pl.when(i == 0)
    def _():
        buf[...] = jnp.zeros(buf.shape, buf.dtype)
        start(0, 0)

    @pl.when(i + 1 < n)
    def _():
        start(i + 1, 1 - slot)

    tot = pl.multiple_of(tot_ref[i], SEG_ALIGN)
    pltpu.make_async_copy(y_hbm.at[pl.ds(0, tot), :], buf.at[slot, pl.ds(0, tot), :], sem.at[slot]).wait()
    lane = lax.broadcasted_iota(jnp.int32, (ts, LANES), 1).astype(F32)
    pos = pos_ref[...]
    r_iota = lax.broadcasted_iota(jnp.int32, (ts, rows), 1).astype(F32)
    pick = ((r_iota == _lane_pick(pos, lane, 0)) | (r_iota == _lane_pick(pos, lane, 1))).astype(BF16)
    ff = jnp.dot(pick, buf[slot], preferred_element_type=F32)
    _store_by_group(i, npt, None, o_ref, lambda v, m: v[0] + m[0] * v[1], [x_ref[...], ff], [gp], [gs])
    if final:
        y = _rms(o_ref[...], fg_ref[...])

        @pl.when(i < npt)
        def _():
            op_ref[...] = y

        @pl.when(i >= npt)
        def _():
            os_ref[...] = y


def _collect(ye, pos, x, seg, mod_p, mod_s, dims, n_experts, final_g=None):
    t, d = x.shape
    ts = SORT_TILE
    npt = dims["npt"]
    rows = TOP_K * ts + n_experts * SEG_ALIGN
    p_specs, s_specs = _mod_specs((5,), d, npt, dims["tps"], dims["n_seq"], dims["dec_batch"])
    final = final_g is not None

    def strip(spec):
        return pl.BlockSpec(spec.block_shape, lambda i, *_, m=spec.index_map: m(i))

    def tile_map(i, *_):
        return (i, 0)

    in_specs = [
        pl.BlockSpec(memory_space=pl.ANY),
        pl.BlockSpec((ts, d), tile_map),
        pl.BlockSpec((ts, LANES), tile_map),
        *[strip(s) for s in p_specs],
        *[strip(s) for s in s_specs],
    ]
    scratch = [pltpu.VMEM((2, rows, d), BF16), pltpu.SemaphoreType.DMA((2,))]
    args = [seg["src"], seg["dst"], seg["size"], seg["tile_tot"], ye, x, pos, mod_p, mod_s]
    if final:
        in_specs.append(pl.BlockSpec((1, d), lambda i, *_: (0, 0)))
        args.append(final_g.reshape(1, d))
        out_shape = (jax.ShapeDtypeStruct((npt * ts, d), F32), jax.ShapeDtypeStruct((t - npt * ts, d), F32))
        out_specs = (pl.BlockSpec((ts, d), lambda i, *_: (jnp.minimum(i, npt - 1), 0)),
                     pl.BlockSpec((ts, d), lambda i, *_: (jnp.maximum(i - npt, 0), 0)))
        scratch.append(pltpu.VMEM((ts, d), F32))
    else:
        out_shape = jax.ShapeDtypeStruct((t, d), F32)
        out_specs = pl.BlockSpec((ts, d), tile_map)
    return pl.pallas_call(
        functools.partial(_collect_kernel, npt=npt, n_experts=n_experts, final=final),
        out_shape=out_shape,
        grid_spec=pltpu.PrefetchScalarGridSpec(
            num_scalar_prefetch=4, grid=(t // ts,), in_specs=in_specs, out_specs=out_specs, scratch_shapes=scratch),
        compiler_params=_params("arbitrary"),
        name="collect_final" if final else "collect",
    )(*args)


def kernel(x_prompt, x_sample, c_prompt, c_sample, state_conv, state_gla, w_ada, b_ada, norm1_g, norm2_g, w_in, conv_w, conv_b, conv_ln_g, conv_ln_b, gate_w2, gate_b, gla_norm_g, w_out, router_grp_w, router_grp_b, router_exp_w, router_exp_b, exp_w_gate, exp_w_up, exp_w_down, final_norm_g):
    n_seq, seq_len, d = x_prompt.shape
    dec_batch, dec_seq, _ = x_sample.shape
    depth = w_ada.shape[0]
    kw, d_conv = conv_w.shape[1:]
    heads, dv = gla_norm_g.shape[1:]
    rank, qkw = gate_w2.shape[1:]
    dk = qkw // heads
    n_grp, n_exp = router_exp_w.shape[2:]
    n_experts = n_grp * n_exp
    tp, ts = n_seq * seq_len, dec_batch * dec_seq
    t = tp + ts
    tm = ROW_TILE
    assert ts == tm and seq_len % tm == 0 and d_conv == heads * dv and kw - 1 <= CONV_HALO
    n_main = 2 * d_conv + 2 * qkw + 2 * heads * dv
    dims = dict(npt=tp // tm, tps=seq_len // tm, n_seq=n_seq, dec_batch=dec_batch, dec_seq=dec_seq)

    xp, xs, xs_blk = x_prompt.reshape(tp, d), x_sample.transpose(1, 0, 2).reshape(ts, d), 0
    w_in_t = jnp.swapaxes(w_in, 1, 2)
    st_t = jnp.transpose(state_conv, (0, 2, 1, 3))
    pad = (-n_seq) % 8
    c_all = jnp.concatenate([c_prompt, jnp.zeros((pad, d), F32), c_sample], axis=0)
    mod = _ada(c_all, w_ada, b_ada)
    st = SORT_TILE
    assert tm % st == 0 and st % dec_batch == 0
    dims_sort = dict(npt=tp // st, tps=seq_len // st, n_seq=n_seq, dec_batch=dec_batch, dec_seq=dec_seq)
    n_tiles_max = -(-(TOP_K * t + (t // st) * n_experts * (SEG_ALIGN - 1)) // EXPERT_TILE) + n_experts

    conv_p, gla_p, conv_u, gla_s = [], [], [], []
    for l in range(depth):
        mod_p = mod[l, :n_seq].reshape(n_seq, 1, 6 * d)
        mod_s = mod[l, n_seq + pad:]
        gw2 = jnp.pad(gate_w2[l], ((0, LANES - rank), (0, 0))).astype(BF16)
        h, la = _norm1(xp, xs, xs_blk, norm1_g[l], mod_p, mod_s, w_in_t, l, n_main, rank, gw2, gate_b[l], dims)
        z = _inproj(h, w_in_t, l, n_main)

        cv_p, cb_p = _conv_prompt(z, conv_w[l], conv_b[l], conv_ln_g[l], conv_ln_b[l], n_seq, seq_len)
        cv_s, u_s = _conv_sample(z, st_t, l, conv_w[l], conv_b[l], conv_ln_g[l], conv_ln_b[l], tp, dec_seq, dec_batch)
        conv_p.append(cb_p)
        conv_u.append(u_s)

        go_p, gs_p = _gla_prompt(z, la, gla_norm_g[l], n_seq, seq_len, heads, dk, dv)

        def seq_major(a):
            return a.reshape(dec_seq, dec_batch, a.shape[-1]).transpose(1, 0, 2)

        zs = z[tp:]
        q3 = seq_major(zs[:, 2 * d_conv:2 * d_conv + qkw])
        k3 = seq_major(zs[:, 2 * d_conv + qkw:2 * d_conv + 2 * qkw])
        v3 = seq_major(zs[:, 2 * d_conv + 2 * qkw:2 * d_conv + 2 * qkw + heads * dv])
        g3 = seq_major(zs[:, 2 * d_conv + 2 * qkw + heads * dv:n_main])
        go_s, gs_s = _gla_sample(q3, k3, v3, g3, seq_major(la[tp:]), state_gla, l, gla_norm_g[l], heads, dk, dv,
                                 prev_states=tuple(gla_s) if l + 1 == depth else ())
        go_s = go_s.transpose(1, 0, 2).reshape(ts, heads * dv).astype(BF16)
        gla_p.append(gs_p)
        gla_s.append(gs_s)

        x = _outproj(cv_p, go_p, cv_s.reshape(ts, d_conv), go_s, xp, xs, xs_blk, w_out, l, mod_p, mod_s, dims)

        wr = jnp.concatenate([router_grp_w[l], router_exp_w[l].reshape(d, n_experts)], axis=1)
        wr = jnp.pad(wr, ((0, 0), (0, LANES - wr.shape[1])))
        wr_hi = wr.astype(BF16)
        wr_lo = (wr - wr_hi.astype(F32)).astype(BF16)
        br = jnp.concatenate([router_grp_b[l], router_exp_b[l].reshape(-1)])
        br = jnp.pad(br, (0, LANES - br.shape[0])).reshape(1, LANES)
        hb, ids, wts = _norm2(x, norm2_g[l], mod_p, mod_s, wr_hi, wr_lo, br, dims, n_grp, n_exp)
        seg = _segments(ids[:, :TOP_K], n_experts, n_tiles_max)
        xe, we, pos = _dispatch_rows(hb, ids, wts, seg, n_experts, n_tiles_max)
        f = exp_w_gate.shape[-1]
        ye = _expert_rows(xe, we, seg, exp_w_gate.reshape(depth * n_experts, d, f),
                          exp_w_up.reshape(depth * n_experts, d, f), exp_w_down.reshape(depth * n_experts, f, d),
                          l * n_experts)
        if l + 1 < depth:
            x = _collect(ye, pos, x, seg, mod_p, mod_s, dims_sort, n_experts)
            xp, xs, xs_blk = x, x, dims["npt"]
        else:
            y_p, y_s = _collect(ye, pos, x, seg, mod_p, mod_s, dims_sort, n_experts, final_g=final_norm_g)

    y_prompt = y_p.reshape(n_seq, seq_len, d)
    y_sample = y_s.reshape(dec_seq, dec_batch, d).transpose(1, 0, 2)
    conv_s = jnp.transpose(_conv_state_sample(st_t, jnp.stack(conv_u)), (0, 2, 1, 3))
    gla_s_all = gla_s[-1] if depth > 1 else gla_s[0][None]
    return (y_prompt, y_sample, jnp.stack(conv_p), jnp.stack(gla_p), conv_s, gla_s_all)
```

```python
import functools

import jax
import jax.numpy as jnp
from jax import lax
from jax.experimental import pallas as pl
from jax.experimental.pallas import tpu as pltpu

F32 = jnp.float32
BF16 = jnp.bfloat16

EPS = 1e-6
GATE_TAU = 16.0
GLA_CHUNK = 32
GLA_BLOCK = 128
TOP_K = 2

ROW_TILE = 512
EXPERT_TILE = 512
SORT_TILE = 512
SEG_ALIGN = 16
CONV_ROWS = 64
CONV_HALO = 32
LANES = 128
SUBLANES = 8
VMEM_LIMIT = 56 * 1024 * 1024


def _params(*sem):
    return pltpu.CompilerParams(dimension_semantics=sem, vmem_limit_bytes=VMEM_LIMIT)


def _bdot(a, b):
    return jnp.dot(a.astype(BF16), b.astype(BF16), preferred_element_type=F32)


def _round_bf16(x):
    return x.astype(BF16).astype(F32)


def _split3(x):
    hi = x.astype(BF16)
    r = x - hi.astype(F32)
    mid = r.astype(BF16)
    lo = (r - mid.astype(F32)).astype(BF16)
    return hi, mid, lo


def _silu(x):
    return x * jax.nn.sigmoid(x)


def _store_by_group(i, n_prompt_tiles, dec_seq, out_ref, fn, vals, p_refs, s_refs):
    @pl.when(i < n_prompt_tiles)
    def _():
        out_ref[...] = fn(vals, [r[...] for r in p_refs]).astype(out_ref.dtype)

    @pl.when(i >= n_prompt_tiles)
    def _():
        mods = [r[...] for r in s_refs]
        nb = mods[0].shape[0]
        for t in range(out_ref.shape[0] // nb):
            rows = slice(t * nb, (t + 1) * nb)
            out_ref[rows, :] = fn([v[rows] for v in vals], mods).astype(out_ref.dtype)


def _ada_kernel(c_ref, w_ref, b_ref, o_ref):
    c = c_ref[...]
    o_ref[...] = _bdot(_silu(c), w_ref[...]) + b_ref[...]


def _ada(c_all, w_ada, b_ada):
    depth, d, n = w_ada.shape
    rows = c_all.shape[0]
    tn = 1024
    return pl.pallas_call(
        _ada_kernel,
        out_shape=jax.ShapeDtypeStruct((depth, rows, n), F32),
        grid=(depth, n // tn),
        in_specs=[
            pl.BlockSpec((rows, d), lambda l, j: (0, 0)),
            pl.BlockSpec((None, d, tn), lambda l, j: (l, 0, j)),
            pl.BlockSpec((None, 1, tn), lambda l, j: (l, 0, j)),
        ],
        out_specs=pl.BlockSpec((None, rows, tn), lambda l, j: (l, 0, j)),
        compiler_params=_params("arbitrary", "arbitrary"),
        name="ada",
    )(c_all, w_ada, b_ada.reshape(depth, 1, n))


def _mod_specs(cols, width, n_prompt_tiles, tiles_per_seq, n_seq, dec_batch, grid_rank=1, row_axis=0, col_fn=None):
    p_specs, s_specs = [], []
    for c in cols:
        def p_map(*idx, c=c):
            b = jnp.minimum(idx[row_axis] // tiles_per_seq, n_seq - 1)
            return (b, 0, c if col_fn is None else col_fn(c, idx))

        def s_map(*idx, c=c):
            return (0, c if col_fn is None else col_fn(c, idx))

        p_specs.append(pl.BlockSpec((None, 1, width), p_map))
        s_specs.append(pl.BlockSpec((dec_batch, width), s_map))
    return p_specs, s_specs


def _rms(x, g):
    return x * lax.rsqrt(jnp.mean(x * x, axis=-1, keepdims=True) + EPS) * g


def _norm1_kernel(xp_ref, xs_ref, g_ref, shp, scp, shs, scs, wgl_ref, gw2_ref, gb_ref, h_ref, la_ref, *, npt, dec_seq, rank):
    i = pl.program_id(0)
    y = _rms(jnp.where(i < npt, xp_ref[...], xs_ref[...]), g_ref[...])
    _store_by_group(i, npt, dec_seq, h_ref, lambda v, m: v[0] * (1.0 + m[1]) + m[0], [y], [shp, scp], [shs, scs])
    row = lax.broadcasted_iota(jnp.int32, wgl_ref.shape, 0)
    w_gl = jnp.where(row < rank, wgl_ref[...], 0.0).astype(BF16)
    gate_lr = lax.dot_general(h_ref[...], w_gl, (((1,), (1,)), ((), ())), preferred_element_type=F32)
    pre = _bdot(gate_lr, gw2_ref[...]) + gb_ref[...]
    la_ref[...] = (jnp.minimum(pre, 0.0) - jnp.log1p(jnp.exp(-jnp.abs(pre)))) * (1.0 / GATE_TAU)


def _norm1(xp, xs, xs_blk, g, mod_p, mod_s, w_in_t, layer, n_main, rank, gw2, gb, dims):
    d = xp.shape[1]
    tm = ROW_TILE
    npt = dims["npt"]
    t = (npt + 1) * tm
    p_specs, s_specs = _mod_specs((0, 1), d, npt, dims["tps"], dims["n_seq"], dims["dec_batch"])
    qk = gw2.shape[1]
    assert n_main % LANES == 0 and rank <= LANES
    return pl.pallas_call(
        functools.partial(_norm1_kernel, npt=npt, dec_seq=dims["dec_seq"], rank=rank),
        out_shape=(jax.ShapeDtypeStruct((t, d), BF16), jax.ShapeDtypeStruct((t, qk), F32)),
        grid=(t // tm,),
        in_specs=[
            pl.BlockSpec((tm, d), lambda i: (jnp.minimum(i, npt - 1), 0)),
            pl.BlockSpec((tm, d), lambda i: (xs_blk, 0)),
            pl.BlockSpec((1, d), lambda i: (0, 0)),
            *p_specs,
            *s_specs,
            pl.BlockSpec((None, LANES, d), lambda i: (layer, n_main // LANES, 0)),
            pl.BlockSpec(gw2.shape, lambda i: (0, 0)),
            pl.BlockSpec((1, qk), lambda i: (0, 0)),
        ],
        out_specs=(pl.BlockSpec((tm, d), lambda i: (i, 0)), pl.BlockSpec((tm, qk), lambda i: (i, 0))),
        compiler_params=_params("arbitrary"),
        name="norm1",
    )(xp, xs, g.reshape(1, d), mod_p, mod_p, mod_s, mod_s, w_in_t, gw2, gb.reshape(1, qk))


def _inproj_kernel(h_ref, w_ref, o_ref, wb_ref):
    @pl.when(pl.program_id(1) == 0)
    def _():
        wb_ref[...] = w_ref[...].astype(BF16)

    o_ref[...] = lax.dot_general(h_ref[...], wb_ref[...], (((1,), (1,)), ((), ())), preferred_element_type=F32)


def _inproj(h, w_in_t, layer, n_cols):
    t, d = h.shape
    tm, tn = ROW_TILE, 1024
    return pl.pallas_call(
        _inproj_kernel,
        out_shape=jax.ShapeDtypeStruct((t, n_cols), F32),
        grid=(n_cols // tn, t // tm),
        in_specs=[
            pl.BlockSpec((tm, d), lambda j, i: (i, 0)),
            pl.BlockSpec((None, tn, d), lambda j, i: (layer, j, 0)),
        ],
        out_specs=pl.BlockSpec((tm, tn), lambda j, i: (i, j)),
        scratch_shapes=[pltpu.VMEM((tn, d), BF16)],
        compiler_params=_params("arbitrary", "arbitrary"),
        name="inproj",
    )(h, w_in_t)


def _ln_silu(y, g, b):
    mu = jnp.mean(y, axis=-1, keepdims=True)
    yc = y - mu
    var = jnp.mean(yc * yc, axis=-1, keepdims=True)
    return _silu(yc * lax.rsqrt(var + EPS) * g + b)


def _conv_prompt_kernel(a_ref, b_ref, cw_ref, cb_ref, lng_ref, lnb_ref, o_ref, st_ref, full_ref, cwb_ref, y_ref, *, kw):
    j = pl.program_id(1)
    tm, c = a_ref.shape
    halo = CONV_HALO
    phases = full_ref.shape[0]
    assert phases == SUBLANES

    @pl.when(jnp.logical_and(pl.program_id(0) == 0, j == 0))
    def _():
        full_ref[...] = jnp.zeros(full_ref.shape, F32)
        for w in range(kw):
            cwb_ref[w] = jnp.broadcast_to(cw_ref[w:w + 1, :], (SUBLANES, c))

    prev = full_ref[0, tm + halo - SUBLANES:tm + halo, :]
    tail = jnp.where(j == 0, 0.0, prev)

    @pl.when(j == 0)
    def _():
        for p in range(phases):
            full_ref[p, 0:halo, :] = jnp.zeros((halo, c), F32)

    @pl.when(j > 0)
    def _():
        for p in range(phases):
            full_ref[p, 0:halo, :] = full_ref[p, tm:tm + halo, :]

    u = a_ref[...] * jax.nn.sigmoid(b_ref[...])
    full_ref[0, halo:halo + tm, :] = u
    ext = jnp.concatenate([tail, u], axis=0)
    for p in range(1, phases):
        full_ref[p, halo - SUBLANES:halo - SUBLANES + tm, :] = pltpu.roll(ext, tm + SUBLANES - p, 0)[0:tm]
    off = halo - (kw - 1)
    rb = CONV_ROWS

    def body(r, carry):
        r0 = pl.multiple_of(r * rb, rb)
        for lt in range(c // LANES):
            cols = slice(lt * LANES, (lt + 1) * LANES)
            acc = None
            for p in range(phases):
                x = full_ref[p, pl.ds(r0, rb + halo), cols]
                for a in range(halo // phases + 1):
                    w = a * phases + p - off
                    if 0 <= w < kw and a * phases + rb <= rb + halo:
                        term = x[a * phases:a * phases + rb] * jnp.concatenate([cwb_ref[w, :, cols]] * (rb // SUBLANES), axis=0)
                        acc = term if acc is None else acc + term
            y_ref[pl.ds(r0, rb), cols] = acc
        return carry

    lax.fori_loop(0, tm // rb, body, 0)
    y = _ln_silu(y_ref[...] + cb_ref[...], lng_ref[...], lnb_ref[...])
    o_ref[...] = y.astype(o_ref.dtype)

    @pl.when(j == pl.num_programs(1) - 1)
    def _():
        st_ref[...] = full_ref[0, halo + tm - (kw - 1):halo + tm, :]


def _conv_prompt(z, cw, cb, lng, lnb, n_seq, seq_len):
    kw, c = cw.shape
    tm = ROW_TILE
    tps = seq_len // tm
    return pl.pallas_call(
        functools.partial(_conv_prompt_kernel, kw=kw),
        out_shape=(jax.ShapeDtypeStruct((n_seq * seq_len, c), BF16), jax.ShapeDtypeStruct((n_seq, kw - 1, c), F32)),
        grid=(n_seq, tps),
        in_specs=[
            pl.BlockSpec((tm, c), lambda b, j: (b * tps + j, 0)),
            pl.BlockSpec((tm, c), lambda b, j: (b * tps + j, 1)),
            pl.BlockSpec((kw, c), lambda b, j: (0, 0)),
            pl.BlockSpec((1, c), lambda b, j: (0, 0)),
            pl.BlockSpec((1, c), lambda b, j: (0, 0)),
            pl.BlockSpec((1, c), lambda b, j: (0, 0)),
        ],
        out_specs=(
            pl.BlockSpec((tm, c), lambda b, j: (b * tps + j, 0)),
            pl.BlockSpec((None, kw - 1, c), lambda b, j: (b, 0, 0)),
        ),
        scratch_shapes=[pltpu.VMEM((SUBLANES, tm + CONV_HALO, c), F32), pltpu.VMEM((kw, SUBLANES, c), F32),
                        pltpu.VMEM((tm, c), F32)],
        compiler_params=_params("arbitrary", "arbitrary"),
        name="conv_prompt",
    )(z, z, cw, cb.reshape(1, c), lng.reshape(1, c), lnb.reshape(1, c))


def _conv_sample_kernel(*refs, kw, dec_seq):
    a_refs = refs[0:dec_seq]
    b_refs = refs[dec_seq:2 * dec_seq]
    st_ref, cw_ref, cb_ref, lng_ref, lnb_ref, o_ref, u_ref = refs[2 * dec_seq:]
    hist = kw - 1
    u = [a_refs[t][...] * jax.nn.sigmoid(b_refs[t][...]) for t in range(dec_seq)]
    def row(j):
        return st_ref[j] if j < hist else u[j - hist]

    for t in range(dec_seq):
        acc = row(t) * cw_ref[0:1, :]
        for w in range(1, kw):
            acc = acc + row(t + w) * cw_ref[w:w + 1, :]
        y = _ln_silu(acc + cb_ref[...], lng_ref[...], lnb_ref[...])
        o_ref[t] = y.astype(o_ref.dtype)
        u_ref[t] = u[t]


def _conv_sample(z, st_t, layer, cw, cb, lng, lnb, row0, dec_seq, dec_batch):
    kw, c = cw.shape
    bs = 16
    a_specs = [pl.BlockSpec((bs, c), lambda s, t=t: ((row0 + t * dec_batch) // bs + s, 0)) for t in range(dec_seq)]
    b_specs = [pl.BlockSpec((bs, c), lambda s, t=t: ((row0 + t * dec_batch) // bs + s, 1)) for t in range(dec_seq)]
    vec = pl.BlockSpec((1, c), lambda s: (0, 0))
    return pl.pallas_call(
        functools.partial(_conv_sample_kernel, kw=kw, dec_seq=dec_seq),
        out_shape=(jax.ShapeDtypeStruct((dec_seq, dec_batch, c), BF16), jax.ShapeDtypeStruct((dec_seq, dec_batch, c), F32)),
        grid=(dec_batch // bs,),
        in_specs=[*a_specs, *b_specs, pl.BlockSpec((None, kw - 1, bs, c), lambda s: (layer, 0, s, 0)),
                  pl.BlockSpec((kw, c), lambda s: (0, 0)), vec, vec, vec],
        out_specs=(pl.BlockSpec((dec_seq, bs, c), lambda s: (0, s, 0)), pl.BlockSpec((dec_seq, bs, c), lambda s: (0, s, 0))),
        compiler_params=_params("arbitrary"),
        name="conv_sample",
    )(*([z] * (2 * dec_seq)), st_t, cw, cb.reshape(1, c), lng.reshape(1, c), lnb.reshape(1, c))


def _conv_state_kernel(st_ref, u_ref, o_ref):
    hist, n_new = st_ref.shape[0], u_ref.shape[0]
    for j in range(hist - n_new):
        o_ref[j] = st_ref[j + n_new]
    for t in range(n_new):
        o_ref[hist - n_new + t] = u_ref[t]


def _conv_state_sample(st_t, u_all):
    depth, hist, nb, c = st_t.shape
    n_new = u_all.shape[1]
    assert n_new <= hist
    bs = 16
    return pl.pallas_call(
        _conv_state_kernel,
        out_shape=jax.ShapeDtypeStruct(st_t.shape, st_t.dtype),
        grid=(depth, nb // bs),
        in_specs=[pl.BlockSpec((None, hist, bs, c), lambda l, s: (l, 0, s, 0)),
                  pl.BlockSpec((None, n_new, bs, c), lambda l, s: (l, 0, s, 0))],
        out_specs=pl.BlockSpec((None, hist, bs, c), lambda l, s: (l, 0, s, 0)),
        compiler_params=_params("arbitrary", "arbitrary"),
        name="conv_state",
    )(st_t, u_all)


def _gla_prompt_kernel(qk_ref, v_ref, g_ref, la_ref, gn_ref, o_ref, sfin_ref, st_ref, sn_ref, *, heads, dk, dv):
    j = pl.program_id(1)
    tm = qk_ref.shape[0]
    ck = GLA_CHUNK
    nch = tm // ck
    qkw = heads * dk

    @pl.when(j == 0)
    def _():
        st_ref[...] = jnp.zeros(st_ref.shape, F32)

    la = la_ref[...]
    sb = GLA_BLOCK
    blocks = [slice(r, r + sb) for r in range(0, tm, sb)]
    row = lax.broadcasted_iota(jnp.int32, (sb, sb), 0)
    col = lax.broadcasted_iota(jnp.int32, (sb, sb), 1)
    same = (row // ck) == (col // ck)
    causal = same & (col <= row)
    tri_incl = causal.astype(BF16)
    tri_after = (same & (col > row)).astype(BF16)
    sel = (lax.broadcasted_iota(jnp.int32, (nch, tm), 1) // ck == lax.broadcasted_iota(jnp.int32, (nch, tm), 0)).astype(BF16)
    parts = _split3(la)

    def blockwise(tri):
        return jnp.concatenate([sum(jnp.dot(tri, p[rs], preferred_element_type=F32) for p in parts) for rs in blocks], axis=0)

    b = blockwise(tri_incl)
    rest = blockwise(tri_after)
    tot = sum(jnp.dot(sel, p, preferred_element_type=F32) for p in parts)
    qk = qk_ref[...]
    q = qk[:, :qkw] * (dk ** -0.5)
    k = qk[:, qkw:]
    q_dec = (q * jnp.exp(b)).astype(BF16)
    k_inv = (k * jnp.exp(-b)).astype(BF16)
    k_end = _round_bf16(k * jnp.exp(rest))
    decay = jnp.exp(tot)
    v_all = v_ref[...]
    g_all = g_ref[...]
    for h in range(heads):
        ks = slice(h * dk, (h + 1) * dk)
        vs = slice(h * dv, (h + 1) * dv)
        qd, ki, ke = q_dec[:, ks], k_inv[:, ks], k_end[:, ks]
        vh = v_all[:, vs]
        vb = vh.astype(BF16)
        vr = _round_bf16(vh)
        intra = []
        for rs in blocks:
            att = lax.dot_general(qd[rs], ki[rs], (((1,), (1,)), ((), ())), preferred_element_type=F32)
            att = jnp.where(causal, att, 0.0).astype(BF16)
            intra.append(jnp.dot(att, vb[rs], preferred_element_type=F32))
        o = jnp.concatenate(intra, axis=0)
        s = st_ref[h]
        for n in range(nch):
            rs = slice(n * ck, (n + 1) * ck)
            sn_ref[n] = s.astype(BF16)
            upd = lax.dot_general(vr[rs], ke[rs], (((0,), (0,)), ((), ())), preferred_element_type=F32)
            s = s * decay[n:n + 1, ks] + upd
        st_ref[h] = s
        inter = [lax.dot_general(qd[n * ck:(n + 1) * ck], sn_ref[n], (((1,), (1,)), ((), ())), preferred_element_type=F32)
                 for n in range(nch)]
        o = o + jnp.concatenate(inter, axis=0)
        o = _rms(o, gn_ref[:, vs]) * _silu(g_all[:, vs])
        o_ref[:, vs] = o.astype(o_ref.dtype)

    @pl.when(j == pl.num_programs(1) - 1)
    def _():
        for h in range(heads):
            sfin_ref[h] = st_ref[h].T


def _gla_prompt(z, la, gn, n_seq, seq_len, heads, dk, dv):
    tm = ROW_TILE
    tps = seq_len // tm
    w = heads * dv
    qkw = heads * dk
    assert 2 * qkw == w
    return pl.pallas_call(
        functools.partial(_gla_prompt_kernel, heads=heads, dk=dk, dv=dv),
        out_shape=(jax.ShapeDtypeStruct((n_seq * seq_len, w), BF16), jax.ShapeDtypeStruct((n_seq, heads, dk, dv), F32)),
        grid=(n_seq, tps),
        in_specs=[
            pl.BlockSpec((tm, w), lambda b, j: (b * tps + j, 2)),
            pl.BlockSpec((tm, w), lambda b, j: (b * tps + j, 3)),
            pl.BlockSpec((tm, w), lambda b, j: (b * tps + j, 4)),
            pl.BlockSpec((tm, qkw), lambda b, j: (b * tps + j, 0)),
            pl.BlockSpec((1, w), lambda b, j: (0, 0)),
        ],
        out_specs=(
            pl.BlockSpec((tm, w), lambda b, j: (b * tps + j, 0)),
            pl.BlockSpec((None, heads, dk, dv), lambda b, j: (b, 0, 0, 0)),
        ),
        scratch_shapes=[pltpu.VMEM((heads, dv, dk), F32), pltpu.VMEM((tm // GLA_CHUNK, dv, dk), BF16)],
        compiler_params=_params("arbitrary", "arbitrary"),
        name="gla_prompt",
    )(z, z, z, la, gn.reshape(1, w))


def _gla_sample_kernel(q_ref, k_ref, v_ref, g_ref, la_ref, s_ref, gn_ref, *rest, heads, dk, dv, n_prev):
    bs, ln, _ = q_ref.shape
    if n_prev:
        o_ref, all_ref = rest[n_prev:]
        for p in range(n_prev):
            all_ref[p] = rest[p][...]
        ns_ref = all_ref.at[n_prev]
    else:
        o_ref, ns_ref = rest
    tril = lax.broadcasted_iota(jnp.int32, (ln, ln), 1) <= lax.broadcasted_iota(jnp.int32, (ln, ln), 0)
    for s in range(bs):
        q_s, k_s, v_s, g_s, la_s = q_ref[s], k_ref[s], v_ref[s], g_ref[s], la_ref[s]
        for h in range(heads):
            ks = slice(h * dk, (h + 1) * dk)
            vs = slice(h * dv, (h + 1) * dv)
            la = la_s[:, ks]
            rows = [la[0:1]]
            for t in range(1, ln):
                rows.append(rows[-1] + la[t:t + 1])
            b = jnp.concatenate(rows, axis=0)
            b_last = rows[-1]
            q_dec = _round_bf16(q_s[:, ks] * (dk ** -0.5) * jnp.exp(b))
            k_inv = _round_bf16(k_s[:, ks] * jnp.exp(-b))
            k_end = _round_bf16(k_s[:, ks] * jnp.exp(b_last - b))
            vr = _round_bf16(v_s[:, vs])
            s0 = s_ref[s, h]
            att = lax.dot_general(q_dec, k_inv, (((1,), (1,)), ((), ())), preferred_element_type=F32)
            att = _round_bf16(jnp.where(tril, att, 0.0))
            o = jnp.dot(att, vr, preferred_element_type=F32) + jnp.dot(q_dec, _round_bf16(s0), preferred_element_type=F32)
            upd = lax.dot_general(k_end, vr, (((0,), (0,)), ((), ())), preferred_element_type=F32)
            d_col = jnp.broadcast_to(jnp.exp(b_last), (dk, dk)).T
            ns_ref[s, h] = s0 * jnp.concatenate([d_col] * (dv // dk), axis=1) + upd
            o = _rms(o, gn_ref[:, vs]) * _silu(g_s[:, vs])
            o_ref[s, :, vs] = o.astype(o_ref.dtype)


def _gla_sample(q3, k3, v3, g3, la3, state, layer, gn, heads, dk, dv, prev_states=()):
    nb, ln, w = v3.shape
    qkw = heads * dk
    bs = 8
    n_prev = len(prev_states)
    seq_spec = pl.BlockSpec((bs, ln, qkw), lambda s: (s, 0, 0))
    wide_spec = pl.BlockSpec((bs, ln, w), lambda s: (s, 0, 0))
    in_specs = [seq_spec, seq_spec, wide_spec, wide_spec, seq_spec,
                pl.BlockSpec((None, bs, heads, dk, dv), lambda s: (layer, s, 0, 0, 0)),
                pl.BlockSpec((1, w), lambda s: (0, 0)),
                *[pl.BlockSpec((bs, heads, dk, dv), lambda s: (s, 0, 0, 0))] * n_prev]
    if n_prev:
        state_shape = jax.ShapeDtypeStruct((n_prev + 1, nb, heads, dk, dv), F32)
        state_spec = pl.BlockSpec((n_prev + 1, bs, heads, dk, dv), lambda s: (0, s, 0, 0, 0))
    else:
        state_shape = jax.ShapeDtypeStruct((nb, heads, dk, dv), F32)
        state_spec = pl.BlockSpec((bs, heads, dk, dv), lambda s: (s, 0, 0, 0))
    return pl.pallas_call(
        functools.partial(_gla_sample_kernel, heads=heads, dk=dk, dv=dv, n_prev=n_prev),
        out_shape=(jax.ShapeDtypeStruct((nb, ln, w), F32), state_shape),
        grid=(nb // bs,),
        in_specs=in_specs,
        out_specs=(wide_spec, state_spec),
        compiler_params=_params("arbitrary"),
        name="gla_sample_last" if n_prev else "gla_sample",
    )(q3, k3, v3, g3, la3, state, gn.reshape(1, w), *prev_states)


def _outproj_kernel(cp_ref, op_ref, cs_ref, os_ref, xp_ref, xs_ref, w_ref, gp, gs, y_ref, wb_ref, *, npt, dec_seq):
    i = pl.program_id(1)

    @pl.when(i == 0)
    def _():
        wb_ref[...] = w_ref[...].astype(BF16)

    half = cp_ref.shape[1]

    def mixed(c_ref, o_ref):
        return (jnp.dot(c_ref[...], wb_ref[0:half, :], preferred_element_type=F32)
                + jnp.dot(o_ref[...], wb_ref[half:, :], preferred_element_type=F32))

    @pl.when(i < npt)
    def _():
        y_ref[...] = xp_ref[...] + gp[...] * mixed(cp_ref, op_ref)

    @pl.when(i >= npt)
    def _():
        mix = mixed(cs_ref, os_ref)
        nb = gs.shape[0]
        for t in range(dec_seq):
            rows = slice(t * nb, (t + 1) * nb)
            y_ref[rows, :] = xs_ref[rows, :] + gs[...] * mix[rows]


def _outproj(conv_p, gla_p, conv_s, gla_s, xp, xs, xs_blk, w_out, layer, mod_p, mod_s, dims):
    d = xp.shape[1]
    half = conv_p.shape[1]
    tm, tn = ROW_TILE, 1024
    nj = d // tn
    npt = dims["npt"]
    t = (npt + 1) * tm
    p_specs, s_specs = _mod_specs((2,), tn, dims["npt"], dims["tps"], dims["n_seq"], dims["dec_batch"],
                                  row_axis=1, col_fn=lambda c, idx: c * nj + idx[0])
    return pl.pallas_call(
        functools.partial(_outproj_kernel, npt=dims["npt"], dec_seq=dims["dec_seq"]),
        out_shape=jax.ShapeDtypeStruct((t, d), F32),
        grid=(nj, t // tm),
        in_specs=[
            pl.BlockSpec((tm, half), lambda j, i: (jnp.minimum(i, npt - 1), 0)),
            pl.BlockSpec((tm, half), lambda j, i: (jnp.minimum(i, npt - 1), 0)),
            pl.BlockSpec((tm, half), lambda j, i: (0, 0)),
            pl.BlockSpec((tm, half), lambda j, i: (0, 0)),
            pl.BlockSpec((tm, tn), lambda j, i: (jnp.minimum(i, npt - 1), j)),
            pl.BlockSpec((tm, tn), lambda j, i: (xs_blk, j)),
            pl.BlockSpec((None, d, tn), lambda j, i: (layer, 0, j)),
            *p_specs,
            *s_specs,
        ],
        out_specs=pl.BlockSpec((tm, tn), lambda j, i: (i, j)),
        scratch_shapes=[pltpu.VMEM((d, tn), BF16)],
        compiler_params=_params("arbitrary", "arbitrary"),
        name="outproj",
    )(conv_p, gla_p, conv_s, gla_s, xp, xs, w_out, mod_p, mod_s)


def _norm2_kernel(x_ref, g_ref, shp, scp, shs, scs, wr_hi, wr_lo, br_ref, hb_ref, ids_ref, wts_ref, h_ref, *, npt, dec_seq, n_grp, n_exp):
    i = pl.program_id(0)
    y = _rms(x_ref[...], g_ref[...])
    _store_by_group(i, npt, dec_seq, h_ref, lambda v, m: v[0] * (1.0 + m[1]) + m[0], [y], [shp, scp], [shs, scs])
    h = h_ref[...]
    h_hi = h.astype(BF16)
    hb_ref[...] = h_hi
    h_lo = (h - h_hi.astype(F32)).astype(BF16)
    logits = (jnp.dot(h_hi, wr_hi[...], preferred_element_type=F32) + jnp.dot(h_lo, wr_hi[...], preferred_element_type=F32)
              + jnp.dot(h_hi, wr_lo[...], preferred_element_type=F32)) + br_ref[...]
    lane = lax.broadcasted_iota(jnp.int32, logits.shape, 1).astype(F32)
    big = jnp.float32(LANES)
    neg = jnp.float32(-jnp.inf)
    gl = jnp.where(lane < n_grp, logits, neg)
    gmax = jnp.max(gl, axis=-1, keepdims=True)
    gidx = jnp.min(jnp.where(gl == gmax, lane, big), axis=-1, keepdims=True)
    g_w = 1.0 / jnp.sum(jnp.exp(gl - gmax), axis=-1, keepdims=True)
    lo = n_grp + gidx * n_exp
    in_grp = (lane >= lo) & (lane < lo + n_exp)
    sl = jnp.where(in_grp, logits, neg)
    p = jnp.exp(sl - jnp.max(sl, axis=-1, keepdims=True))
    p = p / jnp.sum(p, axis=-1, keepdims=True)
    p = jnp.where(in_grp, p, -1.0)
    p1 = jnp.max(p, axis=-1, keepdims=True)
    i1 = jnp.min(jnp.where(p == p1, lane, big), axis=-1, keepdims=True)
    p_rest = jnp.where(lane == i1, -1.0, p)
    p2 = jnp.max(p_rest, axis=-1, keepdims=True)
    i2 = jnp.min(jnp.where(p_rest == p2, lane, big), axis=-1, keepdims=True)
    denom = p1 + p2
    ids_ref[...] = jnp.where(lane == 0, i1 - n_grp, jnp.where(lane == 1, i2 - n_grp, 0.0)).astype(jnp.int32)
    wts_ref[...] = jnp.where(lane == 0, g_w * (p1 / denom), jnp.where(lane == 1, g_w * (p2 / denom), 0.0))


def _norm2(x, g, mod_p, mod_s, wr_hi, wr_lo, br, dims, n_grp, n_exp):
    t, d = x.shape
    tm = ROW_TILE
    p_specs, s_specs = _mod_specs((3, 4), d, dims["npt"], dims["tps"], dims["n_seq"], dims["dec_batch"])
    return pl.pallas_call(
        functools.partial(_norm2_kernel, npt=dims["npt"], dec_seq=dims["dec_seq"], n_grp=n_grp, n_exp=n_exp),
        out_shape=(jax.ShapeDtypeStruct((t, d), BF16), jax.ShapeDtypeStruct((t, LANES), jnp.int32),
                   jax.ShapeDtypeStruct((t, LANES), F32)),
        grid=(t // tm,),
        in_specs=[
            pl.BlockSpec((tm, d), lambda i: (i, 0)),
            pl.BlockSpec((1, d), lambda i: (0, 0)),
            *p_specs,
            *s_specs,
            pl.BlockSpec((d, LANES), lambda i: (0, 0)),
            pl.BlockSpec((d, LANES), lambda i: (0, 0)),
            pl.BlockSpec((1, LANES), lambda i: (0, 0)),
        ],
        out_specs=(pl.BlockSpec((tm, d), lambda i: (i, 0)), pl.BlockSpec((tm, LANES), lambda i: (i, 0)),
                   pl.BlockSpec((tm, LANES), lambda i: (i, 0))),
        scratch_shapes=[pltpu.VMEM((tm, d), F32)],
        compiler_params=_params("arbitrary"),
        name="norm2_router",
    )(x, g.reshape(1, d), mod_p, mod_p, mod_s, mod_s, wr_hi, wr_lo, br)


def _segments(ids, n_experts, n_tiles_max):
    t = ids.shape[0]
    ts, te, al = SORT_TILE, EXPERT_TILE, SEG_ALIGN
    nt = t // ts
    e_iota = jnp.arange(n_experts, dtype=jnp.int32)
    cnt = jnp.sum((ids.reshape(nt, ts * TOP_K, 1) == e_iota).astype(jnp.int32), axis=1)
    size = (cnt + al - 1) // al * al
    src = jnp.cumsum(size, axis=1) - size
    tot = jnp.sum(size, axis=0)
    tot_pad = (tot + te - 1) // te * te
    ends = jnp.cumsum(tot_pad)
    exp_off = ends - tot_pad
    dst = exp_off[None, :] + jnp.cumsum(size, axis=0) - size
    n_used = (ends[-1] // te).astype(jnp.int32)
    tile_start = jnp.arange(n_tiles_max, dtype=jnp.int32) * te
    tile_expert = jnp.minimum(jnp.sum(ends[None, :] <= tile_start[:, None], axis=1), n_experts - 1).astype(jnp.int32)
    last = tile_expert[jnp.maximum(n_used - 1, 0)]
    used = jnp.arange(n_tiles_max) < n_used
    tile_expert = jnp.where(used, tile_expert, last)
    prev_e = jnp.concatenate([jnp.full((1,), -1, jnp.int32), tile_expert[:-1]])
    tile_first = (used & (tile_expert != prev_e)).astype(jnp.int32)
    tile_slot = ((jnp.cumsum(tile_first) - 1) % 2).astype(jnp.int32)
    next_start = jnp.sum(used[None, :] & (tile_expert[None, :] <= tile_expert[:, None]), axis=1)
    tile_next = jnp.where(next_start < n_used, tile_expert[jnp.minimum(next_start, n_tiles_max - 1)], -1).astype(jnp.int32)
    return dict(tile_first=tile_first, tile_slot=tile_slot, tile_next=tile_next,
                src=src.reshape(-1).astype(jnp.int32), dst=dst.reshape(-1).astype(jnp.int32),
                size=size.reshape(-1).astype(jnp.int32), tile_tot=jnp.sum(size, axis=1).astype(jnp.int32),
                fill_start=(exp_off + tot).astype(jnp.int32), fill_size=(tot_pad - tot).astype(jnp.int32),
                tile_expert=tile_expert, n_used=n_used.reshape(1))


def _lane_pick(x, lane, k):
    return jnp.sum(jnp.where(lane == k, x, 0.0), axis=-1, keepdims=True)


def _sorted_positions(ids, lane):
    ts = ids.shape[0]
    onehot = [(lane == _lane_pick(ids, lane, k)).astype(F32) for k in range(TOP_K)]
    row = lax.broadcasted_iota(jnp.int32, (ts, ts), 0)
    col = lax.broadcasted_iota(jnp.int32, (ts, ts), 1)
    before = (col < row).astype(BF16)
    earlier = [jnp.dot(before, o.astype(BF16), preferred_element_type=F32) for o in onehot]
    cnt = [jnp.sum(o, axis=0, keepdims=True) for o in onehot]
    size = jnp.ceil((cnt[0] + cnt[1]) * (1.0 / SEG_ALIGN)) * SEG_ALIGN
    er = lax.broadcasted_iota(jnp.int32, (LANES, LANES), 0)
    ec = lax.broadcasted_iota(jnp.int32, (LANES, LANES), 1)
    start = jnp.dot(jnp.broadcast_to(size, (SUBLANES, LANES)).astype(BF16), (er < ec).astype(BF16),
                    preferred_element_type=F32)[0:1]
    base = [start + earlier[0], start + cnt[0] + earlier[1]]
    return [jnp.sum(onehot[k] * base[k], axis=-1, keepdims=True) for k in range(TOP_K)]


def _as_row(col_vals, lane):
    hi = jnp.floor(col_vals * (1.0 / 64.0))
    lo = col_vals - hi * 64.0
    ones = jnp.ones((SUBLANES, LANES), BF16)
    nt = (((1,), (1,)), ((), ()))
    hi_row = lax.dot_general(ones, jnp.where(lane == 0, hi, 0.0).astype(BF16), nt, preferred_element_type=F32)
    lo_row = lax.dot_general(ones, jnp.where(lane == 0, lo, 0.0).astype(BF16), nt, preferred_element_type=F32)
    return (hi_row * 64.0 + lo_row)[0:1]


def _dispatch_kernel(src_ref, dst_ref, size_ref, tot_ref, fst_ref, fsz_ref, nu_ref, h_ref, ids_ref, wts_ref, xe_hbm,
                     we_hbm, pos_ref, sbuf, wbuf, zx, zw, semx, semw, semz, semt, *, n_experts):
    i = pl.program_id(0)
    n = pl.num_programs(0)
    ts = h_ref.shape[0]
    rows = sbuf.shape[1]
    te = zx.shape[0]
    slot = lax.rem(i, 2)

    def tail_copies(tl):
        r0 = pl.multiple_of(tl * te, te)
        return (pltpu.make_async_copy(zx, xe_hbm.at[pl.ds(r0, te), :], semt.at[0]),
                pltpu.make_async_copy(zw, we_hbm.at[pl.ds(r0, te), :], semt.at[1]))

    n_tail_tiles = xe_hbm.shape[0] // te

    def for_tail(action, first):
        def body(tl, carry):
            for c in tail_copies(tl):
                action(c)
            return carry

        lax.fori_loop(first, n_tail_tiles, body, 0)

    def seg_copies(tile, sl, e):
        k = tile * n_experts + e
        sz = pl.multiple_of(size_ref[k], SEG_ALIGN)
        s0 = pl.multiple_of(src_ref[k], SEG_ALIGN)
        d0 = pl.multiple_of(dst_ref[k], SEG_ALIGN)
        return sz, (pltpu.make_async_copy(sbuf.at[sl, pl.ds(s0, sz), :], xe_hbm.at[pl.ds(d0, sz), :], semx.at[sl]),
                    pltpu.make_async_copy(wbuf.at[sl, pl.ds(s0, sz), :], we_hbm.at[pl.ds(d0, sz), :], semw.at[sl]))

    def wait_tile(tile, sl):
        tot = pl.multiple_of(tot_ref[tile], SEG_ALIGN)
        pltpu.make_async_copy(sbuf.at[sl, pl.ds(0, tot), :], xe_hbm.at[pl.ds(0, tot), :], semx.at[sl]).wait()
        pltpu.make_async_copy(wbuf.at[sl, pl.ds(0, tot), :], we_hbm.at[pl.ds(0, tot), :], semw.at[sl]).wait()

    @pl.when(i == 0)
    def _():
        zx[...] = jnp.zeros(zx.shape, zx.dtype)
        zw[...] = jnp.zeros(zw.shape, zw.dtype)

        def fill(e, carry):
            sz = pl.multiple_of(fsz_ref[e], SEG_ALIGN)
            d0 = pl.multiple_of(fst_ref[e], SEG_ALIGN)

            @pl.when(sz > 0)
            def _():
                cx = pltpu.make_async_copy(zx.at[pl.ds(0, sz), :], xe_hbm.at[pl.ds(d0, sz), :], semz.at[0])
                cw = pltpu.make_async_copy(zw.at[pl.ds(0, sz), :], we_hbm.at[pl.ds(d0, sz), :], semz.at[1])
                cx.start()
                cw.start()
                cx.wait()
                cw.wait()

            return carry

        lax.fori_loop(0, n_experts, fill, 0)

    @pl.when(i >= 2)
    def _():
        wait_tile(i - 2, slot)

    lane = lax.broadcasted_iota(jnp.int32, (ts, LANES), 1).astype(F32)
    pos = _sorted_positions(ids_ref[...].astype(F32), lane)
    pos_ref[...] = jnp.where(lane == 0, pos[0], jnp.where(lane == 1, pos[1], 0.0))
    r_iota = lax.broadcasted_iota(jnp.int32, (rows, ts), 0).astype(F32)
    assert rows <= 4096
    sel = [(r_iota == _as_row(p, lane)).astype(BF16) for p in pos]
    sbuf[slot] = jnp.dot(sel[0] + sel[1], h_ref[...], preferred_element_type=F32).astype(BF16)
    wts = wts_ref[...]
    wsorted = jnp.zeros((rows, LANES), F32)
    for k in range(TOP_K):
        pieces = _split3(_lane_pick(wts, lane, k))
        wk = jnp.where(lane == 0, pieces[0].astype(F32), jnp.where(lane == 1, pieces[1].astype(F32),
                       jnp.where(lane == 2, pieces[2].astype(F32), 0.0)))
        wsorted = wsorted + jnp.dot(sel[k], wk.astype(BF16), preferred_element_type=F32)
    wbuf[slot] = wsorted

    def issue(e, carry):
        sz, copies = seg_copies(i, slot, e)

        @pl.when(sz > 0)
        def _():
            for c in copies:
                c.start()

        return carry

    lax.fori_loop(0, n_experts, issue, 0)

    tail_tile = nu_ref[0] + i

    @pl.when(jnp.logical_and(i < n - 1, tail_tile < n_tail_tiles))
    def _():
        for c in tail_copies(tail_tile):
            c.start()

    @pl.when(i == n - 1)
    def _():
        for_tail(lambda c: c.start(), tail_tile)

        @pl.when(i >= 1)
        def _():
            wait_tile(i - 1, 1 - slot)

        wait_tile(i, slot)
        for_tail(lambda c: c.wait(), nu_ref[0])


def _dispatch_rows(hb, ids, wts, seg, n_experts, n_tiles_max):
    t, d = hb.shape
    ts, te = SORT_TILE, EXPERT_TILE
    rows = TOP_K * ts + n_experts * SEG_ALIGN
    n_slots = n_tiles_max * te

    def tile_map(i, *_):
        return (i, 0)

    return pl.pallas_call(
        functools.partial(_dispatch_kernel, n_experts=n_experts),
        out_shape=(jax.ShapeDtypeStruct((n_slots, d), BF16), jax.ShapeDtypeStruct((n_slots, LANES), F32),
                   jax.ShapeDtypeStruct((t, LANES), F32)),
        grid_spec=pltpu.PrefetchScalarGridSpec(
            num_scalar_prefetch=7,
            grid=(t // ts,),
            in_specs=[pl.BlockSpec((ts, d), tile_map), pl.BlockSpec((ts, LANES), tile_map),
                      pl.BlockSpec((ts, LANES), tile_map)],
            out_specs=(pl.BlockSpec(memory_space=pl.ANY), pl.BlockSpec(memory_space=pl.ANY),
                       pl.BlockSpec((ts, LANES), tile_map)),
            scratch_shapes=[
                pltpu.VMEM((2, rows, d), BF16),
                pltpu.VMEM((2, rows, LANES), F32),
                pltpu.VMEM((te, d), BF16),
                pltpu.VMEM((te, LANES), F32),
                pltpu.SemaphoreType.DMA((2,)),
                pltpu.SemaphoreType.DMA((2,)),
                pltpu.SemaphoreType.DMA((2,)),
                pltpu.SemaphoreType.DMA((2,)),
            ],
        ),
        compiler_params=_params("arbitrary"),
        name="dispatch",
    )(seg["src"], seg["dst"], seg["size"], seg["tile_tot"], seg["fill_start"], seg["fill_size"], seg["n_used"],
      hb, ids, wts)


def _expert_rows_kernel(te_ref, nu_ref, first_ref, slot_ref, next_ref, x_ref, w_ref, wg_hbm, wu_hbm, wd_hbm, o_ref,
                        wgf, wuf, wdf, wsem, wgb, wub, wdb, *, layer_base):
    i = pl.program_id(0)
    n_used = nu_ref[0]

    def weight_copies(e, sl):
        return (pltpu.make_async_copy(wg_hbm.at[layer_base + e], wgf.at[sl], wsem.at[sl]),
                pltpu.make_async_copy(wu_hbm.at[layer_base + e], wuf.at[sl], wsem.at[sl]),
                pltpu.make_async_copy(wd_hbm.at[layer_base + e], wdf.at[sl], wsem.at[sl]))

    @pl.when(i == 0)
    def _():
        for c in weight_copies(te_ref[0], 0):
            c.start()

    @pl.when(i < n_used)
    def _():
        @pl.when(first_ref[i] == 1)
        def _():
            sl = slot_ref[i]
            for c in weight_copies(te_ref[i], sl):
                c.wait()
            wgb[...] = wgf[sl].astype(BF16)
            wub[...] = wuf[sl].astype(BF16)
            wdb[...] = wdf[sl].astype(BF16)
            nxt = next_ref[i]

            @pl.when(nxt >= 0)
            def _():
                for c in weight_copies(nxt, 1 - sl):
                    c.start()

        x = x_ref[...]
        a = jnp.dot(x, wgb[...], preferred_element_type=F32)
        u = jnp.dot(x, wub[...], preferred_element_type=F32)
        w = jnp.sum(w_ref[...], axis=-1, keepdims=True)
        hid = _silu(a) * u * w
        o_ref[...] = jnp.dot(hid.astype(BF16), wdb[...], preferred_element_type=F32).astype(o_ref.dtype)


def _expert_rows(xe, we, seg, wg, wu, wd, layer_base):
    n_slots, d = xe.shape
    f = wg.shape[-1]
    te = EXPERT_TILE

    def row_map(i, te_ref, nu_ref, *_):
        return (jnp.minimum(i, nu_ref[0] - 1), 0)

    return pl.pallas_call(
        functools.partial(_expert_rows_kernel, layer_base=layer_base),
        out_shape=jax.ShapeDtypeStruct((n_slots, d), BF16),
        grid_spec=pltpu.PrefetchScalarGridSpec(
            num_scalar_prefetch=5,
            grid=(n_slots // te,),
            in_specs=[
                pl.BlockSpec((te, d), row_map),
                pl.BlockSpec((te, LANES), row_map),
                pl.BlockSpec(memory_space=pl.ANY),
                pl.BlockSpec(memory_space=pl.ANY),
                pl.BlockSpec(memory_space=pl.ANY),
            ],
            out_specs=pl.BlockSpec((te, d), row_map),
            scratch_shapes=[
                pltpu.VMEM((2, d, f), F32), pltpu.VMEM((2, d, f), F32), pltpu.VMEM((2, f, d), F32),
                pltpu.SemaphoreType.DMA((2,)),
                pltpu.VMEM((d, f), BF16), pltpu.VMEM((d, f), BF16), pltpu.VMEM((f, d), BF16),
            ],
        ),
        input_output_aliases={5: 0},
        compiler_params=_params("arbitrary"),
        name="experts",
    )(seg["tile_expert"], seg["n_used"], seg["tile_first"], seg["tile_slot"], seg["tile_next"], xe, we, wg, wu, wd)


def _collect_kernel(src_ref, dst_ref, size_ref, tot_ref, y_hbm, x_ref, pos_ref, gp, gs, *rest, npt, n_experts, final):
    if final:
        fg_ref, op_ref, os_ref, buf, sem, o_ref = rest
    else:
        o_ref, buf, sem = rest
    i = pl.program_id(0)
    n = pl.num_programs(0)
    ts = x_ref.shape[0]
    rows = buf.shape[1]
    slot = lax.rem(i, 2)

    def start(tile, sl):
        def body(e, carry):
            k = tile * n_experts + e
            sz = pl.multiple_of(size_ref[k], SEG_ALIGN)
            s0 = pl.multiple_of(src_ref[k], SEG_ALIGN)
            d0 = pl.multiple_of(dst_ref[k], SEG_ALIGN)

            @pl.when(sz > 0)
            def _():
                pltpu.make_async_copy(y_hbm.at[pl.ds(d0, sz), :], buf.at[sl, pl.ds(s0, sz), :], sem.at[sl]).start()

            return carry

        lax.fori_loop(0, n_experts, body, 0)

    @pl.when(i == 0)
    def _():
        buf[...] = jnp.zeros(buf.shape, buf.dtype)
        start(0, 0)

    @pl.when(i + 1 < n)
    def _():
        start(i + 1, 1 - slot)

    tot = pl.multiple_of(tot_ref[i], SEG_ALIGN)
    pltpu.make_async_copy(y_hbm.at[pl.ds(0, tot), :], buf.at[slot, pl.ds(0, tot), :], sem.at[slot]).wait()
    lane = lax.broadcasted_iota(jnp.int32, (ts, LANES), 1).astype(F32)
    pos = pos_ref[...]
    r_iota = lax.broadcasted_iota(jnp.int32, (ts, rows), 1).astype(F32)
    pick = ((r_iota == _lane_pick(pos, lane, 0)) | (r_iota == _lane_pick(pos, lane, 1))).astype(BF16)
    ff = jnp.dot(pick, buf[slot], preferred_element_type=F32)
    _store_by_group(i, npt, None, o_ref, lambda v, m: v[0] + m[0] * v[1], [x_ref[...], ff], [gp], [gs])
    if final:
        y = _rms(o_ref[...], fg_ref[...])

        @pl.when(i < npt)
        def _():
            op_ref[...] = y

        @pl.when(i >= npt)
        def _():
            os_ref[...] = y


def _collect(ye, pos, x, seg, mod_p, mod_s, dims, n_experts, final_g=None):
    t, d = x.shape
    ts = SORT_TILE
    npt = dims["npt"]
    rows = TOP_K * ts + n_experts * SEG_ALIGN
    p_specs, s_specs = _mod_specs((5,), d, npt, dims["tps"], dims["n_seq"], dims["dec_batch"])
    final = final_g is not None

    def strip(spec):
        return pl.BlockSpec(spec.block_shape, lambda i, *_, m=spec.index_map: m(i))

    def tile_map(i, *_):
        return (i, 0)

    in_specs = [
        pl.BlockSpec(memory_space=pl.ANY),
        pl.BlockSpec((ts, d), tile_map),
        pl.BlockSpec((ts, LANES), tile_map),
        *[strip(s) for s in p_specs],
        *[strip(s) for s in s_specs],
    ]
    scratch = [pltpu.VMEM((2, rows, d), BF16), pltpu.SemaphoreType.DMA((2,))]
    args = [seg["src"], seg["dst"], seg["size"], seg["tile_tot"], ye, x, pos, mod_p, mod_s]
    if final:
        in_specs.append(pl.BlockSpec((1, d), lambda i, *_: (0, 0)))
        args.append(final_g.reshape(1, d))
        out_shape = (jax.ShapeDtypeStruct((npt * ts, d), F32), jax.ShapeDtypeStruct((t - npt * ts, d), F32))
        out_specs = (pl.BlockSpec((ts, d), lambda i, *_: (jnp.minimum(i, npt - 1), 0)),
                     pl.BlockSpec((ts, d), lambda i, *_: (jnp.maximum(i - npt, 0), 0)))
        scratch.append(pltpu.VMEM((ts, d), F32))
    else:
        out_shape = jax.ShapeDtypeStruct((t, d), F32)
        out_specs = pl.BlockSpec((ts, d), tile_map)
    return pl.pallas_call(
        functools.partial(_collect_kernel, npt=npt, n_experts=n_experts, final=final),
        out_shape=out_shape,
        grid_spec=pltpu.PrefetchScalarGridSpec(
            num_scalar_prefetch=4, grid=(t // ts,), in_specs=in_specs, out_specs=out_specs, scratch_shapes=scratch),
        compiler_params=_params("arbitrary"),
        name="collect_final" if final else "collect",
    )(*args)


def kernel(x_prompt, x_sample, c_prompt, c_sample, state_conv, state_gla, w_ada, b_ada, norm1_g, norm2_g, w_in, conv_w, conv_b, conv_ln_g, conv_ln_b, gate_w2, gate_b, gla_norm_g, w_out, router_grp_w, router_grp_b, router_exp_w, router_exp_b, exp_w_gate, exp_w_up, exp_w_down, final_norm_g):
    n_seq, seq_len, d = x_prompt.shape
    dec_batch, dec_seq, _ = x_sample.shape
    depth = w_ada.shape[0]
    kw, d_conv = conv_w.shape[1:]
    heads, dv = gla_norm_g.shape[1:]
    rank, qkw = gate_w2.shape[1:]
    dk = qkw // heads
    n_grp, n_exp = router_exp_w.shape[2:]
    n_experts = n_grp * n_exp
    tp, ts = n_seq * seq_len, dec_batch * dec_seq
    t = tp + ts
    tm = ROW_TILE
    assert ts == tm and seq_len % tm == 0 and d_conv == heads * dv and kw - 1 <= CONV_HALO
    n_main = 2 * d_conv + 2 * qkw + 2 * heads * dv
    dims = dict(npt=tp // tm, tps=seq_len // tm, n_seq=n_seq, dec_batch=dec_batch, dec_seq=dec_seq)

    xp, xs, xs_blk = x_prompt.reshape(tp, d), x_sample.transpose(1, 0, 2).reshape(ts, d), 0
    w_in_t = jnp.swapaxes(w_in, 1, 2)
    st_t = jnp.transpose(state_conv, (0, 2, 1, 3))
    pad = (-n_seq) % 8
    c_all = jnp.concatenate([c_prompt, jnp.zeros((pad, d), F32), c_sample], axis=0)
    mod = _ada(c_all, w_ada, b_ada)
    st = SORT_TILE
    assert tm % st == 0 and st % dec_batch == 0
    dims_sort = dict(npt=tp // st, tps=seq_len // st, n_seq=n_seq, dec_batch=dec_batch, dec_seq=dec_seq)
    n_tiles_max = -(-(TOP_K * t + (t // st) * n_experts * (SEG_ALIGN - 1)) // EXPERT_TILE) + n_experts

    conv_p, gla_p, conv_u, gla_s = [], [], [], []
    for l in range(depth):
        mod_p = mod[l, :n_seq].reshape(n_seq, 1, 6 * d)
        mod_s = mod[l, n_seq + pad:]
        gw2 = jnp.pad(gate_w2[l], ((0, LANES - rank), (0, 0))).astype(BF16)
        h, la = _norm1(xp, xs, xs_blk, norm1_g[l], mod_p, mod_s, w_in_t, l, n_main, rank, gw2, gate_b[l], dims)
        z = _inproj(h, w_in_t, l, n_main)

        cv_p, cb_p = _conv_prompt(z, conv_w[l], conv_b[l], conv_ln_g[l], conv_ln_b[l], n_seq, seq_len)
        cv_s, u_s = _conv_sample(z, st_t, l, conv_w[l], conv_b[l], conv_ln_g[l], conv_ln_b[l], tp, dec_seq, dec_batch)
        conv_p.append(cb_p)
        conv_u.append(u_s)

        go_p, gs_p = _gla_prompt(z, la, gla_norm_g[l], n_seq, seq_len, heads, dk, dv)

        def seq_major(a):
            return a.reshape(dec_seq, dec_batch, a.shape[-1]).transpose(1, 0, 2)

        zs = z[tp:]
        q3 = seq_major(zs[:, 2 * d_conv:2 * d_conv + qkw])
        k3 = seq_major(zs[:, 2 * d_conv + qkw:2 * d_conv + 2 * qkw])
        v3 = seq_major(zs[:, 2 * d_conv + 2 * qkw:2 * d_conv + 2 * qkw + heads * dv])
        g3 = seq_major(zs[:, 2 * d_conv + 2 * qkw + heads * dv:n_main])
        go_s, gs_s = _gla_sample(q3, k3, v3, g3, seq_major(la[tp:]), state_gla, l, gla_norm_g[l], heads, dk, dv,
                                 prev_states=tuple(gla_s) if l + 1 == depth else ())
        go_s = go_s.transpose(1, 0, 2).reshape(ts, heads * dv).astype(BF16)
        gla_p.append(gs_p)
        gla_s.append(gs_s)

        x = _outproj(cv_p, go_p, cv_s.reshape(ts, d_conv), go_s, xp, xs, xs_blk, w_out, l, mod_p, mod_s, dims)

        wr = jnp.concatenate([router_grp_w[l], router_exp_w[l].reshape(d, n_experts)], axis=1)
        wr = jnp.pad(wr, ((0, 0), (0, LANES - wr.shape[1])))
        wr_hi = wr.astype(BF16)
        wr_lo = (wr - wr_hi.astype(F32)).astype(BF16)
        br = jnp.concatenate([router_grp_b[l], router_exp_b[l].reshape(-1)])
        br = jnp.pad(br, (0, LANES - br.shape[0])).reshape(1, LANES)
        hb, ids, wts = _norm2(x, norm2_g[l], mod_p, mod_s, wr_hi, wr_lo, br, dims, n_grp, n_exp)
        seg = _segments(ids[:, :TOP_K], n_experts, n_tiles_max)
        xe, we, pos = _dispatch_rows(hb, ids, wts, seg, n_experts, n_tiles_max)
        f = exp_w_gate.shape[-1]
        ye = _expert_rows(xe, we, seg, exp_w_gate.reshape(depth * n_experts, d, f),
                          exp_w_up.reshape(depth * n_experts, d, f), exp_w_down.reshape(depth * n_experts, f, d),
                          l * n_experts)
        if l + 1 < depth:
            x = _collect(ye, pos, x, seg, mod_p, mod_s, dims_sort, n_experts)
            xp, xs, xs_blk = x, x, dims["npt"]
        else:
            y_p, y_s = _collect(ye, pos, x, seg, mod_p, mod_s, dims_sort, n_experts, final_g=final_norm_g)

    y_prompt = y_p.reshape(n_seq, seq_len, d)
    y_sample = y_s.reshape(dec_seq, dec_batch, d).transpose(1, 0, 2)
    conv_s = jnp.transpose(_conv_state_sample(st_t, jnp.stack(conv_u)), (0, 2, 1, 3))
    gla_s_all = gla_s[-1] if depth > 1 else gla_s[0][None]
    return (y_prompt, y_sample, jnp.stack(conv_p), jnp.stack(gla_p), conv_s, gla_s_all)
```

```python
import functools

import jax
import jax.numpy as jnp
from jax import lax
from jax.experimental import pallas as pl
from jax.experimental.pallas import tpu as pltpu

F32 = jnp.float32
BF16 = jnp.bfloat16

EPS = 1e-6
GATE_TAU = 16.0
GLA_CHUNK = 32
GLA_BLOCK = 128
TOP_K = 2

ROW_TILE = 512
EXPERT_TILE = 512
SORT_TILE = 512
SEG_ALIGN = 16
CONV_ROWS = 64
CONV_HALO = 32
LANES = 128
SUBLANES = 8
VMEM_LIMIT = 56 * 1024 * 1024


def _params(*sem):
    return pltpu.CompilerParams(dimension_semantics=sem, vmem_limit_bytes=VMEM_LIMIT)


def _bdot(a, b):
    return jnp.dot(a.astype(BF16), b.astype(BF16), preferred_element_type=F32)


def _round_bf16(x):
    return x.astype(BF16).astype(F32)


def _split3(x):
    hi = x.astype(BF16)
    r = x - hi.astype(F32)
    mid = r.astype(BF16)
    lo = (r - mid.astype(F32)).astype(BF16)
    return hi, mid, lo


def _silu(x):
    return x * jax.nn.sigmoid(x)


def _store_by_group(i, n_prompt_tiles, dec_seq, out_ref, fn, vals, p_refs, s_refs):
    @pl.when(i < n_prompt_tiles)
    def _():
        out_ref[...] = fn(vals, [r[...] for r in p_refs]).astype(out_ref.dtype)

    @pl.when(i >= n_prompt_tiles)
    def _():
        mods = [r[...] for r in s_refs]
        nb = mods[0].shape[0]
        for t in range(out_ref.shape[0] // nb):
            rows = slice(t * nb, (t + 1) * nb)
            out_ref[rows, :] = fn([v[rows] for v in vals], mods).astype(out_ref.dtype)


def _ada_kernel(c_ref, w_ref, b_ref, o_ref):
    c = c_ref[...]
    o_ref[...] = _bdot(_silu(c), w_ref[...]) + b_ref[...]


def _ada(c_all, w_ada, b_ada):
    depth, d, n = w_ada.shape
    rows = c_all.shape[0]
    tn = 1024
    return pl.pallas_call(
        _ada_kernel,
        out_shape=jax.ShapeDtypeStruct((depth, rows, n), F32),
        grid=(depth, n // tn),
        in_specs=[
            pl.BlockSpec((rows, d), lambda l, j: (0, 0)),
            pl.BlockSpec((None, d, tn), lambda l, j: (l, 0, j)),
            pl.BlockSpec((None, 1, tn), lambda l, j: (l, 0, j)),
        ],
        out_specs=pl.BlockSpec((None, rows, tn), lambda l, j: (l, 0, j)),
        compiler_params=_params("arbitrary", "arbitrary"),
        name="ada",
    )(c_all, w_ada, b_ada.reshape(depth, 1, n))


def _mod_specs(cols, width, n_prompt_tiles, tiles_per_seq, n_seq, dec_batch, grid_rank=1, row_axis=0, col_fn=None):
    p_specs, s_specs = [], []
    for c in cols:
        def p_map(*idx, c=c):
            b = jnp.minimum(idx[row_axis] // tiles_per_seq, n_seq - 1)
            return (b, 0, c if col_fn is None else col_fn(c, idx))

        def s_map(*idx, c=c):
            return (0, c if col_fn is None else col_fn(c, idx))

        p_specs.append(pl.BlockSpec((None, 1, width), p_map))
        s_specs.append(pl.BlockSpec((dec_batch, width), s_map))
    return p_specs, s_specs


def _rms(x, g):
    return x * lax.rsqrt(jnp.mean(x * x, axis=-1, keepdims=True) + EPS) * g


def _norm1_kernel(xp_ref, xs_ref, g_ref, shp, scp, shs, scs, wgl_ref, gw2_ref, gb_ref, h_ref, la_ref, *, npt, dec_seq, rank):
    i = pl.program_id(0)
    y = _rms(jnp.where(i < npt, xp_ref[...], xs_ref[...]), g_ref[...])
    _store_by_group(i, npt, dec_seq, h_ref, lambda v, m: v[0] * (1.0 + m[1]) + m[0], [y], [shp, scp], [shs, scs])
    row = lax.broadcasted_iota(jnp.int32, wgl_ref.shape, 0)
    w_gl = jnp.where(row < rank, wgl_ref[...], 0.0).astype(BF16)
    gate_lr = lax.dot_general(h_ref[...], w_gl, (((1,), (1,)), ((), ())), preferred_element_type=F32)
    pre = _bdot(gate_lr, gw2_ref[...]) + gb_ref[...]
    la_ref[...] = (jnp.minimum(pre, 0.0) - jnp.log1p(jnp.exp(-jnp.abs(pre)))) * (1.0 / GATE_TAU)


def _norm1(xp, xs, xs_blk, g, mod_p, mod_s, w_in_t, layer, n_main, rank, gw2, gb, dims):
    d = xp.shape[1]
    tm = ROW_TILE
    npt = dims["npt"]
    t = (npt + 1) * tm
    p_specs, s_specs = _mod_specs((0, 1), d, npt, dims["tps"], dims["n_seq"], dims["dec_batch"])
    qk = gw2.shape[1]
    assert n_main % LANES == 0 and rank <= LANES
    return pl.pallas_call(
        functools.partial(_norm1_kernel, npt=npt, dec_seq=dims["dec_seq"], rank=rank),
        out_shape=(jax.ShapeDtypeStruct((t, d), BF16), jax.ShapeDtypeStruct((t, qk), F32)),
        grid=(t // tm,),
        in_specs=[
            pl.BlockSpec((tm, d), lambda i: (jnp.minimum(i, npt - 1), 0)),
            pl.BlockSpec((tm, d), lambda i: (xs_blk, 0)),
            pl.BlockSpec((1, d), lambda i: (0, 0)),
            *p_specs,
            *s_specs,
            pl.BlockSpec((None, LANES, d), lambda i: (layer, n_main // LANES, 0)),
            pl.BlockSpec(gw2.shape, lambda i: (0, 0)),
            pl.BlockSpec((1, qk), lambda i: (0, 0)),
        ],
        out_specs=(pl.BlockSpec((tm, d), lambda i: (i, 0)), pl.BlockSpec((tm, qk), lambda i: (i, 0))),
        compiler_params=_params("arbitrary"),
        name="norm1",
    )(xp, xs, g.reshape(1, d), mod_p, mod_p, mod_s, mod_s, w_in_t, gw2, gb.reshape(1, qk))


def _inproj_kernel(h_ref, w_ref, o_ref, wb_ref):
    @pl.when(pl.program_id(1) == 0)
    def _():
        wb_ref[...] = w_ref[...].astype(BF16)

    o_ref[...] = lax.dot_general(h_ref[...], wb_ref[...], (((1,), (1,)), ((), ())), preferred_element_type=F32)


def _inproj(h, w_in_t, layer, n_cols):
    t, d = h.shape
    tm, tn = 2 * ROW_TILE, 1024
    return pl.pallas_call(
        _inproj_kernel,
        out_shape=jax.ShapeDtypeStruct((t, n_cols), F32),
        grid=(n_cols // tn, pl.cdiv(t, tm)),
        in_specs=[
            pl.BlockSpec((tm, d), lambda j, i: (i, 0)),
            pl.BlockSpec((None, tn, d), lambda j, i: (layer, j, 0)),
        ],
        out_specs=pl.BlockSpec((tm, tn), lambda j, i: (i, j)),
        scratch_shapes=[pltpu.VMEM((tn, d), BF16)],
        compiler_params=_params("arbitrary", "arbitrary"),
        name="inproj",
    )(h, w_in_t)


def _ln_silu(y, g, b):
    mu = jnp.mean(y, axis=-1, keepdims=True)
    yc = y - mu
    var = jnp.mean(yc * yc, axis=-1, keepdims=True)
    return _silu(yc * lax.rsqrt(var + EPS) * g + b)


def _conv_prompt_kernel(a_ref, b_ref, cw_ref, cb_ref, lng_ref, lnb_ref, o_ref, st_ref, full_ref, cwb_ref, y_ref, *, kw):
    j = pl.program_id(1)
    tm, c = a_ref.shape
    halo = CONV_HALO
    phases = full_ref.shape[0]
    assert phases == SUBLANES

    @pl.when(jnp.logical_and(pl.program_id(0) == 0, j == 0))
    def _():
        full_ref[...] = jnp.zeros(full_ref.shape, F32)
        for w in range(kw):
            cwb_ref[w] = jnp.broadcast_to(cw_ref[w:w + 1, :], (SUBLANES, c))

    prev = full_ref[0, tm + halo - SUBLANES:tm + halo, :]
    tail = jnp.where(j == 0, 0.0, prev)

    @pl.when(j == 0)
    def _():
        for p in range(phases):
            full_ref[p, 0:halo, :] = jnp.zeros((halo, c), F32)

    @pl.when(j > 0)
    def _():
        for p in range(phases):
            full_ref[p, 0:halo, :] = full_ref[p, tm:tm + halo, :]

    u = a_ref[...] * jax.nn.sigmoid(b_ref[...])
    full_ref[0, halo:halo + tm, :] = u
    ext = jnp.concatenate([tail, u], axis=0)
    for p in range(1, phases):
        full_ref[p, halo - SUBLANES:halo - SUBLANES + tm, :] = pltpu.roll(ext, tm + SUBLANES - p, 0)[0:tm]
    off = halo - (kw - 1)
    rb = CONV_ROWS

    def body(r, carry):
        r0 = pl.multiple_of(r * rb, rb)
        for lt in range(c // LANES):
            cols = slice(lt * LANES, (lt + 1) * LANES)
            acc = None
            for p in range(phases):
                x = full_ref[p, pl.ds(r0, rb + halo), cols]
                for a in range(halo // phases + 1):
                    w = a * phases + p - off
                    if 0 <= w < kw and a * phases + rb <= rb + halo:
                        term = x[a * phases:a * phases + rb] * jnp.concatenate([cwb_ref[w, :, cols]] * (rb // SUBLANES), axis=0)
                        acc = term if acc is None else acc + term
            y_ref[pl.ds(r0, rb), cols] = acc
        return carry

    lax.fori_loop(0, tm // rb, body, 0)
    y = _ln_silu(y_ref[...] + cb_ref[...], lng_ref[...], lnb_ref[...])
    o_ref[...] = y.astype(o_ref.dtype)

    @pl.when(j == pl.num_programs(1) - 1)
    def _():
        st_ref[...] = full_ref[0, halo + tm - (kw - 1):halo + tm, :]


def _conv_prompt(z, cw, cb, lng, lnb, n_seq, seq_len):
    kw, c = cw.shape
    tm = ROW_TILE
    tps = seq_len // tm
    return pl.pallas_call(
        functools.partial(_conv_prompt_kernel, kw=kw),
        out_shape=(jax.ShapeDtypeStruct((n_seq * seq_len, c), BF16), jax.ShapeDtypeStruct((n_seq, kw - 1, c), F32)),
        grid=(n_seq, tps),
        in_specs=[
            pl.BlockSpec((tm, c), lambda b, j: (b * tps + j, 0)),
            pl.BlockSpec((tm, c), lambda b, j: (b * tps + j, 1)),
            pl.BlockSpec((kw, c), lambda b, j: (0, 0)),
            pl.BlockSpec((1, c), lambda b, j: (0, 0)),
            pl.BlockSpec((1, c), lambda b, j: (0, 0)),
            pl.BlockSpec((1, c), lambda b, j: (0, 0)),
        ],
        out_specs=(
            pl.BlockSpec((tm, c), lambda b, j: (b * tps + j, 0)),
            pl.BlockSpec((None, kw - 1, c), lambda b, j: (b, 0, 0)),
        ),
        scratch_shapes=[pltpu.VMEM((SUBLANES, tm + CONV_HALO, c), F32), pltpu.VMEM((kw, SUBLANES, c), F32),
                        pltpu.VMEM((tm, c), F32)],
        compiler_params=_params("arbitrary", "arbitrary"),
        name="conv_prompt",
    )(z, z, cw, cb.reshape(1, c), lng.reshape(1, c), lnb.reshape(1, c))


def _conv_sample_kernel(*refs, kw, dec_seq):
    a_refs = refs[0:dec_seq]
    b_refs = refs[dec_seq:2 * dec_seq]
    st_ref, cw_ref, cb_ref, lng_ref, lnb_ref, o_ref, u_ref = refs[2 * dec_seq:]
    hist = kw - 1
    u = [a_refs[t][...] * jax.nn.sigmoid(b_refs[t][...]) for t in range(dec_seq)]
    def row(j):
        return st_ref[j] if j < hist else u[j - hist]

    for t in range(dec_seq):
        acc = row(t) * cw_ref[0:1, :]
        for w in range(1, kw):
            acc = acc + row(t + w) * cw_ref[w:w + 1, :]
        y = _ln_silu(acc + cb_ref[...], lng_ref[...], lnb_ref[...])
        o_ref[t] = y.astype(o_ref.dtype)
        u_ref[t] = u[t]


def _conv_sample(z, st_t, layer, cw, cb, lng, lnb, row0, dec_seq, dec_batch):
    kw, c = cw.shape
    bs = 16
    a_specs = [pl.BlockSpec((bs, c), lambda s, t=t: ((row0 + t * dec_batch) // bs + s, 0)) for t in range(dec_seq)]
    b_specs = [pl.BlockSpec((bs, c), lambda s, t=t: ((row0 + t * dec_batch) // bs + s, 1)) for t in range(dec_seq)]
    vec = pl.BlockSpec((1, c), lambda s: (0, 0))
    return pl.pallas_call(
        functools.partial(_conv_sample_kernel, kw=kw, dec_seq=dec_seq),
        out_shape=(jax.ShapeDtypeStruct((dec_seq, dec_batch, c), BF16), jax.ShapeDtypeStruct((dec_seq, dec_batch, c), F32)),
        grid=(dec_batch // bs,),
        in_specs=[*a_specs, *b_specs, pl.BlockSpec((None, kw - 1, bs, c), lambda s: (layer, 0, s, 0)),
                  pl.BlockSpec((kw, c), lambda s: (0, 0)), vec, vec, vec],
        out_specs=(pl.BlockSpec((dec_seq, bs, c), lambda s: (0, s, 0)), pl.BlockSpec((dec_seq, bs, c), lambda s: (0, s, 0))),
        compiler_params=_params("arbitrary"),
        name="conv_sample",
    )(*([z] * (2 * dec_seq)), st_t, cw, cb.reshape(1, c), lng.reshape(1, c), lnb.reshape(1, c))


def _conv_state_kernel(st_ref, u_ref, o_ref):
    hist, n_new = st_ref.shape[0], u_ref.shape[0]
    for j in range(hist - n_new):
        o_ref[j] = st_ref[j + n_new]
    for t in range(n_new):
        o_ref[hist - n_new + t] = u_ref[t]


def _conv_state_sample(st_t, u_all):
    depth, hist, nb, c = st_t.shape
    n_new = u_all.shape[1]
    assert n_new <= hist
    bs = 16
    return pl.pallas_call(
        _conv_state_kernel,
        out_shape=jax.ShapeDtypeStruct(st_t.shape, st_t.dtype),
        grid=(depth, nb // bs),
        in_specs=[pl.BlockSpec((None, hist, bs, c), lambda l, s: (l, 0, s, 0)),
                  pl.BlockSpec((None, n_new, bs, c), lambda l, s: (l, 0, s, 0))],
        out_specs=pl.BlockSpec((None, hist, bs, c), lambda l, s: (l, 0, s, 0)),
        compiler_params=_params("arbitrary", "arbitrary"),
        name="conv_state",
    )(st_t, u_all)


def _gla_prompt_kernel(qk_ref, v_ref, g_ref, la_ref, gn_ref, o_ref, sfin_ref, st_ref, sn_ref, *, heads, dk, dv):
    j = pl.program_id(1)
    tm = qk_ref.shape[0]
    ck = GLA_CHUNK
    nch = tm // ck
    qkw = heads * dk

    @pl.when(j == 0)
    def _():
        st_ref[...] = jnp.zeros(st_ref.shape, F32)

    la = la_ref[...]
    sb = GLA_BLOCK
    blocks = [slice(r, r + sb) for r in range(0, tm, sb)]
    row = lax.broadcasted_iota(jnp.int32, (sb, sb), 0)
    col = lax.broadcasted_iota(jnp.int32, (sb, sb), 1)
    same = (row // ck) == (col // ck)
    causal = same & (col <= row)
    tri_incl = causal.astype(BF16)
    tri_after = (same & (col > row)).astype(BF16)
    sel = (lax.broadcasted_iota(jnp.int32, (nch, tm), 1) // ck == lax.broadcasted_iota(jnp.int32, (nch, tm), 0)).astype(BF16)
    parts = _split3(la)

    def blockwise(tri):
        return jnp.concatenate([sum(jnp.dot(tri, p[rs], preferred_element_type=F32) for p in parts) for rs in blocks], axis=0)

    b = blockwise(tri_incl)
    rest = blockwise(tri_after)
    tot = sum(jnp.dot(sel, p, preferred_element_type=F32) for p in parts)
    qk = qk_ref[...]
    q = qk[:, :qkw] * (dk ** -0.5)
    k = qk[:, qkw:]
    q_dec = (q * jnp.exp(b)).astype(BF16)
    k_inv = (k * jnp.exp(-b)).astype(BF16)
    k_end = _round_bf16(k * jnp.exp(rest))
    decay = jnp.exp(tot)
    v_all = v_ref[...]
    g_all = g_ref[...]
    for h in range(heads):
        ks = slice(h * dk, (h + 1) * dk)
        vs = slice(h * dv, (h + 1) * dv)
        qd, ki, ke = q_dec[:, ks], k_inv[:, ks], k_end[:, ks]
        vh = v_all[:, vs]
        vb = vh.astype(BF16)
        vr = _round_bf16(vh)
        intra = []
        for rs in blocks:
            att = lax.dot_general(qd[rs], ki[rs], (((1,), (1,)), ((), ())), preferred_element_type=F32)
            att = jnp.where(causal, att, 0.0).astype(BF16)
            intra.append(jnp.dot(att, vb[rs], preferred_element_type=F32))
        o = jnp.concatenate(intra, axis=0)
        s = st_ref[h]
        for n in range(nch):
            rs = slice(n * ck, (n + 1) * ck)
            sn_ref[n] = s.astype(BF16)
            upd = lax.dot_general(vr[rs], ke[rs], (((0,), (0,)), ((), ())), preferred_element_type=F32)
            s = s * decay[n:n + 1, ks] + upd
        st_ref[h] = s
        inter = [lax.dot_general(qd[n * ck:(n + 1) * ck], sn_ref[n], (((1,), (1,)), ((), ())), preferred_element_type=F32)
                 for n in range(nch)]
        o = o + jnp.concatenate(inter, axis=0)
        o = _rms(o, gn_ref[:, vs]) * _silu(g_all[:, vs])
        o_ref[:, vs] = o.astype(o_ref.dtype)

    @pl.when(j == pl.num_programs(1) - 1)
    def _():
        for h in range(heads):
            sfin_ref[h] = st_ref[h].T


def _gla_prompt(z, la, gn, n_seq, seq_len, heads, dk, dv):
    tm = ROW_TILE
    tps = seq_len // tm
    w = heads * dv
    qkw = heads * dk
    assert 2 * qkw == w
    return pl.pallas_call(
        functools.partial(_gla_prompt_kernel, heads=heads, dk=dk, dv=dv),
        out_shape=(jax.ShapeDtypeStruct((n_seq * seq_len, w), BF16), jax.ShapeDtypeStruct((n_seq, heads, dk, dv), F32)),
        grid=(n_seq, tps),
        in_specs=[
            pl.BlockSpec((tm, w), lambda b, j: (b * tps + j, 2)),
            pl.BlockSpec((tm, w), lambda b, j: (b * tps + j, 3)),
            pl.BlockSpec((tm, w), lambda b, j: (b * tps + j, 4)),
            pl.BlockSpec((tm, qkw), lambda b, j: (b * tps + j, 0)),
            pl.BlockSpec((1, w), lambda b, j: (0, 0)),
        ],
        out_specs=(
            pl.BlockSpec((tm, w), lambda b, j: (b * tps + j, 0)),
            pl.BlockSpec((None, heads, dk, dv), lambda b, j: (b, 0, 0, 0)),
        ),
        scratch_shapes=[pltpu.VMEM((heads, dv, dk), F32), pltpu.VMEM((tm // GLA_CHUNK, dv, dk), BF16)],
        compiler_params=_params("arbitrary", "arbitrary"),
        name="gla_prompt",
    )(z, z, z, la, gn.reshape(1, w))


def _gla_sample_kernel(q_ref, k_ref, v_ref, g_ref, la_ref, s_ref, gn_ref, *rest, heads, dk, dv, n_prev):
    bs, ln, _ = q_ref.shape
    if n_prev:
        o_ref, all_ref = rest[n_prev:]
        for p in range(n_prev):
            all_ref[p] = rest[p][...]
        ns_ref = all_ref.at[n_prev]
    else:
        o_ref, ns_ref = rest
    tril = lax.broadcasted_iota(jnp.int32, (ln, ln), 1) <= lax.broadcasted_iota(jnp.int32, (ln, ln), 0)
    for s in range(bs):
        q_s, k_s, v_s, g_s, la_s = q_ref[s], k_ref[s], v_ref[s], g_ref[s], la_ref[s]
        for h in range(heads):
            ks = slice(h * dk, (h + 1) * dk)
            vs = slice(h * dv, (h + 1) * dv)
            la = la_s[:, ks]
            rows = [la[0:1]]
            for t in range(1, ln):
                rows.append(rows[-1] + la[t:t + 1])
            b = jnp.concatenate(rows, axis=0)
            b_last = rows[-1]
            q_dec = _round_bf16(q_s[:, ks] * (dk ** -0.5) * jnp.exp(b))
            k_inv = _round_bf16(k_s[:, ks] * jnp.exp(-b))
            k_end = _round_bf16(k_s[:, ks] * jnp.exp(b_last - b))
            vr = _round_bf16(v_s[:, vs])
            s0 = s_ref[s, h]
            att = lax.dot_general(q_dec, k_inv, (((1,), (1,)), ((), ())), preferred_element_type=F32)
            att = _round_bf16(jnp.where(tril, att, 0.0))
            o = jnp.dot(att, vr, preferred_element_type=F32) + jnp.dot(q_dec, _round_bf16(s0), preferred_element_type=F32)
            upd = lax.dot_general(k_end, vr, (((0,), (0,)), ((), ())), preferred_element_type=F32)
            d_col = jnp.broadcast_to(jnp.exp(b_last), (dk, dk)).T
            ns_ref[s, h] = s0 * jnp.concatenate([d_col] * (dv // dk), axis=1) + upd
            o = _rms(o, gn_ref[:, vs]) * _silu(g_s[:, vs])
            o_ref[s, :, vs] = o.astype(o_ref.dtype)


def _gla_sample(q3, k3, v3, g3, la3, state, layer, gn, heads, dk, dv, prev_states=()):
    nb, ln, w = v3.shape
    qkw = heads * dk
    bs = 8
    n_prev = len(prev_states)
    seq_spec = pl.BlockSpec((bs, ln, qkw), lambda s: (s, 0, 0))
    wide_spec = pl.BlockSpec((bs, ln, w), lambda s: (s, 0, 0))
    in_specs = [seq_spec, seq_spec, wide_spec, wide_spec, seq_spec,
                pl.BlockSpec((None, bs, heads, dk, dv), lambda s: (layer, s, 0, 0, 0)),
                pl.BlockSpec((1, w), lambda s: (0, 0)),
                *[pl.BlockSpec((bs, heads, dk, dv), lambda s: (s, 0, 0, 0))] * n_prev]
    if n_prev:
        state_shape = jax.ShapeDtypeStruct((n_prev + 1, nb, heads, dk, dv), F32)
        state_spec = pl.BlockSpec((n_prev + 1, bs, heads, dk, dv), lambda s: (0, s, 0, 0, 0))
    else:
        state_shape = jax.ShapeDtypeStruct((nb, heads, dk, dv), F32)
        state_spec = pl.BlockSpec((bs, heads, dk, dv), lambda s: (s, 0, 0, 0))
    return pl.pallas_call(
        functools.partial(_gla_sample_kernel, heads=heads, dk=dk, dv=dv, n_prev=n_prev),
        out_shape=(jax.ShapeDtypeStruct((nb, ln, w), F32), state_shape),
        grid=(nb // bs,),
        in_specs=in_specs,
        out_specs=(wide_spec, state_spec),
        compiler_params=_params("arbitrary"),
        name="gla_sample_last" if n_prev else "gla_sample",
    )(q3, k3, v3, g3, la3, state, gn.reshape(1, w), *prev_states)


def _outproj_kernel(cp_ref, op_ref, cs_ref, os_ref, xp_ref, xs_ref, w_ref, gp, gs, y_ref, wb_ref, *, npt, dec_seq):
    i = pl.program_id(1)

    @pl.when(i == 0)
    def _():
        wb_ref[...] = w_ref[...].astype(BF16)

    half = cp_ref.shape[1]

    def mixed(c_ref, o_ref):
        return (jnp.dot(c_ref[...], wb_ref[0:half, :], preferred_element_type=F32)
                + jnp.dot(o_ref[...], wb_ref[half:, :], preferred_element_type=F32))

    @pl.when(i < npt)
    def _():
        y_ref[...] = xp_ref[...] + gp[...] * mixed(cp_ref, op_ref)

    @pl.when(i >= npt)
    def _():
        mix = mixed(cs_ref, os_ref)
        nb = gs.shape[0]
        for t in range(dec_seq):
            rows = slice(t * nb, (t + 1) * nb)
            y_ref[rows, :] = xs_ref[rows, :] + gs[...] * mix[rows]


def _outproj(conv_p, gla_p, conv_s, gla_s, xp, xs, xs_blk, w_out, layer, mod_p, mod_s, dims):
    d = xp.shape[1]
    half = conv_p.shape[1]
    tm, tn = ROW_TILE, 1024
    nj = d // tn
    npt = dims["npt"]
    t = (npt + 1) * tm
    p_specs, s_specs = _mod_specs((2,), tn, dims["npt"], dims["tps"], dims["n_seq"], dims["dec_batch"],
                                  row_axis=1, col_fn=lambda c, idx: c * nj + idx[0])
    return pl.pallas_call(
        functools.partial(_outproj_kernel, npt=dims["npt"], dec_seq=dims["dec_seq"]),
        out_shape=jax.ShapeDtypeStruct((t, d), F32),
        grid=(nj, t // tm),
        in_specs=[
            pl.BlockSpec((tm, half), lambda j, i: (jnp.minimum(i, npt - 1), 0)),
            pl.BlockSpec((tm, half), lambda j, i: (jnp.minimum(i, npt - 1), 0)),
            pl.BlockSpec((tm, half), lambda j, i: (0, 0)),
            pl.BlockSpec((tm, half), lambda j, i: (0, 0)),
            pl.BlockSpec((tm, tn), lambda j, i: (jnp.minimum(i, npt - 1), j)),
            pl.BlockSpec((tm, tn), lambda j, i: (xs_blk, j)),
            pl.BlockSpec((None, d, tn), lambda j, i: (layer, 0, j)),
            *p_specs,
            *s_specs,
        ],
        out_specs=pl.BlockSpec((tm, tn), lambda j, i: (i, j)),
        scratch_shapes=[pltpu.VMEM((d, tn), BF16)],
        compiler_params=_params("arbitrary", "arbitrary"),
        name="outproj",
    )(conv_p, gla_p, conv_s, gla_s, xp, xs, w_out, mod_p, mod_s)


def _norm2_kernel(x_ref, g_ref, shp, scp, shs, scs, wr_hi, wr_lo, br_ref, hb_ref, ids_ref, wts_ref, h_ref, *, npt, dec_seq, n_grp, n_exp):
    i = pl.program_id(0)
    y = _rms(x_ref[...], g_ref[...])
    _store_by_group(i, npt, dec_seq, h_ref, lambda v, m: v[0] * (1.0 + m[1]) + m[0], [y], [shp, scp], [shs, scs])
    h = h_ref[...]
    h_hi = h.astype(BF16)
    hb_ref[...] = h_hi
    h_lo = (h - h_hi.astype(F32)).astype(BF16)
    logits = (jnp.dot(h_hi, wr_hi[...], preferred_element_type=F32) + jnp.dot(h_lo, wr_hi[...], preferred_element_type=F32)
              + jnp.dot(h_hi, wr_lo[...], preferred_element_type=F32)) + br_ref[...]
    lane = lax.broadcasted_iota(jnp.int32, logits.shape, 1).astype(F32)
    big = jnp.float32(LANES)
    neg = jnp.float32(-jnp.inf)
    gl = jnp.where(lane < n_grp, logits, neg)
    gmax = jnp.max(gl, axis=-1, keepdims=True)
    gidx = jnp.min(jnp.where(gl == gmax, lane, big), axis=-1, keepdims=True)
    g_w = 1.0 / jnp.sum(jnp.exp(gl - gmax), axis=-1, keepdims=True)
    lo = n_grp + gidx * n_exp
    in_grp = (lane >= lo) & (lane < lo + n_exp)
    sl = jnp.where(in_grp, logits, neg)
    p = jnp.exp(sl - jnp.max(sl, axis=-1, keepdims=True))
    p = p / jnp.sum(p, axis=-1, keepdims=True)
    p = jnp.where(in_grp, p, -1.0)
    p1 = jnp.max(p, axis=-1, keepdims=True)
    i1 = jnp.min(jnp.where(p == p1, lane, big), axis=-1, keepdims=True)
    p_rest = jnp.where(lane == i1, -1.0, p)
    p2 = jnp.max(p_rest, axis=-1, keepdims=True)
    i2 = jnp.min(jnp.where(p_rest == p2, lane, big), axis=-1, keepdims=True)
    denom = p1 + p2
    ids_ref[...] = jnp.where(lane == 0, i1 - n_grp, jnp.where(lane == 1, i2 - n_grp, 0.0)).astype(jnp.int32)
    wts_ref[...] = jnp.where(lane == 0, g_w * (p1 / denom), jnp.where(lane == 1, g_w * (p2 / denom), 0.0))


def _norm2(x, g, mod_p, mod_s, wr_hi, wr_lo, br, dims, n_grp, n_exp):
    t, d = x.shape
    tm = ROW_TILE
    p_specs, s_specs = _mod_specs((3, 4), d, dims["npt"], dims["tps"], dims["n_seq"], dims["dec_batch"])
    return pl.pallas_call(
        functools.partial(_norm2_kernel, npt=dims["npt"], dec_seq=dims["dec_seq"], n_grp=n_grp, n_exp=n_exp),
        out_shape=(jax.ShapeDtypeStruct((t, d), BF16), jax.ShapeDtypeStruct((t, LANES), jnp.int32),
                   jax.ShapeDtypeStruct((t, LANES), F32)),
        grid=(t // tm,),
        in_specs=[
            pl.BlockSpec((tm, d), lambda i: (i, 0)),
            pl.BlockSpec((1, d), lambda i: (0, 0)),
            *p_specs,
            *s_specs,
            pl.BlockSpec((d, LANES), lambda i: (0, 0)),
            pl.BlockSpec((d, LANES), lambda i: (0, 0)),
            pl.BlockSpec((1, LANES), lambda i: (0, 0)),
        ],
        out_specs=(pl.BlockSpec((tm, d), lambda i: (i, 0)), pl.BlockSpec((tm, LANES), lambda i: (i, 0)),
                   pl.BlockSpec((tm, LANES), lambda i: (i, 0))),
        scratch_shapes=[pltpu.VMEM((tm, d), F32)],
        compiler_params=_params("arbitrary"),
        name="norm2_router",
    )(x, g.reshape(1, d), mod_p, mod_p, mod_s, mod_s, wr_hi, wr_lo, br)


def _segments(ids, n_experts, n_tiles_max):
    t = ids.shape[0]
    ts, te, al = SORT_TILE, EXPERT_TILE, SEG_ALIGN
    nt = t // ts
    e_iota = jnp.arange(n_experts, dtype=jnp.int32)
    cnt = jnp.sum((ids.reshape(nt, ts * TOP_K, 1) == e_iota).astype(jnp.int32), axis=1)
    size = (cnt + al - 1) // al * al
    src = jnp.cumsum(size, axis=1) - size
    tot = jnp.sum(size, axis=0)
    tot_pad = (tot + te - 1) // te * te
    ends = jnp.cumsum(tot_pad)
    exp_off = ends - tot_pad
    dst = exp_off[None, :] + jnp.cumsum(size, axis=0) - size
    n_used = (ends[-1] // te).astype(jnp.int32)
    tile_start = jnp.arange(n_tiles_max, dtype=jnp.int32) * te
    tile_expert = jnp.minimum(jnp.sum(ends[None, :] <= tile_start[:, None], axis=1), n_experts - 1).astype(jnp.int32)
    last = tile_expert[jnp.maximum(n_used - 1, 0)]
    used = jnp.arange(n_tiles_max) < n_used
    tile_expert = jnp.where(used, tile_expert, last)
    prev_e = jnp.concatenate([jnp.full((1,), -1, jnp.int32), tile_expert[:-1]])
    tile_first = (used & (tile_expert != prev_e)).astype(jnp.int32)
    tile_slot = ((jnp.cumsum(tile_first) - 1) % 2).astype(jnp.int32)
    next_start = jnp.sum(used[None, :] & (tile_expert[None, :] <= tile_expert[:, None]), axis=1)
    tile_next = jnp.where(next_start < n_used, tile_expert[jnp.minimum(next_start, n_tiles_max - 1)], -1).astype(jnp.int32)
    return dict(tile_first=tile_first, tile_slot=tile_slot, tile_next=tile_next,
                src=src.reshape(-1).astype(jnp.int32), dst=dst.reshape(-1).astype(jnp.int32),
                size=size.reshape(-1).astype(jnp.int32), tile_tot=jnp.sum(size, axis=1).astype(jnp.int32),
                fill_start=(exp_off + tot).astype(jnp.int32), fill_size=(tot_pad - tot).astype(jnp.int32),
                tile_expert=tile_expert, n_used=n_used.reshape(1))


def _lane_pick(x, lane, k):
    return jnp.sum(jnp.where(lane == k, x, 0.0), axis=-1, keepdims=True)


def _sorted_positions(ids, lane):
    ts = ids.shape[0]
    onehot = [(lane == _lane_pick(ids, lane, k)).astype(F32) for k in range(TOP_K)]
    row = lax.broadcasted_iota(jnp.int32, (ts, ts), 0)
    col = lax.broadcasted_iota(jnp.int32, (ts, ts), 1)
    before = (col < row).astype(BF16)
    earlier = [jnp.dot(before, o.astype(BF16), preferred_element_type=F32) for o in onehot]
    cnt = [jnp.sum(o, axis=0, keepdims=True) for o in onehot]
    size = jnp.ceil((cnt[0] + cnt[1]) * (1.0 / SEG_ALIGN)) * SEG_ALIGN
    er = lax.broadcasted_iota(jnp.int32, (LANES, LANES), 0)
    ec = lax.broadcasted_iota(jnp.int32, (LANES, LANES), 1)
    start = jnp.dot(jnp.broadcast_to(size, (SUBLANES, LANES)).astype(BF16), (er < ec).astype(BF16),
                    preferred_element_type=F32)[0:1]
    base = [start + earlier[0], start + cnt[0] + earlier[1]]
    return [jnp.sum(onehot[k] * base[k], axis=-1, keepdims=True) for k in range(TOP_K)]


def _as_row(col_vals, lane):
    hi = jnp.floor(col_vals * (1.0 / 64.0))
    lo = col_vals - hi * 64.0
    ones = jnp.ones((SUBLANES, LANES), BF16)
    nt = (((1,), (1,)), ((), ()))
    hi_row = lax.dot_general(ones, jnp.where(lane == 0, hi, 0.0).astype(BF16), nt, preferred_element_type=F32)
    lo_row = lax.dot_general(ones, jnp.where(lane == 0, lo, 0.0).astype(BF16), nt, preferred_element_type=F32)
    return (hi_row * 64.0 + lo_row)[0:1]


def _dispatch_kernel(src_ref, dst_ref, size_ref, tot_ref, fst_ref, fsz_ref, nu_ref, h_ref, ids_ref, wts_ref, xe_hbm,
                     we_hbm, pos_ref, sbuf, wbuf, zx, zw, semx, semw, semz, semt, *, n_experts):
    i = pl.program_id(0)
    n = pl.num_programs(0)
    ts = h_ref.shape[0]
    rows = sbuf.shape[1]
    te = zx.shape[0]
    slot = lax.rem(i, 2)

    def tail_copies(tl):
        r0 = pl.multiple_of(tl * te, te)
        return (pltpu.make_async_copy(zx, xe_hbm.at[pl.ds(r0, te), :], semt.at[0]),
                pltpu.make_async_copy(zw, we_hbm.at[pl.ds(r0, te), :], semt.at[1]))

    n_tail_tiles = xe_hbm.shape[0] // te

    def for_tail(action, first):
        def body(tl, carry):
            for c in tail_copies(tl):
                action(c)
            return carry

        lax.fori_loop(first, n_tail_tiles, body, 0)

    def seg_copies(tile, sl, e):
        k = tile * n_experts + e
        sz = pl.multiple_of(size_ref[k], SEG_ALIGN)
        s0 = pl.multiple_of(src_ref[k], SEG_ALIGN)
        d0 = pl.multiple_of(dst_ref[k], SEG_ALIGN)
        return sz, (pltpu.make_async_copy(sbuf.at[sl, pl.ds(s0, sz), :], xe_hbm.at[pl.ds(d0, sz), :], semx.at[sl]),
                    pltpu.make_async_copy(wbuf.at[sl, pl.ds(s0, sz), :], we_hbm.at[pl.ds(d0, sz), :], semw.at[sl]))

    def wait_tile(tile, sl):
        tot = pl.multiple_of(tot_ref[tile], SEG_ALIGN)
        pltpu.make_async_copy(sbuf.at[sl, pl.ds(0, tot), :], xe_hbm.at[pl.ds(0, tot), :], semx.at[sl]).wait()
        pltpu.make_async_copy(wbuf.at[sl, pl.ds(0, tot), :], we_hbm.at[pl.ds(0, tot), :], semw.at[sl]).wait()

    @pl.when(i == 0)
    def _():
        zx[...] = jnp.zeros(zx.shape, zx.dtype)
        zw[...] = jnp.zeros(zw.shape, zw.dtype)

        def fill(e, carry):
            sz = pl.multiple_of(fsz_ref[e], SEG_ALIGN)
            d0 = pl.multiple_of(fst_ref[e], SEG_ALIGN)

            @pl.when(sz > 0)
            def _():
                cx = pltpu.make_async_copy(zx.at[pl.ds(0, sz), :], xe_hbm.at[pl.ds(d0, sz), :], semz.at[0])
                cw = pltpu.make_async_copy(zw.at[pl.ds(0, sz), :], we_hbm.at[pl.ds(d0, sz), :], semz.at[1])
                cx.start()
                cw.start()
                cx.wait()
                cw.wait()

            return carry

        lax.fori_loop(0, n_experts, fill, 0)

    @pl.when(i >= 2)
    def _():
        wait_tile(i - 2, slot)

    lane = lax.broadcasted_iota(jnp.int32, (ts, LANES), 1).astype(F32)
    pos = _sorted_positions(ids_ref[...].astype(F32), lane)
    pos_ref[...] = jnp.where(lane == 0, pos[0], jnp.where(lane == 1, pos[1], 0.0))
    r_iota = lax.broadcasted_iota(jnp.int32, (rows, ts), 0).astype(F32)
    assert rows <= 4096
    sel = [(r_iota == _as_row(p, lane)).astype(BF16) for p in pos]
    sbuf[slot] = jnp.dot(sel[0] + sel[1], h_ref[...], preferred_element_type=F32).astype(BF16)
    wts = wts_ref[...]
    wsorted = jnp.zeros((rows, LANES), F32)
    for k in range(TOP_K):
        pieces = _split3(_lane_pick(wts, lane, k))
        wk = jnp.where(lane == 0, pieces[0].astype(F32), jnp.where(lane == 1, pieces[1].astype(F32),
                       jnp.where(lane == 2, pieces[2].astype(F32), 0.0)))
        wsorted = wsorted + jnp.dot(sel[k], wk.astype(BF16), preferred_element_type=F32)
    wbuf[slot] = wsorted

    def issue(e, carry):
        sz, copies = seg_copies(i, slot, e)

        @pl.when(sz > 0)
        def _():
            for c in copies:
                c.start()

        return carry

    lax.fori_loop(0, n_experts, issue, 0)

    tail_tile = nu_ref[0] + i

    @pl.when(jnp.logical_and(i < n - 1, tail_tile < n_tail_tiles))
    def _():
        for c in tail_copies(tail_tile):
            c.start()

    @pl.when(i == n - 1)
    def _():
        for_tail(lambda c: c.start(), tail_tile)

        @pl.when(i >= 1)
        def _():
            wait_tile(i - 1, 1 - slot)

        wait_tile(i, slot)
        for_tail(lambda c: c.wait(), nu_ref[0])


def _dispatch_rows(hb, ids, wts, seg, n_experts, n_tiles_max):
    t, d = hb.shape
    ts, te = SORT_TILE, EXPERT_TILE
    rows = TOP_K * ts + n_experts * SEG_ALIGN
    n_slots = n_tiles_max * te

    def tile_map(i, *_):
        return (i, 0)

    return pl.pallas_call(
        functools.partial(_dispatch_kernel, n_experts=n_experts),
        out_shape=(jax.ShapeDtypeStruct((n_slots, d), BF16), jax.ShapeDtypeStruct((n_slots, LANES), F32),
                   jax.ShapeDtypeStruct((t, LANES), F32)),
        grid_spec=pltpu.PrefetchScalarGridSpec(
            num_scalar_prefetch=7,
            grid=(t // ts,),
            in_specs=[pl.BlockSpec((ts, d), tile_map), pl.BlockSpec((ts, LANES), tile_map),
                      pl.BlockSpec((ts, LANES), tile_map)],
            out_specs=(pl.BlockSpec(memory_space=pl.ANY), pl.BlockSpec(memory_space=pl.ANY),
                       pl.BlockSpec((ts, LANES), tile_map)),
            scratch_shapes=[
                pltpu.VMEM((2, rows, d), BF16),
                pltpu.VMEM((2, rows, LANES), F32),
                pltpu.VMEM((te, d), BF16),
                pltpu.VMEM((te, LANES), F32),
                pltpu.SemaphoreType.DMA((2,)),
                pltpu.SemaphoreType.DMA((2,)),
                pltpu.SemaphoreType.DMA((2,)),
                pltpu.SemaphoreType.DMA((2,)),
            ],
        ),
        compiler_params=_params("arbitrary"),
        name="dispatch",
    )(seg["src"], seg["dst"], seg["size"], seg["tile_tot"], seg["fill_start"], seg["fill_size"], seg["n_used"],
      hb, ids, wts)


def _expert_rows_kernel(te_ref, nu_ref, first_ref, slot_ref, next_ref, x_ref, w_ref, wg_hbm, wu_hbm, wd_hbm, o_ref,
                        wgf, wuf, wdf, wsem, wgb, wub, wdb, *, layer_base):
    i = pl.program_id(0)
    n_used = nu_ref[0]

    def weight_copies(e, sl):
        return (pltpu.make_async_copy(wg_hbm.at[layer_base + e], wgf.at[sl], wsem.at[sl]),
                pltpu.make_async_copy(wu_hbm.at[layer_base + e], wuf.at[sl], wsem.at[sl]),
                pltpu.make_async_copy(wd_hbm.at[layer_base + e], wdf.at[sl], wsem.at[sl]))

    @pl.when(i == 0)
    def _():
        for c in weight_copies(te_ref[0], 0):
            c.start()

    @pl.when(i < n_used)
    def _():
        @pl.when(first_ref[i] == 1)
        def _():
            sl = slot_ref[i]
            for c in weight_copies(te_ref[i], sl):
                c.wait()
            wgb[...] = wgf[sl].astype(BF16)
            wub[...] = wuf[sl].astype(BF16)
            wdb[...] = wdf[sl].astype(BF16)
            nxt = next_ref[i]

            @pl.when(nxt >= 0)
            def _():
                for c in weight_copies(nxt, 1 - sl):
                    c.start()

        x = x_ref[...]
        a = jnp.dot(x, wgb[...], preferred_element_type=F32)
        u = jnp.dot(x, wub[...], preferred_element_type=F32)
        w = jnp.sum(w_ref[...], axis=-1, keepdims=True)
        hid = _silu(a) * u * w
        o_ref[...] = jnp.dot(hid.astype(BF16), wdb[...], preferred_element_type=F32).astype(o_ref.dtype)


def _expert_rows(xe, we, seg, wg, wu, wd, layer_base):
    n_slots, d = xe.shape
    f = wg.shape[-1]
    te = EXPERT_TILE

    def row_map(i, te_ref, nu_ref, *_):
        return (jnp.minimum(i, nu_ref[0] - 1), 0)

    return pl.pallas_call(
        functools.partial(_expert_rows_kernel, layer_base=layer_base),
        out_shape=jax.ShapeDtypeStruct((n_slots, d), BF16),
        grid_spec=pltpu.PrefetchScalarGridSpec(
            num_scalar_prefetch=5,
            grid=(n_slots // te,),
            in_specs=[
                pl.BlockSpec((te, d), row_map),
                pl.BlockSpec((te, LANES), row_map),
                pl.BlockSpec(memory_space=pl.ANY),
                pl.BlockSpec(memory_space=pl.ANY),
                pl.BlockSpec(memory_space=pl.ANY),
            ],
            out_specs=pl.BlockSpec((te, d), row_map),
            scratch_shapes=[
                pltpu.VMEM((2, d, f), F32), pltpu.VMEM((2, d, f), F32), pltpu.VMEM((2, f, d), F32),
                pltpu.SemaphoreType.DMA((2,)),
                pltpu.VMEM((d, f), BF16), pltpu.VMEM((d, f), BF16), pltpu.VMEM((f, d), BF16),
            ],
        ),
        input_output_aliases={5: 0},
        compiler_params=_params("arbitrary"),
        name="experts",
    )(seg["tile_expert"], seg["n_used"], seg["tile_first"], seg["tile_slot"], seg["tile_next"], xe, we, wg, wu, wd)


def _collect_kernel(src_ref, dst_ref, size_ref, tot_ref, y_hbm, x_ref, pos_ref, gp, gs, *rest, npt, n_experts, final):
    if final:
        fg_ref, op_ref, os_ref, buf, sem, o_ref = rest
    else:
        o_ref, buf, sem = rest
    i = pl.program_id(0)
    n = pl.num_programs(0)
    ts = x_ref.shape[0]
    rows = buf.shape[1]
    slot = lax.rem(i, 2)

    def start(tile, sl):
        def body(e, carry):
            k = tile * n_experts + e
            sz = pl.multiple_of(size_ref[k], SEG_ALIGN)
            s0 = pl.multiple_of(src_ref[k], SEG_ALIGN)
            d0 = pl.multiple_of(dst_ref[k], SEG_ALIGN)

            @pl.when(sz > 0)
            def _():
                pltpu.make_async_copy(y_hbm.at[pl.ds(d0, sz), :], buf.at[sl, pl.ds(s0, sz), :], sem.at[sl]).start()

            return carry

        lax.fori_loop(0, n_experts, body, 0)

    @pl.when(i == 0)
    def _():
        buf[...] = jnp.zeros(buf.shape, buf.dtype)
        start(0, 0)

    @pl.when(i + 1 < n)
    def _():
        start(i + 1, 1 - slot)

    tot = pl.multiple_of(tot_ref[i], SEG_ALIGN)
    pltpu.make_async_copy(y_hbm.at[pl.ds(0, tot), :], buf.at[slot, pl.ds(0, tot), :], sem.at[slot]).wait()
    lane = lax.broadcasted_iota(jnp.int32, (ts, LANES), 1).astype(F32)
    pos = pos_ref[...]
    r_iota = lax.broadcasted_iota(jnp.int32, (ts, rows), 1).astype(F32)
    pick = ((r_iota == _lane_pick(pos, lane, 0)) | (r_iota == _lane_pick(pos, lane, 1))).astype(BF16)
    ff = jnp.dot(pick, buf[slot], preferred_element_type=F32)
    _store_by_group(i, npt, None, o_ref, lambda v, m: v[0] + m[0] * v[1], [x_ref[...], ff], [gp], [gs])
    if final:
        y = _rms(o_ref[...], fg_ref[...])

        @pl.when(i < npt)
        def _():
            op_ref[...] = y

        @pl.when(i >= npt)
        def _():
            os_ref[...] = y


def _collect(ye, pos, x, seg, mod_p, mod_s, dims, n_experts, final_g=None):
    t, d = x.shape
    ts = SORT_TILE
    npt = dims["npt"]
    rows = TOP_K * ts + n_experts * SEG_ALIGN
    p_specs, s_specs = _mod_specs((5,), d, npt, dims["tps"], dims["n_seq"], dims["dec_batch"])
    final = final_g is not None

    def strip(spec):
        return pl.BlockSpec(spec.block_shape, lambda i, *_, m=spec.index_map: m(i))

    def tile_map(i, *_):
        return (i, 0)

    in_specs = [
        pl.BlockSpec(memory_space=pl.ANY),
        pl.BlockSpec((ts, d), tile_map),
        pl.BlockSpec((ts, LANES), tile_map),
        *[strip(s) for s in p_specs],
        *[strip(s) for s in s_specs],
    ]
    scratch = [pltpu.VMEM((2, rows, d), BF16), pltpu.SemaphoreType.DMA((2,))]
    args = [seg["src"], seg["dst"], seg["size"], seg["tile_tot"], ye, x, pos, mod_p, mod_s]
    if final:
        in_specs.append(pl.BlockSpec((1, d), lambda i, *_: (0, 0)))
        args.append(final_g.reshape(1, d))
        out_shape = (jax.ShapeDtypeStruct((npt * ts, d), F32), jax.ShapeDtypeStruct((t - npt * ts, d), F32))
        out_specs = (pl.BlockSpec((ts, d), lambda i, *_: (jnp.minimum(i, npt - 1), 0)),
                     pl.BlockSpec((ts, d), lambda i, *_: (jnp.maximum(i - npt, 0), 0)))
        scratch.append(pltpu.VMEM((ts, d), F32))
    else:
        out_shape = jax.ShapeDtypeStruct((t, d), F32)
        out_specs = pl.BlockSpec((ts, d), tile_map)
    return pl.pallas_call(
        functools.partial(_collect_kernel, npt=npt, n_experts=n_experts, final=final),
        out_shape=out_shape,
        grid_spec=pltpu.PrefetchScalarGridSpec(
            num_scalar_prefetch=4, grid=(t // ts,), in_specs=in_specs, out_specs=out_specs, scratch_shapes=scratch),
        compiler_params=_params("arbitrary"),
        name="collect_final" if final else "collect",
    )(*args)


def kernel(x_prompt, x_sample, c_prompt, c_sample, state_conv, state_gla, w_ada, b_ada, norm1_g, norm2_g, w_in, conv_w, conv_b, conv_ln_g, conv_ln_b, gate_w2, gate_b, gla_norm_g, w_out, router_grp_w, router_grp_b, router_exp_w, router_exp_b, exp_w_gate, exp_w_up, exp_w_down, final_norm_g):
    n_seq, seq_len, d = x_prompt.shape
    dec_batch, dec_seq, _ = x_sample.shape
    depth = w_ada.shape[0]
    kw, d_conv = conv_w.shape[1:]
    heads, dv = gla_norm_g.shape[1:]
    rank, qkw = gate_w2.shape[1:]
    dk = qkw // heads
    n_grp, n_exp = router_exp_w.shape[2:]
    n_experts = n_grp * n_exp
    tp, ts = n_seq * seq_len, dec_batch * dec_seq
    t = tp + ts
    tm = ROW_TILE
    assert ts == tm and seq_len % tm == 0 and d_conv == heads * dv and kw - 1 <= CONV_HALO
    n_main = 2 * d_conv + 2 * qkw + 2 * heads * dv
    dims = dict(npt=tp // tm, tps=seq_len // tm, n_seq=n_seq, dec_batch=dec_batch, dec_seq=dec_seq)

    xp, xs, xs_blk = x_prompt.reshape(tp, d), x_sample.transpose(1, 0, 2).reshape(ts, d), 0
    w_in_t = jnp.swapaxes(w_in, 1, 2)
    st_t = jnp.transpose(state_conv, (0, 2, 1, 3))
    pad = (-n_seq) % 8
    c_all = jnp.concatenate([c_prompt, jnp.zeros((pad, d), F32), c_sample], axis=0)
    mod = _ada(c_all, w_ada, b_ada)
    st = SORT_TILE
    assert tm % st == 0 and st % dec_batch == 0
    dims_sort = dict(npt=tp // st, tps=seq_len // st, n_seq=n_seq, dec_batch=dec_batch, dec_seq=dec_seq)
    n_tiles_max = -(-(TOP_K * t + (t // st) * n_experts * (SEG_ALIGN - 1)) // EXPERT_TILE) + n_experts

    conv_p, gla_p, conv_u, gla_s = [], [], [], []
    for l in range(depth):
        mod_p = mod[l, :n_seq].reshape(n_seq, 1, 6 * d)
        mod_s = mod[l, n_seq + pad:]
        gw2 = jnp.pad(gate_w2[l], ((0, LANES - rank), (0, 0))).astype(BF16)
        h, la = _norm1(xp, xs, xs_blk, norm1_g[l], mod_p, mod_s, w_in_t, l, n_main, rank, gw2, gate_b[l], dims)
        z = _inproj(h, w_in_t, l, n_main)

        cv_p, cb_p = _conv_prompt(z, conv_w[l], conv_b[l], conv_ln_g[l], conv_ln_b[l], n_seq, seq_len)
        cv_s, u_s = _conv_sample(z, st_t, l, conv_w[l], conv_b[l], conv_ln_g[l], conv_ln_b[l], tp, dec_seq, dec_batch)
        conv_p.append(cb_p)
        conv_u.append(u_s)

        go_p, gs_p = _gla_prompt(z, la, gla_norm_g[l], n_seq, seq_len, heads, dk, dv)

        def seq_major(a):
            return a.reshape(dec_seq, dec_batch, a.shape[-1]).transpose(1, 0, 2)

        zs = z[tp:]
        q3 = seq_major(zs[:, 2 * d_conv:2 * d_conv + qkw])
        k3 = seq_major(zs[:, 2 * d_conv + qkw:2 * d_conv + 2 * qkw])
        v3 = seq_major(zs[:, 2 * d_conv + 2 * qkw:2 * d_conv + 2 * qkw + heads * dv])
        g3 = seq_major(zs[:, 2 * d_conv + 2 * qkw + heads * dv:n_main])
        go_s, gs_s = _gla_sample(q3, k3, v3, g3, seq_major(la[tp:]), state_gla, l, gla_norm_g[l], heads, dk, dv,
                                 prev_states=tuple(gla_s) if l + 1 == depth else ())
        go_s = go_s.transpose(1, 0, 2).reshape(ts, heads * dv).astype(BF16)
        gla_p.append(gs_p)
        gla_s.append(gs_s)

        x = _outproj(cv_p, go_p, cv_s.reshape(ts, d_conv), go_s, xp, xs, xs_blk, w_out, l, mod_p, mod_s, dims)

        wr = jnp.concatenate([router_grp_w[l], router_exp_w[l].reshape(d, n_experts)], axis=1)
        wr = jnp.pad(wr, ((0, 0), (0, LANES - wr.shape[1])))
        wr_hi = wr.astype(BF16)
        wr_lo = (wr - wr_hi.astype(F32)).astype(BF16)
        br = jnp.concatenate([router_grp_b[l], router_exp_b[l].reshape(-1)])
        br = jnp.pad(br, (0, LANES - br.shape[0])).reshape(1, LANES)
        hb, ids, wts = _norm2(x, norm2_g[l], mod_p, mod_s, wr_hi, wr_lo, br, dims, n_grp, n_exp)
        seg = _segments(ids[:, :TOP_K], n_experts, n_tiles_max)
        xe, we, pos = _dispatch_rows(hb, ids, wts, seg, n_experts, n_tiles_max)
        f = exp_w_gate.shape[-1]
        ye = _expert_rows(xe, we, seg, exp_w_gate.reshape(depth * n_experts, d, f),
                          exp_w_up.reshape(depth * n_experts, d, f), exp_w_down.reshape(depth * n_experts, f, d),
                          l * n_experts)
        if l + 1 < depth:
            x = _collect(ye, pos, x, seg, mod_p, mod_s, dims_sort, n_experts)
            xp, xs, xs_blk = x, x, dims["npt"]
        else:
            y_p, y_s = _collect(ye, pos, x, seg, mod_p, mod_s, dims_sort, n_experts, final_g=final_norm_g)

    y_prompt = y_p.reshape(n_seq, seq_len, d)
    y_sample = y_s.reshape(dec_seq, dec_batch, d).transpose(1, 0, 2)
    conv_s = jnp.transpose(_conv_state_sample(st_t, jnp.stack(conv_u)), (0, 2, 1, 3))
    gla_s_all = gla_s[-1] if depth > 1 else gla_s[0][None]
    return (y_prompt, y_sample, jnp.stack(conv_p), jnp.stack(gla_p), conv_s, gla_s_all)
```

```python
import functools

import jax
import jax.numpy as jnp
from jax import lax
from jax.experimental import pallas as pl
from jax.experimental.pallas import tpu as pltpu

F32 = jnp.float32
BF16 = jnp.bfloat16

EPS = 1e-6
GATE_TAU = 16.0
GLA_CHUNK = 32
GLA_BLOCK = 128
TOP_K = 2

ROW_TILE = 512
EXPERT_TILE = 512
SORT_TILE = 512
SEG_ALIGN = 16
SORT_CHUNK = 256
CONV_ROWS = 64
CONV_HALO = 32
LANES = 128
SUBLANES = 8
VMEM_LIMIT = 56 * 1024 * 1024


def _params(*sem):
    return pltpu.CompilerParams(dimension_semantics=sem, vmem_limit_bytes=VMEM_LIMIT)


def _bdot(a, b):
    return jnp.dot(a.astype(BF16), b.astype(BF16), preferred_element_type=F32)


def _round_bf16(x):
    return x.astype(BF16).astype(F32)


def _split3(x):
    hi = x.astype(BF16)
    r = x - hi.astype(F32)
    mid = r.astype(BF16)
    lo = (r - mid.astype(F32)).astype(BF16)
    return hi, mid, lo


def _silu(x):
    return x * jax.nn.sigmoid(x)


def _store_by_group(i, n_prompt_tiles, dec_seq, out_ref, fn, vals, p_refs, s_refs):
    @pl.when(i < n_prompt_tiles)
    def _():
        out_ref[...] = fn(vals, [r[...] for r in p_refs]).astype(out_ref.dtype)

    @pl.when(i >= n_prompt_tiles)
    def _():
        mods = [r[...] for r in s_refs]
        nb = mods[0].shape[0]
        for t in range(out_ref.shape[0] // nb):
            rows = slice(t * nb, (t + 1) * nb)
            out_ref[rows, :] = fn([v[rows] for v in vals], mods).astype(out_ref.dtype)


def _ada_kernel(c_ref, w_ref, b_ref, o_ref):
    c = c_ref[...]
    o_ref[...] = _bdot(_silu(c), w_ref[...]) + b_ref[...]


def _ada(c_all, w_ada, b_ada):
    depth, d, n = w_ada.shape
    rows = c_all.shape[0]
    tn = 1024
    return pl.pallas_call(
        _ada_kernel,
        out_shape=jax.ShapeDtypeStruct((depth, rows, n), F32),
        grid=(depth, n // tn),
        in_specs=[
            pl.BlockSpec((rows, d), lambda l, j: (0, 0)),
            pl.BlockSpec((None, d, tn), lambda l, j: (l, 0, j)),
            pl.BlockSpec((None, 1, tn), lambda l, j: (l, 0, j)),
        ],
        out_specs=pl.BlockSpec((None, rows, tn), lambda l, j: (l, 0, j)),
        compiler_params=_params("arbitrary", "arbitrary"),
        name="ada",
    )(c_all, w_ada, b_ada.reshape(depth, 1, n))


def _mod_specs(cols, width, n_prompt_tiles, tiles_per_seq, n_seq, dec_batch, grid_rank=1, row_axis=0, col_fn=None):
    p_specs, s_specs = [], []
    for c in cols:
        def p_map(*idx, c=c):
            b = jnp.minimum(idx[row_axis] // tiles_per_seq, n_seq - 1)
            return (b, 0, c if col_fn is None else col_fn(c, idx))

        def s_map(*idx, c=c):
            return (0, c if col_fn is None else col_fn(c, idx))

        p_specs.append(pl.BlockSpec((None, 1, width), p_map))
        s_specs.append(pl.BlockSpec((dec_batch, width), s_map))
    return p_specs, s_specs


def _rms(x, g):
    return x * lax.rsqrt(jnp.mean(x * x, axis=-1, keepdims=True) + EPS) * g


def _norm1_kernel(xp_ref, xs_ref, g_ref, shp, scp, shs, scs, wgl_ref, gw2_ref, gb_ref, h_ref, la_ref, *, npt, dec_seq, rank):
    i = pl.program_id(0)
    y = _rms(jnp.where(i < npt, xp_ref[...], xs_ref[...]), g_ref[...])
    _store_by_group(i, npt, dec_seq, h_ref, lambda v, m: v[0] * (1.0 + m[1]) + m[0], [y], [shp, scp], [shs, scs])
    row = lax.broadcasted_iota(jnp.int32, wgl_ref.shape, 0)
    w_gl = jnp.where(row < rank, wgl_ref[...], 0.0).astype(BF16)
    gate_lr = lax.dot_general(h_ref[...], w_gl, (((1,), (1,)), ((), ())), preferred_element_type=F32)
    pre = _bdot(gate_lr, gw2_ref[...]) + gb_ref[...]
    la_ref[...] = (jnp.minimum(pre, 0.0) - jnp.log1p(jnp.exp(-jnp.abs(pre)))) * (1.0 / GATE_TAU)


def _norm1(xp, xs, xs_blk, g, mod_p, mod_s, w_in_t, layer, n_main, rank, gw2, gb, dims):
    d = xp.shape[1]
    tm = ROW_TILE
    npt = dims["npt"]
    t = (npt + 1) * tm
    p_specs, s_specs = _mod_specs((0, 1), d, npt, dims["tps"], dims["n_seq"], dims["dec_batch"])
    qk = gw2.shape[1]
    assert n_main % LANES == 0 and rank <= LANES
    return pl.pallas_call(
        functools.partial(_norm1_kernel, npt=npt, dec_seq=dims["dec_seq"], rank=rank),
        out_shape=(jax.ShapeDtypeStruct((t, d), BF16), jax.ShapeDtypeStruct((t, qk), F32)),
        grid=(t // tm,),
        in_specs=[
            pl.BlockSpec((tm, d), lambda i: (jnp.minimum(i, npt - 1), 0)),
            pl.BlockSpec((tm, d), lambda i: (xs_blk, 0)),
            pl.BlockSpec((1, d), lambda i: (0, 0)),
            *p_specs,
            *s_specs,
            pl.BlockSpec((None, LANES, d), lambda i: (layer, n_main // LANES, 0)),
            pl.BlockSpec(gw2.shape, lambda i: (0, 0)),
            pl.BlockSpec((1, qk), lambda i: (0, 0)),
        ],
        out_specs=(pl.BlockSpec((tm, d), lambda i: (i, 0)), pl.BlockSpec((tm, qk), lambda i: (i, 0))),
        compiler_params=_params("arbitrary"),
        name="norm1",
    )(xp, xs, g.reshape(1, d), mod_p, mod_p, mod_s, mod_s, w_in_t, gw2, gb.reshape(1, qk))


def _inproj_kernel(h_ref, w_ref, o_ref, wb_ref):
    @pl.when(pl.program_id(1) == 0)
    def _():
        wb_ref[...] = w_ref[...].T.astype(BF16)

    o_ref[...] = jnp.dot(h_ref[...], wb_ref[...], preferred_element_type=F32)


def _inproj(h, w_in_t, layer, n_cols):
    t, d = h.shape
    tm, tn = 2 * ROW_TILE, 1024
    return pl.pallas_call(
        _inproj_kernel,
        out_shape=jax.ShapeDtypeStruct((t, n_cols), F32),
        grid=(n_cols // tn, pl.cdiv(t, tm)),
        in_specs=[
            pl.BlockSpec((tm, d), lambda j, i: (i, 0)),
            pl.BlockSpec((None, tn, d), lambda j, i: (layer, j, 0)),
        ],
        out_specs=pl.BlockSpec((tm, tn), lambda j, i: (i, j)),
        scratch_shapes=[pltpu.VMEM((d, tn), BF16)],
        compiler_params=_params("arbitrary", "arbitrary"),
        name="inproj",
    )(h, w_in_t)


def _ln_silu(y, g, b):
    mu = jnp.mean(y, axis=-1, keepdims=True)
    yc = y - mu
    var = jnp.mean(yc * yc, axis=-1, keepdims=True)
    return _silu(yc * lax.rsqrt(var + EPS) * g + b)


def _conv_prompt_kernel(a_ref, b_ref, cw_ref, cb_ref, lng_ref, lnb_ref, o_ref, st_ref, full_ref, cwb_ref, y_ref, *, kw):
    j = pl.program_id(1)
    tm, c = a_ref.shape
    halo = CONV_HALO
    phases = full_ref.shape[0]
    assert phases == SUBLANES

    @pl.when(jnp.logical_and(pl.program_id(0) == 0, j == 0))
    def _():
        full_ref[...] = jnp.zeros(full_ref.shape, F32)
        for w in range(kw):
            cwb_ref[w] = jnp.broadcast_to(cw_ref[w:w + 1, :], (SUBLANES, c))

    prev = full_ref[0, tm + halo - SUBLANES:tm + halo, :]
    tail = jnp.where(j == 0, 0.0, prev)

    @pl.when(j == 0)
    def _():
        for p in range(phases):
            full_ref[p, 0:halo, :] = jnp.zeros((halo, c), F32)

    @pl.when(j > 0)
    def _():
        for p in range(phases):
            full_ref[p, 0:halo, :] = full_ref[p, tm:tm + halo, :]

    u = a_ref[...] * jax.nn.sigmoid(b_ref[...])
    full_ref[0, halo:halo + tm, :] = u
    ext = jnp.concatenate([tail, u], axis=0)
    for p in range(1, phases):
        full_ref[p, halo - SUBLANES:halo - SUBLANES + tm, :] = pltpu.roll(ext, tm + SUBLANES - p, 0)[0:tm]
    off = halo - (kw - 1)
    rb = CONV_ROWS

    def body(r, carry):
        r0 = pl.multiple_of(r * rb, rb)
        for lt in range(c // LANES):
            cols = slice(lt * LANES, (lt + 1) * LANES)
            acc = None
            for p in range(phases):
                x = full_ref[p, pl.ds(r0, rb + halo), cols]
                for a in range(halo // phases + 1):
                    w = a * phases + p - off
                    if 0 <= w < kw and a * phases + rb <= rb + halo:
                        term = x[a * phases:a * phases + rb] * jnp.concatenate([cwb_ref[w, :, cols]] * (rb // SUBLANES), axis=0)
                        acc = term if acc is None else acc + term
            y_ref[pl.ds(r0, rb), cols] = acc
        return carry

    lax.fori_loop(0, tm // rb, body, 0)
    y = _ln_silu(y_ref[...] + cb_ref[...], lng_ref[...], lnb_ref[...])
    o_ref[...] = y.astype(o_ref.dtype)

    @pl.when(j == pl.num_programs(1) - 1)
    def _():
        st_ref[...] = full_ref[0, halo + tm - (kw - 1):halo + tm, :]


def _conv_prompt(z, cw, cb, lng, lnb, n_seq, seq_len):
    kw, c = cw.shape
    tm = ROW_TILE
    tps = seq_len // tm
    return pl.pallas_call(
        functools.partial(_conv_prompt_kernel, kw=kw),
        out_shape=(jax.ShapeDtypeStruct((n_seq * seq_len, c), BF16), jax.ShapeDtypeStruct((n_seq, kw - 1, c), F32)),
        grid=(n_seq, tps),
        in_specs=[
            pl.BlockSpec((tm, c), lambda b, j: (b * tps + j, 0)),
            pl.BlockSpec((tm, c), lambda b, j: (b * tps + j, 1)),
            pl.BlockSpec((kw, c), lambda b, j: (0, 0)),
            pl.BlockSpec((1, c), lambda b, j: (0, 0)),
            pl.BlockSpec((1, c), lambda b, j: (0, 0)),
            pl.BlockSpec((1, c), lambda b, j: (0, 0)),
        ],
        out_specs=(
            pl.BlockSpec((tm, c), lambda b, j: (b * tps + j, 0)),
            pl.BlockSpec((None, kw - 1, c), lambda b, j: (b, 0, 0)),
        ),
        scratch_shapes=[pltpu.VMEM((SUBLANES, tm + CONV_HALO, c), F32), pltpu.VMEM((kw, SUBLANES, c), F32),
                        pltpu.VMEM((tm, c), F32)],
        compiler_params=_params("arbitrary", "arbitrary"),
        name="conv_prompt",
    )(z, z, cw, cb.reshape(1, c), lng.reshape(1, c), lnb.reshape(1, c))


def _conv_sample_kernel(*refs, kw, dec_seq):
    a_refs = refs[0:dec_seq]
    b_refs = refs[dec_seq:2 * dec_seq]
    st_ref, cw_ref, cb_ref, lng_ref, lnb_ref, o_ref, u_ref = refs[2 * dec_seq:]
    hist = kw - 1
    u = [a_refs[t][...] * jax.nn.sigmoid(b_refs[t][...]) for t in range(dec_seq)]
    def row(j):
        return st_ref[j] if j < hist else u[j - hist]

    for t in range(dec_seq):
        acc = row(t) * cw_ref[0:1, :]
        for w in range(1, kw):
            acc = acc + row(t + w) * cw_ref[w:w + 1, :]
        y = _ln_silu(acc + cb_ref[...], lng_ref[...], lnb_ref[...])
        o_ref[t] = y.astype(o_ref.dtype)
        u_ref[t] = u[t]


def _conv_sample(z, st_t, layer, cw, cb, lng, lnb, row0, dec_seq, dec_batch):
    kw, c = cw.shape
    bs = 16
    a_specs = [pl.BlockSpec((bs, c), lambda s, t=t: ((row0 + t * dec_batch) // bs + s, 0)) for t in range(dec_seq)]
    b_specs = [pl.BlockSpec((bs, c), lambda s, t=t: ((row0 + t * dec_batch) // bs + s, 1)) for t in range(dec_seq)]
    vec = pl.BlockSpec((1, c), lambda s: (0, 0))
    return pl.pallas_call(
        functools.partial(_conv_sample_kernel, kw=kw, dec_seq=dec_seq),
        out_shape=(jax.ShapeDtypeStruct((dec_seq, dec_batch, c), BF16), jax.ShapeDtypeStruct((dec_seq, dec_batch, c), F32)),
        grid=(dec_batch // bs,),
        in_specs=[*a_specs, *b_specs, pl.BlockSpec((None, kw - 1, bs, c), lambda s: (layer, 0, s, 0)),
                  pl.BlockSpec((kw, c), lambda s: (0, 0)), vec, vec, vec],
        out_specs=(pl.BlockSpec((dec_seq, bs, c), lambda s: (0, s, 0)), pl.BlockSpec((dec_seq, bs, c), lambda s: (0, s, 0))),
        compiler_params=_params("arbitrary"),
        name="conv_sample",
    )(*([z] * (2 * dec_seq)), st_t, cw, cb.reshape(1, c), lng.reshape(1, c), lnb.reshape(1, c))


def _conv_state_kernel(st_ref, u_ref, o_ref):
    hist, n_new = st_ref.shape[0], u_ref.shape[0]
    for j in range(hist - n_new):
        o_ref[j] = st_ref[j + n_new]
    for t in range(n_new):
        o_ref[hist - n_new + t] = u_ref[t]


def _conv_state_sample(st_t, u_all):
    depth, hist, nb, c = st_t.shape
    n_new = u_all.shape[1]
    assert n_new <= hist
    bs = 16
    return pl.pallas_call(
        _conv_state_kernel,
        out_shape=jax.ShapeDtypeStruct(st_t.shape, st_t.dtype),
        grid=(depth, nb // bs),
        in_specs=[pl.BlockSpec((None, hist, bs, c), lambda l, s: (l, 0, s, 0)),
                  pl.BlockSpec((None, n_new, bs, c), lambda l, s: (l, 0, s, 0))],
        out_specs=pl.BlockSpec((None, hist, bs, c), lambda l, s: (l, 0, s, 0)),
        compiler_params=_params("arbitrary", "arbitrary"),
        name="conv_state",
    )(st_t, u_all)


def _gla_prompt_kernel(qk_ref, v_ref, g_ref, la_ref, gn_ref, o_ref, sfin_ref, st_ref, sn_ref, *, heads, dk, dv):
    j = pl.program_id(1)
    tm = qk_ref.shape[0]
    ck = GLA_CHUNK
    nch = tm // ck
    qkw = heads * dk

    @pl.when(j == 0)
    def _():
        st_ref[...] = jnp.zeros(st_ref.shape, F32)

    la = la_ref[...]
    sb = GLA_BLOCK
    blocks = [slice(r, r + sb) for r in range(0, tm, sb)]
    row = lax.broadcasted_iota(jnp.int32, (sb, sb), 0)
    col = lax.broadcasted_iota(jnp.int32, (sb, sb), 1)
    same = (row // ck) == (col // ck)
    causal = same & (col <= row)
    tri_incl = causal.astype(BF16)
    tri_after = (same & (col > row)).astype(BF16)
    sel = (lax.broadcasted_iota(jnp.int32, (nch, tm), 1) // ck == lax.broadcasted_iota(jnp.int32, (nch, tm), 0)).astype(BF16)
    parts = _split3(la)

    def blockwise(tri):
        return jnp.concatenate([sum(jnp.dot(tri, p[rs], preferred_element_type=F32) for p in parts) for rs in blocks], axis=0)

    b = blockwise(tri_incl)
    rest = blockwise(tri_after)
    tot = sum(jnp.dot(sel, p, preferred_element_type=F32) for p in parts)
    qk = qk_ref[...]
    q = qk[:, :qkw] * (dk ** -0.5)
    k = qk[:, qkw:]
    q_dec = (q * jnp.exp(b)).astype(BF16)
    k_inv = (k * jnp.exp(-b)).astype(BF16)
    k_end = _round_bf16(k * jnp.exp(rest))
    decay = jnp.exp(tot)
    v_all = v_ref[...]
    g_all = g_ref[...]
    for h in range(heads):
        ks = slice(h * dk, (h + 1) * dk)
        vs = slice(h * dv, (h + 1) * dv)
        qd, ki, ke = q_dec[:, ks], k_inv[:, ks], k_end[:, ks]
        vh = v_all[:, vs]
        vb = vh.astype(BF16)
        vr = _round_bf16(vh)
        intra = []
        for rs in blocks:
            att = lax.dot_general(qd[rs], ki[rs], (((1,), (1,)), ((), ())), preferred_element_type=F32)
            att = jnp.where(causal, att, 0.0).astype(BF16)
            intra.append(jnp.dot(att, vb[rs], preferred_element_type=F32))
        o = jnp.concatenate(intra, axis=0)
        s = st_ref[h]
        for n in range(nch):
            rs = slice(n * ck, (n + 1) * ck)
            sn_ref[n] = s.astype(BF16)
            upd = lax.dot_general(vr[rs], ke[rs], (((0,), (0,)), ((), ())), preferred_element_type=F32)
            s = s * decay[n:n + 1, ks] + upd
        st_ref[h] = s
        inter = [lax.dot_general(qd[n * ck:(n + 1) * ck], sn_ref[n], (((1,), (1,)), ((), ())), preferred_element_type=F32)
                 for n in range(nch)]
        o = o + jnp.concatenate(inter, axis=0)
        o = _rms(o, gn_ref[:, vs]) * _silu(g_all[:, vs])
        o_ref[:, vs] = o.astype(o_ref.dtype)

    @pl.when(j == pl.num_programs(1) - 1)
    def _():
        for h in range(heads):
            sfin_ref[h] = st_ref[h].T


def _gla_prompt(z, la, gn, n_seq, seq_len, heads, dk, dv):
    tm = ROW_TILE
    tps = seq_len // tm
    w = heads * dv
    qkw = heads * dk
    assert 2 * qkw == w
    return pl.pallas_call(
        functools.partial(_gla_prompt_kernel, heads=heads, dk=dk, dv=dv),
        out_shape=(jax.ShapeDtypeStruct((n_seq * seq_len, w), BF16), jax.ShapeDtypeStruct((n_seq, heads, dk, dv), F32)),
        grid=(n_seq, tps),
        in_specs=[
            pl.BlockSpec((tm, w), lambda b, j: (b * tps + j, 2)),
            pl.BlockSpec((tm, w), lambda b, j: (b * tps + j, 3)),
            pl.BlockSpec((tm, w), lambda b, j: (b * tps + j, 4)),
            pl.BlockSpec((tm, qkw), lambda b, j: (b * tps + j, 0)),
            pl.BlockSpec((1, w), lambda b, j: (0, 0)),
        ],
        out_specs=(
            pl.BlockSpec((tm, w), lambda b, j: (b * tps + j, 0)),
            pl.BlockSpec((None, heads, dk, dv), lambda b, j: (b, 0, 0, 0)),
        ),
        scratch_shapes=[pltpu.VMEM((heads, dv, dk), F32), pltpu.VMEM((tm // GLA_CHUNK, dv, dk), BF16)],
        compiler_params=_params("arbitrary", "arbitrary"),
        name="gla_prompt",
    )(z, z, z, la, gn.reshape(1, w))


def _gla_sample_kernel(q_ref, k_ref, v_ref, g_ref, la_ref, s_ref, gn_ref, *rest, heads, dk, dv, n_prev):
    bs, ln, _ = q_ref.shape
    if n_prev:
        o_ref, all_ref = rest[n_prev:]
        for p in range(n_prev):
            all_ref[p] = rest[p][...]
        ns_ref = all_ref.at[n_prev]
    else:
        o_ref, ns_ref = rest
    tril = lax.broadcasted_iota(jnp.int32, (ln, ln), 1) <= lax.broadcasted_iota(jnp.int32, (ln, ln), 0)
    for s in range(bs):
        q_s, k_s, v_s, g_s, la_s = q_ref[s], k_ref[s], v_ref[s], g_ref[s], la_ref[s]
        for h in range(heads):
            ks = slice(h * dk, (h + 1) * dk)
            vs = slice(h * dv, (h + 1) * dv)
            la = la_s[:, ks]
            rows = [la[0:1]]
            for t in range(1, ln):
                rows.append(rows[-1] + la[t:t + 1])
            b = jnp.concatenate(rows, axis=0)
            b_last = rows[-1]
            q_dec = _round_bf16(q_s[:, ks] * (dk ** -0.5) * jnp.exp(b))
            k_inv = _round_bf16(k_s[:, ks] * jnp.exp(-b))
            k_end = _round_bf16(k_s[:, ks] * jnp.exp(b_last - b))
            vr = _round_bf16(v_s[:, vs])
            s0 = s_ref[s, h]
            att = lax.dot_general(q_dec, k_inv, (((1,), (1,)), ((), ())), preferred_element_type=F32)
            att = _round_bf16(jnp.where(tril, att, 0.0))
            o = jnp.dot(att, vr, preferred_element_type=F32) + jnp.dot(q_dec, _round_bf16(s0), preferred_element_type=F32)
            upd = lax.dot_general(k_end, vr, (((0,), (0,)), ((), ())), preferred_element_type=F32)
            d_col = jnp.broadcast_to(jnp.exp(b_last), (dk, dk)).T
            ns_ref[s, h] = s0 * jnp.concatenate([d_col] * (dv // dk), axis=1) + upd
            o = _rms(o, gn_ref[:, vs]) * _silu(g_s[:, vs])
            o_ref[s, :, vs] = o.astype(o_ref.dtype)


def _gla_sample(q3, k3, v3, g3, la3, state, layer, gn, heads, dk, dv, prev_states=()):
    nb, ln, w = v3.shape
    qkw = heads * dk
    bs = 8
    n_prev = len(prev_states)
    seq_spec = pl.BlockSpec((bs, ln, qkw), lambda s: (s, 0, 0))
    wide_spec = pl.BlockSpec((bs, ln, w), lambda s: (s, 0, 0))
    in_specs = [seq_spec, seq_spec, wide_spec, wide_spec, seq_spec,
                pl.BlockSpec((None, bs, heads, dk, dv), lambda s: (layer, s, 0, 0, 0)),
                pl.BlockSpec((1, w), lambda s: (0, 0)),
                *[pl.BlockSpec((bs, heads, dk, dv), lambda s: (s, 0, 0, 0))] * n_prev]
    if n_prev:
        state_shape = jax.ShapeDtypeStruct((n_prev + 1, nb, heads, dk, dv), F32)
        state_spec = pl.BlockSpec((n_prev + 1, bs, heads, dk, dv), lambda s: (0, s, 0, 0, 0))
    else:
        state_shape = jax.ShapeDtypeStruct((nb, heads, dk, dv), F32)
        state_spec = pl.BlockSpec((bs, heads, dk, dv), lambda s: (s, 0, 0, 0))
    return pl.pallas_call(
        functools.partial(_gla_sample_kernel, heads=heads, dk=dk, dv=dv, n_prev=n_prev),
        out_shape=(jax.ShapeDtypeStruct((nb, ln, w), F32), state_shape),
        grid=(nb // bs,),
        in_specs=in_specs,
        out_specs=(wide_spec, state_spec),
        compiler_params=_params("arbitrary"),
        name="gla_sample_last" if n_prev else "gla_sample",
    )(q3, k3, v3, g3, la3, state, gn.reshape(1, w), *prev_states)


def _outproj_kernel(cp_ref, op_ref, cs_ref, os_ref, xp_ref, xs_ref, w_ref, gp, gs, y_ref, wb_ref, *, npt, dec_seq):
    i = pl.program_id(1)

    @pl.when(i == 0)
    def _():
        wb_ref[...] = w_ref[...].astype(BF16)

    half = cp_ref.shape[1]

    def mixed(c_ref, o_ref):
        return (jnp.dot(c_ref[...], wb_ref[0:half, :], preferred_element_type=F32)
                + jnp.dot(o_ref[...], wb_ref[half:, :], preferred_element_type=F32))

    @pl.when(i < npt)
    def _():
        y_ref[...] = xp_ref[...] + gp[...] * mixed(cp_ref, op_ref)

    @pl.when(i >= npt)
    def _():
        mix = mixed(cs_ref, os_ref)
        nb = gs.shape[0]
        for t in range(dec_seq):
            rows = slice(t * nb, (t + 1) * nb)
            y_ref[rows, :] = xs_ref[rows, :] + gs[...] * mix[rows]


def _outproj(conv_p, gla_p, conv_s, gla_s, xp, xs, xs_blk, w_out, layer, mod_p, mod_s, dims):
    d = xp.shape[1]
    half = conv_p.shape[1]
    tm, tn = ROW_TILE, 1024
    nj = d // tn
    npt = dims["npt"]
    t = (npt + 1) * tm
    p_specs, s_specs = _mod_specs((2,), tn, dims["npt"], dims["tps"], dims["n_seq"], dims["dec_batch"],
                                  row_axis=1, col_fn=lambda c, idx: c * nj + idx[0])
    return pl.pallas_call(
        functools.partial(_outproj_kernel, npt=dims["npt"], dec_seq=dims["dec_seq"]),
        out_shape=jax.ShapeDtypeStruct((t, d), F32),
        grid=(nj, t // tm),
        in_specs=[
            pl.BlockSpec((tm, half), lambda j, i: (jnp.minimum(i, npt - 1), 0)),
            pl.BlockSpec((tm, half), lambda j, i: (jnp.minimum(i, npt - 1), 0)),
            pl.BlockSpec((tm, half), lambda j, i: (0, 0)),
            pl.BlockSpec((tm, half), lambda j, i: (0, 0)),
            pl.BlockSpec((tm, tn), lambda j, i: (jnp.minimum(i, npt - 1), j)),
            pl.BlockSpec((tm, tn), lambda j, i: (xs_blk, j)),
            pl.BlockSpec((None, d, tn), lambda j, i: (layer, 0, j)),
            *p_specs,
            *s_specs,
        ],
        out_specs=pl.BlockSpec((tm, tn), lambda j, i: (i, j)),
        scratch_shapes=[pltpu.VMEM((d, tn), BF16)],
        compiler_params=_params("arbitrary", "arbitrary"),
        name="outproj",
    )(conv_p, gla_p, conv_s, gla_s, xp, xs, w_out, mod_p, mod_s)


def _norm2_kernel(x_ref, g_ref, shp, scp, shs, scs, wr_hi, wr_lo, br_ref, hb_ref, ids_ref, wts_ref, h_ref, *, npt, dec_seq, n_grp, n_exp):
    i = pl.program_id(0)
    y = _rms(x_ref[...], g_ref[...])
    _store_by_group(i, npt, dec_seq, h_ref, lambda v, m: v[0] * (1.0 + m[1]) + m[0], [y], [shp, scp], [shs, scs])
    h = h_ref[...]
    h_hi = h.astype(BF16)
    hb_ref[...] = h_hi
    h_lo = (h - h_hi.astype(F32)).astype(BF16)
    logits = (jnp.dot(h_hi, wr_hi[...], preferred_element_type=F32) + jnp.dot(h_lo, wr_hi[...], preferred_element_type=F32)
              + jnp.dot(h_hi, wr_lo[...], preferred_element_type=F32)) + br_ref[...]
    lane = lax.broadcasted_iota(jnp.int32, logits.shape, 1).astype(F32)
    big = jnp.float32(LANES)
    neg = jnp.float32(-jnp.inf)
    gl = jnp.where(lane < n_grp, logits, neg)
    gmax = jnp.max(gl, axis=-1, keepdims=True)
    gidx = jnp.min(jnp.where(gl == gmax, lane, big), axis=-1, keepdims=True)
    g_w = 1.0 / jnp.sum(jnp.exp(gl - gmax), axis=-1, keepdims=True)
    lo = n_grp + gidx * n_exp
    in_grp = (lane >= lo) & (lane < lo + n_exp)
    sl = jnp.where(in_grp, logits, neg)
    p = jnp.exp(sl - jnp.max(sl, axis=-1, keepdims=True))
    p = p / jnp.sum(p, axis=-1, keepdims=True)
    p = jnp.where(in_grp, p, -1.0)
    p1 = jnp.max(p, axis=-1, keepdims=True)
    i1 = jnp.min(jnp.where(p == p1, lane, big), axis=-1, keepdims=True)
    p_rest = jnp.where(lane == i1, -1.0, p)
    p2 = jnp.max(p_rest, axis=-1, keepdims=True)
    i2 = jnp.min(jnp.where(p_rest == p2, lane, big), axis=-1, keepdims=True)
    denom = p1 + p2
    ids_ref[...] = jnp.where(lane == 0, i1 - n_grp, jnp.where(lane == 1, i2 - n_grp, 0.0)).astype(jnp.int32)
    wts_ref[...] = jnp.where(lane == 0, g_w * (p1 / denom), jnp.where(lane == 1, g_w * (p2 / denom), 0.0))


def _norm2(x, g, mod_p, mod_s, wr_hi, wr_lo, br, dims, n_grp, n_exp):
    t, d = x.shape
    tm = ROW_TILE
    p_specs, s_specs = _mod_specs((3, 4), d, dims["npt"], dims["tps"], dims["n_seq"], dims["dec_batch"])
    return pl.pallas_call(
        functools.partial(_norm2_kernel, npt=dims["npt"], dec_seq=dims["dec_seq"], n_grp=n_grp, n_exp=n_exp),
        out_shape=(jax.ShapeDtypeStruct((t, d), BF16), jax.ShapeDtypeStruct((t, LANES), jnp.int32),
                   jax.ShapeDtypeStruct((t, LANES), F32)),
        grid=(t // tm,),
        in_specs=[
            pl.BlockSpec((tm, d), lambda i: (i, 0)),
            pl.BlockSpec((1, d), lambda i: (0, 0)),
            *p_specs,
            *s_specs,
            pl.BlockSpec((d, LANES), lambda i: (0, 0)),
            pl.BlockSpec((d, LANES), lambda i: (0, 0)),
            pl.BlockSpec((1, LANES), lambda i: (0, 0)),
        ],
        out_specs=(pl.BlockSpec((tm, d), lambda i: (i, 0)), pl.BlockSpec((tm, LANES), lambda i: (i, 0)),
                   pl.BlockSpec((tm, LANES), lambda i: (i, 0))),
        scratch_shapes=[pltpu.VMEM((tm, d), F32)],
        compiler_params=_params("arbitrary"),
        name="norm2_router",
    )(x, g.reshape(1, d), mod_p, mod_p, mod_s, mod_s, wr_hi, wr_lo, br)


def _segments(ids, n_experts, n_tiles_max):
    t = ids.shape[0]
    ts, te, al = SORT_TILE, EXPERT_TILE, SEG_ALIGN
    nt = t // ts
    e_iota = jnp.arange(n_experts, dtype=jnp.int32)
    cnt = jnp.sum((ids.reshape(nt, ts * TOP_K, 1) == e_iota).astype(jnp.int32), axis=1)
    size = (cnt + al - 1) // al * al
    src = jnp.cumsum(size, axis=1) - size
    tot = jnp.sum(size, axis=0)
    tot_pad = (tot + te - 1) // te * te
    ends = jnp.cumsum(tot_pad)
    exp_off = ends - tot_pad
    dst = exp_off[None, :] + jnp.cumsum(size, axis=0) - size
    n_used = (ends[-1] // te).astype(jnp.int32)
    tile_start = jnp.arange(n_tiles_max, dtype=jnp.int32) * te
    tile_expert = jnp.minimum(jnp.sum(ends[None, :] <= tile_start[:, None], axis=1), n_experts - 1).astype(jnp.int32)
    last = tile_expert[jnp.maximum(n_used - 1, 0)]
    used = jnp.arange(n_tiles_max) < n_used
    tile_expert = jnp.where(used, tile_expert, last)
    prev_e = jnp.concatenate([jnp.full((1,), -1, jnp.int32), tile_expert[:-1]])
    tile_first = (used & (tile_expert != prev_e)).astype(jnp.int32)
    tile_slot = ((jnp.cumsum(tile_first) - 1) % 2).astype(jnp.int32)
    next_start = jnp.sum(used[None, :] & (tile_expert[None, :] <= tile_expert[:, None]), axis=1)
    tile_next = jnp.where(next_start < n_used, tile_expert[jnp.minimum(next_start, n_tiles_max - 1)], -1).astype(jnp.int32)
    return dict(tile_first=tile_first, tile_slot=tile_slot, tile_next=tile_next,
                src=src.reshape(-1).astype(jnp.int32), dst=dst.reshape(-1).astype(jnp.int32),
                size=size.reshape(-1).astype(jnp.int32), tile_tot=jnp.sum(size, axis=1).astype(jnp.int32),
                fill_start=(exp_off + tot).astype(jnp.int32), fill_size=(tot_pad - tot).astype(jnp.int32),
                tile_expert=tile_expert, n_used=n_used.reshape(1))


def _lane_pick(x, lane, k):
    return jnp.sum(jnp.where(lane == k, x, 0.0), axis=-1, keepdims=True)


def _sorted_positions(ids, lane):
    ts = ids.shape[0]
    onehot = [(lane == _lane_pick(ids, lane, k)).astype(F32) for k in range(TOP_K)]
    row = lax.broadcasted_iota(jnp.int32, (ts, ts), 0)
    col = lax.broadcasted_iota(jnp.int32, (ts, ts), 1)
    before = (col < row).astype(BF16)
    earlier = [jnp.dot(before, o.astype(BF16), preferred_element_type=F32) for o in onehot]
    cnt = [jnp.sum(o, axis=0, keepdims=True) for o in onehot]
    size = jnp.ceil((cnt[0] + cnt[1]) * (1.0 / SEG_ALIGN)) * SEG_ALIGN
    er = lax.broadcasted_iota(jnp.int32, (LANES, LANES), 0)
    ec = lax.broadcasted_iota(jnp.int32, (LANES, LANES), 1)
    start = jnp.dot(jnp.broadcast_to(size, (SUBLANES, LANES)).astype(BF16), (er < ec).astype(BF16),
                    preferred_element_type=F32)[0:1]
    base = [start + earlier[0], start + cnt[0] + earlier[1]]
    return [jnp.sum(onehot[k] * base[k], axis=-1, keepdims=True) for k in range(TOP_K)]


def _as_row(col_vals, lane):
    hi = jnp.floor(col_vals * (1.0 / 64.0))
    lo = col_vals - hi * 64.0
    ones = jnp.ones((SUBLANES, LANES), BF16)
    nt = (((1,), (1,)), ((), ()))
    hi_row = lax.dot_general(ones, jnp.where(lane == 0, hi, 0.0).astype(BF16), nt, preferred_element_type=F32)
    lo_row = lax.dot_general(ones, jnp.where(lane == 0, lo, 0.0).astype(BF16), nt, preferred_element_type=F32)
    return (hi_row * 64.0 + lo_row)[0:1]


def _dispatch_kernel(src_ref, dst_ref, size_ref, tot_ref, fst_ref, fsz_ref, nu_ref, h_ref, ids_ref, wts_ref, xe_hbm,
                     we_hbm, pos_ref, sbuf, wbuf, zx, zw, semx, semw, semz, semt, *, n_experts):
    i = pl.program_id(0)
    n = pl.num_programs(0)
    ts = h_ref.shape[0]
    rows = sbuf.shape[1]
    te = zx.shape[0]
    slot = lax.rem(i, 2)

    def tail_copies(tl):
        r0 = pl.multiple_of(tl * te, te)
        return (pltpu.make_async_copy(zx, xe_hbm.at[pl.ds(r0, te), :], semt.at[0]),
                pltpu.make_async_copy(zw, we_hbm.at[pl.ds(r0, te), :], semt.at[1]))

    n_tail_tiles = xe_hbm.shape[0] // te

    def for_tail(action, first):
        def body(tl, carry):
            for c in tail_copies(tl):
                action(c)
            return carry

        lax.fori_loop(first, n_tail_tiles, body, 0)

    def seg_copies(tile, sl, e):
        k = tile * n_experts + e
        sz = pl.multiple_of(size_ref[k], SEG_ALIGN)
        s0 = pl.multiple_of(src_ref[k], SEG_ALIGN)
        d0 = pl.multiple_of(dst_ref[k], SEG_ALIGN)
        return sz, (pltpu.make_async_copy(sbuf.at[sl, pl.ds(s0, sz), :], xe_hbm.at[pl.ds(d0, sz), :], semx.at[sl]),
                    pltpu.make_async_copy(wbuf.at[sl, pl.ds(s0, sz), :], we_hbm.at[pl.ds(d0, sz), :], semw.at[sl]))

    def wait_tile(tile, sl):
        tot = pl.multiple_of(tot_ref[tile], SEG_ALIGN)
        pltpu.make_async_copy(sbuf.at[sl, pl.ds(0, tot), :], xe_hbm.at[pl.ds(0, tot), :], semx.at[sl]).wait()
        pltpu.make_async_copy(wbuf.at[sl, pl.ds(0, tot), :], we_hbm.at[pl.ds(0, tot), :], semw.at[sl]).wait()

    @pl.when(i == 0)
    def _():
        zx[...] = jnp.zeros(zx.shape, zx.dtype)
        zw[...] = jnp.zeros(zw.shape, zw.dtype)

        def fill(e, carry):
            sz = pl.multiple_of(fsz_ref[e], SEG_ALIGN)
            d0 = pl.multiple_of(fst_ref[e], SEG_ALIGN)

            @pl.when(sz > 0)
            def _():
                cx = pltpu.make_async_copy(zx.at[pl.ds(0, sz), :], xe_hbm.at[pl.ds(d0, sz), :], semz.at[0])
                cw = pltpu.make_async_copy(zw.at[pl.ds(0, sz), :], we_hbm.at[pl.ds(d0, sz), :], semz.at[1])
                cx.start()
                cw.start()
                cx.wait()
                cw.wait()

            return carry

        lax.fori_loop(0, n_experts, fill, 0)

    @pl.when(i >= 2)
    def _():
        wait_tile(i - 2, slot)

    lane = lax.broadcasted_iota(jnp.int32, (ts, LANES), 1).astype(F32)
    pos = _sorted_positions(ids_ref[...].astype(F32), lane)
    pos_ref[...] = jnp.where(lane == 0, pos[0], jnp.where(lane == 1, pos[1], 0.0))
    assert rows <= 4096 and rows % SORT_CHUNK == 0
    pos_rows = [_as_row(p, lane) for p in pos]
    wts = wts_ref[...]
    w_cols = []
    for k in range(TOP_K):
        pieces = _split3(_lane_pick(wts, lane, k))
        wk = jnp.where(lane == 0, pieces[0].astype(F32), jnp.where(lane == 1, pieces[1].astype(F32),
                       jnp.where(lane == 2, pieces[2].astype(F32), 0.0)))
        w_cols.append(wk.astype(BF16))
    for c0 in range(0, rows, SORT_CHUNK):
        @pl.when(c0 < tot_ref[i])
        def _():
            r_iota = (lax.broadcasted_iota(jnp.int32, (SORT_CHUNK, ts), 0) + c0).astype(F32)
            sel = [(r_iota == pr).astype(BF16) for pr in pos_rows]
            chunk = slice(c0, c0 + SORT_CHUNK)
            sbuf[slot, chunk] = jnp.dot(sel[0] + sel[1], h_ref[...], preferred_element_type=F32).astype(BF16)
            wbuf[slot, chunk] = sum(jnp.dot(sel[k], w_cols[k], preferred_element_type=F32) for k in range(TOP_K))

    def issue(e, carry):
        sz, copies = seg_copies(i, slot, e)

        @pl.when(sz > 0)
        def _():
            for c in copies:
                c.start()

        return carry

    lax.fori_loop(0, n_experts, issue, 0)

    tail_tile = nu_ref[0] + i

    @pl.when(jnp.logical_and(i < n - 1, tail_tile < n_tail_tiles))
    def _():
        for c in tail_copies(tail_tile):
            c.start()

    @pl.when(i == n - 1)
    def _():
        for_tail(lambda c: c.start(), tail_tile)

        @pl.when(i >= 1)
        def _():
            wait_tile(i - 1, 1 - slot)

        wait_tile(i, slot)
        for_tail(lambda c: c.wait(), nu_ref[0])


def _dispatch_rows(hb, ids, wts, seg, n_experts, n_tiles_max):
    t, d = hb.shape
    ts, te = SORT_TILE, EXPERT_TILE
    rows = TOP_K * ts + n_experts * SEG_ALIGN
    n_slots = n_tiles_max * te

    def tile_map(i, *_):
        return (i, 0)

    return pl.pallas_call(
        functools.partial(_dispatch_kernel, n_experts=n_experts),
        out_shape=(jax.ShapeDtypeStruct((n_slots, d), BF16), jax.ShapeDtypeStruct((n_slots, LANES), F32),
                   jax.ShapeDtypeStruct((t, LANES), F32)),
        grid_spec=pltpu.PrefetchScalarGridSpec(
            num_scalar_prefetch=7,
            grid=(t // ts,),
            in_specs=[pl.BlockSpec((ts, d), tile_map), pl.BlockSpec((ts, LANES), tile_map),
                      pl.BlockSpec((ts, LANES), tile_map)],
            out_specs=(pl.BlockSpec(memory_space=pl.ANY), pl.BlockSpec(memory_space=pl.ANY),
                       pl.BlockSpec((ts, LANES), tile_map)),
            scratch_shapes=[
                pltpu.VMEM((2, rows, d), BF16),
                pltpu.VMEM((2, rows, LANES), F32),
                pltpu.VMEM((te, d), BF16),
                pltpu.VMEM((te, LANES), F32),
                pltpu.SemaphoreType.DMA((2,)),
                pltpu.SemaphoreType.DMA((2,)),
                pltpu.SemaphoreType.DMA((2,)),
                pltpu.SemaphoreType.DMA((2,)),
            ],
        ),
        compiler_params=_params("arbitrary"),
        name="dispatch",
    )(seg["src"], seg["dst"], seg["size"], seg["tile_tot"], seg["fill_start"], seg["fill_size"], seg["n_used"],
      hb, ids, wts)


def _expert_rows_kernel(te_ref, nu_ref, first_ref, slot_ref, next_ref, x_ref, w_ref, wg_hbm, wu_hbm, wd_hbm, o_ref,
                        wgf, wuf, wdf, wsem, wgb, wub, wdb, *, layer_base):
    i = pl.program_id(0)
    n_used = nu_ref[0]

    def weight_copies(e, sl):
        return (pltpu.make_async_copy(wg_hbm.at[layer_base + e], wgf.at[sl], wsem.at[sl]),
                pltpu.make_async_copy(wu_hbm.at[layer_base + e], wuf.at[sl], wsem.at[sl]),
                pltpu.make_async_copy(wd_hbm.at[layer_base + e], wdf.at[sl], wsem.at[sl]))

    @pl.when(i == 0)
    def _():
        for c in weight_copies(te_ref[0], 0):
            c.start()

    @pl.when(i < n_used)
    def _():
        @pl.when(first_ref[i] == 1)
        def _():
            sl = slot_ref[i]
            for c in weight_copies(te_ref[i], sl):
                c.wait()
            wgb[...] = wgf[sl].astype(BF16)
            wub[...] = wuf[sl].astype(BF16)
            wdb[...] = wdf[sl].astype(BF16)
            nxt = next_ref[i]

            @pl.when(nxt >= 0)
            def _():
                for c in weight_copies(nxt, 1 - sl):
                    c.start()

        x = x_ref[...]
        a = jnp.dot(x, wgb[...], preferred_element_type=F32)
        u = jnp.dot(x, wub[...], preferred_element_type=F32)
        w = jnp.sum(w_ref[...], axis=-1, keepdims=True)
        hid = _silu(a) * u * w
        o_ref[...] = jnp.dot(hid.astype(BF16), wdb[...], preferred_element_type=F32).astype(o_ref.dtype)


def _expert_rows(xe, we, seg, wg, wu, wd, layer_base):
    n_slots, d = xe.shape
    f = wg.shape[-1]
    te = EXPERT_TILE

    def row_map(i, te_ref, nu_ref, *_):
        return (jnp.minimum(i, nu_ref[0] - 1), 0)

    return pl.pallas_call(
        functools.partial(_expert_rows_kernel, layer_base=layer_base),
        out_shape=jax.ShapeDtypeStruct((n_slots, d), BF16),
        grid_spec=pltpu.PrefetchScalarGridSpec(
            num_scalar_prefetch=5,
            grid=(n_slots // te,),
            in_specs=[
                pl.BlockSpec((te, d), row_map),
                pl.BlockSpec((te, LANES), row_map),
                pl.BlockSpec(memory_space=pl.ANY),
                pl.BlockSpec(memory_space=pl.ANY),
                pl.BlockSpec(memory_space=pl.ANY),
            ],
            out_specs=pl.BlockSpec((te, d), row_map),
            scratch_shapes=[
                pltpu.VMEM((2, d, f), F32), pltpu.VMEM((2, d, f), F32), pltpu.VMEM((2, f, d), F32),
                pltpu.SemaphoreType.DMA((2,)),
                pltpu.VMEM((d, f), BF16), pltpu.VMEM((d, f), BF16), pltpu.VMEM((f, d), BF16),
            ],
        ),
        input_output_aliases={5: 0},
        compiler_params=_params("arbitrary"),
        name="experts",
    )(seg["tile_expert"], seg["n_used"], seg["tile_first"], seg["tile_slot"], seg["tile_next"], xe, we, wg, wu, wd)


def _collect_kernel(src_ref, dst_ref, size_ref, tot_ref, y_hbm, x_ref, pos_ref, gp, gs, *rest, npt, n_experts, final):
    if final:
        fg_ref, op_ref, os_ref, buf, sem, o_ref = rest
    else:
        o_ref, buf, sem = rest
    i = pl.program_id(0)
    n = pl.num_programs(0)
    ts = x_ref.shape[0]
    rows = buf.shape[1]
    slot = lax.rem(i, 2)

    def start(tile, sl):
        def body(e, carry):
            k = tile * n_experts + e
            sz = pl.multiple_of(size_ref[k], SEG_ALIGN)
            s0 = pl.multiple_of(src_ref[k], SEG_ALIGN)
            d0 = pl.multiple_of(dst_ref[k], SEG_ALIGN)

            @pl.when(sz > 0)
            def _():
                pltpu.make_async_copy(y_hbm.at[pl.ds(d0, sz), :], buf.at[sl, pl.ds(s0, sz), :], sem.at[sl]).start()

            return carry

        lax.fori_loop(0, n_experts, body, 0)

    @pl.when(i == 0)
    def _():
        buf[...] = jnp.zeros(buf.shape, buf.dtype)
        start(0, 0)

    @pl.when(i + 1 < n)
    def _():
        start(i + 1, 1 - slot)

    tot = pl.multiple_of(tot_ref[i], SEG_ALIGN)
    pltpu.make_async_copy(y_hbm.at[pl.ds(0, tot), :], buf.at[slot, pl.ds(0, tot), :], sem.at[slot]).wait()
    lane = lax.broadcasted_iota(jnp.int32, (ts, LANES), 1).astype(F32)
    pos = pos_ref[...]
    r_iota = lax.broadcasted_iota(jnp.int32, (ts, rows), 1).astype(F32)
    pick = ((r_iota == _lane_pick(pos, lane, 0)) | (r_iota == _lane_pick(pos, lane, 1))).astype(BF16)
    ff = jnp.dot(pick, buf[slot], preferred_element_type=F32)
    _store_by_group(i, npt, None, o_ref, lambda v, m: v[0] + m[0] * v[1], [x_ref[...], ff], [gp], [gs])
    if final:
        y = _rms(o_ref[...], fg_ref[...])

        @pl.when(i < npt)
        def _():
            op_ref[...] = y

        @pl.when(i >= npt)
        def _():
            os_ref[...] = y


def _collect(ye, pos, x, seg, mod_p, mod_s, dims, n_experts, final_g=None):
    t, d = x.shape
    ts = SORT_TILE
    npt = dims["npt"]
    rows = TOP_K * ts + n_experts * SEG_ALIGN
    p_specs, s_specs = _mod_specs((5,), d, npt, dims["tps"], dims["n_seq"], dims["dec_batch"])
    final = final_g is not None

    def strip(spec):
        return pl.BlockSpec(spec.block_shape, lambda i, *_, m=spec.index_map: m(i))

    def tile_map(i, *_):
        return (i, 0)

    in_specs = [
        pl.BlockSpec(memory_space=pl.ANY),
        pl.BlockSpec((ts, d), tile_map),
        pl.BlockSpec((ts, LANES), tile_map),
        *[strip(s) for s in p_specs],
        *[strip(s) for s in s_specs],
    ]
    scratch = [pltpu.VMEM((2, rows, d), BF16), pltpu.SemaphoreType.DMA((2,))]
    args = [seg["src"], seg["dst"], seg["size"], seg["tile_tot"], ye, x, pos, mod_p, mod_s]
    if final:
        in_specs.append(pl.BlockSpec((1, d), lambda i, *_: (0, 0)))
        args.append(final_g.reshape(1, d))
        out_shape = (jax.ShapeDtypeStruct((npt * ts, d), F32), jax.ShapeDtypeStruct((t - npt * ts, d), F32))
        out_specs = (pl.BlockSpec((ts, d), lambda i, *_: (jnp.minimum(i, npt - 1), 0)),
                     pl.BlockSpec((ts, d), lambda i, *_: (jnp.maximum(i - npt, 0), 0)))
        scratch.append(pltpu.VMEM((ts, d), F32))
    else:
        out_shape = jax.ShapeDtypeStruct((t, d), F32)
        out_specs = pl.BlockSpec((ts, d), tile_map)
    return pl.pallas_call(
        functools.partial(_collect_kernel, npt=npt, n_experts=n_experts, final=final),
        out_shape=out_shape,
        grid_spec=pltpu.PrefetchScalarGridSpec(
            num_scalar_prefetch=4, grid=(t // ts,), in_specs=in_specs, out_specs=out_specs, scratch_shapes=scratch),
        compiler_params=_params("arbitrary"),
        name="collect_final" if final else "collect",
    )(*args)


def kernel(x_prompt, x_sample, c_prompt, c_sample, state_conv, state_gla, w_ada, b_ada, norm1_g, norm2_g, w_in, conv_w, conv_b, conv_ln_g, conv_ln_b, gate_w2, gate_b, gla_norm_g, w_out, router_grp_w, router_grp_b, router_exp_w, router_exp_b, exp_w_gate, exp_w_up, exp_w_down, final_norm_g):
    n_seq, seq_len, d = x_prompt.shape
    dec_batch, dec_seq, _ = x_sample.shape
    depth = w_ada.shape[0]
    kw, d_conv = conv_w.shape[1:]
    heads, dv = gla_norm_g.shape[1:]
    rank, qkw = gate_w2.shape[1:]
    dk = qkw // heads
    n_grp, n_exp = router_exp_w.shape[2:]
    n_experts = n_grp * n_exp
    tp, ts = n_seq * seq_len, dec_batch * dec_seq
    t = tp + ts
    tm = ROW_TILE
    assert ts == tm and seq_len % tm == 0 and d_conv == heads * dv and kw - 1 <= CONV_HALO
    n_main = 2 * d_conv + 2 * qkw + 2 * heads * dv
    dims = dict(npt=tp // tm, tps=seq_len // tm, n_seq=n_seq, dec_batch=dec_batch, dec_seq=dec_seq)

    xp, xs, xs_blk = x_prompt.reshape(tp, d), x_sample.transpose(1, 0, 2).reshape(ts, d), 0
    w_in_t = jnp.swapaxes(w_in, 1, 2)
    st_t = jnp.transpose(state_conv, (0, 2, 1, 3))
    pad = (-n_seq) % 8
    c_all = jnp.concatenate([c_prompt, jnp.zeros((pad, d), F32), c_sample], axis=0)
    mod = _ada(c_all, w_ada, b_ada)
    st = SORT_TILE
    assert tm % st == 0 and st % dec_batch == 0
    dims_sort = dict(npt=tp // st, tps=seq_len // st, n_seq=n_seq, dec_batch=dec_batch, dec_seq=dec_seq)
    n_tiles_max = -(-(TOP_K * t + (t // st) * n_experts * (SEG_ALIGN - 1)) // EXPERT_TILE) + n_experts

    conv_p, gla_p, conv_u, gla_s = [], [], [], []
    for l in range(depth):
        mod_p = mod[l, :n_seq].reshape(n_seq, 1, 6 * d)
        mod_s = mod[l, n_seq + pad:]
        gw2 = jnp.pad(gate_w2[l], ((0, LANES - rank), (0, 0))).astype(BF16)
        h, la = _norm1(xp, xs, xs_blk, norm1_g[l], mod_p, mod_s, w_in_t, l, n_main, rank, gw2, gate_b[l], dims)
        z = _inproj(h, w_in_t, l, n_main)

        cv_p, cb_p = _conv_prompt(z, conv_w[l], conv_b[l], conv_ln_g[l], conv_ln_b[l], n_seq, seq_len)
        cv_s, u_s = _conv_sample(z, st_t, l, conv_w[l], conv_b[l], conv_ln_g[l], conv_ln_b[l], tp, dec_seq, dec_batch)
        conv_p.append(cb_p)
        conv_u.append(u_s)

        go_p, gs_p = _gla_prompt(z, la, gla_norm_g[l], n_seq, seq_len, heads, dk, dv)

        def seq_major(a):
            return a.reshape(dec_seq, dec_batch, a.shape[-1]).transpose(1, 0, 2)

        zs = z[tp:]
        q3 = seq_major(zs[:, 2 * d_conv:2 * d_conv + qkw])
        k3 = seq_major(zs[:, 2 * d_conv + qkw:2 * d_conv + 2 * qkw])
        v3 = seq_major(zs[:, 2 * d_conv + 2 * qkw:2 * d_conv + 2 * qkw + heads * dv])
        g3 = seq_major(zs[:, 2 * d_conv + 2 * qkw + heads * dv:n_main])
        go_s, gs_s = _gla_sample(q3, k3, v3, g3, seq_major(la[tp:]), state_gla, l, gla_norm_g[l], heads, dk, dv,
                                 prev_states=tuple(gla_s) if l + 1 == depth else ())
        go_s = go_s.transpose(1, 0, 2).reshape(ts, heads * dv).astype(BF16)
        gla_p.append(gs_p)
        gla_s.append(gs_s)

        x = _outproj(cv_p, go_p, cv_s.reshape(ts, d_conv), go_s, xp, xs, xs_blk, w_out, l, mod_p, mod_s, dims)

        wr = jnp.concatenate([router_grp_w[l], router_exp_w[l].reshape(d, n_experts)], axis=1)
        wr = jnp.pad(wr, ((0, 0), (0, LANES - wr.shape[1])))
        wr_hi = wr.astype(BF16)
        wr_lo = (wr - wr_hi.astype(F32)).astype(BF16)
        br = jnp.concatenate([router_grp_b[l], router_exp_b[l].reshape(-1)])
        br = jnp.pad(br, (0, LANES - br.shape[0])).reshape(1, LANES)
        hb, ids, wts = _norm2(x, norm2_g[l], mod_p, mod_s, wr_hi, wr_lo, br, dims, n_grp, n_exp)
        seg = _segments(ids[:, :TOP_K], n_experts, n_tiles_max)
        xe, we, pos = _dispatch_rows(hb, ids, wts, seg, n_experts, n_tiles_max)
        f = exp_w_gate.shape[-1]
        ye = _expert_rows(xe, we, seg, exp_w_gate.reshape(depth * n_experts, d, f),
                          exp_w_up.reshape(depth * n_experts, d, f), exp_w_down.reshape(depth * n_experts, f, d),
                          l * n_experts)
        if l + 1 < depth:
            x = _collect(ye, pos, x, seg, mod_p, mod_s, dims_sort, n_experts)
            xp, xs, xs_blk = x, x, dims["npt"]
        else:
            y_p, y_s = _collect(ye, pos, x, seg, mod_p, mod_s, dims_sort, n_experts, final_g=final_norm_g)

    y_prompt = y_p.reshape(n_seq, seq_len, d)
    y_sample = y_s.reshape(dec_seq, dec_batch, d).transpose(1, 0, 2)
    conv_s = jnp.transpose(_conv_state_sample(st_t, jnp.stack(conv_u)), (0, 2, 1, 3))
    gla_s_all = gla_s[-1] if depth > 1 else gla_s[0][None]
    return (y_prompt, y_sample, jnp.stack(conv_p), jnp.stack(gla_p), conv_s, gla_s_all)
```

```python
import functools

import jax
import jax.numpy as jnp
from jax import lax
from jax.experimental import pallas as pl
from jax.experimental.pallas import tpu as pltpu

F32 = jnp.float32
BF16 = jnp.bfloat16

EPS = 1e-6
GATE_TAU = 16.0
GLA_CHUNK = 32
GLA_BLOCK = 128
TOP_K = 2

ROW_TILE = 512
EXPERT_TILE = 512
SORT_TILE = 512
SEG_ALIGN = 16
CONV_ROWS = 64
CONV_HALO = 32
LANES = 128
SUBLANES = 8
VMEM_LIMIT = 56 * 1024 * 1024


def _params(*sem):
    return pltpu.CompilerParams(dimension_semantics=sem, vmem_limit_bytes=VMEM_LIMIT)


def _bdot(a, b):
    return jnp.dot(a.astype(BF16), b.astype(BF16), preferred_element_type=F32)


def _round_bf16(x):
    return x.astype(BF16).astype(F32)


def _split3(x):
    hi = x.astype(BF16)
    r = x - hi.astype(F32)
    mid = r.astype(BF16)
    lo = (r - mid.astype(F32)).astype(BF16)
    return hi, mid, lo


def _silu(x):
    return x * jax.nn.sigmoid(x)


def _store_by_group(i, n_prompt_tiles, dec_seq, out_ref, fn, vals, p_refs, s_refs):
    @pl.when(i < n_prompt_tiles)
    def _():
        out_ref[...] = fn(vals, [r[...] for r in p_refs]).astype(out_ref.dtype)

    @pl.when(i >= n_prompt_tiles)
    def _():
        mods = [r[...] for r in s_refs]
        nb = mods[0].shape[0]
        for t in range(out_ref.shape[0] // nb):
            rows = slice(t * nb, (t + 1) * nb)
            out_ref[rows, :] = fn([v[rows] for v in vals], mods).astype(out_ref.dtype)


def _ada_kernel(c_ref, w_ref, b_ref, o_ref):
    c = c_ref[...]
    o_ref[...] = _bdot(_silu(c), w_ref[...]) + b_ref[...]


def _ada(c_all, w_ada, b_ada):
    depth, d, n = w_ada.shape
    rows = c_all.shape[0]
    tn = 1024
    return pl.pallas_call(
        _ada_kernel,
        out_shape=jax.ShapeDtypeStruct((depth, rows, n), F32),
        grid=(depth, n // tn),
        in_specs=[
            pl.BlockSpec((rows, d), lambda l, j: (0, 0)),
            pl.BlockSpec((None, d, tn), lambda l, j: (l, 0, j)),
            pl.BlockSpec((None, 1, tn), lambda l, j: (l, 0, j)),
        ],
        out_specs=pl.BlockSpec((None, rows, tn), lambda l, j: (l, 0, j)),
        compiler_params=_params("arbitrary", "arbitrary"),
        name="ada",
    )(c_all, w_ada, b_ada.reshape(depth, 1, n))


def _mod_specs(cols, width, n_prompt_tiles, tiles_per_seq, n_seq, dec_batch, grid_rank=1, row_axis=0, col_fn=None):
    p_specs, s_specs = [], []
    for c in cols:
        def p_map(*idx, c=c):
            b = jnp.minimum(idx[row_axis] // tiles_per_seq, n_seq - 1)
            return (b, 0, c if col_fn is None else col_fn(c, idx))

        def s_map(*idx, c=c):
            return (0, c if col_fn is None else col_fn(c, idx))

        p_specs.append(pl.BlockSpec((None, 1, width), p_map))
        s_specs.append(pl.BlockSpec((dec_batch, width), s_map))
    return p_specs, s_specs


def _rms(x, g):
    return x * lax.rsqrt(jnp.mean(x * x, axis=-1, keepdims=True) + EPS) * g


def _norm1_kernel(xp_ref, xs_ref, g_ref, shp, scp, shs, scs, wgl_ref, gw2_ref, gb_ref, h_ref, la_ref, *, npt, dec_seq, rank):
    i = pl.program_id(0)
    y = _rms(jnp.where(i < npt, xp_ref[...], xs_ref[...]), g_ref[...])
    _store_by_group(i, npt, dec_seq, h_ref, lambda v, m: v[0] * (1.0 + m[1]) + m[0], [y], [shp, scp], [shs, scs])
    row = lax.broadcasted_iota(jnp.int32, wgl_ref.shape, 0)
    w_gl = jnp.where(row < rank, wgl_ref[...], 0.0).astype(BF16)
    gate_lr = lax.dot_general(h_ref[...], w_gl, (((1,), (1,)), ((), ())), preferred_element_type=F32)
    pre = _bdot(gate_lr, gw2_ref[...]) + gb_ref[...]
    la_ref[...] = (jnp.minimum(pre, 0.0) - jnp.log1p(jnp.exp(-jnp.abs(pre)))) * (1.0 / GATE_TAU)


def _norm1(xp, xs, xs_blk, g, mod_p, mod_s, w_in_t, layer, n_main, rank, gw2, gb, dims):
    d = xp.shape[1]
    tm = ROW_TILE
    npt = dims["npt"]
    t = (npt + 1) * tm
    p_specs, s_specs = _mod_specs((0, 1), d, npt, dims["tps"], dims["n_seq"], dims["dec_batch"])
    qk = gw2.shape[1]
    assert n_main % LANES == 0 and rank <= LANES
    return pl.pallas_call(
        functools.partial(_norm1_kernel, npt=npt, dec_seq=dims["dec_seq"], rank=rank),
        out_shape=(jax.ShapeDtypeStruct((t, d), BF16), jax.ShapeDtypeStruct((t, qk), F32)),
        grid=(t // tm,),
        in_specs=[
            pl.BlockSpec((tm, d), lambda i: (jnp.minimum(i, npt - 1), 0)),
            pl.BlockSpec((tm, d), lambda i: (xs_blk, 0)),
            pl.BlockSpec((1, d), lambda i: (0, 0)),
            *p_specs,
            *s_specs,
            pl.BlockSpec((None, LANES, d), lambda i: (layer, n_main // LANES, 0)),
            pl.BlockSpec(gw2.shape, lambda i: (0, 0)),
            pl.BlockSpec((1, qk), lambda i: (0, 0)),
        ],
        out_specs=(pl.BlockSpec((tm, d), lambda i: (i, 0)), pl.BlockSpec((tm, qk), lambda i: (i, 0))),
        compiler_params=_params("arbitrary"),
        name="norm1",
    )(xp, xs, g.reshape(1, d), mod_p, mod_p, mod_s, mod_s, w_in_t, gw2, gb.reshape(1, qk))


def _inproj_kernel(h_ref, w_ref, o_ref, wb_ref):
    @pl.when(pl.program_id(1) == 0)
    def _():
        wb_ref[...] = w_ref[...].astype(BF16)

    o_ref[...] = lax.dot_general(h_ref[...], wb_ref[...], (((1,), (1,)), ((), ())), preferred_element_type=F32)


def _inproj(h, w_in_t, layer, n_cols):
    t, d = h.shape
    tm, tn = 2 * ROW_TILE, 1024
    return pl.pallas_call(
        _inproj_kernel,
        out_shape=jax.ShapeDtypeStruct((t, n_cols), F32),
        grid=(n_cols // tn, pl.cdiv(t, tm)),
        in_specs=[
            pl.BlockSpec((tm, d), lambda j, i: (i, 0)),
            pl.BlockSpec((None, tn, d), lambda j, i: (layer, j, 0)),
        ],
        out_specs=pl.BlockSpec((tm, tn), lambda j, i: (i, j)),
        scratch_shapes=[pltpu.VMEM((tn, d), BF16)],
        compiler_params=_params("arbitrary", "arbitrary"),
        name="inproj",
    )(h, w_in_t)


def _ln_silu(y, g, b):
    mu = jnp.mean(y, axis=-1, keepdims=True)
    yc = y - mu
    var = jnp.mean(yc * yc, axis=-1, keepdims=True)
    return _silu(yc * lax.rsqrt(var + EPS) * g + b)


def _conv_prompt_kernel(a_ref, b_ref, cw_ref, cb_ref, lng_ref, lnb_ref, o_ref, st_ref, full_ref, cwb_ref, y_ref, *, kw):
    j = pl.program_id(1)
    tm, c = a_ref.shape
    halo = CONV_HALO
    phases = full_ref.shape[0]
    assert phases == SUBLANES

    @pl.when(jnp.logical_and(pl.program_id(0) == 0, j == 0))
    def _():
        full_ref[...] = jnp.zeros(full_ref.shape, F32)
        for w in range(kw):
            cwb_ref[w] = jnp.broadcast_to(cw_ref[w:w + 1, :], (SUBLANES, c))

    prev = full_ref[0, tm + halo - SUBLANES:tm + halo, :]
    tail = jnp.where(j == 0, 0.0, prev)

    @pl.when(j == 0)
    def _():
        for p in range(phases):
            full_ref[p, 0:halo, :] = jnp.zeros((halo, c), F32)

    @pl.when(j > 0)
    def _():
        for p in range(phases):
            full_ref[p, 0:halo, :] = full_ref[p, tm:tm + halo, :]

    u = a_ref[...] * jax.nn.sigmoid(b_ref[...])
    full_ref[0, halo:halo + tm, :] = u
    ext = jnp.concatenate([tail, u], axis=0)
    for p in range(1, phases):
        full_ref[p, halo - SUBLANES:halo - SUBLANES + tm, :] = pltpu.roll(ext, tm + SUBLANES - p, 0)[0:tm]
    off = halo - (kw - 1)
    rb = CONV_ROWS

    def body(r, carry):
        r0 = pl.multiple_of(r * rb, rb)
        for lt in range(c // LANES):
            cols = slice(lt * LANES, (lt + 1) * LANES)
            acc = None
            for p in range(phases):
                x = full_ref[p, pl.ds(r0, rb + halo), cols]
                for a in range(halo // phases + 1):
                    w = a * phases + p - off
                    if 0 <= w < kw and a * phases + rb <= rb + halo:
                        term = x[a * phases:a * phases + rb] * jnp.concatenate([cwb_ref[w, :, cols]] * (rb // SUBLANES), axis=0)
                        acc = term if acc is None else acc + term
            y_ref[pl.ds(r0, rb), cols] = acc
        return carry

    lax.fori_loop(0, tm // rb, body, 0)
    y = _ln_silu(y_ref[...] + cb_ref[...], lng_ref[...], lnb_ref[...])
    o_ref[...] = y.astype(o_ref.dtype)

    @pl.when(j == pl.num_programs(1) - 1)
    def _():
        st_ref[...] = full_ref[0, halo + tm - (kw - 1):halo + tm, :]


def _conv_prompt(z, cw, cb, lng, lnb, n_seq, seq_len):
    kw, c = cw.shape
    tm = ROW_TILE
    tps = seq_len // tm
    return pl.pallas_call(
        functools.partial(_conv_prompt_kernel, kw=kw),
        out_shape=(jax.ShapeDtypeStruct((n_seq * seq_len, c), BF16), jax.ShapeDtypeStruct((n_seq, kw - 1, c), F32)),
        grid=(n_seq, tps),
        in_specs=[
            pl.BlockSpec((tm, c), lambda b, j: (b * tps + j, 0)),
            pl.BlockSpec((tm, c), lambda b, j: (b * tps + j, 1)),
            pl.BlockSpec((kw, c), lambda b, j: (0, 0)),
            pl.BlockSpec((1, c), lambda b, j: (0, 0)),
            pl.BlockSpec((1, c), lambda b, j: (0, 0)),
            pl.BlockSpec((1, c), lambda b, j: (0, 0)),
        ],
        out_specs=(
            pl.BlockSpec((tm, c), lambda b, j: (b * tps + j, 0)),
            pl.BlockSpec((None, kw - 1, c), lambda b, j: (b, 0, 0)),
        ),
        scratch_shapes=[pltpu.VMEM((SUBLANES, tm + CONV_HALO, c), F32), pltpu.VMEM((kw, SUBLANES, c), F32),
                        pltpu.VMEM((tm, c), F32)],
        compiler_params=_params("arbitrary", "arbitrary"),
        name="conv_prompt",
    )(z, z, cw, cb.reshape(1, c), lng.reshape(1, c), lnb.reshape(1, c))


def _conv_sample_kernel(*refs, kw, dec_seq):
    a_refs = refs[0:dec_seq]
    b_refs = refs[dec_seq:2 * dec_seq]
    st_ref, cw_ref, cb_ref, lng_ref, lnb_ref, o_ref, u_ref = refs[2 * dec_seq:]
    hist = kw - 1
    u = [a_refs[t][...] * jax.nn.sigmoid(b_refs[t][...]) for t in range(dec_seq)]
    def row(j):
        return st_ref[j] if j < hist else u[j - hist]

    for t in range(dec_seq):
        acc = row(t) * cw_ref[0:1, :]
        for w in range(1, kw):
            acc = acc + row(t + w) * cw_ref[w:w + 1, :]
        y = _ln_silu(acc + cb_ref[...], lng_ref[...], lnb_ref[...])
        o_ref[t] = y.astype(o_ref.dtype)
        u_ref[t] = u[t]


def _conv_sample(z, st_t, layer, cw, cb, lng, lnb, row0, dec_seq, dec_batch):
    kw, c = cw.shape
    bs = 16
    a_specs = [pl.BlockSpec((bs, c), lambda s, t=t: ((row0 + t * dec_batch) // bs + s, 0)) for t in range(dec_seq)]
    b_specs = [pl.BlockSpec((bs, c), lambda s, t=t: ((row0 + t * dec_batch) // bs + s, 1)) for t in range(dec_seq)]
    vec = pl.BlockSpec((1, c), lambda s: (0, 0))
    return pl.pallas_call(
        functools.partial(_conv_sample_kernel, kw=kw, dec_seq=dec_seq),
        out_shape=(jax.ShapeDtypeStruct((dec_seq, dec_batch, c), BF16), jax.ShapeDtypeStruct((dec_seq, dec_batch, c), F32)),
        grid=(dec_batch // bs,),
        in_specs=[*a_specs, *b_specs, pl.BlockSpec((None, kw - 1, bs, c), lambda s: (layer, 0, s, 0)),
                  pl.BlockSpec((kw, c), lambda s: (0, 0)), vec, vec, vec],
        out_specs=(pl.BlockSpec((dec_seq, bs, c), lambda s: (0, s, 0)), pl.BlockSpec((dec_seq, bs, c), lambda s: (0, s, 0))),
        compiler_params=_params("arbitrary"),
        name="conv_sample",
    )(*([z] * (2 * dec_seq)), st_t, cw, cb.reshape(1, c), lng.reshape(1, c), lnb.reshape(1, c))


def _conv_state_kernel(st_ref, u_ref, o_ref):
    hist, n_new = st_ref.shape[0], u_ref.shape[0]
    for j in range(hist - n_new):
        o_ref[j] = st_ref[j + n_new]
    for t in range(n_new):
        o_ref[hist - n_new + t] = u_ref[t]


def _conv_state_sample(st_t, u_all):
    depth, hist, nb, c = st_t.shape
    n_new = u_all.shape[1]
    assert n_new <= hist
    bs = 16
    return pl.pallas_call(
        _conv_state_kernel,
        out_shape=jax.ShapeDtypeStruct(st_t.shape, st_t.dtype),
        grid=(depth, nb // bs),
        in_specs=[pl.BlockSpec((None, hist, bs, c), lambda l, s: (l, 0, s, 0)),
                  pl.BlockSpec((None, n_new, bs, c), lambda l, s: (l, 0, s, 0))],
        out_specs=pl.BlockSpec((None, hist, bs, c), lambda l, s: (l, 0, s, 0)),
        compiler_params=_params("arbitrary", "arbitrary"),
        name="conv_state",
    )(st_t, u_all)


def _gla_prompt_kernel(qk_ref, v_ref, g_ref, la_ref, gn_ref, o_ref, sfin_ref, st_ref, sn_ref, *, heads, dk, dv):
    j = pl.program_id(1)
    tm = qk_ref.shape[0]
    ck = GLA_CHUNK
    nch = tm // ck
    qkw = heads * dk

    @pl.when(j == 0)
    def _():
        st_ref[...] = jnp.zeros(st_ref.shape, F32)

    la = la_ref[...]
    sb = GLA_BLOCK
    blocks = [slice(r, r + sb) for r in range(0, tm, sb)]
    row = lax.broadcasted_iota(jnp.int32, (sb, sb), 0)
    col = lax.broadcasted_iota(jnp.int32, (sb, sb), 1)
    same = (row // ck) == (col // ck)
    causal = same & (col <= row)
    tri_incl = causal.astype(BF16)
    tri_after = (same & (col > row)).astype(BF16)
    sel = (lax.broadcasted_iota(jnp.int32, (nch, tm), 1) // ck == lax.broadcasted_iota(jnp.int32, (nch, tm), 0)).astype(BF16)
    parts = _split3(la)

    def blockwise(tri):
        return jnp.concatenate([sum(jnp.dot(tri, p[rs], preferred_element_type=F32) for p in parts) for rs in blocks], axis=0)

    b = blockwise(tri_incl)
    rest = blockwise(tri_after)
    tot = sum(jnp.dot(sel, p, preferred_element_type=F32) for p in parts)
    qk = qk_ref[...]
    q = qk[:, :qkw] * (dk ** -0.5)
    k = qk[:, qkw:]
    q_dec = (q * jnp.exp(b)).astype(BF16)
    k_inv = (k * jnp.exp(-b)).astype(BF16)
    k_end = _round_bf16(k * jnp.exp(rest))
    decay = jnp.exp(tot)
    v_all = v_ref[...]
    g_all = g_ref[...]
    for h in range(heads):
        ks = slice(h * dk, (h + 1) * dk)
        vs = slice(h * dv, (h + 1) * dv)
        qd, ki, ke = q_dec[:, ks], k_inv[:, ks], k_end[:, ks]
        vh = v_all[:, vs]
        vb = vh.astype(BF16)
        vr = _round_bf16(vh)
        intra = []
        for rs in blocks:
            att = lax.dot_general(qd[rs], ki[rs], (((1,), (1,)), ((), ())), preferred_element_type=F32)
            att = jnp.where(causal, att, 0.0).astype(BF16)
            intra.append(jnp.dot(att, vb[rs], preferred_element_type=F32))
        o = jnp.concatenate(intra, axis=0)
        s = st_ref[h]
        for n in range(nch):
            rs = slice(n * ck, (n + 1) * ck)
            sn_ref[n] = s.astype(BF16)
            upd = lax.dot_general(vr[rs], ke[rs], (((0,), (0,)), ((), ())), preferred_element_type=F32)
            s = s * decay[n:n + 1, ks] + upd
        st_ref[h] = s
        inter = [lax.dot_general(qd[n * ck:(n + 1) * ck], sn_ref[n], (((1,), (1,)), ((), ())), preferred_element_type=F32)
                 for n in range(nch)]
        o = o + jnp.concatenate(inter, axis=0)
        o = _rms(o, gn_ref[:, vs]) * _silu(g_all[:, vs])
        o_ref[:, vs] = o.astype(o_ref.dtype)

    @pl.when(j == pl.num_programs(1) - 1)
    def _():
        for h in range(heads):
            sfin_ref[h] = st_ref[h].T


def _gla_prompt(z, la, gn, n_seq, seq_len, heads, dk, dv):
    tm = ROW_TILE
    tps = seq_len // tm
    w = heads * dv
    qkw = heads * dk
    assert 2 * qkw == w
    return pl.pallas_call(
        functools.partial(_gla_prompt_kernel, heads=heads, dk=dk, dv=dv),
        out_shape=(jax.ShapeDtypeStruct((n_seq * seq_len, w), BF16), jax.ShapeDtypeStruct((n_seq, heads, dk, dv), F32)),
        grid=(n_seq, tps),
        in_specs=[
            pl.BlockSpec((tm, w), lambda b, j: (b * tps + j, 2)),
            pl.BlockSpec((tm, w), lambda b, j: (b * tps + j, 3)),
            pl.BlockSpec((tm, w), lambda b, j: (b * tps + j, 4)),
            pl.BlockSpec((tm, qkw), lambda b, j: (b * tps + j, 0)),
            pl.BlockSpec((1, w), lambda b, j: (0, 0)),
        ],
        out_specs=(
            pl.BlockSpec((tm, w), lambda b, j: (b * tps + j, 0)),
            pl.BlockSpec((None, heads, dk, dv), lambda b, j: (b, 0, 0, 0)),
        ),
        scratch_shapes=[pltpu.VMEM((heads, dv, dk), F32), pltpu.VMEM((tm // GLA_CHUNK, dv, dk), BF16)],
        compiler_params=_params("arbitrary", "arbitrary"),
        name="gla_prompt",
    )(z, z, z, la, gn.reshape(1, w))


def _gla_sample_kernel(q_ref, k_ref, v_ref, g_ref, la_ref, s_ref, gn_ref, *rest, heads, dk, dv, n_prev):
    bs, ln, _ = q_ref.shape
    if n_prev:
        o_ref, all_ref = rest[n_prev:]
        for p in range(n_prev):
            all_ref[p] = rest[p][...]
        ns_ref = all_ref.at[n_prev]
    else:
        o_ref, ns_ref = rest
    tril = lax.broadcasted_iota(jnp.int32, (ln, ln), 1) <= lax.broadcasted_iota(jnp.int32, (ln, ln), 0)
    for s in range(bs):
        q_s, k_s, v_s, g_s, la_s = q_ref[s], k_ref[s], v_ref[s], g_ref[s], la_ref[s]
        for h in range(heads):
            ks = slice(h * dk, (h + 1) * dk)
            vs = slice(h * dv, (h + 1) * dv)
            la = la_s[:, ks]
            rows = [la[0:1]]
            for t in range(1, ln):
                rows.append(rows[-1] + la[t:t + 1])
            b = jnp.concatenate(rows, axis=0)
            b_last = rows[-1]
            q_dec = _round_bf16(q_s[:, ks] * (dk ** -0.5) * jnp.exp(b))
            k_inv = _round_bf16(k_s[:, ks] * jnp.exp(-b))
            k_end = _round_bf16(k_s[:, ks] * jnp.exp(b_last - b))
            vr = _round_bf16(v_s[:, vs])
            s0 = s_ref[s, h]
            att = lax.dot_general(q_dec, k_inv, (((1,), (1,)), ((), ())), preferred_element_type=F32)
            att = _round_bf16(jnp.where(tril, att, 0.0))
            o = jnp.dot(att, vr, preferred_element_type=F32) + jnp.dot(q_dec, _round_bf16(s0), preferred_element_type=F32)
            upd = lax.dot_general(k_end, vr, (((0,), (0,)), ((), ())), preferred_element_type=F32)
            d_col = jnp.broadcast_to(jnp.exp(b_last), (dk, dk)).T
            ns_ref[s, h] = s0 * jnp.concatenate([d_col] * (dv // dk), axis=1) + upd
            o = _rms(o, gn_ref[:, vs]) * _silu(g_s[:, vs])
            o_ref[s, :, vs] = o.astype(o_ref.dtype)


def _gla_sample(q3, k3, v3, g3, la3, state, layer, gn, heads, dk, dv, prev_states=()):
    nb, ln, w = v3.shape
    qkw = heads * dk
    bs = 8
    n_prev = len(prev_states)
    seq_spec = pl.BlockSpec((bs, ln, qkw), lambda s: (s, 0, 0))
    wide_spec = pl.BlockSpec((bs, ln, w), lambda s: (s, 0, 0))
    in_specs = [seq_spec, seq_spec, wide_spec, wide_spec, seq_spec,
                pl.BlockSpec((None, bs, heads, dk, dv), lambda s: (layer, s, 0, 0, 0)),
                pl.BlockSpec((1, w), lambda s: (0, 0)),
                *[pl.BlockSpec((bs, heads, dk, dv), lambda s: (s, 0, 0, 0))] * n_prev]
    if n_prev:
        state_shape = jax.ShapeDtypeStruct((n_prev + 1, nb, heads, dk, dv), F32)
        state_spec = pl.BlockSpec((n_prev + 1, bs, heads, dk, dv), lambda s: (0, s, 0, 0, 0))
    else:
        state_shape = jax.ShapeDtypeStruct((nb, heads, dk, dv), F32)
        state_spec = pl.BlockSpec((bs, heads, dk, dv), lambda s: (s, 0, 0, 0))
    return pl.pallas_call(
        functools.partial(_gla_sample_kernel, heads=heads, dk=dk, dv=dv, n_prev=n_prev),
        out_shape=(jax.ShapeDtypeStruct((nb, ln, w), F32), state_shape),
        grid=(nb // bs,),
        in_specs=in_specs,
        out_specs=(wide_spec, state_spec),
        compiler_params=_params("arbitrary"),
        name="gla_sample_last" if n_prev else "gla_sample",
    )(q3, k3, v3, g3, la3, state, gn.reshape(1, w), *prev_states)


def _outproj_kernel(cp_ref, op_ref, cs_ref, os_ref, xp_ref, xs_ref, w_ref, gp, gs, y_ref, wb_ref, *, npt, dec_seq):
    i = pl.program_id(1)

    @pl.when(i == 0)
    def _():
        wb_ref[...] = w_ref[...].astype(BF16)

    half = cp_ref.shape[1]

    def mixed(c_ref, o_ref):
        return (jnp.dot(c_ref[...], wb_ref[0:half, :], preferred_element_type=F32)
                + jnp.dot(o_ref[...], wb_ref[half:, :], preferred_element_type=F32))

    @pl.when(i < npt)
    def _():
        y_ref[...] = xp_ref[...] + gp[...] * mixed(cp_ref, op_ref)

    @pl.when(i >= npt)
    def _():
        mix = mixed(cs_ref, os_ref)
        nb = gs.shape[0]
        for t in range(dec_seq):
            rows = slice(t * nb, (t + 1) * nb)
            y_ref[rows, :] = xs_ref[rows, :] + gs[...] * mix[rows]


def _outproj(conv_p, gla_p, conv_s, gla_s, xp, xs, xs_blk, w_out, layer, mod_p, mod_s, dims):
    d = xp.shape[1]
    half = conv_p.shape[1]
    tm, tn = ROW_TILE, 1024
    nj = d // tn
    npt = dims["npt"]
    t = (npt + 1) * tm
    p_specs, s_specs = _mod_specs((2,), tn, dims["npt"], dims["tps"], dims["n_seq"], dims["dec_batch"],
                                  row_axis=1, col_fn=lambda c, idx: c * nj + idx[0])
    return pl.pallas_call(
        functools.partial(_outproj_kernel, npt=dims["npt"], dec_seq=dims["dec_seq"]),
        out_shape=jax.ShapeDtypeStruct((t, d), F32),
        grid=(nj, t // tm),
        in_specs=[
            pl.BlockSpec((tm, half), lambda j, i: (jnp.minimum(i, npt - 1), 0)),
            pl.BlockSpec((tm, half), lambda j, i: (jnp.minimum(i, npt - 1), 0)),
            pl.BlockSpec((tm, half), lambda j, i: (0, 0)),
            pl.BlockSpec((tm, half), lambda j, i: (0, 0)),
            pl.BlockSpec((tm, tn), lambda j, i: (jnp.minimum(i, npt - 1), j)),
            pl.BlockSpec((tm, tn), lambda j, i: (xs_blk, j)),
            pl.BlockSpec((None, d, tn), lambda j, i: (layer, 0, j)),
            *p_specs,
            *s_specs,
        ],
        out_specs=pl.BlockSpec((tm, tn), lambda j, i: (i, j)),
        scratch_shapes=[pltpu.VMEM((d, tn), BF16)],
        compiler_params=_params("arbitrary", "arbitrary"),
        name="outproj",
    )(conv_p, gla_p, conv_s, gla_s, xp, xs, w_out, mod_p, mod_s)


def _norm2_kernel(x_ref, g_ref, shp, scp, shs, scs, wr_hi, wr_lo, br_ref, hb_ref, ids_ref, wts_ref, h_ref, *, npt, dec_seq, n_grp, n_exp):
    i = pl.program_id(0)
    y = _rms(x_ref[...], g_ref[...])
    _store_by_group(i, npt, dec_seq, h_ref, lambda v, m: v[0] * (1.0 + m[1]) + m[0], [y], [shp, scp], [shs, scs])
    h = h_ref[...]
    h_hi = h.astype(BF16)
    hb_ref[...] = h_hi
    h_lo = (h - h_hi.astype(F32)).astype(BF16)
    logits = (jnp.dot(h_hi, wr_hi[...], preferred_element_type=F32) + jnp.dot(h_lo, wr_hi[...], preferred_element_type=F32)
              + jnp.dot(h_hi, wr_lo[...], preferred_element_type=F32)) + br_ref[...]
    lane = lax.broadcasted_iota(jnp.int32, logits.shape, 1).astype(F32)
    big = jnp.float32(LANES)
    neg = jnp.float32(-jnp.inf)
    gl = jnp.where(lane < n_grp, logits, neg)
    gmax = jnp.max(gl, axis=-1, keepdims=True)
    gidx = jnp.min(jnp.where(gl == gmax, lane, big), axis=-1, keepdims=True)
    g_w = 1.0 / jnp.sum(jnp.exp(gl - gmax), axis=-1, keepdims=True)
    lo = n_grp + gidx * n_exp
    in_grp = (lane >= lo) & (lane < lo + n_exp)
    sl = jnp.where(in_grp, logits, neg)
    p = jnp.exp(sl - jnp.max(sl, axis=-1, keepdims=True))
    p = p / jnp.sum(p, axis=-1, keepdims=True)
    p = jnp.where(in_grp, p, -1.0)
    p1 = jnp.max(p, axis=-1, keepdims=True)
    i1 = jnp.min(jnp.where(p == p1, lane, big), axis=-1, keepdims=True)
    p_rest = jnp.where(lane == i1, -1.0, p)
    p2 = jnp.max(p_rest, axis=-1, keepdims=True)
    i2 = jnp.min(jnp.where(p_rest == p2, lane, big), axis=-1, keepdims=True)
    denom = p1 + p2
    ids_ref[...] = jnp.where(lane == 0, i1 - n_grp, jnp.where(lane == 1, i2 - n_grp, 0.0)).astype(jnp.int32)
    wts_ref[...] = jnp.where(lane == 0, g_w * (p1 / denom), jnp.where(lane == 1, g_w * (p2 / denom), 0.0))


def _norm2(x, g, mod_p, mod_s, wr_hi, wr_lo, br, dims, n_grp, n_exp):
    t, d = x.shape
    tm = ROW_TILE
    p_specs, s_specs = _mod_specs((3, 4), d, dims["npt"], dims["tps"], dims["n_seq"], dims["dec_batch"])
    return pl.pallas_call(
        functools.partial(_norm2_kernel, npt=dims["npt"], dec_seq=dims["dec_seq"], n_grp=n_grp, n_exp=n_exp),
        out_shape=(jax.ShapeDtypeStruct((t, d), BF16), jax.ShapeDtypeStruct((t, LANES), jnp.int32),
                   jax.ShapeDtypeStruct((t, LANES), F32)),
        grid=(t // tm,),
        in_specs=[
            pl.BlockSpec((tm, d), lambda i: (i, 0)),
            pl.BlockSpec((1, d), lambda i: (0, 0)),
            *p_specs,
            *s_specs,
            pl.BlockSpec((d, LANES), lambda i: (0, 0)),
            pl.BlockSpec((d, LANES), lambda i: (0, 0)),
            pl.BlockSpec((1, LANES), lambda i: (0, 0)),
        ],
        out_specs=(pl.BlockSpec((tm, d), lambda i: (i, 0)), pl.BlockSpec((tm, LANES), lambda i: (i, 0)),
                   pl.BlockSpec((tm, LANES), lambda i: (i, 0))),
        scratch_shapes=[pltpu.VMEM((tm, d), F32)],
        compiler_params=_params("arbitrary"),
        name="norm2_router",
    )(x, g.reshape(1, d), mod_p, mod_p, mod_s, mod_s, wr_hi, wr_lo, br)


def _segments(ids, n_experts, n_tiles_max):
    t = ids.shape[0]
    ts, te, al = SORT_TILE, EXPERT_TILE, SEG_ALIGN
    nt = t // ts
    e_iota = jnp.arange(n_experts, dtype=jnp.int32)
    cnt = jnp.sum((ids.reshape(nt, ts * TOP_K, 1) == e_iota).astype(jnp.int32), axis=1)
    size = (cnt + al - 1) // al * al
    src = jnp.cumsum(size, axis=1) - size
    tot = jnp.sum(size, axis=0)
    tot_pad = (tot + te - 1) // te * te
    ends = jnp.cumsum(tot_pad)
    exp_off = ends - tot_pad
    dst = exp_off[None, :] + jnp.cumsum(size, axis=0) - size
    n_used = (ends[-1] // te).astype(jnp.int32)
    tile_start = jnp.arange(n_tiles_max, dtype=jnp.int32) * te
    tile_expert = jnp.minimum(jnp.sum(ends[None, :] <= tile_start[:, None], axis=1), n_experts - 1).astype(jnp.int32)
    last = tile_expert[jnp.maximum(n_used - 1, 0)]
    used = jnp.arange(n_tiles_max) < n_used
    tile_expert = jnp.where(used, tile_expert, last)
    prev_e = jnp.concatenate([jnp.full((1,), -1, jnp.int32), tile_expert[:-1]])
    tile_first = (used & (tile_expert != prev_e)).astype(jnp.int32)
    tile_slot = ((jnp.cumsum(tile_first) - 1) % 2).astype(jnp.int32)
    next_start = jnp.sum(used[None, :] & (tile_expert[None, :] <= tile_expert[:, None]), axis=1)
    tile_next = jnp.where(next_start < n_used, tile_expert[jnp.minimum(next_start, n_tiles_max - 1)], -1).astype(jnp.int32)
    return dict(tile_first=tile_first, tile_slot=tile_slot, tile_next=tile_next,
                src=src.reshape(-1).astype(jnp.int32), dst=dst.reshape(-1).astype(jnp.int32),
                size=size.reshape(-1).astype(jnp.int32), tile_tot=jnp.sum(size, axis=1).astype(jnp.int32),
                fill_start=(exp_off + tot).astype(jnp.int32), fill_size=(tot_pad - tot).astype(jnp.int32),
                tile_expert=tile_expert, n_used=n_used.reshape(1))


def _lane_pick(x, lane, k):
    return jnp.sum(jnp.where(lane == k, x, 0.0), axis=-1, keepdims=True)


def _sorted_positions(ids, lane):
    ts = ids.shape[0]
    onehot = [(lane == _lane_pick(ids, lane, k)).astype(F32) for k in range(TOP_K)]
    row = lax.broadcasted_iota(jnp.int32, (ts, ts), 0)
    col = lax.broadcasted_iota(jnp.int32, (ts, ts), 1)
    before = (col < row).astype(BF16)
    earlier = [jnp.dot(before, o.astype(BF16), preferred_element_type=F32) for o in onehot]
    cnt = [jnp.sum(o, axis=0, keepdims=True) for o in onehot]
    size = jnp.ceil((cnt[0] + cnt[1]) * (1.0 / SEG_ALIGN)) * SEG_ALIGN
    er = lax.broadcasted_iota(jnp.int32, (LANES, LANES), 0)
    ec = lax.broadcasted_iota(jnp.int32, (LANES, LANES), 1)
    start = jnp.dot(jnp.broadcast_to(size, (SUBLANES, LANES)).astype(BF16), (er < ec).astype(BF16),
                    preferred_element_type=F32)[0:1]
    base = [start + earlier[0], start + cnt[0] + earlier[1]]
    return [jnp.sum(onehot[k] * base[k], axis=-1, keepdims=True) for k in range(TOP_K)]


def _as_row(col_vals, lane):
    hi = jnp.floor(col_vals * (1.0 / 64.0))
    lo = col_vals - hi * 64.0
    ones = jnp.ones((SUBLANES, LANES), BF16)
    nt = (((1,), (1,)), ((), ()))
    hi_row = lax.dot_general(ones, jnp.where(lane == 0, hi, 0.0).astype(BF16), nt, preferred_element_type=F32)
    lo_row = lax.dot_general(ones, jnp.where(lane == 0, lo, 0.0).astype(BF16), nt, preferred_element_type=F32)
    return (hi_row * 64.0 + lo_row)[0:1]


def _dispatch_kernel(src_ref, dst_ref, size_ref, tot_ref, fst_ref, fsz_ref, nu_ref, h_ref, ids_ref, wts_ref, xe_hbm,
                     we_hbm, pos_ref, sbuf, wbuf, zx, zw, semx, semw, semz, semt, *, n_experts):
    i = pl.program_id(0)
    n = pl.num_programs(0)
    ts = h_ref.shape[0]
    rows = sbuf.shape[1]
    te = zx.shape[0]
    slot = lax.rem(i, 2)

    def tail_copies(tl):
        r0 = pl.multiple_of(tl * te, te)
        return (pltpu.make_async_copy(zx, xe_hbm.at[pl.ds(r0, te), :], semt.at[0]),
                pltpu.make_async_copy(zw, we_hbm.at[pl.ds(r0, te), :], semt.at[1]))

    n_tail_tiles = xe_hbm.shape[0] // te

    def for_expert_tails(action):
        def body(e, carry):
            sz = pl.multiple_of(fsz_ref[e], SEG_ALIGN)
            d0 = pl.multiple_of(fst_ref[e], SEG_ALIGN)

            @pl.when(sz > 0)
            def _():
                action(pltpu.make_async_copy(zx.at[pl.ds(0, sz), :], xe_hbm.at[pl.ds(d0, sz), :], semz.at[0]))
                action(pltpu.make_async_copy(zw.at[pl.ds(0, sz), :], we_hbm.at[pl.ds(d0, sz), :], semz.at[1]))

            return carry

        lax.fori_loop(0, n_experts, body, 0)

    def for_tail(action, first):
        def body(tl, carry):
            for c in tail_copies(tl):
                action(c)
            return carry

        lax.fori_loop(first, n_tail_tiles, body, 0)

    def seg_copies(tile, sl, e):
        k = tile * n_experts + e
        sz = pl.multiple_of(size_ref[k], SEG_ALIGN)
        s0 = pl.multiple_of(src_ref[k], SEG_ALIGN)
        d0 = pl.multiple_of(dst_ref[k], SEG_ALIGN)
        return sz, (pltpu.make_async_copy(sbuf.at[sl, pl.ds(s0, sz), :], xe_hbm.at[pl.ds(d0, sz), :], semx.at[sl]),
                    pltpu.make_async_copy(wbuf.at[sl, pl.ds(s0, sz), :], we_hbm.at[pl.ds(d0, sz), :], semw.at[sl]))

    def wait_tile(tile, sl):
        tot = pl.multiple_of(tot_ref[tile], SEG_ALIGN)
        pltpu.make_async_copy(sbuf.at[sl, pl.ds(0, tot), :], xe_hbm.at[pl.ds(0, tot), :], semx.at[sl]).wait()
        pltpu.make_async_copy(wbuf.at[sl, pl.ds(0, tot), :], we_hbm.at[pl.ds(0, tot), :], semw.at[sl]).wait()

    @pl.when(i == 0)
    def _():
        zx[...] = jnp.zeros(zx.shape, zx.dtype)
        zw[...] = jnp.zeros(zw.shape, zw.dtype)
        for_expert_tails(lambda c: c.start())

    @pl.when(i >= 2)
    def _():
        wait_tile(i - 2, slot)

    lane = lax.broadcasted_iota(jnp.int32, (ts, LANES), 1).astype(F32)
    pos = _sorted_positions(ids_ref[...].astype(F32), lane)
    pos_ref[...] = jnp.where(lane == 0, pos[0], jnp.where(lane == 1, pos[1], 0.0))
    r_iota = lax.broadcasted_iota(jnp.int32, (rows, ts), 0).astype(F32)
    assert rows <= 4096
    sel = [(r_iota == _as_row(p, lane)).astype(BF16) for p in pos]
    sbuf[slot] = jnp.dot(sel[0] + sel[1], h_ref[...], preferred_element_type=F32).astype(BF16)
    wts = wts_ref[...]
    wsorted = jnp.zeros((rows, LANES), F32)
    for k in range(TOP_K):
        pieces = _split3(_lane_pick(wts, lane, k))
        wk = jnp.where(lane == 0, pieces[0].astype(F32), jnp.where(lane == 1, pieces[1].astype(F32),
                       jnp.where(lane == 2, pieces[2].astype(F32), 0.0)))
        wsorted = wsorted + jnp.dot(sel[k], wk.astype(BF16), preferred_element_type=F32)
    wbuf[slot] = wsorted

    def issue(e, carry):
        sz, copies = seg_copies(i, slot, e)

        @pl.when(sz > 0)
        def _():
            for c in copies:
                c.start()

        return carry

    lax.fori_loop(0, n_experts, issue, 0)

    tail_tile = nu_ref[0] + i

    @pl.when(jnp.logical_and(i < n - 1, tail_tile < n_tail_tiles))
    def _():
        for c in tail_copies(tail_tile):
            c.start()

    @pl.when(i == n - 1)
    def _():
        for_tail(lambda c: c.start(), tail_tile)

        @pl.when(i >= 1)
        def _():
            wait_tile(i - 1, 1 - slot)

        wait_tile(i, slot)
        for_tail(lambda c: c.wait(), nu_ref[0])
        for_expert_tails(lambda c: c.wait())


def _dispatch_rows(hb, ids, wts, seg, n_experts, n_tiles_max):
    t, d = hb.shape
    ts, te = SORT_TILE, EXPERT_TILE
    rows = TOP_K * ts + n_experts * SEG_ALIGN
    n_slots = n_tiles_max * te

    def tile_map(i, *_):
        return (i, 0)

    return pl.pallas_call(
        functools.partial(_dispatch_kernel, n_experts=n_experts),
        out_shape=(jax.ShapeDtypeStruct((n_slots, d), BF16), jax.ShapeDtypeStruct((n_slots, LANES), F32),
                   jax.ShapeDtypeStruct((t, LANES), F32)),
        grid_spec=pltpu.PrefetchScalarGridSpec(
            num_scalar_prefetch=7,
            grid=(t // ts,),
            in_specs=[pl.BlockSpec((ts, d), tile_map), pl.BlockSpec((ts, LANES), tile_map),
                      pl.BlockSpec((ts, LANES), tile_map)],
            out_specs=(pl.BlockSpec(memory_space=pl.ANY), pl.BlockSpec(memory_space=pl.ANY),
                       pl.BlockSpec((ts, LANES), tile_map)),
            scratch_shapes=[
                pltpu.VMEM((2, rows, d), BF16),
                pltpu.VMEM((2, rows, LANES), F32),
                pltpu.VMEM((te, d), BF16),
                pltpu.VMEM((te, LANES), F32),
                pltpu.SemaphoreType.DMA((2,)),
                pltpu.SemaphoreType.DMA((2,)),
                pltpu.SemaphoreType.DMA((2,)),
                pltpu.SemaphoreType.DMA((2,)),
            ],
        ),
        compiler_params=_params("arbitrary"),
        name="dispatch",
    )(seg["src"], seg["dst"], seg["size"], seg["tile_tot"], seg["fill_start"], seg["fill_size"], seg["n_used"],
      hb, ids, wts)


def _expert_rows_kernel(te_ref, nu_ref, first_ref, slot_ref, next_ref, x_ref, w_ref, wg_hbm, wu_hbm, wd_hbm, o_ref,
                        wgf, wuf, wdf, wsem, wgb, wub, wdb, *, layer_base):
    i = pl.program_id(0)
    n_used = nu_ref[0]

    def weight_copies(e, sl):
        return (pltpu.make_async_copy(wg_hbm.at[layer_base + e], wgf.at[sl], wsem.at[sl]),
                pltpu.make_async_copy(wu_hbm.at[layer_base + e], wuf.at[sl], wsem.at[sl]),
                pltpu.make_async_copy(wd_hbm.at[layer_base + e], wdf.at[sl], wsem.at[sl]))

    @pl.when(i == 0)
    def _():
        for c in weight_copies(te_ref[0], 0):
            c.start()

    @pl.when(i < n_used)
    def _():
        @pl.when(first_ref[i] == 1)
        def _():
            sl = slot_ref[i]
            for c in weight_copies(te_ref[i], sl):
                c.wait()
            wgb[...] = wgf[sl].astype(BF16)
            wub[...] = wuf[sl].astype(BF16)
            wdb[...] = wdf[sl].astype(BF16)
            nxt = next_ref[i]

            @pl.when(nxt >= 0)
            def _():
                for c in weight_copies(nxt, 1 - sl):
                    c.start()

        x = x_ref[...]
        a = jnp.dot(x, wgb[...], preferred_element_type=F32)
        u = jnp.dot(x, wub[...], preferred_element_type=F32)
        w = jnp.sum(w_ref[...], axis=-1, keepdims=True)
        hid = _silu(a) * u * w
        o_ref[...] = jnp.dot(hid.astype(BF16), wdb[...], preferred_element_type=F32).astype(o_ref.dtype)


def _expert_rows(xe, we, seg, wg, wu, wd, layer_base):
    n_slots, d = xe.shape
    f = wg.shape[-1]
    te = EXPERT_TILE

    def row_map(i, te_ref, nu_ref, *_):
        return (jnp.minimum(i, nu_ref[0] - 1), 0)

    return pl.pallas_call(
        functools.partial(_expert_rows_kernel, layer_base=layer_base),
        out_shape=jax.ShapeDtypeStruct((n_slots, d), BF16),
        grid_spec=pltpu.PrefetchScalarGridSpec(
            num_scalar_prefetch=5,
            grid=(n_slots // te,),
            in_specs=[
                pl.BlockSpec((te, d), row_map),
                pl.BlockSpec((te, LANES), row_map),
                pl.BlockSpec(memory_space=pl.ANY),
                pl.BlockSpec(memory_space=pl.ANY),
                pl.BlockSpec(memory_space=pl.ANY),
            ],
            out_specs=pl.BlockSpec((te, d), row_map),
            scratch_shapes=[
                pltpu.VMEM((2, d, f), F32), pltpu.VMEM((2, d, f), F32), pltpu.VMEM((2, f, d), F32),
                pltpu.SemaphoreType.DMA((2,)),
                pltpu.VMEM((d, f), BF16), pltpu.VMEM((d, f), BF16), pltpu.VMEM((f, d), BF16),
            ],
        ),
        input_output_aliases={5: 0},
        compiler_params=_params("arbitrary"),
        name="experts",
    )(seg["tile_expert"], seg["n_used"], seg["tile_first"], seg["tile_slot"], seg["tile_next"], xe, we, wg, wu, wd)


def _collect_kernel(src_ref, dst_ref, size_ref, tot_ref, y_hbm, x_ref, pos_ref, gp, gs, *rest, npt, n_experts, final):
    if final:
        fg_ref, op_ref, os_ref, buf, sem, o_ref = rest
    else:
        o_ref, buf, sem = rest
    i = pl.program_id(0)
    n = pl.num_programs(0)
    ts = x_ref.shape[0]
    rows = buf.shape[1]
    slot = lax.rem(i, 2)

    def start(tile, sl):
        def body(e, carry):
            k = tile * n_experts + e
            sz = pl.multiple_of(size_ref[k], SEG_ALIGN)
            s0 = pl.multiple_of(src_ref[k], SEG_ALIGN)
            d0 = pl.multiple_of(dst_ref[k], SEG_ALIGN)

            @pl.when(sz > 0)
            def _():
                pltpu.make_async_copy(y_hbm.at[pl.ds(d0, sz), :], buf.at[sl, pl.ds(s0, sz), :], sem.at[sl]).start()

            return carry

        lax.fori_loop(0, n_experts, body, 0)

    @pl.when(i == 0)
    def _():
        buf[...] = jnp.zeros(buf.shape, buf.dtype)
        start(0, 0)

    @pl.when(i + 1 < n)
    def _():
        start(i + 1, 1 - slot)

    tot = pl.multiple_of(tot_ref[i], SEG_ALIGN)
    pltpu.make_async_copy(y_hbm.at[pl.ds(0, tot), :], buf.at[slot, pl.ds(0, tot), :], sem.at[slot]).wait()
    lane = lax.broadcasted_iota(jnp.int32, (ts, LANES), 1).astype(F32)
    pos = pos_ref[...]
    r_iota = lax.broadcasted_iota(jnp.int32, (ts, rows), 1).astype(F32)
    pick = ((r_iota == _lane_pick(pos, lane, 0)) | (r_iota == _lane_pick(pos, lane, 1))).astype(BF16)
    ff = jnp.dot(pick, buf[slot], preferred_element_type=F32)
    _store_by_group(i, npt, None, o_ref, lambda v, m: v[0] + m[0] * v[1], [x_ref[...], ff], [gp], [gs])
    if final:
        y = _rms(o_ref[...], fg_ref[...])

        @pl.when(i < npt)
        def _():
            op_ref[...] = y

        @pl.when(i >= npt)
        def _():
            os_ref[...] = y


def _collect(ye, pos, x, seg, mod_p, mod_s, dims, n_experts, final_g=None):
    t, d = x.shape
    ts = SORT_TILE
    npt = dims["npt"]
    rows = TOP_K * ts + n_experts * SEG_ALIGN
    p_specs, s_specs = _mod_specs((5,), d, npt, dims["tps"], dims["n_seq"], dims["dec_batch"])
    final = final_g is not None

    def strip(spec):
        return pl.BlockSpec(spec.block_shape, lambda i, *_, m=spec.index_map: m(i))

    def tile_map(i, *_):
        return (i, 0)

    in_specs = [
        pl.BlockSpec(memory_space=pl.ANY),
        pl.BlockSpec((ts, d), tile_map),
        pl.BlockSpec((ts, LANES), tile_map),
        *[strip(s) for s in p_specs],
        *[strip(s) for s in s_specs],
    ]
    scratch = [pltpu.VMEM((2, rows, d), BF16), pltpu.SemaphoreType.DMA((2,))]
    args = [seg["src"], seg["dst"], seg["size"], seg["tile_tot"], ye, x, pos, mod_p, mod_s]
    if final:
        in_specs.append(pl.BlockSpec((1, d), lambda i, *_: (0, 0)))
        args.append(final_g.reshape(1, d))
        out_shape = (jax.ShapeDtypeStruct((npt * ts, d), F32), jax.ShapeDtypeStruct((t - npt * ts, d), F32))
        out_specs = (pl.BlockSpec((ts, d), lambda i, *_: (jnp.minimum(i, npt - 1), 0)),
                     pl.BlockSpec((ts, d), lambda i, *_: (jnp.maximum(i - npt, 0), 0)))
        scratch.append(pltpu.VMEM((ts, d), F32))
    else:
        out_shape = jax.ShapeDtypeStruct((t, d), F32)
        out_specs = pl.BlockSpec((ts, d), tile_map)
    return pl.pallas_call(
        functools.partial(_collect_kernel, npt=npt, n_experts=n_experts, final=final),
        out_shape=out_shape,
        grid_spec=pltpu.PrefetchScalarGridSpec(
            num_scalar_prefetch=4, grid=(t // ts,), in_specs=in_specs, out_specs=out_specs, scratch_shapes=scratch),
        compiler_params=_params("arbitrary"),
        name="collect_final" if final else "collect",
    )(*args)


def kernel(x_prompt, x_sample, c_prompt, c_sample, state_conv, state_gla, w_ada, b_ada, norm1_g, norm2_g, w_in, conv_w, conv_b, conv_ln_g, conv_ln_b, gate_w2, gate_b, gla_norm_g, w_out, router_grp_w, router_grp_b, router_exp_w, router_exp_b, exp_w_gate, exp_w_up, exp_w_down, final_norm_g):
    n_seq, seq_len, d = x_prompt.shape
    dec_batch, dec_seq, _ = x_sample.shape
    depth = w_ada.shape[0]
    kw, d_conv = conv_w.shape[1:]
    heads, dv = gla_norm_g.shape[1:]
    rank, qkw = gate_w2.shape[1:]
    dk = qkw // heads
    n_grp, n_exp = router_exp_w.shape[2:]
    n_experts = n_grp * n_exp
    tp, ts = n_seq * seq_len, dec_batch * dec_seq
    t = tp + ts
    tm = ROW_TILE
    assert ts == tm and seq_len % tm == 0 and d_conv == heads * dv and kw - 1 <= CONV_HALO
    n_main = 2 * d_conv + 2 * qkw + 2 * heads * dv
    dims = dict(npt=tp // tm, tps=seq_len // tm, n_seq=n_seq, dec_batch=dec_batch, dec_seq=dec_seq)

    xp, xs, xs_blk = x_prompt.reshape(tp, d), x_sample.transpose(1, 0, 2).reshape(ts, d), 0
    w_in_t = jnp.swapaxes(w_in, 1, 2)
    st_t = jnp.transpose(state_conv, (0, 2, 1, 3))
    pad = (-n_seq) % 8
    c_all = jnp.concatenate([c_prompt, jnp.zeros((pad, d), F32), c_sample], axis=0)
    mod = _ada(c_all, w_ada, b_ada)
    st = SORT_TILE
    assert tm % st == 0 and st % dec_batch == 0
    dims_sort = dict(npt=tp // st, tps=seq_len // st, n_seq=n_seq, dec_batch=dec_batch, dec_seq=dec_seq)
    n_tiles_max = -(-(TOP_K * t + (t // st) * n_experts * (SEG_ALIGN - 1)) // EXPERT_TILE) + n_experts

    conv_p, gla_p, conv_u, gla_s = [], [], [], []
    for l in range(depth):
        mod_p = mod[l, :n_seq].reshape(n_seq, 1, 6 * d)
        mod_s = mod[l, n_seq + pad:]
        gw2 = jnp.pad(gate_w2[l], ((0, LANES - rank), (0, 0))).astype(BF16)
        h, la = _norm1(xp, xs, xs_blk, norm1_g[l], mod_p, mod_s, w_in_t, l, n_main, rank, gw2, gate_b[l], dims)
        z = _inproj(h, w_in_t, l, n_main)

        cv_p, cb_p = _conv_prompt(z, conv_w[l], conv_b[l], conv_ln_g[l], conv_ln_b[l], n_seq, seq_len)
        cv_s, u_s = _conv_sample(z, st_t, l, conv_w[l], conv_b[l], conv_ln_g[l], conv_ln_b[l], tp, dec_seq, dec_batch)
        conv_p.append(cb_p)
        conv_u.append(u_s)

        go_p, gs_p = _gla_prompt(z, la, gla_norm_g[l], n_seq, seq_len, heads, dk, dv)

        def seq_major(a):
            return a.reshape(dec_seq, dec_batch, a.shape[-1]).transpose(1, 0, 2)

        zs = z[tp:]
        q3 = seq_major(zs[:, 2 * d_conv:2 * d_conv + qkw])
        k3 = seq_major(zs[:, 2 * d_conv + qkw:2 * d_conv + 2 * qkw])
        v3 = seq_major(zs[:, 2 * d_conv + 2 * qkw:2 * d_conv + 2 * qkw + heads * dv])
        g3 = seq_major(zs[:, 2 * d_conv + 2 * qkw + heads * dv:n_main])
        go_s, gs_s = _gla_sample(q3, k3, v3, g3, seq_major(la[tp:]), state_gla, l, gla_norm_g[l], heads, dk, dv,
                                 prev_states=tuple(gla_s) if l + 1 == depth else ())
        go_s = go_s.transpose(1, 0, 2).reshape(ts, heads * dv).astype(BF16)
        gla_p.append(gs_p)
        gla_s.append(gs_s)

        x = _outproj(cv_p, go_p, cv_s.reshape(ts, d_conv), go_s, xp, xs, xs_blk, w_out, l, mod_p, mod_s, dims)

        wr = jnp.concatenate([router_grp_w[l], router_exp_w[l].reshape(d, n_experts)], axis=1)
        wr = jnp.pad(wr, ((0, 0), (0, LANES - wr.shape[1])))
        wr_hi = wr.astype(BF16)
        wr_lo = (wr - wr_hi.astype(F32)).astype(BF16)
        br = jnp.concatenate([router_grp_b[l], router_exp_b[l].reshape(-1)])
        br = jnp.pad(br, (0, LANES - br.shape[0])).reshape(1, LANES)
        hb, ids, wts = _norm2(x, norm2_g[l], mod_p, mod_s, wr_hi, wr_lo, br, dims, n_grp, n_exp)
        seg = _segments(ids[:, :TOP_K], n_experts, n_tiles_max)
        xe, we, pos = _dispatch_rows(hb, ids, wts, seg, n_experts, n_tiles_max)
        f = exp_w_gate.shape[-1]
        ye = _expert_rows(xe, we, seg, exp_w_gate.reshape(depth * n_experts, d, f),
                          exp_w_up.reshape(depth * n_experts, d, f), exp_w_down.reshape(depth * n_experts, f, d),
                          l * n_experts)
        if l + 1 < depth:
            x = _collect(ye, pos, x, seg, mod_p, mod_s, dims_sort, n_experts)
            xp, xs, xs_blk = x, x, dims["npt"]
        else:
            y_p, y_s = _collect(ye, pos, x, seg, mod_p, mod_s, dims_sort, n_experts, final_g=final_norm_g)

    y_prompt = y_p.reshape(n_seq, seq_len, d)
    y_sample = y_s.reshape(dec_seq, dec_batch, d).transpose(1, 0, 2)
    conv_s = jnp.transpose(_conv_state_sample(st_t, jnp.stack(conv_u)), (0, 2, 1, 3))
    gla_s_all = gla_s[-1] if depth > 1 else gla_s[0][None]
    return (y_prompt, y_sample, jnp.stack(conv_p), jnp.stack(gla_p), conv_s, gla_s_all)
```

```python
import functools

import jax
import jax.numpy as jnp
from jax import lax
from jax.experimental import pallas as pl
from jax.experimental.pallas import tpu as pltpu

F32 = jnp.float32
BF16 = jnp.bfloat16

EPS = 1e-6
GATE_TAU = 16.0
GLA_CHUNK = 32
GLA_BLOCK = 128
TOP_K = 2

ROW_TILE = 512
EXPERT_TILE = 256
SORT_TILE = 512
SEG_ALIGN = 16
CONV_ROWS = 64
CONV_HALO = 32
LANES = 128
SUBLANES = 8
VMEM_LIMIT = 56 * 1024 * 1024


def _params(*sem):
    return pltpu.CompilerParams(dimension_semantics=sem, vmem_limit_bytes=VMEM_LIMIT)


def _bdot(a, b):
    return jnp.dot(a.astype(BF16), b.astype(BF16), preferred_element_type=F32)


def _round_bf16(x):
    return x.astype(BF16).astype(F32)


def _split3(x):
    hi = x.astype(BF16)
    r = x - hi.astype(F32)
    mid = r.astype(BF16)
    lo = (r - mid.astype(F32)).astype(BF16)
    return hi, mid, lo


def _silu(x):
    return x * jax.nn.sigmoid(x)


def _store_by_group(i, n_prompt_tiles, dec_seq, out_ref, fn, vals, p_refs, s_refs):
    @pl.when(i < n_prompt_tiles)
    def _():
        out_ref[...] = fn(vals, [r[...] for r in p_refs]).astype(out_ref.dtype)

    @pl.when(i >= n_prompt_tiles)
    def _():
        mods = [r[...] for r in s_refs]
        nb = mods[0].shape[0]
        for t in range(out_ref.shape[0] // nb):
            rows = slice(t * nb, (t + 1) * nb)
            out_ref[rows, :] = fn([v[rows] for v in vals], mods).astype(out_ref.dtype)


def _ada_kernel(c_ref, w_ref, b_ref, o_ref):
    c = c_ref[...]
    o_ref[...] = _bdot(_silu(c), w_ref[...]) + b_ref[...]


def _ada(c_all, w_ada, b_ada):
    depth, d, n = w_ada.shape
    rows = c_all.shape[0]
    tn = 1024
    return pl.pallas_call(
        _ada_kernel,
        out_shape=jax.ShapeDtypeStruct((depth, rows, n), F32),
        grid=(depth, n // tn),
        in_specs=[
            pl.BlockSpec((rows, d), lambda l, j: (0, 0)),
            pl.BlockSpec((None, d, tn), lambda l, j: (l, 0, j)),
            pl.BlockSpec((None, 1, tn), lambda l, j: (l, 0, j)),
        ],
        out_specs=pl.BlockSpec((None, rows, tn), lambda l, j: (l, 0, j)),
        compiler_params=_params("arbitrary", "arbitrary"),
        name="ada",
    )(c_all, w_ada, b_ada.reshape(depth, 1, n))


def _mod_specs(cols, width, n_prompt_tiles, tiles_per_seq, n_seq, dec_batch, grid_rank=1, row_axis=0, col_fn=None):
    p_specs, s_specs = [], []
    for c in cols:
        def p_map(*idx, c=c):
            b = jnp.minimum(idx[row_axis] // tiles_per_seq, n_seq - 1)
            return (b, 0, c if col_fn is None else col_fn(c, idx))

        def s_map(*idx, c=c):
            return (0, c if col_fn is None else col_fn(c, idx))

        p_specs.append(pl.BlockSpec((None, 1, width), p_map))
        s_specs.append(pl.BlockSpec((dec_batch, width), s_map))
    return p_specs, s_specs


def _rms(x, g):
    return x * lax.rsqrt(jnp.mean(x * x, axis=-1, keepdims=True) + EPS) * g


def _norm1_kernel(xp_ref, xs_ref, *refs, npt, dec_seq, rank):
    i = pl.program_id(0)
    _norm1_body(i, jnp.where(i < npt, xp_ref[...], xs_ref[...]), *refs, npt=npt, rank=rank)


def _norm1_body(i, x, g_ref, shp, scp, shs, scs, wgl_ref, gw2_ref, gb_ref, h_ref, la_ref, *, npt, rank):
    y = _rms(x, g_ref[...])
    _store_by_group(i, npt, None, h_ref, lambda v, m: v[0] * (1.0 + m[1]) + m[0], [y], [shp, scp], [shs, scs])
    row = lax.broadcasted_iota(jnp.int32, wgl_ref.shape, 0)
    w_gl = jnp.where(row < rank, wgl_ref[...], 0.0).astype(BF16)
    gate_lr = lax.dot_general(h_ref[...], w_gl, (((1,), (1,)), ((), ())), preferred_element_type=F32)
    pre = _bdot(gate_lr, gw2_ref[...]) + gb_ref[...]
    la_ref[...] = (jnp.minimum(pre, 0.0) - jnp.log1p(jnp.exp(-jnp.abs(pre)))) * (1.0 / GATE_TAU)


def _norm1(xp, xs, xs_blk, g, mod_p, mod_s, w_in_t, layer, n_main, rank, gw2, gb, dims):
    d = xp.shape[1]
    tm = ROW_TILE
    npt = dims["npt"]
    t = (npt + 1) * tm
    p_specs, s_specs = _mod_specs((0, 1), d, npt, dims["tps"], dims["n_seq"], dims["dec_batch"])
    qk = gw2.shape[1]
    assert n_main % LANES == 0 and rank <= LANES
    return pl.pallas_call(
        functools.partial(_norm1_kernel, npt=npt, dec_seq=dims["dec_seq"], rank=rank),
        out_shape=(jax.ShapeDtypeStruct((t, d), BF16), jax.ShapeDtypeStruct((t, qk), F32)),
        grid=(t // tm,),
        in_specs=[
            pl.BlockSpec((tm, d), lambda i: (jnp.minimum(i, npt - 1), 0)),
            pl.BlockSpec((tm, d), lambda i: (xs_blk, 0)),
            pl.BlockSpec((1, d), lambda i: (0, 0)),
            *p_specs,
            *s_specs,
            pl.BlockSpec((None, LANES, d), lambda i: (layer, n_main // LANES, 0)),
            pl.BlockSpec(gw2.shape, lambda i: (0, 0)),
            pl.BlockSpec((1, qk), lambda i: (0, 0)),
        ],
        out_specs=(pl.BlockSpec((tm, d), lambda i: (i, 0)), pl.BlockSpec((tm, qk), lambda i: (i, 0))),
        compiler_params=_params("arbitrary"),
        name="norm1",
    )(xp, xs, g.reshape(1, d), mod_p, mod_p, mod_s, mod_s, w_in_t, gw2, gb.reshape(1, qk))


def _inproj_kernel(h_ref, w_ref, o_ref, wb_ref):
    @pl.when(pl.program_id(1) == 0)
    def _():
        wb_ref[...] = w_ref[...].astype(BF16)

    o_ref[...] = lax.dot_general(h_ref[...], wb_ref[...], (((1,), (1,)), ((), ())), preferred_element_type=F32)


def _inproj(h, w_in_t, layer, n_cols):
    t, d = h.shape
    tm, tn = 2 * ROW_TILE, 1024
    return pl.pallas_call(
        _inproj_kernel,
        out_shape=jax.ShapeDtypeStruct((t, n_cols), F32),
        grid=(n_cols // tn, pl.cdiv(t, tm)),
        in_specs=[
            pl.BlockSpec((tm, d), lambda j, i: (i, 0)),
            pl.BlockSpec((None, tn, d), lambda j, i: (layer, j, 0)),
        ],
        out_specs=pl.BlockSpec((tm, tn), lambda j, i: (i, j)),
        scratch_shapes=[pltpu.VMEM((tn, d), BF16)],
        compiler_params=_params("arbitrary", "arbitrary"),
        name="inproj",
    )(h, w_in_t)


def _ln_silu(y, g, b):
    mu = jnp.mean(y, axis=-1, keepdims=True)
    yc = y - mu
    var = jnp.mean(yc * yc, axis=-1, keepdims=True)
    return _silu(yc * lax.rsqrt(var + EPS) * g + b)


def _conv_prompt_kernel(a_ref, b_ref, cw_ref, cb_ref, lng_ref, lnb_ref, o_ref, st_ref, full_ref, cwb_ref, y_ref, *, kw):
    j = pl.program_id(1)
    tm, c = a_ref.shape
    halo = CONV_HALO
    phases = full_ref.shape[0]
    assert phases == SUBLANES

    @pl.when(jnp.logical_and(pl.program_id(0) == 0, j == 0))
    def _():
        full_ref[...] = jnp.zeros(full_ref.shape, F32)
        for w in range(kw):
            cwb_ref[w] = jnp.broadcast_to(cw_ref[w:w + 1, :], (SUBLANES, c))

    prev = full_ref[0, tm + halo - SUBLANES:tm + halo, :]
    tail = jnp.where(j == 0, 0.0, prev)

    @pl.when(j == 0)
    def _():
        for p in range(phases):
            full_ref[p, 0:halo, :] = jnp.zeros((halo, c), F32)

    @pl.when(j > 0)
    def _():
        for p in range(phases):
            full_ref[p, 0:halo, :] = full_ref[p, tm:tm + halo, :]

    u = a_ref[...] * jax.nn.sigmoid(b_ref[...])
    full_ref[0, halo:halo + tm, :] = u
    ext = jnp.concatenate([tail, u], axis=0)
    for p in range(1, phases):
        full_ref[p, halo - SUBLANES:halo - SUBLANES + tm, :] = pltpu.roll(ext, tm + SUBLANES - p, 0)[0:tm]
    off = halo - (kw - 1)
    rb = CONV_ROWS

    def body(r, carry):
        r0 = pl.multiple_of(r * rb, rb)
        for lt in range(c // LANES):
            cols = slice(lt * LANES, (lt + 1) * LANES)
            acc = None
            for p in range(phases):
                x = full_ref[p, pl.ds(r0, rb + halo), cols]
                for a in range(halo // phases + 1):
                    w = a * phases + p - off
                    if 0 <= w < kw and a * phases + rb <= rb + halo:
                        term = x[a * phases:a * phases + rb] * jnp.concatenate([cwb_ref[w, :, cols]] * (rb // SUBLANES), axis=0)
                        acc = term if acc is None else acc + term
            y_ref[pl.ds(r0, rb), cols] = acc
        return carry

    lax.fori_loop(0, tm // rb, body, 0)
    y = _ln_silu(y_ref[...] + cb_ref[...], lng_ref[...], lnb_ref[...])
    o_ref[...] = y.astype(o_ref.dtype)

    @pl.when(j == pl.num_programs(1) - 1)
    def _():
        st_ref[...] = full_ref[0, halo + tm - (kw - 1):halo + tm, :]


def _conv_prompt(z, cw, cb, lng, lnb, n_seq, seq_len):
    kw, c = cw.shape
    tm = ROW_TILE
    tps = seq_len // tm
    return pl.pallas_call(
        functools.partial(_conv_prompt_kernel, kw=kw),
        out_shape=(jax.ShapeDtypeStruct((n_seq * seq_len, c), BF16), jax.ShapeDtypeStruct((n_seq, kw - 1, c), F32)),
        grid=(n_seq, tps),
        in_specs=[
            pl.BlockSpec((tm, c), lambda b, j: (b * tps + j, 0)),
            pl.BlockSpec((tm, c), lambda b, j: (b * tps + j, 1)),
            pl.BlockSpec((kw, c), lambda b, j: (0, 0)),
            pl.BlockSpec((1, c), lambda b, j: (0, 0)),
            pl.BlockSpec((1, c), lambda b, j: (0, 0)),
            pl.BlockSpec((1, c), lambda b, j: (0, 0)),
        ],
        out_specs=(
            pl.BlockSpec((tm, c), lambda b, j: (b * tps + j, 0)),
            pl.BlockSpec((None, kw - 1, c), lambda b, j: (b, 0, 0)),
        ),
        scratch_shapes=[pltpu.VMEM((SUBLANES, tm + CONV_HALO, c), F32), pltpu.VMEM((kw, SUBLANES, c), F32),
                        pltpu.VMEM((tm, c), F32)],
        compiler_params=_params("arbitrary", "arbitrary"),
        name="conv_prompt",
    )(z, z, cw, cb.reshape(1, c), lng.reshape(1, c), lnb.reshape(1, c))


def _conv_sample_kernel(*refs, kw, dec_seq):
    a_refs = refs[0:dec_seq]
    b_refs = refs[dec_seq:2 * dec_seq]
    st_ref, cw_ref, cb_ref, lng_ref, lnb_ref, o_ref, u_ref = refs[2 * dec_seq:]
    hist = kw - 1
    u = [a_refs[t][...] * jax.nn.sigmoid(b_refs[t][...]) for t in range(dec_seq)]
    def row(j):
        return st_ref[j] if j < hist else u[j - hist]

    for t in range(dec_seq):
        acc = row(t) * cw_ref[0:1, :]
        for w in range(1, kw):
            acc = acc + row(t + w) * cw_ref[w:w + 1, :]
        y = _ln_silu(acc + cb_ref[...], lng_ref[...], lnb_ref[...])
        o_ref[t] = y.astype(o_ref.dtype)
        u_ref[t] = u[t]


def _conv_sample(z, st_t, layer, cw, cb, lng, lnb, row0, dec_seq, dec_batch):
    kw, c = cw.shape
    bs = 16
    a_specs = [pl.BlockSpec((bs, c), lambda s, t=t: ((row0 + t * dec_batch) // bs + s, 0)) for t in range(dec_seq)]
    b_specs = [pl.BlockSpec((bs, c), lambda s, t=t: ((row0 + t * dec_batch) // bs + s, 1)) for t in range(dec_seq)]
    vec = pl.BlockSpec((1, c), lambda s: (0, 0))
    return pl.pallas_call(
        functools.partial(_conv_sample_kernel, kw=kw, dec_seq=dec_seq),
        out_shape=(jax.ShapeDtypeStruct((dec_seq, dec_batch, c), BF16), jax.ShapeDtypeStruct((dec_seq, dec_batch, c), F32)),
        grid=(dec_batch // bs,),
        in_specs=[*a_specs, *b_specs, pl.BlockSpec((None, kw - 1, bs, c), lambda s: (layer, 0, s, 0)),
                  pl.BlockSpec((kw, c), lambda s: (0, 0)), vec, vec, vec],
        out_specs=(pl.BlockSpec((dec_seq, bs, c), lambda s: (0, s, 0)), pl.BlockSpec((dec_seq, bs, c), lambda s: (0, s, 0))),
        compiler_params=_params("arbitrary"),
        name="conv_sample",
    )(*([z] * (2 * dec_seq)), st_t, cw, cb.reshape(1, c), lng.reshape(1, c), lnb.reshape(1, c))


def _conv_state_kernel(st_ref, u_ref, o_ref):
    hist, n_new = st_ref.shape[0], u_ref.shape[0]
    for j in range(hist - n_new):
        o_ref[j] = st_ref[j + n_new]
    for t in range(n_new):
        o_ref[hist - n_new + t] = u_ref[t]


def _conv_state_sample(st_t, u_all):
    depth, hist, nb, c = st_t.shape
    n_new = u_all.shape[1]
    assert n_new <= hist
    bs = 16
    return pl.pallas_call(
        _conv_state_kernel,
        out_shape=jax.ShapeDtypeStruct(st_t.shape, st_t.dtype),
        grid=(depth, nb // bs),
        in_specs=[pl.BlockSpec((None, hist, bs, c), lambda l, s: (l, 0, s, 0)),
                  pl.BlockSpec((None, n_new, bs, c), lambda l, s: (l, 0, s, 0))],
        out_specs=pl.BlockSpec((None, hist, bs, c), lambda l, s: (l, 0, s, 0)),
        compiler_params=_params("arbitrary", "arbitrary"),
        name="conv_state",
    )(st_t, u_all)


def _gla_prompt_kernel(qk_ref, v_ref, g_ref, la_ref, gn_ref, o_ref, sfin_ref, st_ref, sn_ref, *, heads, dk, dv):
    j = pl.program_id(1)
    tm = qk_ref.shape[0]
    ck = GLA_CHUNK
    nch = tm // ck
    qkw = heads * dk

    @pl.when(j == 0)
    def _():
        st_ref[...] = jnp.zeros(st_ref.shape, F32)

    la = la_ref[...]
    sb = GLA_BLOCK
    blocks = [slice(r, r + sb) for r in range(0, tm, sb)]
    row = lax.broadcasted_iota(jnp.int32, (sb, sb), 0)
    col = lax.broadcasted_iota(jnp.int32, (sb, sb), 1)
    same = (row // ck) == (col // ck)
    causal = same & (col <= row)
    tri_incl = causal.astype(BF16)
    tri_after = (same & (col > row)).astype(BF16)
    sel = (lax.broadcasted_iota(jnp.int32, (nch, tm), 1) // ck == lax.broadcasted_iota(jnp.int32, (nch, tm), 0)).astype(BF16)
    parts = _split3(la)

    def blockwise(tri):
        return jnp.concatenate([sum(jnp.dot(tri, p[rs], preferred_element_type=F32) for p in parts) for rs in blocks], axis=0)

    b = blockwise(tri_incl)
    rest = blockwise(tri_after)
    tot = sum(jnp.dot(sel, p, preferred_element_type=F32) for p in parts)
    qk = qk_ref[...]
    q = qk[:, :qkw] * (dk ** -0.5)
    k = qk[:, qkw:]
    q_dec = (q * jnp.exp(b)).astype(BF16)
    k_inv = (k * jnp.exp(-b)).astype(BF16)
    k_end = _round_bf16(k * jnp.exp(rest))
    decay = jnp.exp(tot)
    v_all = v_ref[...]
    g_all = g_ref[...]
    for h in range(heads):
        ks = slice(h * dk, (h + 1) * dk)
        vs = slice(h * dv, (h + 1) * dv)
        qd, ki, ke = q_dec[:, ks], k_inv[:, ks], k_end[:, ks]
        vh = v_all[:, vs]
        vb = vh.astype(BF16)
        vr = _round_bf16(vh)
        intra = []
        for rs in blocks:
            att = lax.dot_general(qd[rs], ki[rs], (((1,), (1,)), ((), ())), preferred_element_type=F32)
            att = jnp.where(causal, att, 0.0).astype(BF16)
            intra.append(jnp.dot(att, vb[rs], preferred_element_type=F32))
        o = jnp.concatenate(intra, axis=0)
        s = st_ref[h]
        for n in range(nch):
            rs = slice(n * ck, (n + 1) * ck)
            sn_ref[n] = s.astype(BF16)
            upd = lax.dot_general(vr[rs], ke[rs], (((0,), (0,)), ((), ())), preferred_element_type=F32)
            s = s * decay[n:n + 1, ks] + upd
        st_ref[h] = s
        inter = [lax.dot_general(qd[n * ck:(n + 1) * ck], sn_ref[n], (((1,), (1,)), ((), ())), preferred_element_type=F32)
                 for n in range(nch)]
        o = o + jnp.concatenate(inter, axis=0)
        o = _rms(o, gn_ref[:, vs]) * _silu(g_all[:, vs])
        o_ref[:, vs] = o.astype(o_ref.dtype)

    @pl.when(j == pl.num_programs(1) - 1)
    def _():
        for h in range(heads):
            sfin_ref[h] = st_ref[h].T


def _gla_prompt(z, la, gn, n_seq, seq_len, heads, dk, dv):
    tm = ROW_TILE
    tps = seq_len // tm
    w = heads * dv
    qkw = heads * dk
    assert 2 * qkw == w
    return pl.pallas_call(
        functools.partial(_gla_prompt_kernel, heads=heads, dk=dk, dv=dv),
        out_shape=(jax.ShapeDtypeStruct((n_seq * seq_len, w), BF16), jax.ShapeDtypeStruct((n_seq, heads, dk, dv), F32)),
        grid=(n_seq, tps),
        in_specs=[
            pl.BlockSpec((tm, w), lambda b, j: (b * tps + j, 2)),
            pl.BlockSpec((tm, w), lambda b, j: (b * tps + j, 3)),
            pl.BlockSpec((tm, w), lambda b, j: (b * tps + j, 4)),
            pl.BlockSpec((tm, qkw), lambda b, j: (b * tps + j, 0)),
            pl.BlockSpec((1, w), lambda b, j: (0, 0)),
        ],
        out_specs=(
            pl.BlockSpec((tm, w), lambda b, j: (b * tps + j, 0)),
            pl.BlockSpec((None, heads, dk, dv), lambda b, j: (b, 0, 0, 0)),
        ),
        scratch_shapes=[pltpu.VMEM((heads, dv, dk), F32), pltpu.VMEM((tm // GLA_CHUNK, dv, dk), BF16)],
        compiler_params=_params("arbitrary", "arbitrary"),
        name="gla_prompt",
    )(z, z, z, la, gn.reshape(1, w))


def _gla_sample_kernel(q_ref, k_ref, v_ref, g_ref, la_ref, s_ref, gn_ref, *rest, heads, dk, dv, n_prev):
    bs, ln, _ = q_ref.shape
    if n_prev:
        o_ref, all_ref = rest[n_prev:]
        for p in range(n_prev):
            all_ref[p] = rest[p][...]
        ns_ref = all_ref.at[n_prev]
    else:
        o_ref, ns_ref = rest
    tril = lax.broadcasted_iota(jnp.int32, (ln, ln), 1) <= lax.broadcasted_iota(jnp.int32, (ln, ln), 0)
    for s in range(bs):
        q_s, k_s, v_s, g_s, la_s = q_ref[s], k_ref[s], v_ref[s], g_ref[s], la_ref[s]
        for h in range(heads):
            ks = slice(h * dk, (h + 1) * dk)
            vs = slice(h * dv, (h + 1) * dv)
            la = la_s[:, ks]
            rows = [la[0:1]]
            for t in range(1, ln):
                rows.append(rows[-1] + la[t:t + 1])
            b = jnp.concatenate(rows, axis=0)
            b_last = rows[-1]
            q_dec = _round_bf16(q_s[:, ks] * (dk ** -0.5) * jnp.exp(b))
            k_inv = _round_bf16(k_s[:, ks] * jnp.exp(-b))
            k_end = _round_bf16(k_s[:, ks] * jnp.exp(b_last - b))
            vr = _round_bf16(v_s[:, vs])
            s0 = s_ref[s, h]
            att = lax.dot_general(q_dec, k_inv, (((1,), (1,)), ((), ())), preferred_element_type=F32)
            att = _round_bf16(jnp.where(tril, att, 0.0))
            o = jnp.dot(att, vr, preferred_element_type=F32) + jnp.dot(q_dec, _round_bf16(s0), preferred_element_type=F32)
            upd = lax.dot_general(k_end, vr, (((0,), (0,)), ((), ())), preferred_element_type=F32)
            d_col = jnp.broadcast_to(jnp.exp(b_last), (dk, dk)).T
            ns_ref[s, h] = s0 * jnp.concatenate([d_col] * (dv // dk), axis=1) + upd
            o = _rms(o, gn_ref[:, vs]) * _silu(g_s[:, vs])
            o_ref[s, :, vs] = o.astype(o_ref.dtype)


def _gla_sample(q3, k3, v3, g3, la3, state, layer, gn, heads, dk, dv, prev_states=()):
    nb, ln, w = v3.shape
    qkw = heads * dk
    bs = 8
    n_prev = len(prev_states)
    seq_spec = pl.BlockSpec((bs, ln, qkw), lambda s: (s, 0, 0))
    wide_spec = pl.BlockSpec((bs, ln, w), lambda s: (s, 0, 0))
    in_specs = [seq_spec, seq_spec, wide_spec, wide_spec, seq_spec,
                pl.BlockSpec((None, bs, heads, dk, dv), lambda s: (layer, s, 0, 0, 0)),
                pl.BlockSpec((1, w), lambda s: (0, 0)),
                *[pl.BlockSpec((bs, heads, dk, dv), lambda s: (s, 0, 0, 0))] * n_prev]
    if n_prev:
        state_shape = jax.ShapeDtypeStruct((n_prev + 1, nb, heads, dk, dv), F32)
        state_spec = pl.BlockSpec((n_prev + 1, bs, heads, dk, dv), lambda s: (0, s, 0, 0, 0))
    else:
        state_shape = jax.ShapeDtypeStruct((nb, heads, dk, dv), F32)
        state_spec = pl.BlockSpec((bs, heads, dk, dv), lambda s: (s, 0, 0, 0))
    return pl.pallas_call(
        functools.partial(_gla_sample_kernel, heads=heads, dk=dk, dv=dv, n_prev=n_prev),
        out_shape=(jax.ShapeDtypeStruct((nb, ln, w), F32), state_shape),
        grid=(nb // bs,),
        in_specs=in_specs,
        out_specs=(wide_spec, state_spec),
        compiler_params=_params("arbitrary"),
        name="gla_sample_last" if n_prev else "gla_sample",
    )(q3, k3, v3, g3, la3, state, gn.reshape(1, w), *prev_states)


def _outproj_kernel(cp_ref, op_ref, cs_ref, os_ref, xp_ref, xs_ref, w_ref, gp, gs, y_ref, wb_ref, *, npt, dec_seq):
    i = pl.program_id(1)

    @pl.when(i == 0)
    def _():
        wb_ref[...] = w_ref[...].astype(BF16)

    half = cp_ref.shape[1]

    def mixed(c_ref, o_ref):
        return (jnp.dot(c_ref[...], wb_ref[0:half, :], preferred_element_type=F32)
                + jnp.dot(o_ref[...], wb_ref[half:, :], preferred_element_type=F32))

    @pl.when(i < npt)
    def _():
        y_ref[...] = xp_ref[...] + gp[...] * mixed(cp_ref, op_ref)

    @pl.when(i >= npt)
    def _():
        mix = mixed(cs_ref, os_ref)
        nb = gs.shape[0]
        for t in range(dec_seq):
            rows = slice(t * nb, (t + 1) * nb)
            y_ref[rows, :] = xs_ref[rows, :] + gs[...] * mix[rows]


def _outproj(conv_p, gla_p, conv_s, gla_s, xp, xs, xs_blk, w_out, layer, mod_p, mod_s, dims):
    d = xp.shape[1]
    half = conv_p.shape[1]
    tm, tn = ROW_TILE, 1024
    nj = d // tn
    npt = dims["npt"]
    t = (npt + 1) * tm
    p_specs, s_specs = _mod_specs((2,), tn, dims["npt"], dims["tps"], dims["n_seq"], dims["dec_batch"],
                                  row_axis=1, col_fn=lambda c, idx: c * nj + idx[0])
    return pl.pallas_call(
        functools.partial(_outproj_kernel, npt=dims["npt"], dec_seq=dims["dec_seq"]),
        out_shape=jax.ShapeDtypeStruct((t, d), F32),
        grid=(nj, t // tm),
        in_specs=[
            pl.BlockSpec((tm, half), lambda j, i: (jnp.minimum(i, npt - 1), 0)),
            pl.BlockSpec((tm, half), lambda j, i: (jnp.minimum(i, npt - 1), 0)),
            pl.BlockSpec((tm, half), lambda j, i: (0, 0)),
            pl.BlockSpec((tm, half), lambda j, i: (0, 0)),
            pl.BlockSpec((tm, tn), lambda j, i: (jnp.minimum(i, npt - 1), j)),
            pl.BlockSpec((tm, tn), lambda j, i: (xs_blk, j)),
            pl.BlockSpec((None, d, tn), lambda j, i: (layer, 0, j)),
            *p_specs,
            *s_specs,
        ],
        out_specs=pl.BlockSpec((tm, tn), lambda j, i: (i, j)),
        scratch_shapes=[pltpu.VMEM((d, tn), BF16)],
        compiler_params=_params("arbitrary", "arbitrary"),
        name="outproj",
    )(conv_p, gla_p, conv_s, gla_s, xp, xs, w_out, mod_p, mod_s)


def _norm2_kernel(x_ref, g_ref, shp, scp, shs, scs, wr_hi, wr_lo, br_ref, hb_ref, ids_ref, wts_ref, h_ref, *, npt, dec_seq, n_grp, n_exp):
    i = pl.program_id(0)
    y = _rms(x_ref[...], g_ref[...])
    _store_by_group(i, npt, dec_seq, h_ref, lambda v, m: v[0] * (1.0 + m[1]) + m[0], [y], [shp, scp], [shs, scs])
    h = h_ref[...]
    h_hi = h.astype(BF16)
    hb_ref[...] = h_hi
    h_lo = (h - h_hi.astype(F32)).astype(BF16)
    logits = (jnp.dot(h_hi, wr_hi[...], preferred_element_type=F32) + jnp.dot(h_lo, wr_hi[...], preferred_element_type=F32)
              + jnp.dot(h_hi, wr_lo[...], preferred_element_type=F32)) + br_ref[...]
    lane = lax.broadcasted_iota(jnp.int32, logits.shape, 1).astype(F32)
    big = jnp.float32(LANES)
    neg = jnp.float32(-jnp.inf)
    gl = jnp.where(lane < n_grp, logits, neg)
    gmax = jnp.max(gl, axis=-1, keepdims=True)
    gidx = jnp.min(jnp.where(gl == gmax, lane, big), axis=-1, keepdims=True)
    g_w = 1.0 / jnp.sum(jnp.exp(gl - gmax), axis=-1, keepdims=True)
    lo = n_grp + gidx * n_exp
    in_grp = (lane >= lo) & (lane < lo + n_exp)
    sl = jnp.where(in_grp, logits, neg)
    p = jnp.exp(sl - jnp.max(sl, axis=-1, keepdims=True))
    p = p / jnp.sum(p, axis=-1, keepdims=True)
    p = jnp.where(in_grp, p, -1.0)
    p1 = jnp.max(p, axis=-1, keepdims=True)
    i1 = jnp.min(jnp.where(p == p1, lane, big), axis=-1, keepdims=True)
    p_rest = jnp.where(lane == i1, -1.0, p)
    p2 = jnp.max(p_rest, axis=-1, keepdims=True)
    i2 = jnp.min(jnp.where(p_rest == p2, lane, big), axis=-1, keepdims=True)
    denom = p1 + p2
    ids_ref[...] = jnp.where(lane == 0, i1 - n_grp, jnp.where(lane == 1, i2 - n_grp, 0.0)).astype(jnp.int32)
    wts_ref[...] = jnp.where(lane == 0, g_w * (p1 / denom), jnp.where(lane == 1, g_w * (p2 / denom), 0.0))


def _norm2(x, g, mod_p, mod_s, wr_hi, wr_lo, br, dims, n_grp, n_exp):
    t, d = x.shape
    tm = ROW_TILE
    p_specs, s_specs = _mod_specs((3, 4), d, dims["npt"], dims["tps"], dims["n_seq"], dims["dec_batch"])
    return pl.pallas_call(
        functools.partial(_norm2_kernel, npt=dims["npt"], dec_seq=dims["dec_seq"], n_grp=n_grp, n_exp=n_exp),
        out_shape=(jax.ShapeDtypeStruct((t, d), BF16), jax.ShapeDtypeStruct((t, LANES), jnp.int32),
                   jax.ShapeDtypeStruct((t, LANES), F32)),
        grid=(t // tm,),
        in_specs=[
            pl.BlockSpec((tm, d), lambda i: (i, 0)),
            pl.BlockSpec((1, d), lambda i: (0, 0)),
            *p_specs,
            *s_specs,
            pl.BlockSpec((d, LANES), lambda i: (0, 0)),
            pl.BlockSpec((d, LANES), lambda i: (0, 0)),
            pl.BlockSpec((1, LANES), lambda i: (0, 0)),
        ],
        out_specs=(pl.BlockSpec((tm, d), lambda i: (i, 0)), pl.BlockSpec((tm, LANES), lambda i: (i, 0)),
                   pl.BlockSpec((tm, LANES), lambda i: (i, 0))),
        scratch_shapes=[pltpu.VMEM((tm, d), F32)],
        compiler_params=_params("arbitrary"),
        name="norm2_router",
    )(x, g.reshape(1, d), mod_p, mod_p, mod_s, mod_s, wr_hi, wr_lo, br)


def _segments(ids, n_experts, n_tiles_max):
    t = ids.shape[0]
    ts, te, al = SORT_TILE, EXPERT_TILE, SEG_ALIGN
    nt = t // ts
    e_iota = jnp.arange(n_experts, dtype=jnp.int32)
    cnt = jnp.sum((ids.reshape(nt, ts * TOP_K, 1) == e_iota).astype(jnp.int32), axis=1)
    size = (cnt + al - 1) // al * al
    src = jnp.cumsum(size, axis=1) - size
    tot = jnp.sum(size, axis=0)
    tot_pad = (tot + te - 1) // te * te
    ends = jnp.cumsum(tot_pad)
    exp_off = ends - tot_pad
    dst = exp_off[None, :] + jnp.cumsum(size, axis=0) - size
    n_used = (ends[-1] // te).astype(jnp.int32)
    tile_start = jnp.arange(n_tiles_max, dtype=jnp.int32) * te
    tile_expert = jnp.minimum(jnp.sum(ends[None, :] <= tile_start[:, None], axis=1), n_experts - 1).astype(jnp.int32)
    last = tile_expert[jnp.maximum(n_used - 1, 0)]
    used = jnp.arange(n_tiles_max) < n_used
    tile_expert = jnp.where(used, tile_expert, last)
    prev_e = jnp.concatenate([jnp.full((1,), -1, jnp.int32), tile_expert[:-1]])
    tile_first = (used & (tile_expert != prev_e)).astype(jnp.int32)
    tile_slot = ((jnp.cumsum(tile_first) - 1) % 2).astype(jnp.int32)
    next_start = jnp.sum(used[None, :] & (tile_expert[None, :] <= tile_expert[:, None]), axis=1)
    tile_next = jnp.where(next_start < n_used, tile_expert[jnp.minimum(next_start, n_tiles_max - 1)], -1).astype(jnp.int32)
    return dict(tile_first=tile_first, tile_slot=tile_slot, tile_next=tile_next,
                src=src.reshape(-1).astype(jnp.int32), dst=dst.reshape(-1).astype(jnp.int32),
                size=size.reshape(-1).astype(jnp.int32), tile_tot=jnp.sum(size, axis=1).astype(jnp.int32),
                fill_start=(exp_off + tot).astype(jnp.int32), fill_size=(tot_pad - tot).astype(jnp.int32),
                tile_expert=tile_expert, n_used=n_used.reshape(1))


def _lane_pick(x, lane, k):
    return jnp.sum(jnp.where(lane == k, x, 0.0), axis=-1, keepdims=True)


def _sorted_positions(ids, lane):
    ts = ids.shape[0]
    onehot = [(lane == _lane_pick(ids, lane, k)).astype(F32) for k in range(TOP_K)]
    row = lax.broadcasted_iota(jnp.int32, (ts, ts), 0)
    col = lax.broadcasted_iota(jnp.int32, (ts, ts), 1)
    before = (col < row).astype(BF16)
    earlier = [jnp.dot(before, o.astype(BF16), preferred_element_type=F32) for o in onehot]
    cnt = [jnp.sum(o, axis=0, keepdims=True) for o in onehot]
    size = jnp.ceil((cnt[0] + cnt[1]) * (1.0 / SEG_ALIGN)) * SEG_ALIGN
    er = lax.broadcasted_iota(jnp.int32, (LANES, LANES), 0)
    ec = lax.broadcasted_iota(jnp.int32, (LANES, LANES), 1)
    start = jnp.dot(jnp.broadcast_to(size, (SUBLANES, LANES)).astype(BF16), (er < ec).astype(BF16),
                    preferred_element_type=F32)[0:1]
    base = [start + earlier[0], start + cnt[0] + earlier[1]]
    return [jnp.sum(onehot[k] * base[k], axis=-1, keepdims=True) for k in range(TOP_K)]


def _as_row(col_vals, lane):
    hi = jnp.floor(col_vals * (1.0 / 64.0))
    lo = col_vals - hi * 64.0
    ones = jnp.ones((SUBLANES, LANES), BF16)
    nt = (((1,), (1,)), ((), ()))
    hi_row = lax.dot_general(ones, jnp.where(lane == 0, hi, 0.0).astype(BF16), nt, preferred_element_type=F32)
    lo_row = lax.dot_general(ones, jnp.where(lane == 0, lo, 0.0).astype(BF16), nt, preferred_element_type=F32)
    return (hi_row * 64.0 + lo_row)[0:1]


def _dispatch_kernel(src_ref, dst_ref, size_ref, tot_ref, fst_ref, fsz_ref, nu_ref, h_ref, ids_ref, wts_ref, xe_hbm,
                     we_hbm, pos_ref, sbuf, wbuf, zx, zw, semx, semw, semz, semt, *, n_experts):
    i = pl.program_id(0)
    n = pl.num_programs(0)
    ts = h_ref.shape[0]
    rows = sbuf.shape[1]
    te = zx.shape[0]
    slot = lax.rem(i, 2)

    def tail_copies(tl):
        r0 = pl.multiple_of(tl * te, te)
        return (pltpu.make_async_copy(zx, xe_hbm.at[pl.ds(r0, te), :], semt.at[0]),
                pltpu.make_async_copy(zw, we_hbm.at[pl.ds(r0, te), :], semt.at[1]))

    n_tail_tiles = xe_hbm.shape[0] // te

    def for_expert_tails(action):
        def body(e, carry):
            sz = pl.multiple_of(fsz_ref[e], SEG_ALIGN)
            d0 = pl.multiple_of(fst_ref[e], SEG_ALIGN)

            @pl.when(sz > 0)
            def _():
                action(pltpu.make_async_copy(zx.at[pl.ds(0, sz), :], xe_hbm.at[pl.ds(d0, sz), :], semz.at[0]))
                action(pltpu.make_async_copy(zw.at[pl.ds(0, sz), :], we_hbm.at[pl.ds(d0, sz), :], semz.at[1]))

            return carry

        lax.fori_loop(0, n_experts, body, 0)

    def for_tail(action, first):
        def body(tl, carry):
            for c in tail_copies(tl):
                action(c)
            return carry

        lax.fori_loop(first, n_tail_tiles, body, 0)

    def seg_copies(tile, sl, e):
        k = tile * n_experts + e
        sz = pl.multiple_of(size_ref[k], SEG_ALIGN)
        s0 = pl.multiple_of(src_ref[k], SEG_ALIGN)
        d0 = pl.multiple_of(dst_ref[k], SEG_ALIGN)
        return sz, (pltpu.make_async_copy(sbuf.at[sl, pl.ds(s0, sz), :], xe_hbm.at[pl.ds(d0, sz), :], semx.at[sl]),
                    pltpu.make_async_copy(wbuf.at[sl, pl.ds(s0, sz), :], we_hbm.at[pl.ds(d0, sz), :], semw.at[sl]))

    def wait_tile(tile, sl):
        tot = pl.multiple_of(tot_ref[tile], SEG_ALIGN)
        pltpu.make_async_copy(sbuf.at[sl, pl.ds(0, tot), :], xe_hbm.at[pl.ds(0, tot), :], semx.at[sl]).wait()
        pltpu.make_async_copy(wbuf.at[sl, pl.ds(0, tot), :], we_hbm.at[pl.ds(0, tot), :], semw.at[sl]).wait()

    @pl.when(i == 0)
    def _():
        zx[...] = jnp.zeros(zx.shape, zx.dtype)
        zw[...] = jnp.zeros(zw.shape, zw.dtype)
        for_expert_tails(lambda c: c.start())

    @pl.when(i >= 2)
    def _():
        wait_tile(i - 2, slot)

    lane = lax.broadcasted_iota(jnp.int32, (ts, LANES), 1).astype(F32)
    pos = _sorted_positions(ids_ref[...].astype(F32), lane)
    pos_ref[...] = jnp.where(lane == 0, pos[0], jnp.where(lane == 1, pos[1], 0.0))
    r_iota = lax.broadcasted_iota(jnp.int32, (rows, ts), 0).astype(F32)
    assert rows <= 4096
    sel = [(r_iota == _as_row(p, lane)).astype(BF16) for p in pos]
    sbuf[slot] = jnp.dot(sel[0] + sel[1], h_ref[...], preferred_element_type=F32).astype(BF16)
    wts = wts_ref[...]
    wsorted = jnp.zeros((rows, LANES), F32)
    for k in range(TOP_K):
        pieces = _split3(_lane_pick(wts, lane, k))
        wk = jnp.where(lane == 0, pieces[0].astype(F32), jnp.where(lane == 1, pieces[1].astype(F32),
                       jnp.where(lane == 2, pieces[2].astype(F32), 0.0)))
        wsorted = wsorted + jnp.dot(sel[k], wk.astype(BF16), preferred_element_type=F32)
    wbuf[slot] = wsorted

    def issue(e, carry):
        sz, copies = seg_copies(i, slot, e)

        @pl.when(sz > 0)
        def _():
            for c in copies:
                c.start()

        return carry

    lax.fori_loop(0, n_experts, issue, 0)

    tail_tile = nu_ref[0] + i

    @pl.when(jnp.logical_and(i < n - 1, tail_tile < n_tail_tiles))
    def _():
        for c in tail_copies(tail_tile):
            c.start()

    @pl.when(i == n - 1)
    def _():
        for_tail(lambda c: c.start(), tail_tile)

        @pl.when(i >= 1)
        def _():
            wait_tile(i - 1, 1 - slot)

        wait_tile(i, slot)
        for_tail(lambda c: c.wait(), nu_ref[0])
        for_expert_tails(lambda c: c.wait())


def _dispatch_rows(hb, ids, wts, seg, n_experts, n_tiles_max):
    t, d = hb.shape
    ts, te = SORT_TILE, EXPERT_TILE
    rows = TOP_K * ts + n_experts * SEG_ALIGN
    n_slots = n_tiles_max * te

    def tile_map(i, *_):
        return (i, 0)

    return pl.pallas_call(
        functools.partial(_dispatch_kernel, n_experts=n_experts),
        out_shape=(jax.ShapeDtypeStruct((n_slots, d), BF16), jax.ShapeDtypeStruct((n_slots, LANES), F32),
                   jax.ShapeDtypeStruct((t, LANES), F32)),
        grid_spec=pltpu.PrefetchScalarGridSpec(
            num_scalar_prefetch=7,
            grid=(t // ts,),
            in_specs=[pl.BlockSpec((ts, d), tile_map), pl.BlockSpec((ts, LANES), tile_map),
                      pl.BlockSpec((ts, LANES), tile_map)],
            out_specs=(pl.BlockSpec(memory_space=pl.ANY), pl.BlockSpec(memory_space=pl.ANY),
                       pl.BlockSpec((ts, LANES), tile_map)),
            scratch_shapes=[
                pltpu.VMEM((2, rows, d), BF16),
                pltpu.VMEM((2, rows, LANES), F32),
                pltpu.VMEM((te, d), BF16),
                pltpu.VMEM((te, LANES), F32),
                pltpu.SemaphoreType.DMA((2,)),
                pltpu.SemaphoreType.DMA((2,)),
                pltpu.SemaphoreType.DMA((2,)),
                pltpu.SemaphoreType.DMA((2,)),
            ],
        ),
        compiler_params=_params("arbitrary"),
        name="dispatch",
    )(seg["src"], seg["dst"], seg["size"], seg["tile_tot"], seg["fill_start"], seg["fill_size"], seg["n_used"],
      hb, ids, wts)


def _expert_rows_kernel(te_ref, nu_ref, first_ref, slot_ref, next_ref, x_ref, w_ref, wg_hbm, wu_hbm, wd_hbm, o_ref,
                        wgf, wuf, wdf, wsem, wgb, wub, wdb, *, layer_base):
    i = pl.program_id(0)
    n_used = nu_ref[0]

    def weight_copies(e, sl):
        return (pltpu.make_async_copy(wg_hbm.at[layer_base + e], wgf.at[sl], wsem.at[sl]),
                pltpu.make_async_copy(wu_hbm.at[layer_base + e], wuf.at[sl], wsem.at[sl]),
                pltpu.make_async_copy(wd_hbm.at[layer_base + e], wdf.at[sl], wsem.at[sl]))

    @pl.when(i == 0)
    def _():
        for c in weight_copies(te_ref[0], 0):
            c.start()

    @pl.when(i < n_used)
    def _():
        @pl.when(first_ref[i] == 1)
        def _():
            sl = slot_ref[i]
            for c in weight_copies(te_ref[i], sl):
                c.wait()
            wgb[...] = wgf[sl].astype(BF16)
            wub[...] = wuf[sl].astype(BF16)
            wdb[...] = wdf[sl].astype(BF16)
            nxt = next_ref[i]

            @pl.when(nxt >= 0)
            def _():
                for c in weight_copies(nxt, 1 - sl):
                    c.start()

        x = x_ref[...]
        a = jnp.dot(x, wgb[...], preferred_element_type=F32)
        u = jnp.dot(x, wub[...], preferred_element_type=F32)
        w = jnp.sum(w_ref[...], axis=-1, keepdims=True)
        hid = _silu(a) * u * w
        o_ref[...] = jnp.dot(hid.astype(BF16), wdb[...], preferred_element_type=F32).astype(o_ref.dtype)


def _expert_rows(xe, we, seg, wg, wu, wd, layer_base):
    n_slots, d = xe.shape
    f = wg.shape[-1]
    te = EXPERT_TILE

    def row_map(i, te_ref, nu_ref, *_):
        return (jnp.minimum(i, nu_ref[0] - 1), 0)

    return pl.pallas_call(
        functools.partial(_expert_rows_kernel, layer_base=layer_base),
        out_shape=jax.ShapeDtypeStruct((n_slots, d), BF16),
        grid_spec=pltpu.PrefetchScalarGridSpec(
            num_scalar_prefetch=5,
            grid=(n_slots // te,),
            in_specs=[
                pl.BlockSpec((te, d), row_map),
                pl.BlockSpec((te, LANES), row_map),
                pl.BlockSpec(memory_space=pl.ANY),
                pl.BlockSpec(memory_space=pl.ANY),
                pl.BlockSpec(memory_space=pl.ANY),
            ],
            out_specs=pl.BlockSpec((te, d), row_map),
            scratch_shapes=[
                pltpu.VMEM((2, d, f), F32), pltpu.VMEM((2, d, f), F32), pltpu.VMEM((2, f, d), F32),
                pltpu.SemaphoreType.DMA((2,)),
                pltpu.VMEM((d, f), BF16), pltpu.VMEM((d, f), BF16), pltpu.VMEM((f, d), BF16),
            ],
        ),
        input_output_aliases={5: 0},
        compiler_params=_params("arbitrary"),
        name="experts",
    )(seg["tile_expert"], seg["n_used"], seg["tile_first"], seg["tile_slot"], seg["tile_next"], xe, we, wg, wu, wd)


def _collect_kernel(src_ref, dst_ref, size_ref, tot_ref, y_hbm, x_ref, pos_ref, gp, gs, *rest, npt, n_experts, final,
                    rank):
    if final:
        fg_ref, op_ref, os_ref, buf, sem, o_ref = rest
    else:
        n1_refs = rest[:8]
        o_ref, h_ref, la_ref, buf, sem = rest[8:]
    i = pl.program_id(0)
    n = pl.num_programs(0)
    ts = x_ref.shape[0]
    rows = buf.shape[1]
    slot = lax.rem(i, 2)

    def start(tile, sl):
        def body(e, carry):
            k = tile * n_experts + e
            sz = pl.multiple_of(size_ref[k], SEG_ALIGN)
            s0 = pl.multiple_of(src_ref[k], SEG_ALIGN)
            d0 = pl.multiple_of(dst_ref[k], SEG_ALIGN)

            @pl.when(sz > 0)
            def _():
                pltpu.make_async_copy(y_hbm.at[pl.ds(d0, sz), :], buf.at[sl, pl.ds(s0, sz), :], sem.at[sl]).start()

            return carry

        lax.fori_loop(0, n_experts, body, 0)

    @pl.when(i == 0)
    def _():
        buf[...] = jnp.zeros(buf.shape, buf.dtype)
        start(0, 0)

    @pl.when(i + 1 < n)
    def _():
        start(i + 1, 1 - slot)

    tot = pl.multiple_of(tot_ref[i], SEG_ALIGN)
    pltpu.make_async_copy(y_hbm.at[pl.ds(0, tot), :], buf.at[slot, pl.ds(0, tot), :], sem.at[slot]).wait()
    lane = lax.broadcasted_iota(jnp.int32, (ts, LANES), 1).astype(F32)
    pos = pos_ref[...]
    r_iota = lax.broadcasted_iota(jnp.int32, (ts, rows), 1).astype(F32)
    pick = ((r_iota == _lane_pick(pos, lane, 0)) | (r_iota == _lane_pick(pos, lane, 1))).astype(BF16)
    ff = jnp.dot(pick, buf[slot], preferred_element_type=F32)
    _store_by_group(i, npt, None, o_ref, lambda v, m: v[0] + m[0] * v[1], [x_ref[...], ff], [gp], [gs])
    if final:
        y = _rms(o_ref[...], fg_ref[...])

        @pl.when(i < npt)
        def _():
            op_ref[...] = y

        @pl.when(i >= npt)
        def _():
            os_ref[...] = y
    else:
        _norm1_body(i, o_ref[...], *n1_refs, h_ref, la_ref, npt=npt, rank=rank)


def _collect(ye, pos, x, seg, mod_p, mod_s, dims, n_experts, final_g=None, next_norm=None):
    t, d = x.shape
    ts = SORT_TILE
    npt = dims["npt"]
    rows = TOP_K * ts + n_experts * SEG_ALIGN
    p_specs, s_specs = _mod_specs((5,), d, npt, dims["tps"], dims["n_seq"], dims["dec_batch"])
    final = final_g is not None
    assert final != (next_norm is not None)

    def strip(spec):
        return pl.BlockSpec(spec.block_shape, lambda i, *_, m=spec.index_map: m(i))

    def tile_map(i, *_):
        return (i, 0)

    in_specs = [
        pl.BlockSpec(memory_space=pl.ANY),
        pl.BlockSpec((ts, d), tile_map),
        pl.BlockSpec((ts, LANES), tile_map),
        *[strip(s) for s in p_specs],
        *[strip(s) for s in s_specs],
    ]
    scratch = [pltpu.VMEM((2, rows, d), BF16), pltpu.SemaphoreType.DMA((2,))]
    args = [seg["src"], seg["dst"], seg["size"], seg["tile_tot"], ye, x, pos, mod_p, mod_s]
    if final:
        in_specs.append(pl.BlockSpec((1, d), lambda i, *_: (0, 0)))
        args.append(final_g.reshape(1, d))
        out_shape = (jax.ShapeDtypeStruct((npt * ts, d), F32), jax.ShapeDtypeStruct((t - npt * ts, d), F32))
        out_specs = (pl.BlockSpec((ts, d), lambda i, *_: (jnp.minimum(i, npt - 1), 0)),
                     pl.BlockSpec((ts, d), lambda i, *_: (jnp.maximum(i - npt, 0), 0)))
        scratch.append(pltpu.VMEM((ts, d), F32))
        rank = None
    else:
        nn = next_norm
        rank, qk = nn["rank"], nn["gw2"].shape[1]
        n1_p, n1_s = _mod_specs((0, 1), d, npt, dims["tps"], dims["n_seq"], dims["dec_batch"])
        in_specs += [pl.BlockSpec((1, d), lambda i, *_: (0, 0)),
                     *[strip(s) for s in n1_p], *[strip(s) for s in n1_s],
                     pl.BlockSpec((None, LANES, d), lambda i, *_: (nn["layer"], nn["n_main"] // LANES, 0)),
                     pl.BlockSpec(nn["gw2"].shape, lambda i, *_: (0, 0)),
                     pl.BlockSpec((1, qk), lambda i, *_: (0, 0))]
        args += [nn["g"].reshape(1, d), nn["mod_p"], nn["mod_p"], nn["mod_s"], nn["mod_s"], nn["w_in_t"], nn["gw2"],
                 nn["gb"].reshape(1, qk)]
        out_shape = (jax.ShapeDtypeStruct((t, d), F32), jax.ShapeDtypeStruct((t, d), BF16),
                     jax.ShapeDtypeStruct((t, qk), F32))
        out_specs = (pl.BlockSpec((ts, d), tile_map), pl.BlockSpec((ts, d), tile_map), pl.BlockSpec((ts, qk), tile_map))
    return pl.pallas_call(
        functools.partial(_collect_kernel, npt=npt, n_experts=n_experts, final=final, rank=rank),
        out_shape=out_shape,
        grid_spec=pltpu.PrefetchScalarGridSpec(
            num_scalar_prefetch=4, grid=(t // ts,), in_specs=in_specs, out_specs=out_specs, scratch_shapes=scratch),
        compiler_params=_params("arbitrary"),
        name="collect_final" if final else "collect",
    )(*args)


def kernel(x_prompt, x_sample, c_prompt, c_sample, state_conv, state_gla, w_ada, b_ada, norm1_g, norm2_g, w_in, conv_w, conv_b, conv_ln_g, conv_ln_b, gate_w2, gate_b, gla_norm_g, w_out, router_grp_w, router_grp_b, router_exp_w, router_exp_b, exp_w_gate, exp_w_up, exp_w_down, final_norm_g):
    n_seq, seq_len, d = x_prompt.shape
    dec_batch, dec_seq, _ = x_sample.shape
    depth = w_ada.shape[0]
    kw, d_conv = conv_w.shape[1:]
    heads, dv = gla_norm_g.shape[1:]
    rank, qkw = gate_w2.shape[1:]
    dk = qkw // heads
    n_grp, n_exp = router_exp_w.shape[2:]
    n_experts = n_grp * n_exp
    tp, ts = n_seq * seq_len, dec_batch * dec_seq
    t = tp + ts
    tm = ROW_TILE
    assert ts == tm and seq_len % tm == 0 and d_conv == heads * dv and kw - 1 <= CONV_HALO
    n_main = 2 * d_conv + 2 * qkw + 2 * heads * dv
    dims = dict(npt=tp // tm, tps=seq_len // tm, n_seq=n_seq, dec_batch=dec_batch, dec_seq=dec_seq)

    xp, xs, xs_blk = x_prompt.reshape(tp, d), x_sample.transpose(1, 0, 2).reshape(ts, d), 0
    w_in_t = jnp.swapaxes(w_in, 1, 2)
    st_t = jnp.transpose(state_conv, (0, 2, 1, 3))
    pad = (-n_seq) % 8
    c_all = jnp.concatenate([c_prompt, jnp.zeros((pad, d), F32), c_sample], axis=0)
    mod = _ada(c_all, w_ada, b_ada)
    st = SORT_TILE
    assert tm % st == 0 and st % dec_batch == 0
    dims_sort = dict(npt=tp // st, tps=seq_len // st, n_seq=n_seq, dec_batch=dec_batch, dec_seq=dec_seq)
    n_tiles_max = -(-(TOP_K * t + (t // st) * n_experts * (SEG_ALIGN - 1)) // EXPERT_TILE) + n_experts

    def layer_mods(l):
        return mod[l, :n_seq].reshape(n_seq, 1, 6 * d), mod[l, n_seq + pad:]

    def gate_proj(l):
        return jnp.pad(gate_w2[l], ((0, LANES - rank), (0, 0))).astype(BF16)

    conv_p, gla_p, conv_u, gla_s = [], [], [], []
    for l in range(depth):
        mod_p, mod_s = layer_mods(l)
        if l == 0:
            h, la = _norm1(xp, xs, xs_blk, norm1_g[l], mod_p, mod_s, w_in_t, l, n_main, rank, gate_proj(l), gate_b[l], dims)
        z = _inproj(h, w_in_t, l, n_main)

        cv_p, cb_p = _conv_prompt(z, conv_w[l], conv_b[l], conv_ln_g[l], conv_ln_b[l], n_seq, seq_len)
        cv_s, u_s = _conv_sample(z, st_t, l, conv_w[l], conv_b[l], conv_ln_g[l], conv_ln_b[l], tp, dec_seq, dec_batch)
        conv_p.append(cb_p)
        conv_u.append(u_s)

        go_p, gs_p = _gla_prompt(z, la, gla_norm_g[l], n_seq, seq_len, heads, dk, dv)

        def seq_major(a):
            return a.reshape(dec_seq, dec_batch, a.shape[-1]).transpose(1, 0, 2)

        zs = z[tp:]
        q3 = seq_major(zs[:, 2 * d_conv:2 * d_conv + qkw])
        k3 = seq_major(zs[:, 2 * d_conv + qkw:2 * d_conv + 2 * qkw])
        v3 = seq_major(zs[:, 2 * d_conv + 2 * qkw:2 * d_conv + 2 * qkw + heads * dv])
        g3 = seq_major(zs[:, 2 * d_conv + 2 * qkw + heads * dv:n_main])
        go_s, gs_s = _gla_sample(q3, k3, v3, g3, seq_major(la[tp:]), state_gla, l, gla_norm_g[l], heads, dk, dv,
                                 prev_states=tuple(gla_s) if l + 1 == depth else ())
        go_s = go_s.transpose(1, 0, 2).reshape(ts, heads * dv).astype(BF16)
        gla_p.append(gs_p)
        gla_s.append(gs_s)

        x = _outproj(cv_p, go_p, cv_s.reshape(ts, d_conv), go_s, xp, xs, xs_blk, w_out, l, mod_p, mod_s, dims)

        wr = jnp.concatenate([router_grp_w[l], router_exp_w[l].reshape(d, n_experts)], axis=1)
        wr = jnp.pad(wr, ((0, 0), (0, LANES - wr.shape[1])))
        wr_hi = wr.astype(BF16)
        wr_lo = (wr - wr_hi.astype(F32)).astype(BF16)
        br = jnp.concatenate([router_grp_b[l], router_exp_b[l].reshape(-1)])
        br = jnp.pad(br, (0, LANES - br.shape[0])).reshape(1, LANES)
        hb, ids, wts = _norm2(x, norm2_g[l], mod_p, mod_s, wr_hi, wr_lo, br, dims, n_grp, n_exp)
        seg = _segments(ids[:, :TOP_K], n_experts, n_tiles_max)
        xe, we, pos = _dispatch_rows(hb, ids, wts, seg, n_experts, n_tiles_max)
        f = exp_w_gate.shape[-1]
        ye = _expert_rows(xe, we, seg, exp_w_gate.reshape(depth * n_experts, d, f),
                          exp_w_up.reshape(depth * n_experts, d, f), exp_w_down.reshape(depth * n_experts, f, d),
                          l * n_experts)
        if l + 1 < depth:
            nmod_p, nmod_s = layer_mods(l + 1)
            x, h, la = _collect(ye, pos, x, seg, mod_p, mod_s, dims_sort, n_experts,
                                next_norm=dict(g=norm1_g[l + 1], mod_p=nmod_p, mod_s=nmod_s, w_in_t=w_in_t, layer=l + 1,
                                               n_main=n_main, rank=rank, gw2=gate_proj(l + 1), gb=gate_b[l + 1]))
            xp, xs, xs_blk = x, x, dims["npt"]
        else:
            y_p, y_s = _collect(ye, pos, x, seg, mod_p, mod_s, dims_sort, n_experts, final_g=final_norm_g)

    y_prompt = y_p.reshape(n_seq, seq_len, d)
    y_sample = y_s.reshape(dec_seq, dec_batch, d).transpose(1, 0, 2)
    conv_s = jnp.transpose(_conv_state_sample(st_t, jnp.stack(conv_u)), (0, 2, 1, 3))
    gla_s_all = gla_s[-1] if depth > 1 else gla_s[0][None]
    return (y_prompt, y_sample, jnp.stack(conv_p), jnp.stack(gla_p), conv_s, gla_s_all)
```

```python
import functools

import jax
import jax.numpy as jnp
from jax import lax
from jax.experimental import pallas as pl
from jax.experimental.pallas import tpu as pltpu

F32 = jnp.float32
BF16 = jnp.bfloat16

EPS = 1e-6
GATE_TAU = 16.0
GLA_CHUNK = 32
GLA_BLOCK = 256
TOP_K = 2

ROW_TILE = 512
EXPERT_TILE = 512
SORT_TILE = 512
SEG_ALIGN = 16
CONV_ROWS = 64
CONV_HALO = 32
LANES = 128
SUBLANES = 8
VMEM_LIMIT = 56 * 1024 * 1024


def _params(*sem):
    return pltpu.CompilerParams(dimension_semantics=sem, vmem_limit_bytes=VMEM_LIMIT)


def _bdot(a, b):
    return jnp.dot(a.astype(BF16), b.astype(BF16), preferred_element_type=F32)


def _round_bf16(x):
    return x.astype(BF16).astype(F32)


def _split3(x):
    hi = x.astype(BF16)
    r = x - hi.astype(F32)
    mid = r.astype(BF16)
    lo = (r - mid.astype(F32)).astype(BF16)
    return hi, mid, lo


def _silu(x):
    return x * jax.nn.sigmoid(x)


def _store_by_group(i, n_prompt_tiles, dec_seq, out_ref, fn, vals, p_refs, s_refs):
    @pl.when(i < n_prompt_tiles)
    def _():
        out_ref[...] = fn(vals, [r[...] for r in p_refs]).astype(out_ref.dtype)

    @pl.when(i >= n_prompt_tiles)
    def _():
        mods = [r[...] for r in s_refs]
        nb = mods[0].shape[0]
        for t in range(out_ref.shape[0] // nb):
            rows = slice(t * nb, (t + 1) * nb)
            out_ref[rows, :] = fn([v[rows] for v in vals], mods).astype(out_ref.dtype)


def _ada_kernel(c_ref, w_ref, b_ref, o_ref):
    c = c_ref[...]
    o_ref[...] = _bdot(_silu(c), w_ref[...]) + b_ref[...]


def _ada(c_all, w_ada, b_ada):
    depth, d, n = w_ada.shape
    rows = c_all.shape[0]
    tn = 1024
    return pl.pallas_call(
        _ada_kernel,
        out_shape=jax.ShapeDtypeStruct((depth, rows, n), F32),
        grid=(depth, n // tn),
        in_specs=[
            pl.BlockSpec((rows, d), lambda l, j: (0, 0)),
            pl.BlockSpec((None, d, tn), lambda l, j: (l, 0, j)),
            pl.BlockSpec((None, 1, tn), lambda l, j: (l, 0, j)),
        ],
        out_specs=pl.BlockSpec((None, rows, tn), lambda l, j: (l, 0, j)),
        compiler_params=_params("arbitrary", "arbitrary"),
        name="ada",
    )(c_all, w_ada, b_ada.reshape(depth, 1, n))


def _mod_specs(cols, width, n_prompt_tiles, tiles_per_seq, n_seq, dec_batch, grid_rank=1, row_axis=0, col_fn=None):
    p_specs, s_specs = [], []
    for c in cols:
        def p_map(*idx, c=c):
            b = jnp.minimum(idx[row_axis] // tiles_per_seq, n_seq - 1)
            return (b, 0, c if col_fn is None else col_fn(c, idx))

        def s_map(*idx, c=c):
            return (0, c if col_fn is None else col_fn(c, idx))

        p_specs.append(pl.BlockSpec((None, 1, width), p_map))
        s_specs.append(pl.BlockSpec((dec_batch, width), s_map))
    return p_specs, s_specs


def _rms(x, g):
    return x * lax.rsqrt(jnp.mean(x * x, axis=-1, keepdims=True) + EPS) * g


def _norm1_kernel(xp_ref, xs_ref, *refs, npt, dec_seq, rank):
    i = pl.program_id(0)
    _norm1_body(i, jnp.where(i < npt, xp_ref[...], xs_ref[...]), *refs, npt=npt, rank=rank)


def _norm1_body(i, x, g_ref, shp, scp, shs, scs, wgl_ref, gw2_ref, gb_ref, h_ref, la_ref, *, npt, rank):
    y = _rms(x, g_ref[...])
    _store_by_group(i, npt, None, h_ref, lambda v, m: v[0] * (1.0 + m[1]) + m[0], [y], [shp, scp], [shs, scs])
    row = lax.broadcasted_iota(jnp.int32, wgl_ref.shape, 0)
    w_gl = jnp.where(row < rank, wgl_ref[...], 0.0).astype(BF16)
    gate_lr = lax.dot_general(h_ref[...], w_gl, (((1,), (1,)), ((), ())), preferred_element_type=F32)
    pre = _bdot(gate_lr, gw2_ref[...]) + gb_ref[...]
    la_ref[...] = (jnp.minimum(pre, 0.0) - jnp.log1p(jnp.exp(-jnp.abs(pre)))) * (1.0 / GATE_TAU)


def _norm1(xp, xs, xs_blk, g, mod_p, mod_s, w_in_t, layer, n_main, rank, gw2, gb, dims):
    d = xp.shape[1]
    tm = ROW_TILE
    npt = dims["npt"]
    t = (npt + 1) * tm
    p_specs, s_specs = _mod_specs((0, 1), d, npt, dims["tps"], dims["n_seq"], dims["dec_batch"])
    qk = gw2.shape[1]
    assert n_main % LANES == 0 and rank <= LANES
    return pl.pallas_call(
        functools.partial(_norm1_kernel, npt=npt, dec_seq=dims["dec_seq"], rank=rank),
        out_shape=(jax.ShapeDtypeStruct((t, d), BF16), jax.ShapeDtypeStruct((t, qk), F32)),
        grid=(t // tm,),
        in_specs=[
            pl.BlockSpec((tm, d), lambda i: (jnp.minimum(i, npt - 1), 0)),
            pl.BlockSpec((tm, d), lambda i: (xs_blk, 0)),
            pl.BlockSpec((1, d), lambda i: (0, 0)),
            *p_specs,
            *s_specs,
            pl.BlockSpec((None, LANES, d), lambda i: (layer, n_main // LANES, 0)),
            pl.BlockSpec(gw2.shape, lambda i: (0, 0)),
            pl.BlockSpec((1, qk), lambda i: (0, 0)),
        ],
        out_specs=(pl.BlockSpec((tm, d), lambda i: (i, 0)), pl.BlockSpec((tm, qk), lambda i: (i, 0))),
        compiler_params=_params("arbitrary"),
        name="norm1",
    )(xp, xs, g.reshape(1, d), mod_p, mod_p, mod_s, mod_s, w_in_t, gw2, gb.reshape(1, qk))


def _inproj_kernel(h_ref, w_ref, o_ref, wb_ref):
    @pl.when(pl.program_id(1) == 0)
    def _():
        wb_ref[...] = w_ref[...].astype(BF16)

    o_ref[...] = lax.dot_general(h_ref[...], wb_ref[...], (((1,), (1,)), ((), ())), preferred_element_type=F32)


def _inproj(h, w_in_t, layer, n_cols):
    t, d = h.shape
    tm, tn = 2 * ROW_TILE, 1024
    return pl.pallas_call(
        _inproj_kernel,
        out_shape=jax.ShapeDtypeStruct((t, n_cols), F32),
        grid=(n_cols // tn, pl.cdiv(t, tm)),
        in_specs=[
            pl.BlockSpec((tm, d), lambda j, i: (i, 0)),
            pl.BlockSpec((None, tn, d), lambda j, i: (layer, j, 0)),
        ],
        out_specs=pl.BlockSpec((tm, tn), lambda j, i: (i, j)),
        scratch_shapes=[pltpu.VMEM((tn, d), BF16)],
        compiler_params=_params("arbitrary", "arbitrary"),
        name="inproj",
    )(h, w_in_t)


def _ln_silu(y, g, b):
    mu = jnp.mean(y, axis=-1, keepdims=True)
    yc = y - mu
    var = jnp.mean(yc * yc, axis=-1, keepdims=True)
    return _silu(yc * lax.rsqrt(var + EPS) * g + b)


def _conv_prompt_kernel(a_ref, b_ref, cw_ref, cb_ref, lng_ref, lnb_ref, o_ref, st_ref, full_ref, cwb_ref, y_ref, *, kw):
    j = pl.program_id(1)
    tm, c = a_ref.shape
    halo = CONV_HALO
    phases = full_ref.shape[0]
    assert phases == SUBLANES

    @pl.when(jnp.logical_and(pl.program_id(0) == 0, j == 0))
    def _():
        full_ref[...] = jnp.zeros(full_ref.shape, F32)
        for w in range(kw):
            cwb_ref[w] = jnp.broadcast_to(cw_ref[w:w + 1, :], (SUBLANES, c))

    prev = full_ref[0, tm + halo - SUBLANES:tm + halo, :]
    tail = jnp.where(j == 0, 0.0, prev)

    @pl.when(j == 0)
    def _():
        for p in range(phases):
            full_ref[p, 0:halo, :] = jnp.zeros((halo, c), F32)

    @pl.when(j > 0)
    def _():
        for p in range(phases):
            full_ref[p, 0:halo, :] = full_ref[p, tm:tm + halo, :]

    u = a_ref[...] * jax.nn.sigmoid(b_ref[...])
    full_ref[0, halo:halo + tm, :] = u
    ext = jnp.concatenate([tail, u], axis=0)
    for p in range(1, phases):
        full_ref[p, halo - SUBLANES:halo - SUBLANES + tm, :] = pltpu.roll(ext, tm + SUBLANES - p, 0)[0:tm]
    off = halo - (kw - 1)
    rb = CONV_ROWS

    def body(r, carry):
        r0 = pl.multiple_of(r * rb, rb)
        for lt in range(c // LANES):
            cols = slice(lt * LANES, (lt + 1) * LANES)
            acc = None
            for p in range(phases):
                x = full_ref[p, pl.ds(r0, rb + halo), cols]
                for a in range(halo // phases + 1):
                    w = a * phases + p - off
                    if 0 <= w < kw and a * phases + rb <= rb + halo:
                        term = x[a * phases:a * phases + rb] * jnp.concatenate([cwb_ref[w, :, cols]] * (rb // SUBLANES), axis=0)
                        acc = term if acc is None else acc + term
            y_ref[pl.ds(r0, rb), cols] = acc
        return carry

    lax.fori_loop(0, tm // rb, body, 0)
    y = _ln_silu(y_ref[...] + cb_ref[...], lng_ref[...], lnb_ref[...])
    o_ref[...] = y.astype(o_ref.dtype)

    @pl.when(j == pl.num_programs(1) - 1)
    def _():
        st_ref[...] = full_ref[0, halo + tm - (kw - 1):halo + tm, :]


def _conv_prompt(z, cw, cb, lng, lnb, n_seq, seq_len):
    kw, c = cw.shape
    tm = ROW_TILE
    tps = seq_len // tm
    return pl.pallas_call(
        functools.partial(_conv_prompt_kernel, kw=kw),
        out_shape=(jax.ShapeDtypeStruct((n_seq * seq_len, c), BF16), jax.ShapeDtypeStruct((n_seq, kw - 1, c), F32)),
        grid=(n_seq, tps),
        in_specs=[
            pl.BlockSpec((tm, c), lambda b, j: (b * tps + j, 0)),
            pl.BlockSpec((tm, c), lambda b, j: (b * tps + j, 1)),
            pl.BlockSpec((kw, c), lambda b, j: (0, 0)),
            pl.BlockSpec((1, c), lambda b, j: (0, 0)),
            pl.BlockSpec((1, c), lambda b, j: (0, 0)),
            pl.BlockSpec((1, c), lambda b, j: (0, 0)),
        ],
        out_specs=(
            pl.BlockSpec((tm, c), lambda b, j: (b * tps + j, 0)),
            pl.BlockSpec((None, kw - 1, c), lambda b, j: (b, 0, 0)),
        ),
        scratch_shapes=[pltpu.VMEM((SUBLANES, tm + CONV_HALO, c), F32), pltpu.VMEM((kw, SUBLANES, c), F32),
                        pltpu.VMEM((tm, c), F32)],
        compiler_params=_params("arbitrary", "arbitrary"),
        name="conv_prompt",
    )(z, z, cw, cb.reshape(1, c), lng.reshape(1, c), lnb.reshape(1, c))


def _conv_sample_kernel(*refs, kw, dec_seq):
    a_refs = refs[0:dec_seq]
    b_refs = refs[dec_seq:2 * dec_seq]
    st_ref, cw_ref, cb_ref, lng_ref, lnb_ref, o_ref, u_ref = refs[2 * dec_seq:]
    hist = kw - 1
    u = [a_refs[t][...] * jax.nn.sigmoid(b_refs[t][...]) for t in range(dec_seq)]
    def row(j):
        return st_ref[j] if j < hist else u[j - hist]

    for t in range(dec_seq):
        acc = row(t) * cw_ref[0:1, :]
        for w in range(1, kw):
            acc = acc + row(t + w) * cw_ref[w:w + 1, :]
        y = _ln_silu(acc + cb_ref[...], lng_ref[...], lnb_ref[...])
        o_ref[t] = y.astype(o_ref.dtype)
        u_ref[t] = u[t]


def _conv_sample(z, st_t, layer, cw, cb, lng, lnb, row0, dec_seq, dec_batch):
    kw, c = cw.shape
    bs = 16
    a_specs = [pl.BlockSpec((bs, c), lambda s, t=t: ((row0 + t * dec_batch) // bs + s, 0)) for t in range(dec_seq)]
    b_specs = [pl.BlockSpec((bs, c), lambda s, t=t: ((row0 + t * dec_batch) // bs + s, 1)) for t in range(dec_seq)]
    vec = pl.BlockSpec((1, c), lambda s: (0, 0))
    return pl.pallas_call(
        functools.partial(_conv_sample_kernel, kw=kw, dec_seq=dec_seq),
        out_shape=(jax.ShapeDtypeStruct((dec_seq, dec_batch, c), BF16), jax.ShapeDtypeStruct((dec_seq, dec_batch, c), F32)),
        grid=(dec_batch // bs,),
        in_specs=[*a_specs, *b_specs, pl.BlockSpec((None, kw - 1, bs, c), lambda s: (layer, 0, s, 0)),
                  pl.BlockSpec((kw, c), lambda s: (0, 0)), vec, vec, vec],
        out_specs=(pl.BlockSpec((dec_seq, bs, c), lambda s: (0, s, 0)), pl.BlockSpec((dec_seq, bs, c), lambda s: (0, s, 0))),
        compiler_params=_params("arbitrary"),
        name="conv_sample",
    )(*([z] * (2 * dec_seq)), st_t, cw, cb.reshape(1, c), lng.reshape(1, c), lnb.reshape(1, c))


def _conv_state_kernel(st_ref, u_ref, o_ref):
    hist, n_new = st_ref.shape[0], u_ref.shape[0]
    for j in range(hist - n_new):
        o_ref[j] = st_ref[j + n_new]
    for t in range(n_new):
        o_ref[hist - n_new + t] = u_ref[t]


def _conv_state_sample(st_t, u_all):
    depth, hist, nb, c = st_t.shape
    n_new = u_all.shape[1]
    assert n_new <= hist
    bs = 16
    return pl.pallas_call(
        _conv_state_kernel,
        out_shape=jax.ShapeDtypeStruct(st_t.shape, st_t.dtype),
        grid=(depth, nb // bs),
        in_specs=[pl.BlockSpec((None, hist, bs, c), lambda l, s: (l, 0, s, 0)),
                  pl.BlockSpec((None, n_new, bs, c), lambda l, s: (l, 0, s, 0))],
        out_specs=pl.BlockSpec((None, hist, bs, c), lambda l, s: (l, 0, s, 0)),
        compiler_params=_params("arbitrary", "arbitrary"),
        name="conv_state",
    )(st_t, u_all)


def _gla_prompt_kernel(qk_ref, v_ref, g_ref, la_ref, gn_ref, o_ref, sfin_ref, st_ref, sn_ref, *, heads, dk, dv):
    j = pl.program_id(1)
    tm = qk_ref.shape[0]
    ck = GLA_CHUNK
    nch = tm // ck
    qkw = heads * dk

    @pl.when(j == 0)
    def _():
        st_ref[...] = jnp.zeros(st_ref.shape, F32)

    la = la_ref[...]
    sb = GLA_BLOCK
    blocks = [slice(r, r + sb) for r in range(0, tm, sb)]
    row = lax.broadcasted_iota(jnp.int32, (sb, sb), 0)
    col = lax.broadcasted_iota(jnp.int32, (sb, sb), 1)
    same = (row // ck) == (col // ck)
    causal = same & (col <= row)
    tri_incl = causal.astype(BF16)
    tri_after = (same & (col > row)).astype(BF16)
    sel = (lax.broadcasted_iota(jnp.int32, (nch, tm), 1) // ck == lax.broadcasted_iota(jnp.int32, (nch, tm), 0)).astype(BF16)
    parts = _split3(la)

    def blockwise(tri):
        return jnp.concatenate([sum(jnp.dot(tri, p[rs], preferred_element_type=F32) for p in parts) for rs in blocks], axis=0)

    b = blockwise(tri_incl)
    rest = blockwise(tri_after)
    tot = sum(jnp.dot(sel, p, preferred_element_type=F32) for p in parts)
    qk = qk_ref[...]
    q = qk[:, :qkw] * (dk ** -0.5)
    k = qk[:, qkw:]
    q_dec = (q * jnp.exp(b)).astype(BF16)
    k_inv = (k * jnp.exp(-b)).astype(BF16)
    k_end = _round_bf16(k * jnp.exp(rest))
    decay = jnp.exp(tot)
    v_all = v_ref[...]
    g_all = g_ref[...]
    for h in range(heads):
        ks = slice(h * dk, (h + 1) * dk)
        vs = slice(h * dv, (h + 1) * dv)
        qd, ki, ke = q_dec[:, ks], k_inv[:, ks], k_end[:, ks]
        vh = v_all[:, vs]
        vb = vh.astype(BF16)
        vr = _round_bf16(vh)
        intra = []
        for rs in blocks:
            att = lax.dot_general(qd[rs], ki[rs], (((1,), (1,)), ((), ())), preferred_element_type=F32)
            att = jnp.where(causal, att, 0.0).astype(BF16)
            intra.append(jnp.dot(att, vb[rs], preferred_element_type=F32))
        o = jnp.concatenate(intra, axis=0)
        s = st_ref[h]
        for n in range(nch):
            rs = slice(n * ck, (n + 1) * ck)
            sn_ref[n] = s.astype(BF16)
            upd = lax.dot_general(vr[rs], ke[rs], (((0,), (0,)), ((), ())), preferred_element_type=F32)
            s = s * decay[n:n + 1, ks] + upd
        st_ref[h] = s
        inter = [lax.dot_general(qd[n * ck:(n + 1) * ck], sn_ref[n], (((1,), (1,)), ((), ())), preferred_element_type=F32)
                 for n in range(nch)]
        o = o + jnp.concatenate(inter, axis=0)
        o = _rms(o, gn_ref[:, vs]) * _silu(g_all[:, vs])
        o_ref[:, vs] = o.astype(o_ref.dtype)

    @pl.when(j == pl.num_programs(1) - 1)
    def _():
        for h in range(heads):
            sfin_ref[h] = st_ref[h].T


def _gla_prompt(z, la, gn, n_seq, seq_len, heads, dk, dv):
    tm = ROW_TILE
    tps = seq_len // tm
    w = heads * dv
    qkw = heads * dk
    assert 2 * qkw == w
    return pl.pallas_call(
        functools.partial(_gla_prompt_kernel, heads=heads, dk=dk, dv=dv),
        out_shape=(jax.ShapeDtypeStruct((n_seq * seq_len, w), BF16), jax.ShapeDtypeStruct((n_seq, heads, dk, dv), F32)),
        grid=(n_seq, tps),
        in_specs=[
            pl.BlockSpec((tm, w), lambda b, j: (b * tps + j, 2)),
            pl.BlockSpec((tm, w), lambda b, j: (b * tps + j, 3)),
            pl.BlockSpec((tm, w), lambda b, j: (b * tps + j, 4)),
            pl.BlockSpec((tm, qkw), lambda b, j: (b * tps + j, 0)),
            pl.BlockSpec((1, w), lambda b, j: (0, 0)),
        ],
        out_specs=(
            pl.BlockSpec((tm, w), lambda b, j: (b * tps + j, 0)),
            pl.BlockSpec((None, heads, dk, dv), lambda b, j: (b, 0, 0, 0)),
        ),
        scratch_shapes=[pltpu.VMEM((heads, dv, dk), F32), pltpu.VMEM((tm // GLA_CHUNK, dv, dk), BF16)],
        compiler_params=_params("arbitrary", "arbitrary"),
        name="gla_prompt",
    )(z, z, z, la, gn.reshape(1, w))


def _gla_sample_kernel(q_ref, k_ref, v_ref, g_ref, la_ref, s_ref, gn_ref, *rest, heads, dk, dv, n_prev):
    bs, ln, _ = q_ref.shape
    if n_prev:
        o_ref, all_ref = rest[n_prev:]
        for p in range(n_prev):
            all_ref[p] = rest[p][...]
        ns_ref = all_ref.at[n_prev]
    else:
        o_ref, ns_ref = rest
    tril = lax.broadcasted_iota(jnp.int32, (ln, ln), 1) <= lax.broadcasted_iota(jnp.int32, (ln, ln), 0)
    for s in range(bs):
        q_s, k_s, v_s, g_s, la_s = q_ref[s], k_ref[s], v_ref[s], g_ref[s], la_ref[s]
        for h in range(heads):
            ks = slice(h * dk, (h + 1) * dk)
            vs = slice(h * dv, (h + 1) * dv)
            la = la_s[:, ks]
            rows = [la[0:1]]
            for t in range(1, ln):
                rows.append(rows[-1] + la[t:t + 1])
            b = jnp.concatenate(rows, axis=0)
            b_last = rows[-1]
            q_dec = _round_bf16(q_s[:, ks] * (dk ** -0.5) * jnp.exp(b))
            k_inv = _round_bf16(k_s[:, ks] * jnp.exp(-b))
            k_end = _round_bf16(k_s[:, ks] * jnp.exp(b_last - b))
            vr = _round_bf16(v_s[:, vs])
            s0 = s_ref[s, h]
            att = lax.dot_general(q_dec, k_inv, (((1,), (1,)), ((), ())), preferred_element_type=F32)
            att = _round_bf16(jnp.where(tril, att, 0.0))
            o = jnp.dot(att, vr, preferred_element_type=F32) + jnp.dot(q_dec, _round_bf16(s0), preferred_element_type=F32)
            upd = lax.dot_general(k_end, vr, (((0,), (0,)), ((), ())), preferred_element_type=F32)
            d_col = jnp.broadcast_to(jnp.exp(b_last), (dk, dk)).T
            ns_ref[s, h] = s0 * jnp.concatenate([d_col] * (dv // dk), axis=1) + upd
            o = _rms(o, gn_ref[:, vs]) * _silu(g_s[:, vs])
            o_ref[s, :, vs] = o.astype(o_ref.dtype)


def _gla_sample(q3, k3, v3, g3, la3, state, layer, gn, heads, dk, dv, prev_states=()):
    nb, ln, w = v3.shape
    qkw = heads * dk
    bs = 8
    n_prev = len(prev_states)
    seq_spec = pl.BlockSpec((bs, ln, qkw), lambda s: (s, 0, 0))
    wide_spec = pl.BlockSpec((bs, ln, w), lambda s: (s, 0, 0))
    in_specs = [seq_spec, seq_spec, wide_spec, wide_spec, seq_spec,
                pl.BlockSpec((None, bs, heads, dk, dv), lambda s: (layer, s, 0, 0, 0)),
                pl.BlockSpec((1, w), lambda s: (0, 0)),
                *[pl.BlockSpec((bs, heads, dk, dv), lambda s: (s, 0, 0, 0))] * n_prev]
    if n_prev:
        state_shape = jax.ShapeDtypeStruct((n_prev + 1, nb, heads, dk, dv), F32)
        state_spec = pl.BlockSpec((n_prev + 1, bs, heads, dk, dv), lambda s: (0, s, 0, 0, 0))
    else:
        state_shape = jax.ShapeDtypeStruct((nb, heads, dk, dv), F32)
        state_spec = pl.BlockSpec((bs, heads, dk, dv), lambda s: (s, 0, 0, 0))
    return pl.pallas_call(
        functools.partial(_gla_sample_kernel, heads=heads, dk=dk, dv=dv, n_prev=n_prev),
        out_shape=(jax.ShapeDtypeStruct((nb, ln, w), F32), state_shape),
        grid=(nb // bs,),
        in_specs=in_specs,
        out_specs=(wide_spec, state_spec),
        compiler_params=_params("arbitrary"),
        name="gla_sample_last" if n_prev else "gla_sample",
    )(q3, k3, v3, g3, la3, state, gn.reshape(1, w), *prev_states)


def _outproj_kernel(cp_ref, op_ref, cs_ref, os_ref, xp_ref, xs_ref, w_ref, gp, gs, y_ref, wb_ref, *, npt, dec_seq):
    i = pl.program_id(1)

    @pl.when(i == 0)
    def _():
        wb_ref[...] = w_ref[...].astype(BF16)

    half = cp_ref.shape[1]

    def mixed(c_ref, o_ref):
        return (jnp.dot(c_ref[...], wb_ref[0:half, :], preferred_element_type=F32)
                + jnp.dot(o_ref[...], wb_ref[half:, :], preferred_element_type=F32))

    @pl.when(i < npt)
    def _():
        y_ref[...] = xp_ref[...] + gp[...] * mixed(cp_ref, op_ref)

    @pl.when(i >= npt)
    def _():
        mix = mixed(cs_ref, os_ref)
        nb = gs.shape[0]
        for t in range(dec_seq):
            rows = slice(t * nb, (t + 1) * nb)
            y_ref[rows, :] = xs_ref[rows, :] + gs[...] * mix[rows]


def _outproj(conv_p, gla_p, conv_s, gla_s, xp, xs, xs_blk, w_out, layer, mod_p, mod_s, dims):
    d = xp.shape[1]
    half = conv_p.shape[1]
    tm, tn = ROW_TILE, 1024
    nj = d // tn
    npt = dims["npt"]
    t = (npt + 1) * tm
    p_specs, s_specs = _mod_specs((2,), tn, dims["npt"], dims["tps"], dims["n_seq"], dims["dec_batch"],
                                  row_axis=1, col_fn=lambda c, idx: c * nj + idx[0])
    return pl.pallas_call(
        functools.partial(_outproj_kernel, npt=dims["npt"], dec_seq=dims["dec_seq"]),
        out_shape=jax.ShapeDtypeStruct((t, d), F32),
        grid=(nj, t // tm),
        in_specs=[
            pl.BlockSpec((tm, half), lambda j, i: (jnp.minimum(i, npt - 1), 0)),
            pl.BlockSpec((tm, half), lambda j, i: (jnp.minimum(i, npt - 1), 0)),
            pl.BlockSpec((tm, half), lambda j, i: (0, 0)),
            pl.BlockSpec((tm, half), lambda j, i: (0, 0)),
            pl.BlockSpec((tm, tn), lambda j, i: (jnp.minimum(i, npt - 1), j)),
            pl.BlockSpec((tm, tn), lambda j, i: (xs_blk, j)),
            pl.BlockSpec((None, d, tn), lambda j, i: (layer, 0, j)),
            *p_specs,
            *s_specs,
        ],
        out_specs=pl.BlockSpec((tm, tn), lambda j, i: (i, j)),
        scratch_shapes=[pltpu.VMEM((d, tn), BF16)],
        compiler_params=_params("arbitrary", "arbitrary"),
        name="outproj",
    )(conv_p, gla_p, conv_s, gla_s, xp, xs, w_out, mod_p, mod_s)


def _norm2_kernel(x_ref, g_ref, shp, scp, shs, scs, wr_hi, wr_lo, br_ref, hb_ref, ids_ref, wts_ref, h_ref, *, npt, dec_seq, n_grp, n_exp):
    i = pl.program_id(0)
    y = _rms(x_ref[...], g_ref[...])
    _store_by_group(i, npt, dec_seq, h_ref, lambda v, m: v[0] * (1.0 + m[1]) + m[0], [y], [shp, scp], [shs, scs])
    h = h_ref[...]
    h_hi = h.astype(BF16)
    hb_ref[...] = h_hi
    h_lo = (h - h_hi.astype(F32)).astype(BF16)
    logits = (jnp.dot(h_hi, wr_hi[...], preferred_element_type=F32) + jnp.dot(h_lo, wr_hi[...], preferred_element_type=F32)
              + jnp.dot(h_hi, wr_lo[...], preferred_element_type=F32)) + br_ref[...]
    lane = lax.broadcasted_iota(jnp.int32, logits.shape, 1).astype(F32)
    big = jnp.float32(LANES)
    neg = jnp.float32(-jnp.inf)
    gl = jnp.where(lane < n_grp, logits, neg)
    gmax = jnp.max(gl, axis=-1, keepdims=True)
    gidx = jnp.min(jnp.where(gl == gmax, lane, big), axis=-1, keepdims=True)
    g_w = 1.0 / jnp.sum(jnp.exp(gl - gmax), axis=-1, keepdims=True)
    lo = n_grp + gidx * n_exp
    in_grp = (lane >= lo) & (lane < lo + n_exp)
    sl = jnp.where(in_grp, logits, neg)
    p = jnp.exp(sl - jnp.max(sl, axis=-1, keepdims=True))
    p = p / jnp.sum(p, axis=-1, keepdims=True)
    p = jnp.where(in_grp, p, -1.0)
    p1 = jnp.max(p, axis=-1, keepdims=True)
    i1 = jnp.min(jnp.where(p == p1, lane, big), axis=-1, keepdims=True)
    p_rest = jnp.where(lane == i1, -1.0, p)
    p2 = jnp.max(p_rest, axis=-1, keepdims=True)
    i2 = jnp.min(jnp.where(p_rest == p2, lane, big), axis=-1, keepdims=True)
    denom = p1 + p2
    ids_ref[...] = jnp.where(lane == 0, i1 - n_grp, jnp.where(lane == 1, i2 - n_grp, 0.0)).astype(jnp.int32)
    wts_ref[...] = jnp.where(lane == 0, g_w * (p1 / denom), jnp.where(lane == 1, g_w * (p2 / denom), 0.0))


def _norm2(x, g, mod_p, mod_s, wr_hi, wr_lo, br, dims, n_grp, n_exp):
    t, d = x.shape
    tm = ROW_TILE
    p_specs, s_specs = _mod_specs((3, 4), d, dims["npt"], dims["tps"], dims["n_seq"], dims["dec_batch"])
    return pl.pallas_call(
        functools.partial(_norm2_kernel, npt=dims["npt"], dec_seq=dims["dec_seq"], n_grp=n_grp, n_exp=n_exp),
        out_shape=(jax.ShapeDtypeStruct((t, d), BF16), jax.ShapeDtypeStruct((t, LANES), jnp.int32),
                   jax.ShapeDtypeStruct((t, LANES), F32)),
        grid=(t // tm,),
        in_specs=[
            pl.BlockSpec((tm, d), lambda i: (i, 0)),
            pl.BlockSpec((1, d), lambda i: (0, 0)),
            *p_specs,
            *s_specs,
            pl.BlockSpec((d, LANES), lambda i: (0, 0)),
            pl.BlockSpec((d, LANES), lambda i: (0, 0)),
            pl.BlockSpec((1, LANES), lambda i: (0, 0)),
        ],
        out_specs=(pl.BlockSpec((tm, d), lambda i: (i, 0)), pl.BlockSpec((tm, LANES), lambda i: (i, 0)),
                   pl.BlockSpec((tm, LANES), lambda i: (i, 0))),
        scratch_shapes=[pltpu.VMEM((tm, d), F32)],
        compiler_params=_params("arbitrary"),
        name="norm2_router",
    )(x, g.reshape(1, d), mod_p, mod_p, mod_s, mod_s, wr_hi, wr_lo, br)


def _segments(ids, n_experts, n_tiles_max):
    t = ids.shape[0]
    ts, te, al = SORT_TILE, EXPERT_TILE, SEG_ALIGN
    nt = t // ts
    e_iota = jnp.arange(n_experts, dtype=jnp.int32)
    cnt = jnp.sum((ids.reshape(nt, ts * TOP_K, 1) == e_iota).astype(jnp.int32), axis=1)
    size = (cnt + al - 1) // al * al
    src = jnp.cumsum(size, axis=1) - size
    tot = jnp.sum(size, axis=0)
    tot_pad = (tot + te - 1) // te * te
    ends = jnp.cumsum(tot_pad)
    exp_off = ends - tot_pad
    dst = exp_off[None, :] + jnp.cumsum(size, axis=0) - size
    n_used = (ends[-1] // te).astype(jnp.int32)
    tile_start = jnp.arange(n_tiles_max, dtype=jnp.int32) * te
    tile_expert = jnp.minimum(jnp.sum(ends[None, :] <= tile_start[:, None], axis=1), n_experts - 1).astype(jnp.int32)
    last = tile_expert[jnp.maximum(n_used - 1, 0)]
    used = jnp.arange(n_tiles_max) < n_used
    tile_expert = jnp.where(used, tile_expert, last)
    prev_e = jnp.concatenate([jnp.full((1,), -1, jnp.int32), tile_expert[:-1]])
    tile_first = (used & (tile_expert != prev_e)).astype(jnp.int32)
    tile_slot = ((jnp.cumsum(tile_first) - 1) % 2).astype(jnp.int32)
    next_start = jnp.sum(used[None, :] & (tile_expert[None, :] <= tile_expert[:, None]), axis=1)
    tile_next = jnp.where(next_start < n_used, tile_expert[jnp.minimum(next_start, n_tiles_max - 1)], -1).astype(jnp.int32)
    return dict(tile_first=tile_first, tile_slot=tile_slot, tile_next=tile_next,
                src=src.reshape(-1).astype(jnp.int32), dst=dst.reshape(-1).astype(jnp.int32),
                size=size.reshape(-1).astype(jnp.int32), tile_tot=jnp.sum(size, axis=1).astype(jnp.int32),
                fill_start=(exp_off + tot).astype(jnp.int32), fill_size=(tot_pad - tot).astype(jnp.int32),
                tile_expert=tile_expert, n_used=n_used.reshape(1))


def _lane_pick(x, lane, k):
    return jnp.sum(jnp.where(lane == k, x, 0.0), axis=-1, keepdims=True)


def _sorted_positions(ids, lane):
    ts = ids.shape[0]
    onehot = [(lane == _lane_pick(ids, lane, k)).astype(F32) for k in range(TOP_K)]
    row = lax.broadcasted_iota(jnp.int32, (ts, ts), 0)
    col = lax.broadcasted_iota(jnp.int32, (ts, ts), 1)
    before = (col < row).astype(BF16)
    earlier = [jnp.dot(before, o.astype(BF16), preferred_element_type=F32) for o in onehot]
    cnt = [jnp.sum(o, axis=0, keepdims=True) for o in onehot]
    size = jnp.ceil((cnt[0] + cnt[1]) * (1.0 / SEG_ALIGN)) * SEG_ALIGN
    er = lax.broadcasted_iota(jnp.int32, (LANES, LANES), 0)
    ec = lax.broadcasted_iota(jnp.int32, (LANES, LANES), 1)
    start = jnp.dot(jnp.broadcast_to(size, (SUBLANES, LANES)).astype(BF16), (er < ec).astype(BF16),
                    preferred_element_type=F32)[0:1]
    base = [start + earlier[0], start + cnt[0] + earlier[1]]
    return [jnp.sum(onehot[k] * base[k], axis=-1, keepdims=True) for k in range(TOP_K)]


def _as_row(col_vals, lane):
    hi = jnp.floor(col_vals * (1.0 / 64.0))
    lo = col_vals - hi * 64.0
    ones = jnp.ones((SUBLANES, LANES), BF16)
    nt = (((1,), (1,)), ((), ()))
    hi_row = lax.dot_general(ones, jnp.where(lane == 0, hi, 0.0).astype(BF16), nt, preferred_element_type=F32)
    lo_row = lax.dot_general(ones, jnp.where(lane == 0, lo, 0.0).astype(BF16), nt, preferred_element_type=F32)
    return (hi_row * 64.0 + lo_row)[0:1]


def _dispatch_kernel(src_ref, dst_ref, size_ref, tot_ref, fst_ref, fsz_ref, nu_ref, h_ref, ids_ref, wts_ref, xe_hbm,
                     we_hbm, pos_ref, sbuf, wbuf, zx, zw, semx, semw, semz, semt, *, n_experts):
    i = pl.program_id(0)
    n = pl.num_programs(0)
    ts = h_ref.shape[0]
    rows = sbuf.shape[1]
    te = zx.shape[0]
    slot = lax.rem(i, 2)

    def tail_copies(tl):
        r0 = pl.multiple_of(tl * te, te)
        return (pltpu.make_async_copy(zx, xe_hbm.at[pl.ds(r0, te), :], semt.at[0]),
                pltpu.make_async_copy(zw, we_hbm.at[pl.ds(r0, te), :], semt.at[1]))

    n_tail_tiles = xe_hbm.shape[0] // te

    def for_expert_tails(action):
        def body(e, carry):
            sz = pl.multiple_of(fsz_ref[e], SEG_ALIGN)
            d0 = pl.multiple_of(fst_ref[e], SEG_ALIGN)

            @pl.when(sz > 0)
            def _():
                action(pltpu.make_async_copy(zx.at[pl.ds(0, sz), :], xe_hbm.at[pl.ds(d0, sz), :], semz.at[0]))
                action(pltpu.make_async_copy(zw.at[pl.ds(0, sz), :], we_hbm.at[pl.ds(d0, sz), :], semz.at[1]))

            return carry

        lax.fori_loop(0, n_experts, body, 0)

    def for_tail(action, first):
        def body(tl, carry):
            for c in tail_copies(tl):
                action(c)
            return carry

        lax.fori_loop(first, n_tail_tiles, body, 0)

    def seg_copies(tile, sl, e):
        k = tile * n_experts + e
        sz = pl.multiple_of(size_ref[k], SEG_ALIGN)
        s0 = pl.multiple_of(src_ref[k], SEG_ALIGN)
        d0 = pl.multiple_of(dst_ref[k], SEG_ALIGN)
        return sz, (pltpu.make_async_copy(sbuf.at[sl, pl.ds(s0, sz), :], xe_hbm.at[pl.ds(d0, sz), :], semx.at[sl]),
                    pltpu.make_async_copy(wbuf.at[sl, pl.ds(s0, sz), :], we_hbm.at[pl.ds(d0, sz), :], semw.at[sl]))

    def wait_tile(tile, sl):
        tot = pl.multiple_of(tot_ref[tile], SEG_ALIGN)
        pltpu.make_async_copy(sbuf.at[sl, pl.ds(0, tot), :], xe_hbm.at[pl.ds(0, tot), :], semx.at[sl]).wait()
        pltpu.make_async_copy(wbuf.at[sl, pl.ds(0, tot), :], we_hbm.at[pl.ds(0, tot), :], semw.at[sl]).wait()

    @pl.when(i == 0)
    def _():
        zx[...] = jnp.zeros(zx.shape, zx.dtype)
        zw[...] = jnp.zeros(zw.shape, zw.dtype)
        for_expert_tails(lambda c: c.start())

    @pl.when(i >= 2)
    def _():
        wait_tile(i - 2, slot)

    lane = lax.broadcasted_iota(jnp.int32, (ts, LANES), 1).astype(F32)
    pos = _sorted_positions(ids_ref[...].astype(F32), lane)
    pos_ref[...] = jnp.where(lane == 0, pos[0], jnp.where(lane == 1, pos[1], 0.0))
    r_iota = lax.broadcasted_iota(jnp.int32, (rows, ts), 0).astype(F32)
    assert rows <= 4096
    sel = [(r_iota == _as_row(p, lane)).astype(BF16) for p in pos]
    sbuf[slot] = jnp.dot(sel[0] + sel[1], h_ref[...], preferred_element_type=F32).astype(BF16)
    wts = wts_ref[...]
    wsorted = jnp.zeros((rows, LANES), F32)
    for k in range(TOP_K):
        pieces = _split3(_lane_pick(wts, lane, k))
        wk = jnp.where(lane == 0, pieces[0].astype(F32), jnp.where(lane == 1, pieces[1].astype(F32),
                       jnp.where(lane == 2, pieces[2].astype(F32), 0.0)))
        wsorted = wsorted + jnp.dot(sel[k], wk.astype(BF16), preferred_element_type=F32)
    wbuf[slot] = wsorted

    def issue(e, carry):
        sz, copies = seg_copies(i, slot, e)

        @pl.when(sz > 0)
        def _():
            for c in copies:
                c.start()

        return carry

    lax.fori_loop(0, n_experts, issue, 0)

    tail_tile = nu_ref[0] + i

    @pl.when(jnp.logical_and(i < n - 1, tail_tile < n_tail_tiles))
    def _():
        for c in tail_copies(tail_tile):
            c.start()

    @pl.when(i == n - 1)
    def _():
        for_tail(lambda c: c.start(), tail_tile)

        @pl.when(i >= 1)
        def _():
            wait_tile(i - 1, 1 - slot)

        wait_tile(i, slot)
        for_tail(lambda c: c.wait(), nu_ref[0])
        for_expert_tails(lambda c: c.wait())


def _dispatch_rows(hb, ids, wts, seg, n_experts, n_tiles_max):
    t, d = hb.shape
    ts, te = SORT_TILE, EXPERT_TILE
    rows = TOP_K * ts + n_experts * SEG_ALIGN
    n_slots = n_tiles_max * te

    def tile_map(i, *_):
        return (i, 0)

    return pl.pallas_call(
        functools.partial(_dispatch_kernel, n_experts=n_experts),
        out_shape=(jax.ShapeDtypeStruct((n_slots, d), BF16), jax.ShapeDtypeStruct((n_slots, LANES), F32),
                   jax.ShapeDtypeStruct((t, LANES), F32)),
        grid_spec=pltpu.PrefetchScalarGridSpec(
            num_scalar_prefetch=7,
            grid=(t // ts,),
            in_specs=[pl.BlockSpec((ts, d), tile_map), pl.BlockSpec((ts, LANES), tile_map),
                      pl.BlockSpec((ts, LANES), tile_map)],
            out_specs=(pl.BlockSpec(memory_space=pl.ANY), pl.BlockSpec(memory_space=pl.ANY),
                       pl.BlockSpec((ts, LANES), tile_map)),
            scratch_shapes=[
                pltpu.VMEM((2, rows, d), BF16),
                pltpu.VMEM((2, rows, LANES), F32),
                pltpu.VMEM((te, d), BF16),
                pltpu.VMEM((te, LANES), F32),
                pltpu.SemaphoreType.DMA((2,)),
                pltpu.SemaphoreType.DMA((2,)),
                pltpu.SemaphoreType.DMA((2,)),
                pltpu.SemaphoreType.DMA((2,)),
            ],
        ),
        compiler_params=_params("arbitrary"),
        name="dispatch",
    )(seg["src"], seg["dst"], seg["size"], seg["tile_tot"], seg["fill_start"], seg["fill_size"], seg["n_used"],
      hb, ids, wts)


def _expert_rows_kernel(te_ref, nu_ref, first_ref, slot_ref, next_ref, x_ref, w_ref, wg_hbm, wu_hbm, wd_hbm, o_ref,
                        wgf, wuf, wdf, wsem, wgb, wub, wdb, *, layer_base):
    i = pl.program_id(0)
    n_used = nu_ref[0]

    def weight_copies(e, sl):
        return (pltpu.make_async_copy(wg_hbm.at[layer_base + e], wgf.at[sl], wsem.at[sl]),
                pltpu.make_async_copy(wu_hbm.at[layer_base + e], wuf.at[sl], wsem.at[sl]),
                pltpu.make_async_copy(wd_hbm.at[layer_base + e], wdf.at[sl], wsem.at[sl]))

    @pl.when(i == 0)
    def _():
        for c in weight_copies(te_ref[0], 0):
            c.start()

    @pl.when(i < n_used)
    def _():
        @pl.when(first_ref[i] == 1)
        def _():
            sl = slot_ref[i]
            for c in weight_copies(te_ref[i], sl):
                c.wait()
            wgb[...] = wgf[sl].astype(BF16)
            wub[...] = wuf[sl].astype(BF16)
            wdb[...] = wdf[sl].astype(BF16)
            nxt = next_ref[i]

            @pl.when(nxt >= 0)
            def _():
                for c in weight_copies(nxt, 1 - sl):
                    c.start()

        x = x_ref[...]
        a = jnp.dot(x, wgb[...], preferred_element_type=F32)
        u = jnp.dot(x, wub[...], preferred_element_type=F32)
        w = jnp.sum(w_ref[...], axis=-1, keepdims=True)
        hid = _silu(a) * u * w
        o_ref[...] = jnp.dot(hid.astype(BF16), wdb[...], preferred_element_type=F32).astype(o_ref.dtype)


def _expert_rows(xe, we, seg, wg, wu, wd, layer_base):
    n_slots, d = xe.shape
    f = wg.shape[-1]
    te = EXPERT_TILE

    def row_map(i, te_ref, nu_ref, *_):
        return (jnp.minimum(i, nu_ref[0] - 1), 0)

    return pl.pallas_call(
        functools.partial(_expert_rows_kernel, layer_base=layer_base),
        out_shape=jax.ShapeDtypeStruct((n_slots, d), BF16),
        grid_spec=pltpu.PrefetchScalarGridSpec(
            num_scalar_prefetch=5,
            grid=(n_slots // te,),
            in_specs=[
                pl.BlockSpec((te, d), row_map),
                pl.BlockSpec((te, LANES), row_map),
                pl.BlockSpec(memory_space=pl.ANY),
                pl.BlockSpec(memory_space=pl.ANY),
                pl.BlockSpec(memory_space=pl.ANY),
            ],
            out_specs=pl.BlockSpec((te, d), row_map),
            scratch_shapes=[
                pltpu.VMEM((2, d, f), F32), pltpu.VMEM((2, d, f), F32), pltpu.VMEM((2, f, d), F32),
                pltpu.SemaphoreType.DMA((2,)),
                pltpu.VMEM((d, f), BF16), pltpu.VMEM((d, f), BF16), pltpu.VMEM((f, d), BF16),
            ],
        ),
        input_output_aliases={5: 0},
        compiler_params=_params("arbitrary"),
        name="experts",
    )(seg["tile_expert"], seg["n_used"], seg["tile_first"], seg["tile_slot"], seg["tile_next"], xe, we, wg, wu, wd)


def _collect_kernel(src_ref, dst_ref, size_ref, tot_ref, y_hbm, x_ref, pos_ref, gp, gs, *rest, npt, n_experts, final,
                    rank):
    if final:
        fg_ref, op_ref, os_ref, buf, sem, o_ref = rest
    else:
        n1_refs = rest[:8]
        o_ref, h_ref, la_ref, buf, sem = rest[8:]
    i = pl.program_id(0)
    n = pl.num_programs(0)
    ts = x_ref.shape[0]
    rows = buf.shape[1]
    slot = lax.rem(i, 2)

    def start(tile, sl):
        def body(e, carry):
            k = tile * n_experts + e
            sz = pl.multiple_of(size_ref[k], SEG_ALIGN)
            s0 = pl.multiple_of(src_ref[k], SEG_ALIGN)
            d0 = pl.multiple_of(dst_ref[k], SEG_ALIGN)

            @pl.when(sz > 0)
            def _():
                pltpu.make_async_copy(y_hbm.at[pl.ds(d0, sz), :], buf.at[sl, pl.ds(s0, sz), :], sem.at[sl]).start()

            return carry

        lax.fori_loop(0, n_experts, body, 0)

    @pl.when(i == 0)
    def _():
        buf[...] = jnp.zeros(buf.shape, buf.dtype)
        start(0, 0)

    @pl.when(i + 1 < n)
    def _():
        start(i + 1, 1 - slot)

    tot = pl.multiple_of(tot_ref[i], SEG_ALIGN)
    pltpu.make_async_copy(y_hbm.at[pl.ds(0, tot), :], buf.at[slot, pl.ds(0, tot), :], sem.at[slot]).wait()
    lane = lax.broadcasted_iota(jnp.int32, (ts, LANES), 1).astype(F32)
    pos = pos_ref[...]
    r_iota = lax.broadcasted_iota(jnp.int32, (ts, rows), 1).astype(F32)
    pick = ((r_iota == _lane_pick(pos, lane, 0)) | (r_iota == _lane_pick(pos, lane, 1))).astype(BF16)
    ff = jnp.dot(pick, buf[slot], preferred_element_type=F32)
    _store_by_group(i, npt, None, o_ref, lambda v, m: v[0] + m[0] * v[1], [x_ref[...], ff], [gp], [gs])
    if final:
        y = _rms(o_ref[...], fg_ref[...])

        @pl.when(i < npt)
        def _():
            op_ref[...] = y

        @pl.when(i >= npt)
        def _():
            os_ref[...] = y
    else:
        _norm1_body(i, o_ref[...], *n1_refs, h_ref, la_ref, npt=npt, rank=rank)


def _collect(ye, pos, x, seg, mod_p, mod_s, dims, n_experts, final_g=None, next_norm=None):
    t, d = x.shape
    ts = SORT_TILE
    npt = dims["npt"]
    rows = TOP_K * ts + n_experts * SEG_ALIGN
    p_specs, s_specs = _mod_specs((5,), d, npt, dims["tps"], dims["n_seq"], dims["dec_batch"])
    final = final_g is not None
    assert final != (next_norm is not None)

    def strip(spec):
        return pl.BlockSpec(spec.block_shape, lambda i, *_, m=spec.index_map: m(i))

    def tile_map(i, *_):
        return (i, 0)

    in_specs = [
        pl.BlockSpec(memory_space=pl.ANY),
        pl.BlockSpec((ts, d), tile_map),
        pl.BlockSpec((ts, LANES), tile_map),
        *[strip(s) for s in p_specs],
        *[strip(s) for s in s_specs],
    ]
    scratch = [pltpu.VMEM((2, rows, d), BF16), pltpu.SemaphoreType.DMA((2,))]
    args = [seg["src"], seg["dst"], seg["size"], seg["tile_tot"], ye, x, pos, mod_p, mod_s]
    if final:
        in_specs.append(pl.BlockSpec((1, d), lambda i, *_: (0, 0)))
        args.append(final_g.reshape(1, d))
        out_shape = (jax.ShapeDtypeStruct((npt * ts, d), F32), jax.ShapeDtypeStruct((t - npt * ts, d), F32))
        out_specs = (pl.BlockSpec((ts, d), lambda i, *_: (jnp.minimum(i, npt - 1), 0)),
                     pl.BlockSpec((ts, d), lambda i, *_: (jnp.maximum(i - npt, 0), 0)))
        scratch.append(pltpu.VMEM((ts, d), F32))
        rank = None
    else:
        nn = next_norm
        rank, qk = nn["rank"], nn["gw2"].shape[1]
        n1_p, n1_s = _mod_specs((0, 1), d, npt, dims["tps"], dims["n_seq"], dims["dec_batch"])
        in_specs += [pl.BlockSpec((1, d), lambda i, *_: (0, 0)),
                     *[strip(s) for s in n1_p], *[strip(s) for s in n1_s],
                     pl.BlockSpec((None, LANES, d), lambda i, *_: (nn["layer"], nn["n_main"] // LANES, 0)),
                     pl.BlockSpec(nn["gw2"].shape, lambda i, *_: (0, 0)),
                     pl.BlockSpec((1, qk), lambda i, *_: (0, 0))]
        args += [nn["g"].reshape(1, d), nn["mod_p"], nn["mod_p"], nn["mod_s"], nn["mod_s"], nn["w_in_t"], nn["gw2"],
                 nn["gb"].reshape(1, qk)]
        out_shape = (jax.ShapeDtypeStruct((t, d), F32), jax.ShapeDtypeStruct((t, d), BF16),
                     jax.ShapeDtypeStruct((t, qk), F32))
        out_specs = (pl.BlockSpec((ts, d), tile_map), pl.BlockSpec((ts, d), tile_map), pl.BlockSpec((ts, qk), tile_map))
    return pl.pallas_call(
        functools.partial(_collect_kernel, npt=npt, n_experts=n_experts, final=final, rank=rank),
        out_shape=out_shape,
        grid_spec=pltpu.PrefetchScalarGridSpec(
            num_scalar_prefetch=4, grid=(t // ts,), in_specs=in_specs, out_specs=out_specs, scratch_shapes=scratch),
        compiler_params=_params("arbitrary"),
        name="collect_final" if final else "collect",
    )(*args)


def kernel(x_prompt, x_sample, c_prompt, c_sample, state_conv, state_gla, w_ada, b_ada, norm1_g, norm2_g, w_in, conv_w, conv_b, conv_ln_g, conv_ln_b, gate_w2, gate_b, gla_norm_g, w_out, router_grp_w, router_grp_b, router_exp_w, router_exp_b, exp_w_gate, exp_w_up, exp_w_down, final_norm_g):
    n_seq, seq_len, d = x_prompt.shape
    dec_batch, dec_seq, _ = x_sample.shape
    depth = w_ada.shape[0]
    kw, d_conv = conv_w.shape[1:]
    heads, dv = gla_norm_g.shape[1:]
    rank, qkw = gate_w2.shape[1:]
    dk = qkw // heads
    n_grp, n_exp = router_exp_w.shape[2:]
    n_experts = n_grp * n_exp
    tp, ts = n_seq * seq_len, dec_batch * dec_seq
    t = tp + ts
    tm = ROW_TILE
    assert ts == tm and seq_len % tm == 0 and d_conv == heads * dv and kw - 1 <= CONV_HALO
    n_main = 2 * d_conv + 2 * qkw + 2 * heads * dv
    dims = dict(npt=tp // tm, tps=seq_len // tm, n_seq=n_seq, dec_batch=dec_batch, dec_seq=dec_seq)

    xp, xs, xs_blk = x_prompt.reshape(tp, d), x_sample.transpose(1, 0, 2).reshape(ts, d), 0
    w_in_t = jnp.swapaxes(w_in, 1, 2)
    st_t = jnp.transpose(state_conv, (0, 2, 1, 3))
    pad = (-n_seq) % 8
    c_all = jnp.concatenate([c_prompt, jnp.zeros((pad, d), F32), c_sample], axis=0)
    mod = _ada(c_all, w_ada, b_ada)
    st = SORT_TILE
    assert tm % st == 0 and st % dec_batch == 0
    dims_sort = dict(npt=tp // st, tps=seq_len // st, n_seq=n_seq, dec_batch=dec_batch, dec_seq=dec_seq)
    n_tiles_max = -(-(TOP_K * t + (t // st) * n_experts * (SEG_ALIGN - 1)) // EXPERT_TILE) + n_experts

    def layer_mods(l):
        return mod[l, :n_seq].reshape(n_seq, 1, 6 * d), mod[l, n_seq + pad:]

    def gate_proj(l):
        return jnp.pad(gate_w2[l], ((0, LANES - rank), (0, 0))).astype(BF16)

    conv_p, gla_p, conv_u, gla_s = [], [], [], []
    for l in range(depth):
        mod_p, mod_s = layer_mods(l)
        if l == 0:
            h, la = _norm1(xp, xs, xs_blk, norm1_g[l], mod_p, mod_s, w_in_t, l, n_main, rank, gate_proj(l), gate_b[l], dims)
        z = _inproj(h, w_in_t, l, n_main)

        cv_p, cb_p = _conv_prompt(z, conv_w[l], conv_b[l], conv_ln_g[l], conv_ln_b[l], n_seq, seq_len)
        cv_s, u_s = _conv_sample(z, st_t, l, conv_w[l], conv_b[l], conv_ln_g[l], conv_ln_b[l], tp, dec_seq, dec_batch)
        conv_p.append(cb_p)
        conv_u.append(u_s)

        go_p, gs_p = _gla_prompt(z, la, gla_norm_g[l], n_seq, seq_len, heads, dk, dv)

        def seq_major(a):
            return a.reshape(dec_seq, dec_batch, a.shape[-1]).transpose(1, 0, 2)

        zs = z[tp:]
        q3 = seq_major(zs[:, 2 * d_conv:2 * d_conv + qkw])
        k3 = seq_major(zs[:, 2 * d_conv + qkw:2 * d_conv + 2 * qkw])
        v3 = seq_major(zs[:, 2 * d_conv + 2 * qkw:2 * d_conv + 2 * qkw + heads * dv])
        g3 = seq_major(zs[:, 2 * d_conv + 2 * qkw + heads * dv:n_main])
        go_s, gs_s = _gla_sample(q3, k3, v3, g3, seq_major(la[tp:]), state_gla, l, gla_norm_g[l], heads, dk, dv,
                                 prev_states=tuple(gla_s) if l + 1 == depth else ())
        go_s = go_s.transpose(1, 0, 2).reshape(ts, heads * dv).astype(BF16)
        gla_p.append(gs_p)
        gla_s.append(gs_s)

        x = _outproj(cv_p, go_p, cv_s.reshape(ts, d_conv), go_s, xp, xs, xs_blk, w_out, l, mod_p, mod_s, dims)

        wr = jnp.concatenate([router_grp_w[l], router_exp_w[l].reshape(d, n_experts)], axis=1)
        wr = jnp.pad(wr, ((0, 0), (0, LANES - wr.shape[1])))
        wr_hi = wr.astype(BF16)
        wr_lo = (wr - wr_hi.astype(F32)).astype(BF16)
        br = jnp.concatenate([router_grp_b[l], router_exp_b[l].reshape(-1)])
        br = jnp.pad(br, (0, LANES - br.shape[0])).reshape(1, LANES)
        hb, ids, wts = _norm2(x, norm2_g[l], mod_p, mod_s, wr_hi, wr_lo, br, dims, n_grp, n_exp)
        seg = _segments(ids[:, :TOP_K], n_experts, n_tiles_max)
        xe, we, pos = _dispatch_rows(hb, ids, wts, seg, n_experts, n_tiles_max)
        f = exp_w_gate.shape[-1]
        ye = _expert_rows(xe, we, seg, exp_w_gate.reshape(depth * n_experts, d, f),
                          exp_w_up.reshape(depth * n_experts, d, f), exp_w_down.reshape(depth * n_experts, f, d),
                          l * n_experts)
        if l + 1 < depth:
            nmod_p, nmod_s = layer_mods(l + 1)
            x, h, la = _collect(ye, pos, x, seg, mod_p, mod_s, dims_sort, n_experts,
                                next_norm=dict(g=norm1_g[l + 1], mod_p=nmod_p, mod_s=nmod_s, w_in_t=w_in_t, layer=l + 1,
                                               n_main=n_main, rank=rank, gw2=gate_proj(l + 1), gb=gate_b[l + 1]))
            xp, xs, xs_blk = x, x, dims["npt"]
        else:
            y_p, y_s = _collect(ye, pos, x, seg, mod_p, mod_s, dims_sort, n_experts, final_g=final_norm_g)

    y_prompt = y_p.reshape(n_seq, seq_len, d)
    y_sample = y_s.reshape(dec_seq, dec_batch, d).transpose(1, 0, 2)
    conv_s = jnp.transpose(_conv_state_sample(st_t, jnp.stack(conv_u)), (0, 2, 1, 3))
    gla_s_all = gla_s[-1] if depth > 1 else gla_s[0][None]
    return (y_prompt, y_sample, jnp.stack(conv_p), jnp.stack(gla_p), conv_s, gla_s_all)
```

```python
import functools

import jax
import jax.numpy as jnp
from jax import lax
from jax.experimental import pallas as pl
from jax.experimental.pallas import tpu as pltpu

F32 = jnp.float32
BF16 = jnp.bfloat16

EPS = 1e-6
GATE_TAU = 16.0
GLA_CHUNK = 32
GLA_BLOCK = 256
TOP_K = 2

ROW_TILE = 512
EXPERT_TILE = 512
SORT_TILE = 512
SEG_ALIGN = 16
CONV_ROWS = 64
CONV_HALO = 32
LANES = 128
SUBLANES = 8
VMEM_LIMIT = 56 * 1024 * 1024


def _params(*sem):
    return pltpu.CompilerParams(dimension_semantics=sem, vmem_limit_bytes=VMEM_LIMIT)


def _bdot(a, b):
    return jnp.dot(a.astype(BF16), b.astype(BF16), preferred_element_type=F32)


def _round_bf16(x):
    return x.astype(BF16).astype(F32)


def _split3(x):
    hi = x.astype(BF16)
    r = x - hi.astype(F32)
    mid = r.astype(BF16)
    lo = (r - mid.astype(F32)).astype(BF16)
    return hi, mid, lo


def _silu(x):
    return x * jax.nn.sigmoid(x)


def _store_by_group(i, n_prompt_tiles, dec_seq, out_ref, fn, vals, p_refs, s_refs):
    @pl.when(i < n_prompt_tiles)
    def _():
        out_ref[...] = fn(vals, [r[...] for r in p_refs]).astype(out_ref.dtype)

    @pl.when(i >= n_prompt_tiles)
    def _():
        mods = [r[...] for r in s_refs]
        nb = mods[0].shape[0]
        for t in range(out_ref.shape[0] // nb):
            rows = slice(t * nb, (t + 1) * nb)
            out_ref[rows, :] = fn([v[rows] for v in vals], mods).astype(out_ref.dtype)


def _ada_kernel(c_ref, w_ref, b_ref, o_ref):
    c = c_ref[...]
    o_ref[...] = _bdot(_silu(c), w_ref[...]) + b_ref[...]


def _ada(c_all, w_ada, b_ada):
    depth, d, n = w_ada.shape
    rows = c_all.shape[0]
    tn = 1024
    return pl.pallas_call(
        _ada_kernel,
        out_shape=jax.ShapeDtypeStruct((depth, rows, n), F32),
        grid=(depth, n // tn),
        in_specs=[
            pl.BlockSpec((rows, d), lambda l, j: (0, 0)),
            pl.BlockSpec((None, d, tn), lambda l, j: (l, 0, j)),
            pl.BlockSpec((None, 1, tn), lambda l, j: (l, 0, j)),
        ],
        out_specs=pl.BlockSpec((None, rows, tn), lambda l, j: (l, 0, j)),
        compiler_params=_params("arbitrary", "arbitrary"),
        name="ada",
    )(c_all, w_ada, b_ada.reshape(depth, 1, n))


def _mod_specs(cols, width, n_prompt_tiles, tiles_per_seq, n_seq, dec_batch, grid_rank=1, row_axis=0, col_fn=None):
    p_specs, s_specs = [], []
    for c in cols:
        def p_map(*idx, c=c):
            b = jnp.minimum(idx[row_axis] // tiles_per_seq, n_seq - 1)
            return (b, 0, c if col_fn is None else col_fn(c, idx))

        def s_map(*idx, c=c):
            return (0, c if col_fn is None else col_fn(c, idx))

        p_specs.append(pl.BlockSpec((None, 1, width), p_map))
        s_specs.append(pl.BlockSpec((dec_batch, width), s_map))
    return p_specs, s_specs


def _rms(x, g):
    return x * lax.rsqrt(jnp.mean(x * x, axis=-1, keepdims=True) + EPS) * g


def _norm1_kernel(xp_ref, xs_ref, *refs, npt, dec_seq, rank):
    i = pl.program_id(0)
    _norm1_body(i, jnp.where(i < npt, xp_ref[...], xs_ref[...]), *refs, npt=npt, rank=rank)


def _norm1_body(i, x, g_ref, shp, scp, shs, scs, wgl_ref, gw2_ref, gb_ref, h_ref, la_ref, *, npt, rank):
    y = _rms(x, g_ref[...])
    _store_by_group(i, npt, None, h_ref, lambda v, m: v[0] * (1.0 + m[1]) + m[0], [y], [shp, scp], [shs, scs])
    row = lax.broadcasted_iota(jnp.int32, wgl_ref.shape, 0)
    w_gl = jnp.where(row < rank, wgl_ref[...], 0.0).astype(BF16)
    gate_lr = lax.dot_general(h_ref[...], w_gl, (((1,), (1,)), ((), ())), preferred_element_type=F32)
    pre = _bdot(gate_lr, gw2_ref[...]) + gb_ref[...]
    la_ref[...] = (jnp.minimum(pre, 0.0) - jnp.log1p(jnp.exp(-jnp.abs(pre)))) * (1.0 / GATE_TAU)


def _norm1(xp, xs, xs_blk, g, mod_p, mod_s, w_in_t, layer, n_main, rank, gw2, gb, dims):
    d = xp.shape[1]
    tm = ROW_TILE
    npt = dims["npt"]
    t = (npt + 1) * tm
    p_specs, s_specs = _mod_specs((0, 1), d, npt, dims["tps"], dims["n_seq"], dims["dec_batch"])
    qk = gw2.shape[1]
    assert n_main % LANES == 0 and rank <= LANES
    return pl.pallas_call(
        functools.partial(_norm1_kernel, npt=npt, dec_seq=dims["dec_seq"], rank=rank),
        out_shape=(jax.ShapeDtypeStruct((t, d), BF16), jax.ShapeDtypeStruct((t, qk), F32)),
        grid=(t // tm,),
        in_specs=[
            pl.BlockSpec((tm, d), lambda i: (jnp.minimum(i, npt - 1), 0)),
            pl.BlockSpec((tm, d), lambda i: (xs_blk, 0)),
            pl.BlockSpec((1, d), lambda i: (0, 0)),
            *p_specs,
            *s_specs,
            pl.BlockSpec((None, LANES, d), lambda i: (layer, n_main // LANES, 0)),
            pl.BlockSpec(gw2.shape, lambda i: (0, 0)),
            pl.BlockSpec((1, qk), lambda i: (0, 0)),
        ],
        out_specs=(pl.BlockSpec((tm, d), lambda i: (i, 0)), pl.BlockSpec((tm, qk), lambda i: (i, 0))),
        compiler_params=_params("arbitrary"),
        name="norm1",
    )(xp, xs, g.reshape(1, d), mod_p, mod_p, mod_s, mod_s, w_in_t, gw2, gb.reshape(1, qk))


def _inproj_kernel(h_ref, w_ref, o_ref, wb_ref):
    @pl.when(pl.program_id(1) == 0)
    def _():
        wb_ref[...] = w_ref[...].astype(BF16)

    o_ref[...] = lax.dot_general(h_ref[...], wb_ref[...], (((1,), (1,)), ((), ())), preferred_element_type=F32)


def _inproj(h, w_in_t, layer, n_cols):
    t, d = h.shape
    tm, tn = 2 * ROW_TILE, 1024
    return pl.pallas_call(
        _inproj_kernel,
        out_shape=jax.ShapeDtypeStruct((t, n_cols), F32),
        grid=(n_cols // tn, pl.cdiv(t, tm)),
        in_specs=[
            pl.BlockSpec((tm, d), lambda j, i: (i, 0)),
            pl.BlockSpec((None, tn, d), lambda j, i: (layer, j, 0)),
        ],
        out_specs=pl.BlockSpec((tm, tn), lambda j, i: (i, j)),
        scratch_shapes=[pltpu.VMEM((tn, d), BF16)],
        compiler_params=_params("arbitrary", "arbitrary"),
        name="inproj",
    )(h, w_in_t)


def _ln_silu(y, g, b):
    mu = jnp.mean(y, axis=-1, keepdims=True)
    yc = y - mu
    var = jnp.mean(yc * yc, axis=-1, keepdims=True)
    return _silu(yc * lax.rsqrt(var + EPS) * g + b)


def _conv_prompt_kernel(a_ref, b_ref, cw_ref, cb_ref, lng_ref, lnb_ref, o_ref, st_ref, full_ref, cwb_ref, y_ref, *, kw):
    j = pl.program_id(1)
    tm, c = a_ref.shape
    halo = CONV_HALO
    phases = full_ref.shape[0]
    assert phases == SUBLANES

    @pl.when(jnp.logical_and(pl.program_id(0) == 0, j == 0))
    def _():
        full_ref[...] = jnp.zeros(full_ref.shape, F32)
        for w in range(kw):
            cwb_ref[w] = jnp.broadcast_to(cw_ref[w:w + 1, :], (SUBLANES, c))

    prev = full_ref[0, tm + halo - SUBLANES:tm + halo, :]
    tail = jnp.where(j == 0, 0.0, prev)

    @pl.when(j == 0)
    def _():
        for p in range(phases):
            full_ref[p, 0:halo, :] = jnp.zeros((halo, c), F32)

    @pl.when(j > 0)
    def _():
        for p in range(phases):
            full_ref[p, 0:halo, :] = full_ref[p, tm:tm + halo, :]

    u = a_ref[...] * jax.nn.sigmoid(b_ref[...])
    full_ref[0, halo:halo + tm, :] = u
    ext = jnp.concatenate([tail, u], axis=0)
    for p in range(1, phases):
        full_ref[p, halo - SUBLANES:halo - SUBLANES + tm, :] = pltpu.roll(ext, tm + SUBLANES - p, 0)[0:tm]
    off = halo - (kw - 1)
    rb = CONV_ROWS

    def body(r, carry):
        r0 = pl.multiple_of(r * rb, rb)
        for lt in range(c // LANES):
            cols = slice(lt * LANES, (lt + 1) * LANES)
            acc = None
            for p in range(phases):
                x = full_ref[p, pl.ds(r0, rb + halo), cols]
                for a in range(halo // phases + 1):
                    w = a * phases + p - off
                    if 0 <= w < kw and a * phases + rb <= rb + halo:
                        term = x[a * phases:a * phases + rb] * jnp.concatenate([cwb_ref[w, :, cols]] * (rb // SUBLANES), axis=0)
                        acc = term if acc is None else acc + term
            y_ref[pl.ds(r0, rb), cols] = acc
        return carry

    lax.fori_loop(0, tm // rb, body, 0)
    y = _ln_silu(y_ref[...] + cb_ref[...], lng_ref[...], lnb_ref[...])
    o_ref[...] = y.astype(o_ref.dtype)

    @pl.when(j == pl.num_programs(1) - 1)
    def _():
        st_ref[...] = full_ref[0, halo + tm - (kw - 1):halo + tm, :]


def _conv_prompt(z, cw, cb, lng, lnb, n_seq, seq_len):
    kw, c = cw.shape
    tm = ROW_TILE
    tps = seq_len // tm
    return pl.pallas_call(
        functools.partial(_conv_prompt_kernel, kw=kw),
        out_shape=(jax.ShapeDtypeStruct((n_seq * seq_len, c), BF16), jax.ShapeDtypeStruct((n_seq, kw - 1, c), F32)),
        grid=(n_seq, tps),
        in_specs=[
            pl.BlockSpec((tm, c), lambda b, j: (b * tps + j, 0)),
            pl.BlockSpec((tm, c), lambda b, j: (b * tps + j, 1)),
            pl.BlockSpec((kw, c), lambda b, j: (0, 0)),
            pl.BlockSpec((1, c), lambda b, j: (0, 0)),
            pl.BlockSpec((1, c), lambda b, j: (0, 0)),
            pl.BlockSpec((1, c), lambda b, j: (0, 0)),
        ],
        out_specs=(
            pl.BlockSpec((tm, c), lambda b, j: (b * tps + j, 0)),
            pl.BlockSpec((None, kw - 1, c), lambda b, j: (b, 0, 0)),
        ),
        scratch_shapes=[pltpu.VMEM((SUBLANES, tm + CONV_HALO, c), F32), pltpu.VMEM((kw, SUBLANES, c), F32),
                        pltpu.VMEM((tm, c), F32)],
        compiler_params=_params("arbitrary", "arbitrary"),
        name="conv_prompt",
    )(z, z, cw, cb.reshape(1, c), lng.reshape(1, c), lnb.reshape(1, c))


def _conv_sample_kernel(*refs, kw, dec_seq):
    a_refs = refs[0:dec_seq]
    b_refs = refs[dec_seq:2 * dec_seq]
    st_ref, cw_ref, cb_ref, lng_ref, lnb_ref, o_ref, u_ref = refs[2 * dec_seq:]
    hist = kw - 1
    u = [a_refs[t][...] * jax.nn.sigmoid(b_refs[t][...]) for t in range(dec_seq)]
    def row(j):
        return st_ref[j] if j < hist else u[j - hist]

    for t in range(dec_seq):
        acc = row(t) * cw_ref[0:1, :]
        for w in range(1, kw):
            acc = acc + row(t + w) * cw_ref[w:w + 1, :]
        y = _ln_silu(acc + cb_ref[...], lng_ref[...], lnb_ref[...])
        o_ref[t] = y.astype(o_ref.dtype)
        u_ref[t] = u[t]


def _conv_sample(z, st_t, layer, cw, cb, lng, lnb, row0, dec_seq, dec_batch):
    kw, c = cw.shape
    bs = 16
    a_specs = [pl.BlockSpec((bs, c), lambda s, t=t: ((row0 + t * dec_batch) // bs + s, 0)) for t in range(dec_seq)]
    b_specs = [pl.BlockSpec((bs, c), lambda s, t=t: ((row0 + t * dec_batch) // bs + s, 1)) for t in range(dec_seq)]
    vec = pl.BlockSpec((1, c), lambda s: (0, 0))
    return pl.pallas_call(
        functools.partial(_conv_sample_kernel, kw=kw, dec_seq=dec_seq),
        out_shape=(jax.ShapeDtypeStruct((dec_seq, dec_batch, c), BF16), jax.ShapeDtypeStruct((dec_seq, dec_batch, c), F32)),
        grid=(dec_batch // bs,),
        in_specs=[*a_specs, *b_specs, pl.BlockSpec((None, kw - 1, bs, c), lambda s: (layer, 0, s, 0)),
                  pl.BlockSpec((kw, c), lambda s: (0, 0)), vec, vec, vec],
        out_specs=(pl.BlockSpec((dec_seq, bs, c), lambda s: (0, s, 0)), pl.BlockSpec((dec_seq, bs, c), lambda s: (0, s, 0))),
        compiler_params=_params("arbitrary"),
        name="conv_sample",
    )(*([z] * (2 * dec_seq)), st_t, cw, cb.reshape(1, c), lng.reshape(1, c), lnb.reshape(1, c))


def _conv_state_kernel(st_ref, u_ref, o_ref):
    hist, n_new = st_ref.shape[0], u_ref.shape[0]
    for j in range(hist - n_new):
        o_ref[j] = st_ref[j + n_new]
    for t in range(n_new):
        o_ref[hist - n_new + t] = u_ref[t]


def _conv_state_sample(st_t, u_all):
    depth, hist, nb, c = st_t.shape
    n_new = u_all.shape[1]
    assert n_new <= hist
    bs = 16
    return pl.pallas_call(
        _conv_state_kernel,
        out_shape=jax.ShapeDtypeStruct(st_t.shape, st_t.dtype),
        grid=(depth, nb // bs),
        in_specs=[pl.BlockSpec((None, hist, bs, c), lambda l, s: (l, 0, s, 0)),
                  pl.BlockSpec((None, n_new, bs, c), lambda l, s: (l, 0, s, 0))],
        out_specs=pl.BlockSpec((None, hist, bs, c), lambda l, s: (l, 0, s, 0)),
        compiler_params=_params("arbitrary", "arbitrary"),
        name="conv_state",
    )(st_t, u_all)


def _gla_prompt_kernel(qk_ref, v_ref, g_ref, la_ref, gn_ref, o_ref, sfin_ref, st_ref, sn_ref, *, heads, dk, dv):
    j = pl.program_id(1)
    tm = qk_ref.shape[0]
    ck = GLA_CHUNK
    nch = tm // ck
    qkw = heads * dk

    @pl.when(j == 0)
    def _():
        st_ref[...] = jnp.zeros(st_ref.shape, F32)

    la = la_ref[...]
    sb = GLA_BLOCK
    blocks = [slice(r, r + sb) for r in range(0, tm, sb)]
    row = lax.broadcasted_iota(jnp.int32, (sb, sb), 0)
    col = lax.broadcasted_iota(jnp.int32, (sb, sb), 1)
    same = (row // ck) == (col // ck)
    causal = same & (col <= row)
    tri_incl = causal.astype(BF16)
    tri_after = (same & (col > row)).astype(BF16)
    sel = (lax.broadcasted_iota(jnp.int32, (nch, tm), 1) // ck == lax.broadcasted_iota(jnp.int32, (nch, tm), 0)).astype(BF16)
    parts = _split3(la)

    def blockwise(tri):
        return jnp.concatenate([sum(jnp.dot(tri, p[rs], preferred_element_type=F32) for p in parts) for rs in blocks], axis=0)

    b = blockwise(tri_incl)
    rest = blockwise(tri_after)
    tot = sum(jnp.dot(sel, p, preferred_element_type=F32) for p in parts)
    qk = qk_ref[...]
    q = qk[:, :qkw] * (dk ** -0.5)
    k = qk[:, qkw:]
    q_dec = (q * jnp.exp(b)).astype(BF16)
    k_inv = (k * jnp.exp(-b)).astype(BF16)
    k_end = _round_bf16(k * jnp.exp(rest))
    decay = jnp.exp(tot)
    v_all = v_ref[...]
    g_all = g_ref[...]
    for h in range(heads):
        ks = slice(h * dk, (h + 1) * dk)
        vs = slice(h * dv, (h + 1) * dv)
        qd, ki, ke = q_dec[:, ks], k_inv[:, ks], k_end[:, ks]
        vh = v_all[:, vs]
        vb = vh.astype(BF16)
        vr = _round_bf16(vh)
        intra = []
        for rs in blocks:
            att = lax.dot_general(qd[rs], ki[rs], (((1,), (1,)), ((), ())), preferred_element_type=F32)
            att = jnp.where(causal, att, 0.0).astype(BF16)
            intra.append(jnp.dot(att, vb[rs], preferred_element_type=F32))
        o = jnp.concatenate(intra, axis=0)
        s = st_ref[h]
        for n in range(nch):
            rs = slice(n * ck, (n + 1) * ck)
            sn_ref[n] = s.astype(BF16)
            upd = lax.dot_general(vr[rs], ke[rs], (((0,), (0,)), ((), ())), preferred_element_type=F32)
            s = s * decay[n:n + 1, ks] + upd
        st_ref[h] = s
        inter = [lax.dot_general(qd[n * ck:(n + 1) * ck], sn_ref[n], (((1,), (1,)), ((), ())), preferred_element_type=F32)
                 for n in range(nch)]
        o = o + jnp.concatenate(inter, axis=0)
        o = _rms(o, gn_ref[:, vs]) * _silu(g_all[:, vs])
        o_ref[:, vs] = o.astype(o_ref.dtype)

    @pl.when(j == pl.num_programs(1) - 1)
    def _():
        for h in range(heads):
            sfin_ref[h] = st_ref[h].T


def _gla_prompt(z, la, gn, n_seq, seq_len, heads, dk, dv):
    tm = ROW_TILE
    tps = seq_len // tm
    w = heads * dv
    qkw = heads * dk
    assert 2 * qkw == w
    return pl.pallas_call(
        functools.partial(_gla_prompt_kernel, heads=heads, dk=dk, dv=dv),
        out_shape=(jax.ShapeDtypeStruct((n_seq * seq_len, w), BF16), jax.ShapeDtypeStruct((n_seq, heads, dk, dv), F32)),
        grid=(n_seq, tps),
        in_specs=[
            pl.BlockSpec((tm, w), lambda b, j: (b * tps + j, 2)),
            pl.BlockSpec((tm, w), lambda b, j: (b * tps + j, 3)),
            pl.BlockSpec((tm, w), lambda b, j: (b * tps + j, 4)),
            pl.BlockSpec((tm, qkw), lambda b, j: (b * tps + j, 0)),
            pl.BlockSpec((1, w), lambda b, j: (0, 0)),
        ],
        out_specs=(
            pl.BlockSpec((tm, w), lambda b, j: (b * tps + j, 0)),
            pl.BlockSpec((None, heads, dk, dv), lambda b, j: (b, 0, 0, 0)),
        ),
        scratch_shapes=[pltpu.VMEM((heads, dv, dk), F32), pltpu.VMEM((tm // GLA_CHUNK, dv, dk), BF16)],
        compiler_params=_params("arbitrary", "arbitrary"),
        name="gla_prompt",
    )(z, z, z, la, gn.reshape(1, w))


def _gla_sample_kernel(q_ref, k_ref, v_ref, g_ref, la_ref, s_ref, gn_ref, *rest, heads, dk, dv, n_prev):
    bs, ln, _ = q_ref.shape
    if n_prev:
        o_ref, all_ref = rest[n_prev:]
        for p in range(n_prev):
            all_ref[p] = rest[p][...]
        ns_ref = all_ref.at[n_prev]
    else:
        o_ref, ns_ref = rest
    tril = lax.broadcasted_iota(jnp.int32, (ln, ln), 1) <= lax.broadcasted_iota(jnp.int32, (ln, ln), 0)
    for s in range(bs):
        q_s, k_s, v_s, g_s, la_s = q_ref[s], k_ref[s], v_ref[s], g_ref[s], la_ref[s]
        for h in range(heads):
            ks = slice(h * dk, (h + 1) * dk)
            vs = slice(h * dv, (h + 1) * dv)
            la = la_s[:, ks]
            rows = [la[0:1]]
            for t in range(1, ln):
                rows.append(rows[-1] + la[t:t + 1])
            b = jnp.concatenate(rows, axis=0)
            b_last = rows[-1]
            q_dec = _round_bf16(q_s[:, ks] * (dk ** -0.5) * jnp.exp(b))
            k_inv = _round_bf16(k_s[:, ks] * jnp.exp(-b))
            k_end = _round_bf16(k_s[:, ks] * jnp.exp(b_last - b))
            vr = _round_bf16(v_s[:, vs])
            s0 = s_ref[s, h]
            att = lax.dot_general(q_dec, k_inv, (((1,), (1,)), ((), ())), preferred_element_type=F32)
            att = _round_bf16(jnp.where(tril, att, 0.0))
            o = jnp.dot(att, vr, preferred_element_type=F32) + jnp.dot(q_dec, _round_bf16(s0), preferred_element_type=F32)
            upd = lax.dot_general(k_end, vr, (((0,), (0,)), ((), ())), preferred_element_type=F32)
            d_col = jnp.broadcast_to(jnp.exp(b_last), (dk, dk)).T
            ns_ref[s, h] = s0 * jnp.concatenate([d_col] * (dv // dk), axis=1) + upd
            o = _rms(o, gn_ref[:, vs]) * _silu(g_s[:, vs])
            o_ref[s, :, vs] = o.astype(o_ref.dtype)


def _gla_sample(q3, k3, v3, g3, la3, state, layer, gn, heads, dk, dv, prev_states=()):
    nb, ln, w = v3.shape
    qkw = heads * dk
    bs = 8
    n_prev = len(prev_states)
    seq_spec = pl.BlockSpec((bs, ln, qkw), lambda s: (s, 0, 0))
    wide_spec = pl.BlockSpec((bs, ln, w), lambda s: (s, 0, 0))
    in_specs = [seq_spec, seq_spec, wide_spec, wide_spec, seq_spec,
                pl.BlockSpec((None, bs, heads, dk, dv), lambda s: (layer, s, 0, 0, 0)),
                pl.BlockSpec((1, w), lambda s: (0, 0)),
                *[pl.BlockSpec((bs, heads, dk, dv), lambda s: (s, 0, 0, 0))] * n_prev]
    if n_prev:
        state_shape = jax.ShapeDtypeStruct((n_prev + 1, nb, heads, dk, dv), F32)
        state_spec = pl.BlockSpec((n_prev + 1, bs, heads, dk, dv), lambda s: (0, s, 0, 0, 0))
    else:
        state_shape = jax.ShapeDtypeStruct((nb, heads, dk, dv), F32)
        state_spec = pl.BlockSpec((bs, heads, dk, dv), lambda s: (s, 0, 0, 0))
    return pl.pallas_call(
        functools.partial(_gla_sample_kernel, heads=heads, dk=dk, dv=dv, n_prev=n_prev),
        out_shape=(jax.ShapeDtypeStruct((nb, ln, w), F32), state_shape),
        grid=(nb // bs,),
        in_specs=in_specs,
        out_specs=(wide_spec, state_spec),
        compiler_params=_params("arbitrary"),
        name="gla_sample_last" if n_prev else "gla_sample",
    )(q3, k3, v3, g3, la3, state, gn.reshape(1, w), *prev_states)


def _outproj_kernel(cp_ref, op_ref, cs_ref, os_ref, xp_ref, xs_ref, w_ref, gp, gs, y_ref, wb_ref, *, npt, dec_seq):
    i = pl.program_id(1)

    @pl.when(i == 0)
    def _():
        wb_ref[...] = w_ref[...].astype(BF16)

    half = cp_ref.shape[1]

    def mixed(c_ref, o_ref):
        return (jnp.dot(c_ref[...], wb_ref[0:half, :], preferred_element_type=F32)
                + jnp.dot(o_ref[...], wb_ref[half:, :], preferred_element_type=F32))

    @pl.when(i < npt)
    def _():
        y_ref[...] = xp_ref[...] + gp[...] * mixed(cp_ref, op_ref)

    @pl.when(i >= npt)
    def _():
        mix = mixed(cs_ref, os_ref)
        nb = gs.shape[0]
        for t in range(dec_seq):
            rows = slice(t * nb, (t + 1) * nb)
            y_ref[rows, :] = xs_ref[rows, :] + gs[...] * mix[rows]


def _outproj(conv_p, gla_p, conv_s, gla_s, xp, xs, xs_blk, w_out, layer, mod_p, mod_s, dims):
    d = xp.shape[1]
    half = conv_p.shape[1]
    tm, tn = ROW_TILE, 1024
    nj = d // tn
    npt = dims["npt"]
    t = (npt + 1) * tm
    p_specs, s_specs = _mod_specs((2,), tn, dims["npt"], dims["tps"], dims["n_seq"], dims["dec_batch"],
                                  row_axis=1, col_fn=lambda c, idx: c * nj + idx[0])
    return pl.pallas_call(
        functools.partial(_outproj_kernel, npt=dims["npt"], dec_seq=dims["dec_seq"]),
        out_shape=jax.ShapeDtypeStruct((t, d), F32),
        grid=(nj, t // tm),
        in_specs=[
            pl.BlockSpec((tm, half), lambda j, i: (jnp.minimum(i, npt - 1), 0)),
            pl.BlockSpec((tm, half), lambda j, i: (jnp.minimum(i, npt - 1), 0)),
            pl.BlockSpec((tm, half), lambda j, i: (0, 0)),
            pl.BlockSpec((tm, half), lambda j, i: (0, 0)),
            pl.BlockSpec((tm, tn), lambda j, i: (jnp.minimum(i, npt - 1), j)),
            pl.BlockSpec((tm, tn), lambda j, i: (xs_blk, j)),
            pl.BlockSpec((None, d, tn), lambda j, i: (layer, 0, j)),
            *p_specs,
            *s_specs,
        ],
        out_specs=pl.BlockSpec((tm, tn), lambda j, i: (i, j)),
        scratch_shapes=[pltpu.VMEM((d, tn), BF16)],
        compiler_params=_params("arbitrary", "arbitrary"),
        name="outproj",
    )(conv_p, gla_p, conv_s, gla_s, xp, xs, w_out, mod_p, mod_s)


def _norm2_kernel(x_ref, g_ref, shp, scp, shs, scs, wr_hi, wr_lo, br_ref, hb_ref, ids_ref, wts_ref, h_ref, *, npt, dec_seq, n_grp, n_exp):
    i = pl.program_id(0)
    y = _rms(x_ref[...], g_ref[...])
    _store_by_group(i, npt, dec_seq, h_ref, lambda v, m: v[0] * (1.0 + m[1]) + m[0], [y], [shp, scp], [shs, scs])
    h = h_ref[...]
    h_hi = h.astype(BF16)
    hb_ref[...] = h_hi
    h_lo = (h - h_hi.astype(F32)).astype(BF16)
    logits = (jnp.dot(h_hi, wr_hi[...], preferred_element_type=F32) + jnp.dot(h_lo, wr_hi[...], preferred_element_type=F32)
              + jnp.dot(h_hi, wr_lo[...], preferred_element_type=F32)) + br_ref[...]
    lane = lax.broadcasted_iota(jnp.int32, logits.shape, 1).astype(F32)
    big = jnp.float32(LANES)
    neg = jnp.float32(-jnp.inf)
    gl = jnp.where(lane < n_grp, logits, neg)
    gmax = jnp.max(gl, axis=-1, keepdims=True)
    gidx = jnp.min(jnp.where(gl == gmax, lane, big), axis=-1, keepdims=True)
    g_w = 1.0 / jnp.sum(jnp.exp(gl - gmax), axis=-1, keepdims=True)
    lo = n_grp + gidx * n_exp
    in_grp = (lane >= lo) & (lane < lo + n_exp)
    sl = jnp.where(in_grp, logits, neg)
    p = jnp.exp(sl - jnp.max(sl, axis=-1, keepdims=True))
    p = p / jnp.sum(p, axis=-1, keepdims=True)
    p = jnp.where(in_grp, p, -1.0)
    p1 = jnp.max(p, axis=-1, keepdims=True)
    i1 = jnp.min(jnp.where(p == p1, lane, big), axis=-1, keepdims=True)
    p_rest = jnp.where(lane == i1, -1.0, p)
    p2 = jnp.max(p_rest, axis=-1, keepdims=True)
    i2 = jnp.min(jnp.where(p_rest == p2, lane, big), axis=-1, keepdims=True)
    denom = p1 + p2
    ids_ref[...] = jnp.where(lane == 0, i1 - n_grp, jnp.where(lane == 1, i2 - n_grp, 0.0)).astype(jnp.int32)
    wts_ref[...] = jnp.where(lane == 0, g_w * (p1 / denom), jnp.where(lane == 1, g_w * (p2 / denom), 0.0))


def _norm2(x, g, mod_p, mod_s, wr_hi, wr_lo, br, dims, n_grp, n_exp):
    t, d = x.shape
    tm = ROW_TILE
    p_specs, s_specs = _mod_specs((3, 4), d, dims["npt"], dims["tps"], dims["n_seq"], dims["dec_batch"])
    return pl.pallas_call(
        functools.partial(_norm2_kernel, npt=dims["npt"], dec_seq=dims["dec_seq"], n_grp=n_grp, n_exp=n_exp),
        out_shape=(jax.ShapeDtypeStruct((t, d), BF16), jax.ShapeDtypeStruct((t, LANES), jnp.int32),
                   jax.ShapeDtypeStruct((t, LANES), F32)),
        grid=(t // tm,),
        in_specs=[
            pl.BlockSpec((tm, d), lambda i: (i, 0)),
            pl.BlockSpec((1, d), lambda i: (0, 0)),
            *p_specs,
            *s_specs,
            pl.BlockSpec((d, LANES), lambda i: (0, 0)),
            pl.BlockSpec((d, LANES), lambda i: (0, 0)),
            pl.BlockSpec((1, LANES), lambda i: (0, 0)),
        ],
        out_specs=(pl.BlockSpec((tm, d), lambda i: (i, 0)), pl.BlockSpec((tm, LANES), lambda i: (i, 0)),
                   pl.BlockSpec((tm, LANES), lambda i: (i, 0))),
        scratch_shapes=[pltpu.VMEM((tm, d), F32)],
        compiler_params=_params("arbitrary"),
        name="norm2_router",
    )(x, g.reshape(1, d), mod_p, mod_p, mod_s, mod_s, wr_hi, wr_lo, br)


def _segments(ids, n_experts, n_tiles_max):
    t = ids.shape[0]
    ts, te, al = SORT_TILE, EXPERT_TILE, SEG_ALIGN
    nt = t // ts
    e_iota = jnp.arange(n_experts, dtype=jnp.int32)
    cnt = jnp.sum((ids.reshape(nt, ts * TOP_K, 1) == e_iota).astype(jnp.int32), axis=1)
    size = (cnt + al - 1) // al * al
    src = jnp.cumsum(size, axis=1) - size
    tot = jnp.sum(size, axis=0)
    tot_pad = (tot + te - 1) // te * te
    ends = jnp.cumsum(tot_pad)
    exp_off = ends - tot_pad
    dst = exp_off[None, :] + jnp.cumsum(size, axis=0) - size
    n_used = (ends[-1] // te).astype(jnp.int32)
    tile_start = jnp.arange(n_tiles_max, dtype=jnp.int32) * te
    tile_expert = jnp.minimum(jnp.sum(ends[None, :] <= tile_start[:, None], axis=1), n_experts - 1).astype(jnp.int32)
    last = tile_expert[jnp.maximum(n_used - 1, 0)]
    used = jnp.arange(n_tiles_max) < n_used
    tile_expert = jnp.where(used, tile_expert, last)
    prev_e = jnp.concatenate([jnp.full((1,), -1, jnp.int32), tile_expert[:-1]])
    tile_first = (used & (tile_expert != prev_e)).astype(jnp.int32)
    tile_slot = ((jnp.cumsum(tile_first) - 1) % 2).astype(jnp.int32)
    next_start = jnp.sum(used[None, :] & (tile_expert[None, :] <= tile_expert[:, None]), axis=1)
    tile_next = jnp.where(next_start < n_used, tile_expert[jnp.minimum(next_start, n_tiles_max - 1)], -1).astype(jnp.int32)
    return dict(tile_first=tile_first, tile_slot=tile_slot, tile_next=tile_next,
                src=src.reshape(-1).astype(jnp.int32), dst=dst.reshape(-1).astype(jnp.int32),
                size=size.reshape(-1).astype(jnp.int32), tile_tot=jnp.sum(size, axis=1).astype(jnp.int32),
                fill_start=(exp_off + tot).astype(jnp.int32), fill_size=(tot_pad - tot).astype(jnp.int32),
                tile_expert=tile_expert, n_used=n_used.reshape(1))


def _lane_pick(x, lane, k):
    return jnp.sum(jnp.where(lane == k, x, 0.0), axis=-1, keepdims=True)


def _sorted_positions(ids, lane):
    ts = ids.shape[0]
    onehot = [(lane == _lane_pick(ids, lane, k)).astype(F32) for k in range(TOP_K)]
    row = lax.broadcasted_iota(jnp.int32, (ts, ts), 0)
    col = lax.broadcasted_iota(jnp.int32, (ts, ts), 1)
    before = (col < row).astype(BF16)
    earlier = [jnp.dot(before, o.astype(BF16), preferred_element_type=F32) for o in onehot]
    cnt = [jnp.sum(o, axis=0, keepdims=True) for o in onehot]
    size = jnp.ceil((cnt[0] + cnt[1]) * (1.0 / SEG_ALIGN)) * SEG_ALIGN
    er = lax.broadcasted_iota(jnp.int32, (LANES, LANES), 0)
    ec = lax.broadcasted_iota(jnp.int32, (LANES, LANES), 1)
    start = jnp.dot(jnp.broadcast_to(size, (SUBLANES, LANES)).astype(BF16), (er < ec).astype(BF16),
                    preferred_element_type=F32)[0:1]
    base = [start + earlier[0], start + cnt[0] + earlier[1]]
    return [jnp.sum(onehot[k] * base[k], axis=-1, keepdims=True) for k in range(TOP_K)]


def _as_row(col_vals, lane):
    hi = jnp.floor(col_vals * (1.0 / 64.0))
    lo = col_vals - hi * 64.0
    ones = jnp.ones((SUBLANES, LANES), BF16)
    nt = (((1,), (1,)), ((), ()))
    hi_row = lax.dot_general(ones, jnp.where(lane == 0, hi, 0.0).astype(BF16), nt, preferred_element_type=F32)
    lo_row = lax.dot_general(ones, jnp.where(lane == 0, lo, 0.0).astype(BF16), nt, preferred_element_type=F32)
    return (hi_row * 64.0 + lo_row)[0:1]


def _dispatch_kernel(src_ref, dst_ref, size_ref, tot_ref, fst_ref, fsz_ref, nu_ref, h_ref, ids_ref, wts_ref, xe_hbm,
                     we_hbm, pos_ref, sbuf, wbuf, zx, zw, semx, semw, semz, semt, *, n_experts):
    i = pl.program_id(0)
    n = pl.num_programs(0)
    ts = h_ref.shape[0]
    rows = sbuf.shape[1]
    te = zx.shape[0]
    slot = lax.rem(i, 2)

    def tail_copies(tl):
        r0 = pl.multiple_of(tl * te, te)
        return (pltpu.make_async_copy(zx, xe_hbm.at[pl.ds(r0, te), :], semt.at[0]),
                pltpu.make_async_copy(zw, we_hbm.at[pl.ds(r0, te), :], semt.at[1]))

    n_tail_tiles = xe_hbm.shape[0] // te

    def for_expert_tails(action):
        def body(e, carry):
            sz = pl.multiple_of(fsz_ref[e], SEG_ALIGN)
            d0 = pl.multiple_of(fst_ref[e], SEG_ALIGN)

            @pl.when(sz > 0)
            def _():
                action(pltpu.make_async_copy(zx.at[pl.ds(0, sz), :], xe_hbm.at[pl.ds(d0, sz), :], semz.at[0]))
                action(pltpu.make_async_copy(zw.at[pl.ds(0, sz), :], we_hbm.at[pl.ds(d0, sz), :], semz.at[1]))

            return carry

        lax.fori_loop(0, n_experts, body, 0)

    def for_tail(action, first):
        def body(tl, carry):
            for c in tail_copies(tl):
                action(c)
            return carry

        lax.fori_loop(first, n_tail_tiles, body, 0)

    def seg_copies(tile, sl, e):
        k = tile * n_experts + e
        sz = pl.multiple_of(size_ref[k], SEG_ALIGN)
        s0 = pl.multiple_of(src_ref[k], SEG_ALIGN)
        d0 = pl.multiple_of(dst_ref[k], SEG_ALIGN)
        return sz, (pltpu.make_async_copy(sbuf.at[sl, pl.ds(s0, sz), :], xe_hbm.at[pl.ds(d0, sz), :], semx.at[sl]),
                    pltpu.make_async_copy(wbuf.at[sl, pl.ds(s0, sz), :], we_hbm.at[pl.ds(d0, sz), :], semw.at[sl]))

    def wait_tile(tile, sl):
        tot = pl.multiple_of(tot_ref[tile], SEG_ALIGN)
        pltpu.make_async_copy(sbuf.at[sl, pl.ds(0, tot), :], xe_hbm.at[pl.ds(0, tot), :], semx.at[sl]).wait()
        pltpu.make_async_copy(wbuf.at[sl, pl.ds(0, tot), :], we_hbm.at[pl.ds(0, tot), :], semw.at[sl]).wait()

    @pl.when(i == 0)
    def _():
        zx[...] = jnp.zeros(zx.shape, zx.dtype)
        zw[...] = jnp.zeros(zw.shape, zw.dtype)
        for_expert_tails(lambda c: c.start())

    @pl.when(i >= 2)
    def _():
        wait_tile(i - 2, slot)

    lane = lax.broadcasted_iota(jnp.int32, (ts, LANES), 1).astype(F32)
    pos = _sorted_positions(ids_ref[...].astype(F32), lane)
    pos_ref[...] = jnp.where(lane == 0, pos[0], jnp.where(lane == 1, pos[1], 0.0))
    r_iota = lax.broadcasted_iota(jnp.int32, (rows, ts), 0).astype(F32)
    assert rows <= 4096
    sel = [(r_iota == _as_row(p, lane)).astype(BF16) for p in pos]
    sbuf[slot] = jnp.dot(sel[0] + sel[1], h_ref[...], preferred_element_type=F32).astype(BF16)
    wts = wts_ref[...]
    wsorted = jnp.zeros((rows, LANES), F32)
    for k in range(TOP_K):
        pieces = _split3(_lane_pick(wts, lane, k))
        wk = jnp.where(lane == 0, pieces[0].astype(F32), jnp.where(lane == 1, pieces[1].astype(F32),
                       jnp.where(lane == 2, pieces[2].astype(F32), 0.0)))
        wsorted = wsorted + jnp.dot(sel[k], wk.astype(BF16), preferred_element_type=F32)
    wbuf[slot] = wsorted

    def issue(e, carry):
        sz, copies = seg_copies(i, slot, e)

        @pl.when(sz > 0)
        def _():
            copies[0].start(priority=0)
            copies[1].start(priority=1)

        return carry

    lax.fori_loop(0, n_experts, issue, 0)

    tail_tile = nu_ref[0] + i

    @pl.when(jnp.logical_and(i < n - 1, tail_tile < n_tail_tiles))
    def _():
        for c in tail_copies(tail_tile):
            c.start()

    @pl.when(i == n - 1)
    def _():
        for_tail(lambda c: c.start(), tail_tile)

        @pl.when(i >= 1)
        def _():
            wait_tile(i - 1, 1 - slot)

        wait_tile(i, slot)
        for_tail(lambda c: c.wait(), nu_ref[0])
        for_expert_tails(lambda c: c.wait())


def _dispatch_rows(hb, ids, wts, seg, n_experts, n_tiles_max):
    t, d = hb.shape
    ts, te = SORT_TILE, EXPERT_TILE
    rows = TOP_K * ts + n_experts * SEG_ALIGN
    n_slots = n_tiles_max * te

    def tile_map(i, *_):
        return (i, 0)

    return pl.pallas_call(
        functools.partial(_dispatch_kernel, n_experts=n_experts),
        out_shape=(jax.ShapeDtypeStruct((n_slots, d), BF16), jax.ShapeDtypeStruct((n_slots, LANES), F32),
                   jax.ShapeDtypeStruct((t, LANES), F32)),
        grid_spec=pltpu.PrefetchScalarGridSpec(
            num_scalar_prefetch=7,
            grid=(t // ts,),
            in_specs=[pl.BlockSpec((ts, d), tile_map), pl.BlockSpec((ts, LANES), tile_map),
                      pl.BlockSpec((ts, LANES), tile_map)],
            out_specs=(pl.BlockSpec(memory_space=pl.ANY), pl.BlockSpec(memory_space=pl.ANY),
                       pl.BlockSpec((ts, LANES), tile_map)),
            scratch_shapes=[
                pltpu.VMEM((2, rows, d), BF16),
                pltpu.VMEM((2, rows, LANES), F32),
                pltpu.VMEM((te, d), BF16),
                pltpu.VMEM((te, LANES), F32),
                pltpu.SemaphoreType.DMA((2,)),
                pltpu.SemaphoreType.DMA((2,)),
                pltpu.SemaphoreType.DMA((2,)),
                pltpu.SemaphoreType.DMA((2,)),
            ],
        ),
        compiler_params=_params("arbitrary"),
        name="dispatch",
    )(seg["src"], seg["dst"], seg["size"], seg["tile_tot"], seg["fill_start"], seg["fill_size"], seg["n_used"],
      hb, ids, wts)


def _expert_rows_kernel(te_ref, nu_ref, first_ref, slot_ref, next_ref, x_ref, w_ref, wg_hbm, wu_hbm, wd_hbm, o_ref,
                        wgf, wuf, wdf, wsem, wgb, wub, wdb, *, layer_base):
    i = pl.program_id(0)
    n_used = nu_ref[0]

    def weight_copies(e, sl):
        return (pltpu.make_async_copy(wg_hbm.at[layer_base + e], wgf.at[sl], wsem.at[sl]),
                pltpu.make_async_copy(wu_hbm.at[layer_base + e], wuf.at[sl], wsem.at[sl]),
                pltpu.make_async_copy(wd_hbm.at[layer_base + e], wdf.at[sl], wsem.at[sl]))

    @pl.when(i == 0)
    def _():
        for c in weight_copies(te_ref[0], 0):
            c.start()

    @pl.when(i < n_used)
    def _():
        @pl.when(first_ref[i] == 1)
        def _():
            sl = slot_ref[i]
            for c in weight_copies(te_ref[i], sl):
                c.wait()
            wgb[...] = wgf[sl].astype(BF16)
            wub[...] = wuf[sl].astype(BF16)
            wdb[...] = wdf[sl].astype(BF16)
            nxt = next_ref[i]

            @pl.when(nxt >= 0)
            def _():
                for c in weight_copies(nxt, 1 - sl):
                    c.start()

        x = x_ref[...]
        a = jnp.dot(x, wgb[...], preferred_element_type=F32)
        u = jnp.dot(x, wub[...], preferred_element_type=F32)
        w = jnp.sum(w_ref[...], axis=-1, keepdims=True)
        hid = _silu(a) * u * w
        o_ref[...] = jnp.dot(hid.astype(BF16), wdb[...], preferred_element_type=F32).astype(o_ref.dtype)


def _expert_rows(xe, we, seg, wg, wu, wd, layer_base):
    n_slots, d = xe.shape
    f = wg.shape[-1]
    te = EXPERT_TILE

    def row_map(i, te_ref, nu_ref, *_):
        return (jnp.minimum(i, nu_ref[0] - 1), 0)

    return pl.pallas_call(
        functools.partial(_expert_rows_kernel, layer_base=layer_base),
        out_shape=jax.ShapeDtypeStruct((n_slots, d), BF16),
        grid_spec=pltpu.PrefetchScalarGridSpec(
            num_scalar_prefetch=5,
            grid=(n_slots // te,),
            in_specs=[
                pl.BlockSpec((te, d), row_map),
                pl.BlockSpec((te, LANES), row_map),
                pl.BlockSpec(memory_space=pl.ANY),
                pl.BlockSpec(memory_space=pl.ANY),
                pl.BlockSpec(memory_space=pl.ANY),
            ],
            out_specs=pl.BlockSpec((te, d), row_map),
            scratch_shapes=[
                pltpu.VMEM((2, d, f), F32), pltpu.VMEM((2, d, f), F32), pltpu.VMEM((2, f, d), F32),
                pltpu.SemaphoreType.DMA((2,)),
                pltpu.VMEM((d, f), BF16), pltpu.VMEM((d, f), BF16), pltpu.VMEM((f, d), BF16),
            ],
        ),
        input_output_aliases={5: 0},
        compiler_params=_params("arbitrary"),
        name="experts",
    )(seg["tile_expert"], seg["n_used"], seg["tile_first"], seg["tile_slot"], seg["tile_next"], xe, we, wg, wu, wd)


def _collect_kernel(src_ref, dst_ref, size_ref, tot_ref, y_hbm, x_ref, pos_ref, gp, gs, *rest, npt, n_experts, final,
                    rank):
    if final:
        fg_ref, op_ref, os_ref, buf, sem, o_ref = rest
    else:
        n1_refs = rest[:8]
        o_ref, h_ref, la_ref, buf, sem = rest[8:]
    i = pl.program_id(0)
    n = pl.num_programs(0)
    ts = x_ref.shape[0]
    rows = buf.shape[1]
    slot = lax.rem(i, 2)

    def start(tile, sl):
        def body(pair, carry):
            for prio in range(2):
                k = tile * n_experts + 2 * pair + prio
                sz = pl.multiple_of(size_ref[k], SEG_ALIGN)
                s0 = pl.multiple_of(src_ref[k], SEG_ALIGN)
                d0 = pl.multiple_of(dst_ref[k], SEG_ALIGN)

                @pl.when(sz > 0)
                def _():
                    pltpu.make_async_copy(y_hbm.at[pl.ds(d0, sz), :], buf.at[sl, pl.ds(s0, sz), :],
                                          sem.at[sl]).start(priority=prio)

            return carry

        assert n_experts % 2 == 0
        lax.fori_loop(0, n_experts // 2, body, 0)

    @pl.when(i == 0)
    def _():
        buf[...] = jnp.zeros(buf.shape, buf.dtype)
        start(0, 0)

    @pl.when(i + 1 < n)
    def _():
        start(i + 1, 1 - slot)

    tot = pl.multiple_of(tot_ref[i], SEG_ALIGN)
    pltpu.make_async_copy(y_hbm.at[pl.ds(0, tot), :], buf.at[slot, pl.ds(0, tot), :], sem.at[slot]).wait()
    lane = lax.broadcasted_iota(jnp.int32, (ts, LANES), 1).astype(F32)
    pos = pos_ref[...]
    r_iota = lax.broadcasted_iota(jnp.int32, (ts, rows), 1).astype(F32)
    pick = ((r_iota == _lane_pick(pos, lane, 0)) | (r_iota == _lane_pick(pos, lane, 1))).astype(BF16)
    ff = jnp.dot(pick, buf[slot], preferred_element_type=F32)
    _store_by_group(i, npt, None, o_ref, lambda v, m: v[0] + m[0] * v[1], [x_ref[...], ff], [gp], [gs])
    if final:
        y = _rms(o_ref[...], fg_ref[...])

        @pl.when(i < npt)
        def _():
            op_ref[...] = y

        @pl.when(i >= npt)
        def _():
            os_ref[...] = y
    else:
        _norm1_body(i, o_ref[...], *n1_refs, h_ref, la_ref, npt=npt, rank=rank)


def _collect(ye, pos, x, seg, mod_p, mod_s, dims, n_experts, final_g=None, next_norm=None):
    t, d = x.shape
    ts = SORT_TILE
    npt = dims["npt"]
    rows = TOP_K * ts + n_experts * SEG_ALIGN
    p_specs, s_specs = _mod_specs((5,), d, npt, dims["tps"], dims["n_seq"], dims["dec_batch"])
    final = final_g is not None
    assert final != (next_norm is not None)

    def strip(spec):
        return pl.BlockSpec(spec.block_shape, lambda i, *_, m=spec.index_map: m(i))

    def tile_map(i, *_):
        return (i, 0)

    in_specs = [
        pl.BlockSpec(memory_space=pl.ANY),
        pl.BlockSpec((ts, d), tile_map),
        pl.BlockSpec((ts, LANES), tile_map),
        *[strip(s) for s in p_specs],
        *[strip(s) for s in s_specs],
    ]
    scratch = [pltpu.VMEM((2, rows, d), BF16), pltpu.SemaphoreType.DMA((2,))]
    args = [seg["src"], seg["dst"], seg["size"], seg["tile_tot"], ye, x, pos, mod_p, mod_s]
    if final:
        in_specs.append(pl.BlockSpec((1, d), lambda i, *_: (0, 0)))
        args.append(final_g.reshape(1, d))
        out_shape = (jax.ShapeDtypeStruct((npt * ts, d), F32), jax.ShapeDtypeStruct((t - npt * ts, d), F32))
        out_specs = (pl.BlockSpec((ts, d), lambda i, *_: (jnp.minimum(i, npt - 1), 0)),
                     pl.BlockSpec((ts, d), lambda i, *_: (jnp.maximum(i - npt, 0), 0)))
        scratch.append(pltpu.VMEM((ts, d), F32))
        rank = None
    else:
        nn = next_norm
        rank, qk = nn["rank"], nn["gw2"].shape[1]
        n1_p, n1_s = _mod_specs((0, 1), d, npt, dims["tps"], dims["n_seq"], dims["dec_batch"])
        in_specs += [pl.BlockSpec((1, d), lambda i, *_: (0, 0)),
                     *[strip(s) for s in n1_p], *[strip(s) for s in n1_s],
                     pl.BlockSpec((None, LANES, d), lambda i, *_: (nn["layer"], nn["n_main"] // LANES, 0)),
                     pl.BlockSpec(nn["gw2"].shape, lambda i, *_: (0, 0)),
                     pl.BlockSpec((1, qk), lambda i, *_: (0, 0))]
        args += [nn["g"].reshape(1, d), nn["mod_p"], nn["mod_p"], nn["mod_s"], nn["mod_s"], nn["w_in_t"], nn["gw2"],
                 nn["gb"].reshape(1, qk)]
        out_shape = (jax.ShapeDtypeStruct((t, d), F32), jax.ShapeDtypeStruct((t, d), BF16),
                     jax.ShapeDtypeStruct((t, qk), F32))
        out_specs = (pl.BlockSpec((ts, d), tile_map), pl.BlockSpec((ts, d), tile_map), pl.BlockSpec((ts, qk), tile_map))
    return pl.pallas_call(
        functools.partial(_collect_kernel, npt=npt, n_experts=n_experts, final=final, rank=rank),
        out_shape=out_shape,
        grid_spec=pltpu.PrefetchScalarGridSpec(
            num_scalar_prefetch=4, grid=(t // ts,), in_specs=in_specs, out_specs=out_specs, scratch_shapes=scratch),
        compiler_params=_params("arbitrary"),
        name="collect_final" if final else "collect",
    )(*args)


def kernel(x_prompt, x_sample, c_prompt, c_sample, state_conv, state_gla, w_ada, b_ada, norm1_g, norm2_g, w_in, conv_w, conv_b, conv_ln_g, conv_ln_b, gate_w2, gate_b, gla_norm_g, w_out, router_grp_w, router_grp_b, router_exp_w, router_exp_b, exp_w_gate, exp_w_up, exp_w_down, final_norm_g):
    n_seq, seq_len, d = x_prompt.shape
    dec_batch, dec_seq, _ = x_sample.shape
    depth = w_ada.shape[0]
    kw, d_conv = conv_w.shape[1:]
    heads, dv = gla_norm_g.shape[1:]
    rank, qkw = gate_w2.shape[1:]
    dk = qkw // heads
    n_grp, n_exp = router_exp_w.shape[2:]
    n_experts = n_grp * n_exp
    tp, ts = n_seq * seq_len, dec_batch * dec_seq
    t = tp + ts
    tm = ROW_TILE
    assert ts == tm and seq_len % tm == 0 and d_conv == heads * dv and kw - 1 <= CONV_HALO
    n_main = 2 * d_conv + 2 * qkw + 2 * heads * dv
    dims = dict(npt=tp // tm, tps=seq_len // tm, n_seq=n_seq, dec_batch=dec_batch, dec_seq=dec_seq)

    xp, xs, xs_blk = x_prompt.reshape(tp, d), x_sample.transpose(1, 0, 2).reshape(ts, d), 0
    w_in_t = jnp.swapaxes(w_in, 1, 2)
    st_t = jnp.transpose(state_conv, (0, 2, 1, 3))
    pad = (-n_seq) % 8
    c_all = jnp.concatenate([c_prompt, jnp.zeros((pad, d), F32), c_sample], axis=0)
    mod = _ada(c_all, w_ada, b_ada)
    st = SORT_TILE
    assert tm % st == 0 and st % dec_batch == 0
    dims_sort = dict(npt=tp // st, tps=seq_len // st, n_seq=n_seq, dec_batch=dec_batch, dec_seq=dec_seq)
    n_tiles_max = -(-(TOP_K * t + (t // st) * n_experts * (SEG_ALIGN - 1)) // EXPERT_TILE) + n_experts

    def layer_mods(l):
        return mod[l, :n_seq].reshape(n_seq, 1, 6 * d), mod[l, n_seq + pad:]

    def gate_proj(l):
        return jnp.pad(gate_w2[l], ((0, LANES - rank), (0, 0))).astype(BF16)

    conv_p, gla_p, conv_u, gla_s = [], [], [], []
    for l in range(depth):
        mod_p, mod_s = layer_mods(l)
        if l == 0:
            h, la = _norm1(xp, xs, xs_blk, norm1_g[l], mod_p, mod_s, w_in_t, l, n_main, rank, gate_proj(l), gate_b[l], dims)
        z = _inproj(h, w_in_t, l, n_main)

        cv_p, cb_p = _conv_prompt(z, conv_w[l], conv_b[l], conv_ln_g[l], conv_ln_b[l], n_seq, seq_len)
        cv_s, u_s = _conv_sample(z, st_t, l, conv_w[l], conv_b[l], conv_ln_g[l], conv_ln_b[l], tp, dec_seq, dec_batch)
        conv_p.append(cb_p)
        conv_u.append(u_s)

        go_p, gs_p = _gla_prompt(z, la, gla_norm_g[l], n_seq, seq_len, heads, dk, dv)

        def seq_major(a):
            return a.reshape(dec_seq, dec_batch, a.shape[-1]).transpose(1, 0, 2)

        zs = z[tp:]
        q3 = seq_major(zs[:, 2 * d_conv:2 * d_conv + qkw])
        k3 = seq_major(zs[:, 2 * d_conv + qkw:2 * d_conv + 2 * qkw])
        v3 = seq_major(zs[:, 2 * d_conv + 2 * qkw:2 * d_conv + 2 * qkw + heads * dv])
        g3 = seq_major(zs[:, 2 * d_conv + 2 * qkw + heads * dv:n_main])
        go_s, gs_s = _gla_sample(q3, k3, v3, g3, seq_major(la[tp:]), state_gla, l, gla_norm_g[l], heads, dk, dv,
                                 prev_states=tuple(gla_s) if l + 1 == depth else ())
        go_s = go_s.transpose(1, 0, 2).reshape(ts, heads * dv).astype(BF16)
        gla_p.append(gs_p)
        gla_s.append(gs_s)

        x = _outproj(cv_p, go_p, cv_s.reshape(ts, d_conv), go_s, xp, xs, xs_blk, w_out, l, mod_p, mod_s, dims)

        wr = jnp.concatenate([router_grp_w[l], router_exp_w[l].reshape(d, n_experts)], axis=1)
        wr = jnp.pad(wr, ((0, 0), (0, LANES - wr.shape[1])))
        wr_hi = wr.astype(BF16)
        wr_lo = (wr - wr_hi.astype(F32)).astype(BF16)
        br = jnp.concatenate([router_grp_b[l], router_exp_b[l].reshape(-1)])
        br = jnp.pad(br, (0, LANES - br.shape[0])).reshape(1, LANES)
        hb, ids, wts = _norm2(x, norm2_g[l], mod_p, mod_s, wr_hi, wr_lo, br, dims, n_grp, n_exp)
        seg = _segments(ids[:, :TOP_K], n_experts, n_tiles_max)
        xe, we, pos = _dispatch_rows(hb, ids, wts, seg, n_experts, n_tiles_max)
        f = exp_w_gate.shape[-1]
        ye = _expert_rows(xe, we, seg, exp_w_gate.reshape(depth * n_experts, d, f),
                          exp_w_up.reshape(depth * n_experts, d, f), exp_w_down.reshape(depth * n_experts, f, d),
                          l * n_experts)
        if l + 1 < depth:
            nmod_p, nmod_s = layer_mods(l + 1)
            x, h, la = _collect(ye, pos, x, seg, mod_p, mod_s, dims_sort, n_experts,
                                next_norm=dict(g=norm1_g[l + 1], mod_p=nmod_p, mod_s=nmod_s, w_in_t=w_in_t, layer=l + 1,
                                               n_main=n_main, rank=rank, gw2=gate_proj(l + 1), gb=gate_b[l + 1]))
            xp, xs, xs_blk = x, x, dims["npt"]
        else:
            y_p, y_s = _collect(ye, pos, x, seg, mod_p, mod_s, dims_sort, n_experts, final_g=final_norm_g)

    y_prompt = y_p.reshape(n_seq, seq_len, d)
    y_sample = y_s.reshape(dec_seq, dec_batch, d).transpose(1, 0, 2)
    conv_s = jnp.transpose(_conv_state_sample(st_t, jnp.stack(conv_u)), (0, 2, 1, 3))
    gla_s_all = gla_s[-1] if depth > 1 else gla_s[0][None]
    return (y_prompt, y_sample, jnp.stack(conv_p), jnp.stack(gla_p), conv_s, gla_s_all)
```
